```python
import math
import jax
import jax.numpy as jnp
from jax import lax
import numpy as np

D_MODEL = 1024
BATCH = 4
SEQ = 8192
DEPTH = 1
DEC_BATCH = 1
DEC_SEQ = 16384
PAST_LEN = 128

NORM_EPS = 1e-6
RWKV_HEADS = 8
RWKV_HEAD_DIM = 64
RWKV_WIDTH = RWKV_HEADS * RWKV_HEAD_DIM
RWKV_DECAY_RANK = 64
RWKV_ICLR_RANK = 64
RWKV_GATE_RANK = 128
RWKV_GN_EPS = 64e-5
HYENA_WIDTH = D_MODEL // 2
FILTER_EMB_DIM = 33
FILTER_BANDS = (FILTER_EMB_DIM - 1) // 2
FILTER_HIDDEN = 64
FILTER_TARGET = 1e-2
FILTER_FAST_DECAY_PCT = 0.3
FILTER_SLOW_DECAY_PCT = 1.5
C_RWKV_IN = 3 * RWKV_WIDTH + 2 * RWKV_DECAY_RANK + 2 * RWKV_ICLR_RANK + RWKV_GATE_RANK
C_HYENA_IN = 3 * HYENA_WIDTH
C_GATES = 2 * D_MODEL
C_IN = C_RWKV_IN + C_HYENA_IN + C_GATES
MOE_GROUPS = 4
MOE_EXPERTS_PER_GROUP = 8
MOE_EXPERTS = MOE_GROUPS * MOE_EXPERTS_PER_GROUP
MOE_TOP_K = 2
MOE_HIDDEN = D_MODEL // 2
MOE_BLOCK = 128

kernel_name = 'hybrid_rwkv7_hyena_hmoe_encoder'


def rms_norm(x, g):
    xf = x.astype(jnp.float32)
    y = xf * lax.rsqrt(jnp.mean(xf * xf, axis=-1, keepdims=True) + NORM_EPS)
    return (y * g.astype(jnp.float32)).astype(x.dtype)


def neighbours(u):
    prev = jnp.pad(u[:, :-1], ((0, 0), (1, 0), (0, 0)))
    nxt = jnp.pad(u[:, 1:], ((0, 0), (0, 1), (0, 0)))
    return prev, nxt


def rwkv7_scan(r, decay, k, v, kk, a, reverse):
    b, _, h, n = r.shape

    def step(s, inp):
        r_t, w_t, k_t, v_t, kk_t, a_t = inp
        sa = jnp.einsum('bhvk,bhk->bhv', s, -kk_t)
        s = (s * w_t[:, :, None, :] + sa[..., None] * (kk_t * a_t)[:, :, None, :]
             + v_t[..., None] * k_t[:, :, None, :])
        return s, jnp.einsum('bhvk,bhk->bhv', s, r_t)

    xs = tuple(jnp.swapaxes(t, 0, 1) for t in (r, decay, k, v, kk, a))
    s0 = jnp.zeros((b, h, n, n), jnp.float32)
    _, y = lax.scan(step, s0, xs, reverse=reverse)
    return jnp.swapaxes(y, 0, 1)


def rwkv7_branch(p, mu_prev, mu_next, w0, w2, a0, a2, g2, k_k, k_a, r_k, ln_g, ln_b):
    b, l, _ = p.shape
    f32 = jnp.float32
    h, n, c = RWKV_HEADS, RWKV_HEAD_DIM, RWKV_WIDTH
    rd, ri = RWKV_DECAY_RANK, RWKV_ICLR_RANK
    prev, nxt = neighbours(p)
    p = p + mu_prev * (prev - p) + mu_next * (nxt - p)
    r, k, v, lw, la, lg = jnp.split(p, [c, 2 * c, 3 * c, 3 * c + 2 * rd, 3 * c + 2 * rd + 2 * ri], axis=-1)
    w_raw = w0 + jnp.einsum('bldr,drc->bldc', jnp.tanh(lw.reshape(b, l, 2, rd).astype(f32)), w2.astype(f32))
    decay = jnp.exp(-jnp.exp(-jax.nn.softplus(-w_raw) - 0.5))
    a = jax.nn.sigmoid(a0 + jnp.einsum('bldr,drc->bldc', la.reshape(b, l, 2, ri).astype(f32), a2.astype(f32)))
    g = jax.nn.sigmoid(lg.astype(f32)) @ g2.astype(f32)
    kf = k.astype(f32)
    r_h = r.astype(f32).reshape(b, l, h, n)
    v_h = v.astype(f32).reshape(b, l, h, n)
    kk = (kf * k_k).reshape(b, l, h, n)
    kk = kk * lax.rsqrt(jnp.maximum(jnp.sum(kk * kk, axis=-1, keepdims=True), 1e-24))
    k_dir = (kf[:, :, None, :] * (1.0 + (a - 1.0) * k_a)).reshape(b, l, 2, h, n)
    a_h = a.reshape(b, l, 2, h, n)
    w_h = decay.reshape(b, l, 2, h, n)
    y = (rwkv7_scan(r_h, w_h[:, :, 0], k_dir[:, :, 0], v_h, kk, a_h[:, :, 0], reverse=False)
         + rwkv7_scan(r_h, w_h[:, :, 1], k_dir[:, :, 1], v_h, kk, a_h[:, :, 1], reverse=True))
    mean = jnp.mean(y, axis=-1, keepdims=True)
    var = jnp.mean(jnp.square(y - mean), axis=-1, keepdims=True)
    yn = ((y - mean) * lax.rsqrt(var + RWKV_GN_EPS)).reshape(b, l, c) * ln_g + ln_b
    k_bonus = jnp.mean(k_dir, axis=2)
    bonus = jnp.sum(r_h * k_bonus * r_k, axis=-1, keepdims=True) * v_h
    return ((yn + bonus.reshape(b, l, c)) * g).astype(p.dtype)


def implicit_filter(l, f_w1, f_b1, f_w2, f_b2, f_w3, f_freq):
    f32 = jnp.float32
    t = jnp.linspace(0.0, 1.0, l, dtype=f32)[:, None]
    omega = 2.0 * math.pi * jnp.arange(l, dtype=f32)[:, None] / l
    bands = jnp.linspace(1e-4, FILTER_BANDS - 1, FILTER_BANDS, dtype=f32)[None, :]
    z = jnp.concatenate([t, jnp.cos(bands * omega), -jnp.sin(bands * omega)], axis=-1)
    freq = f_freq.astype(f32)
    hid = jnp.sin(freq * (z @ f_w1.astype(f32) + f_b1.astype(f32)))
    hid = jnp.sin(freq * (hid @ f_w2.astype(f32) + f_b2.astype(f32)))
    filt = (hid @ f_w3.astype(f32)).reshape(l, 2, HYENA_WIDTH)
    min_decay = math.log(FILTER_TARGET) / FILTER_SLOW_DECAY_PCT
    max_decay = math.log(FILTER_TARGET) / FILTER_FAST_DECAY_PCT
    deltas = jnp.abs(jnp.linspace(min_decay, max_decay, HYENA_WIDTH, dtype=f32))
    return filt * jnp.exp(-t * deltas)[:, None, :]


def hyena_branch(p, conv_w, conv_b, f_w1, f_b1, f_w2, f_b2, f_w3, f_freq, hy_skip):
    b, l, _ = p.shape
    prev, nxt = neighbours(p)
    u = prev * conv_w[0] + p * conv_w[1] + nxt * conv_w[2] + conv_b
    x1, x2, v = jnp.split(u, 3, axis=-1)
    filt = implicit_filter(l, f_w1, f_b1, f_w2, f_b2, f_w3, f_freq)
    filt_2l = jnp.concatenate([filt[:, 0], jnp.zeros((1, HYENA_WIDTH), jnp.float32), filt[:0:-1, 1]], axis=0)
    src = (x1 * v).astype(jnp.float32)
    spec = jnp.fft.rfft(src, n=2 * l, axis=1) * jnp.fft.rfft(filt_2l, n=2 * l, axis=0)[None]
    y = jnp.fft.irfft(spec, n=2 * l, axis=1)[:, :l] + src * hy_skip.astype(jnp.float32)
    return x2 * y.astype(p.dtype)


def hierarchical_moe(x, w_route_group, b_route_group, w_route_expert, b_route_expert,
                     w_exp_gate, w_exp_up, w_exp_down):
    b, l, d = x.shape
    xt = x.reshape(b * l, d)
    t = xt.shape[0]
    rows = jnp.arange(t)
    grp_logits = (xt @ w_route_group).astype(jnp.float32) + b_route_group.astype(jnp.float32)
    grp = jnp.argmax(grp_logits, axis=-1)
    p_grp = jax.nn.softmax(grp_logits, axis=-1)[rows, grp]
    exp_logits = ((xt @ w_route_expert).astype(jnp.float32) + b_route_expert.astype(jnp.float32))
    exp_logits = exp_logits.reshape(t, MOE_GROUPS, MOE_EXPERTS_PER_GROUP)[rows, grp]
    top_logit, top_idx = lax.top_k(exp_logits, MOE_TOP_K)
    gate = jax.nn.softmax(top_logit, axis=-1) * p_grp[:, None]
    expert = grp[:, None] * MOE_EXPERTS_PER_GROUP + top_idx
    m = t * MOE_TOP_K
    flat_e = expert.reshape(m)
    order = jnp.argsort(flat_e)
    e_sorted = flat_e[order]
    tok_sorted = (order // MOE_TOP_K).astype(jnp.int32)
    counts = jnp.zeros((MOE_EXPERTS,), jnp.int32).at[flat_e].add(1)
    padded = (counts + MOE_BLOCK - 1) // MOE_BLOCK * MOE_BLOCK
    start = jnp.cumsum(counts) - counts
    pend = jnp.cumsum(padded)
    pstart = pend - padded
    dest = pstart[e_sorted] + jnp.arange(m) - start[e_sorted]
    n_blocks = -(-m // MOE_BLOCK) + MOE_EXPERTS
    slot_tok = jnp.zeros((n_blocks * MOE_BLOCK,), jnp.int32).at[dest].set(tok_sorted)
    slot_w = jnp.zeros((n_blocks * MOE_BLOCK,), jnp.float32).at[dest].set(gate.reshape(m)[order])
    block_expert = jnp.minimum(
        jnp.searchsorted(pend, jnp.arange(n_blocks) * MOE_BLOCK, side='right'), MOE_EXPERTS - 1)

    def run_block(args):
        toks, e = args
        xb = xt[toks]
        hid = jax.nn.silu(xb @ w_exp_gate[e]) * (xb @ w_exp_up[e])
        return hid @ w_exp_down[e]

    out = lax.map(run_block, (slot_tok.reshape(n_blocks, MOE_BLOCK), block_expert))
    y = jax.ops.segment_sum(out.reshape(-1, d) * slot_w[:, None].astype(out.dtype), slot_tok, num_segments=t)
    return y.reshape(b, l, d)


def encoder_layer(x, norm_mix_g, w_in, rwkv_mu_prev, rwkv_mu_next, rwkv_w0, rwkv_w2, rwkv_a0, rwkv_a2,
                  rwkv_g2, rwkv_k_k, rwkv_k_a, rwkv_r_k, rwkv_ln_g, rwkv_ln_b, hyena_conv_w, hyena_conv_b,
                  hyena_f_w1, hyena_f_b1, hyena_f_w2, hyena_f_b2, hyena_f_w3, hyena_f_freq, hyena_skip,
                  w_up_a, w_up_b, w_out, norm_ffn_g, moe_w_route_group, moe_b_route_group,
                  moe_w_route_expert, moe_b_route_expert, moe_w_gate, moe_w_up, moe_w_down):
    xn = rms_norm(x, norm_mix_g)
    p = jnp.einsum('bld,dc->blc', xn, w_in)
    p_a = p[..., :C_RWKV_IN]
    p_b = p[..., C_RWKV_IN:C_RWKV_IN + C_HYENA_IN]
    gate_a, gate_b = jnp.split(p[..., C_RWKV_IN + C_HYENA_IN:], 2, axis=-1)
    ya = rwkv7_branch(p_a, rwkv_mu_prev, rwkv_mu_next, rwkv_w0, rwkv_w2, rwkv_a0, rwkv_a2, rwkv_g2,
                      rwkv_k_k, rwkv_k_a, rwkv_r_k, rwkv_ln_g, rwkv_ln_b)
    yb = hyena_branch(p_b, hyena_conv_w, hyena_conv_b, hyena_f_w1, hyena_f_b1, hyena_f_w2, hyena_f_b2,
                      hyena_f_w3, hyena_f_freq, hyena_skip)
    merged = (jax.nn.sigmoid(gate_a) * jnp.einsum('blc,cd->bld', ya, w_up_a)
              + jax.nn.sigmoid(gate_b) * jnp.einsum('blc,cd->bld', yb, w_up_b))
    x = x + jnp.einsum('bld,de->ble', merged, w_out)
    x = x + hierarchical_moe(rms_norm(x, norm_ffn_g), moe_w_route_group, moe_b_route_group,
                             moe_w_route_expert, moe_b_route_expert, moe_w_gate, moe_w_up, moe_w_down)
    return x


def setup_inputs(seed: int = 0) -> dict:
    key = jax.random.key(seed)
    ks = iter(jax.random.split(key, 48))

    def nrm(shape, scale):
        return scale * jax.random.normal(next(ks), shape, jnp.float32)

    def unif(shape, lo, hi):
        return jax.random.uniform(next(ks), shape, jnp.float32, lo, hi)

    L_ = DEPTH
    D, C, HW, FH = D_MODEL, RWKV_WIDTH, HYENA_WIDTH, FILTER_HIDDEN
    return {
        'x_prompt': nrm((BATCH, SEQ, D), 1.0),
        'x_sample': nrm((DEC_BATCH, DEC_SEQ, D), 1.0),
        'norm_mix_g': 1.0 + nrm((L_, D), 0.05),
        'w_in': nrm((L_, D, C_IN), D ** -0.5),
        'rwkv_mu_prev': unif((L_, C_RWKV_IN), 0.0, 0.4),
        'rwkv_mu_next': unif((L_, C_RWKV_IN), 0.0, 0.4),
        'rwkv_w0': nrm((L_, 2, C), 1.0) - 1.0,
        'rwkv_w2': nrm((L_, 2, RWKV_DECAY_RANK, C), 0.5 * RWKV_DECAY_RANK ** -0.5),
        'rwkv_a0': nrm((L_, 2, C), 0.5),
        'rwkv_a2': nrm((L_, 2, RWKV_ICLR_RANK, C), 0.5 * RWKV_ICLR_RANK ** -0.5),
        'rwkv_g2': nrm((L_, RWKV_GATE_RANK, C), RWKV_GATE_RANK ** -0.5),
        'rwkv_k_k': 0.85 + nrm((L_, C), 0.05),
        'rwkv_k_a': 1.0 + nrm((L_, C), 0.05),
        'rwkv_r_k': nrm((L_, RWKV_HEADS, RWKV_HEAD_DIM), 0.1),
        'rwkv_ln_g': 1.0 + nrm((L_, C), 0.05),
        'rwkv_ln_b': nrm((L_, C), 0.01),
        'hyena_conv_w': nrm((L_, 3, C_HYENA_IN), 3 ** -0.5),
        'hyena_conv_b': nrm((L_, C_HYENA_IN), 0.01),
        'hyena_f_w1': nrm((L_, FILTER_EMB_DIM, FH), FILTER_EMB_DIM ** -0.5),
        'hyena_f_b1': nrm((L_, FH), 0.1),
        'hyena_f_w2': nrm((L_, FH, FH), FH ** -0.5),
        'hyena_f_b2': nrm((L_, FH), 0.1),
        'hyena_f_w3': nrm((L_, FH, 2 * HW), 0.05 * FH ** -0.5),
        'hyena_f_freq': 1.0 + nrm((L_, FH), 0.05),
        'hyena_skip': nrm((L_, HW), 0.5),
        'w_up_a': nrm((L_, C, D), C ** -0.5),
        'w_up_b': nrm((L_, HW, D), HW ** -0.5),
        'w_out': nrm((L_, D, D), D ** -0.5),
        'norm_ffn_g': 1.0 + nrm((L_, D), 0.05),
        'moe_w_route_group': nrm((L_, D, MOE_GROUPS), D ** -0.5),
        'moe_b_route_group': nrm((L_, MOE_GROUPS), 0.01),
        'moe_w_route_expert': nrm((L_, D, MOE_EXPERTS), D ** -0.5),
        'moe_b_route_expert': nrm((L_, MOE_EXPERTS), 0.01),
        'moe_w_gate': nrm((L_, MOE_EXPERTS, D, MOE_HIDDEN), D ** -0.5),
        'moe_w_up': nrm((L_, MOE_EXPERTS, D, MOE_HIDDEN), D ** -0.5),
        'moe_w_down': nrm((L_, MOE_EXPERTS, MOE_HIDDEN, D), MOE_HIDDEN ** -0.5),
        'norm_final_g': 1.0 + nrm((D,), 0.05),
    }


def reference(x_prompt, x_sample, norm_mix_g, w_in, rwkv_mu_prev, rwkv_mu_next, rwkv_w0, rwkv_w2,
              rwkv_a0, rwkv_a2, rwkv_g2, rwkv_k_k, rwkv_k_a, rwkv_r_k, rwkv_ln_g, rwkv_ln_b,
              hyena_conv_w, hyena_conv_b, hyena_f_w1, hyena_f_b1, hyena_f_w2, hyena_f_b2, hyena_f_w3,
              hyena_f_freq, hyena_skip, w_up_a, w_up_b, w_out, norm_ffn_g, moe_w_route_group,
              moe_b_route_group, moe_w_route_expert, moe_b_route_expert, moe_w_gate, moe_w_up,
              moe_w_down, norm_final_g):
    layer_params = (norm_mix_g, w_in, rwkv_mu_prev, rwkv_mu_next, rwkv_w0, rwkv_w2, rwkv_a0, rwkv_a2,
                    rwkv_g2, rwkv_k_k, rwkv_k_a, rwkv_r_k, rwkv_ln_g, rwkv_ln_b, hyena_conv_w, hyena_conv_b,
                    hyena_f_w1, hyena_f_b1, hyena_f_w2, hyena_f_b2, hyena_f_w3, hyena_f_freq, hyena_skip,
                    w_up_a, w_up_b, w_out, norm_ffn_g, moe_w_route_group, moe_b_route_group,
                    moe_w_route_expert, moe_b_route_expert, moe_w_gate, moe_w_up, moe_w_down)

    def trunk(x):
        for layer in range(DEPTH):
            x = encoder_layer(x, *[w[layer] for w in layer_params])
        return rms_norm(x, norm_final_g)

    y_prompt = trunk(x_prompt)
    y_sample = trunk(x_sample)
    return (y_prompt, y_sample)
```

```python
import functools
import math

import jax
import jax.numpy as jnp
import numpy as np
from jax import lax
from jax.experimental import pallas as pl
from jax.experimental.pallas import tpu as pltpu

F32 = jnp.float32
BF16 = jnp.bfloat16
I32 = jnp.int32

D_MODEL = 1024
NORM_EPS = 1e-6
RWKV_HEADS = 8
RWKV_HEAD_DIM = 64
RWKV_WIDTH = RWKV_HEADS * RWKV_HEAD_DIM
RWKV_DECAY_RANK = 64
RWKV_ICLR_RANK = 64
RWKV_GATE_RANK = 128
RWKV_GN_EPS = 64e-5
HYENA_WIDTH = D_MODEL // 2
C_RWKV_IN = 3 * RWKV_WIDTH + 2 * RWKV_DECAY_RANK + 2 * RWKV_ICLR_RANK + RWKV_GATE_RANK
C_HYENA_IN = 3 * HYENA_WIDTH
C_GATES = 2 * D_MODEL

CHUNK = 32
STACK = RWKV_HEADS * CHUNK
HALF = RWKV_WIDTH // 2

VMEM_LIMIT = 48 * 1024 * 1024

_NN = (((1,), (0,)), ((), ()))
_NT = (((1,), (1,)), ((), ()))
_TN = (((0,), (0,)), ((), ()))


def _dot(a, b, dims=_NN):
    return lax.dot_general(a.astype(BF16), b.astype(BF16), dims, preferred_element_type=F32)


def _split2(x):
    hi = x.astype(BF16)
    lo = (x - hi.astype(F32)).astype(BF16)
    return hi, lo


def _dot3(a, b, dims=_NN):
    ah, al = _split2(a)
    bh, bl = _split2(b)
    dg = functools.partial(lax.dot_general, dimension_numbers=dims, preferred_element_type=F32)
    return dg(ah, bh) + (dg(ah, bl) + dg(al, bh))


def _dot_exact_lhs(a_bf16, x):
    x1 = x.astype(BF16)
    r1 = x - x1.astype(F32)
    x2 = r1.astype(BF16)
    x3 = (r1 - x2.astype(F32)).astype(BF16)
    dg = functools.partial(lax.dot_general, dimension_numbers=_NN, preferred_element_type=F32)
    return dg(a_bf16, x1) + (dg(a_bf16, x2) + dg(a_bf16, x3))


def _seg_sum(x, ones_bd):
    x1 = x.astype(BF16)
    r1 = x - x1.astype(F32)
    x2 = r1.astype(BF16)
    x3 = (r1 - x2.astype(F32)).astype(BF16)
    dg = functools.partial(lax.dot_general, dimension_numbers=_NN, preferred_element_type=F32)
    return dg(x1, ones_bd) + (dg(x2, ones_bd) + dg(x3, ones_bd))


def _cparams(*sem):
    return pltpu.CompilerParams(dimension_semantics=tuple(sem), vmem_limit_bytes=VMEM_LIMIT)


def _norm_in_proj_body(x_ref, g_ref, wa_ref, wb_ref, wg_ref, pa_ref, pb_ref, pg_ref):
    x = x_ref[...]
    xn = x * lax.rsqrt(jnp.mean(x * x, axis=-1, keepdims=True) + NORM_EPS) * g_ref[...]
    xb = xn.astype(BF16)
    pa_ref[...] = jnp.dot(xb, wa_ref[...], preferred_element_type=F32)
    pb_ref[...] = jnp.dot(xb, wb_ref[...], preferred_element_type=F32)
    pg_ref[...] = jnp.dot(xb, wg_ref[...], preferred_element_type=F32)


def _norm_in_proj(x, g, w_in, tm):
    t = x.shape[0]
    wa = w_in[:, :C_RWKV_IN].astype(BF16)
    wb = w_in[:, C_RWKV_IN:C_RWKV_IN + C_HYENA_IN].astype(BF16)
    wg = w_in[:, C_RWKV_IN + C_HYENA_IN:].astype(BF16)
    full = lambda i: (0, 0)
    row = lambda i: (i, 0)
    return pl.pallas_call(
        _norm_in_proj_body,
        grid=(t // tm,),
        in_specs=[pl.BlockSpec((tm, D_MODEL), row), pl.BlockSpec((1, D_MODEL), full),
                  pl.BlockSpec(wa.shape, full), pl.BlockSpec(wb.shape, full), pl.BlockSpec(wg.shape, full)],
        out_specs=[pl.BlockSpec((tm, C_RWKV_IN), row), pl.BlockSpec((tm, C_HYENA_IN), row),
                   pl.BlockSpec((tm, C_GATES), row)],
        out_shape=[jax.ShapeDtypeStruct((t, C_RWKV_IN), F32), jax.ShapeDtypeStruct((t, C_HYENA_IN), F32),
                   jax.ShapeDtypeStruct((t, C_GATES), F32)],
        compiler_params=_cparams("parallel"),
        name="norm_in_proj",
    )(x, g.reshape(1, D_MODEL), wa, wb, wg)


def _shifted(p, prev_blk, next_blk, is_first, is_last):
    tm = p.shape[0]
    row = lax.broadcasted_iota(I32, p.shape, 0)
    prow = jnp.where(is_first, 0.0, prev_blk[7:8, :])
    nrow = jnp.where(is_last, 0.0, next_blk[0:1, :])
    prev = jnp.where(row == 0, prow, pltpu.roll(p, 1, 0))
    nxt = jnp.where(row == tm - 1, nrow, pltpu.roll(p, tm - 1, 0))
    return prev, nxt


def _stack_heads(x, head_mask):
    return jnp.where(head_mask, jnp.concatenate([x] * RWKV_HEADS, axis=0), 0.0)


def _unstack_heads(z):
    out = z[0:CHUNK]
    for h in range(1, RWKV_HEADS):
        out = out + z[h * CHUNK:(h + 1) * CHUNK]
    return out


def _chunk_local(r, v, kk, lw, k, a, rev):
    ti = lax.broadcasted_iota(I32, (CHUNK, CHUNK), 0)
    si = lax.broadcasted_iota(I32, (CHUNK, CHUNK), 1)
    tri = (ti <= si) if rev else (ti >= si)
    cl = _dot_exact_lhs(tri.astype(BF16), lw)
    tot = cl[0:1] if rev else cl[CHUNK - 1:CHUNK]
    e_neg = jnp.exp(-cl)
    e_tail = jnp.exp(tot - cl)
    beta = kk * a
    alpha_b = -kk * jnp.exp(cl - lw)
    r_b = r * jnp.exp(cl)
    beta_b = beta * e_neg
    k_b = k * e_neg

    srow = lax.broadcasted_iota(I32, (STACK, RWKV_WIDTH), 0)
    slane = lax.broadcasted_iota(I32, (STACK, RWKV_WIDTH), 1)
    head_mask = (srow >> 5) == (slane >> 6)
    sa_ = _stack_heads(alpha_b, head_mask).astype(BF16)
    sr_ = _stack_heads(r_b, head_mask).astype(BF16)
    sv_ = _stack_heads(v, head_mask).astype(BF16)
    lhs = jnp.concatenate([sa_, sr_], axis=0)
    rhs = jnp.concatenate([beta_b.astype(BF16)] * RWKV_HEADS + [k_b.astype(BF16)] * RWKV_HEADS, axis=0)
    pm = lax.dot_general(lhs, rhs, _NT, preferred_element_type=F32)

    mrow = lax.broadcasted_iota(I32, (STACK, STACK), 0)
    mcol = lax.broadcasted_iota(I32, (STACK, STACK), 1)
    same = (mrow >> 5) == (mcol >> 5)
    if rev:
        strict = same & (mrow < mcol)
        incl = same & (mrow <= mcol)
    else:
        strict = same & (mrow > mcol)
        incl = same & (mrow >= mcol)
    a_ab = jnp.where(strict, pm[:STACK, :STACK], 0.0)
    a_ak = jnp.where(strict, pm[:STACK, STACK:], 0.0)
    a_rb = jnp.where(incl, pm[STACK:, :STACK], 0.0)
    a_rk = jnp.where(incl, pm[STACK:, STACK:], 0.0)

    tinv = jnp.where(mrow == mcol, 1.0, 0.0) + a_ab
    apow = a_ab
    for _ in range(int(math.log2(CHUNK)) - 1):
        apow = _dot3(apow, apow)
        tinv = tinv + _dot3(tinv, apow)

    akv = _dot(a_ak, sv_)
    tw = _dot3(tinv, jnp.concatenate([sa_.astype(F32), akv], axis=1))
    rbx = _dot(a_rb, tw)
    rkv = _dot(a_rk, sv_)
    qt = r_b + _unstack_heads(rbx[:, :RWKV_WIDTH])
    yloc = _unstack_heads(rbx[:, RWKV_WIDTH:] + rkv)
    wt = _unstack_heads(tw[:, :RWKV_WIDTH])
    u = _unstack_heads(tw[:, RWKV_WIDTH:])
    return qt, wt, yloc, u, beta * e_tail, k * e_tail, jnp.exp(tot)


def _rwkv_local_body(nt_seq, p_ref, pp_ref, pn_ref, mup_ref, mun_ref, w0_ref, w2_ref, a0_ref, a2_ref, g2_ref,
                     kk_ref, ka_ref, rk_ref, ones_ref,
                     g_out, bonus_out, v_out, qwf, yuf, bkf, pcf, qwr, yur, bkr, pcr,
                     r_s, v_s, kk_s, lw_s, k_s, a_s):
    i = pl.program_id(0) % nt_seq
    p = p_ref[...]
    tm = p.shape[0]
    prev, nxt = _shifted(p, pp_ref[...], pn_ref[...], i == 0, i == nt_seq - 1)
    ps = p + mup_ref[...] * (prev - p) + mun_ref[...] * (nxt - p)
    c = RWKV_WIDTH
    r = ps[:, 0:c]
    k = ps[:, c:2 * c]
    v = ps[:, 2 * c:3 * c]
    lw = ps[:, 3 * c:3 * c + 128]
    la = ps[:, 3 * c + 128:3 * c + 256]
    lg = ps[:, 3 * c + 256:3 * c + 384]
    ones_bd = ones_ref[...]
    w_raw = w0_ref[...] + _dot3(jnp.tanh(lw), w2_ref[...])
    logw = -jnp.exp(-jnp.logaddexp(-w_raw, 0.0) - 0.5)
    a = jax.nn.sigmoid(a0_ref[...] + _dot3(la, a2_ref[...]))
    g_out[...] = _dot3(jax.nn.sigmoid(lg), g2_ref[...])
    kk0 = k * kk_ref[...]
    kk = kk0 * lax.rsqrt(jnp.maximum(_seg_sum(kk0 * kk0, ones_bd), 1e-24))
    ka = ka_ref[...]
    kdir_f = k * (1.0 + (a[:, :c] - 1.0) * ka)
    kdir_r = k * (1.0 + (a[:, c:] - 1.0) * ka)
    bonus_out[...] = _seg_sum(r * ((kdir_f + kdir_r) * 0.5) * rk_ref[...], ones_bd) * v
    v_out[...] = v
    r_s[...] = r
    v_s[...] = v
    kk_s[...] = kk
    lw_s[...] = logw
    k_s[:, :c] = kdir_f
    k_s[:, c:] = kdir_r
    a_s[...] = a

    def chunk(ci, carry):
        r0 = pl.multiple_of(ci * CHUNK, CHUNK)
        rows = pl.ds(r0, CHUNK)
        rc, vc, kkc = r_s[rows, :], v_s[rows, :], kk_s[rows, :]
        for d, (qw, yu, bk, pc) in enumerate(((qwf, yuf, bkf, pcf), (qwr, yur, bkr, pcr))):
            lanes = slice(d * c, (d + 1) * c)
            qt, wt, yloc, u, bt, kt, pcv = _chunk_local(rc, vc, kkc, lw_s[rows, lanes], k_s[rows, lanes],
                                                        a_s[rows, lanes], rev=(d == 1))
            qw[ci] = jnp.concatenate([qt, wt], axis=0).astype(BF16)
            yu[ci] = jnp.concatenate([yloc, u], axis=0)
            bk[ci] = jnp.concatenate([bt, kt], axis=0).astype(BF16)
            pc[ci] = jnp.broadcast_to(pcv, (8, c))
        return carry

    lax.fori_loop(0, tm // CHUNK, chunk, 0)


def _rwkv_local(p_a, nt_seq, tm, prm):
    t = p_a.shape[0]
    nt = t // tm
    nc = tm // CHUNK
    c = RWKV_WIDTH
    rd, ri = RWKV_DECAY_RANK, RWKV_ICLR_RANK
    z = jnp.zeros((rd, c), F32)
    w2pad = jnp.concatenate([jnp.concatenate([prm["rwkv_w2"][0], z], 0), jnp.concatenate([z, prm["rwkv_w2"][1]], 0)], 1)
    z = jnp.zeros((ri, c), F32)
    a2pad = jnp.concatenate([jnp.concatenate([prm["rwkv_a2"][0], z], 0), jnp.concatenate([z, prm["rwkv_a2"][1]], 0)], 1)
    hd = np.arange(c) // RWKV_HEAD_DIM
    ones_bd = jnp.asarray(hd[:, None] == hd[None, :], BF16)
    full = lambda i: (0, 0)
    row = lambda i: (i, 0)
    row3 = lambda i: (i, 0, 0)
    nblk8 = t // 8
    tm8 = tm // 8
    vec = lambda n: pl.BlockSpec((1, n), full)
    chunk_out = lambda rows, dt: (pl.BlockSpec((nc, rows, c), row3), jax.ShapeDtypeStruct((t // CHUNK, rows, c), dt))
    per_dir = [chunk_out(2 * CHUNK, BF16), chunk_out(2 * CHUNK, F32), chunk_out(2 * CHUNK, BF16), chunk_out(8, F32)]
    outs = [(pl.BlockSpec((tm, c), row), jax.ShapeDtypeStruct((t, c), F32))] * 3 + per_dir + per_dir
    return pl.pallas_call(
        functools.partial(_rwkv_local_body, nt_seq),
        grid=(nt,),
        in_specs=[pl.BlockSpec((tm, C_RWKV_IN), row),
                  pl.BlockSpec((8, C_RWKV_IN), lambda i: (jnp.maximum(i * tm8 - 1, 0), 0)),
                  pl.BlockSpec((8, C_RWKV_IN), lambda i: (jnp.minimum((i + 1) * tm8, nblk8 - 1), 0)),
                  vec(C_RWKV_IN), vec(C_RWKV_IN), vec(2 * c), pl.BlockSpec((128, 2 * c), full),
                  vec(2 * c), pl.BlockSpec((128, 2 * c), full), pl.BlockSpec((RWKV_GATE_RANK, c), full),
                  vec(c), vec(c), vec(c), pl.BlockSpec((c, c), full)],
        out_specs=[o[0] for o in outs],
        out_shape=[o[1] for o in outs],
        scratch_shapes=[pltpu.VMEM((tm, c), F32)] * 3 + [pltpu.VMEM((tm, 2 * c), F32)] * 3,
        compiler_params=_cparams("parallel"),
        name="rwkv_local",
    )(p_a, p_a, p_a, prm["rwkv_mu_prev"].reshape(1, -1), prm["rwkv_mu_next"].reshape(1, -1),
      prm["rwkv_w0"].reshape(1, 2 * c), w2pad, prm["rwkv_a0"].reshape(1, 2 * c), a2pad, prm["rwkv_g2"],
      prm["rwkv_k_k"].reshape(1, c), prm["rwkv_k_a"].reshape(1, c), prm["rwkv_r_k"].reshape(1, c), ones_bd)


def _rwkv_scan_body(nc, qwf, yuf, bkf, pcf, vf, qwr, yur, bkr, pcr, vr, yf_ref, yr_ref, s_ref):
    @pl.when(pl.program_id(1) == 0)
    def _():
        s_ref[...] = jnp.zeros(s_ref.shape, F32)

    brow = lax.broadcasted_iota(I32, (HALF, HALF), 0)
    bcol = lax.broadcasted_iota(I32, (HALF, HALF), 1)
    same_head = (brow >> 6) == (bcol >> 6)

    def one(d, ci, qw_ref, yu_ref, bk_ref, pc_ref, v_ref, y_ref):
        rows = pl.ds(pl.multiple_of(ci * CHUNK, CHUNK), CHUNK)
        qw = qw_ref[ci]
        yu = yu_ref[ci]
        bk = bk_ref[ci]
        pc = pc_ref[ci][0:1]
        vc = v_ref[rows, :]
        for hf in range(2):
            lanes = slice(hf * HALF, (hf + 1) * HALF)
            s = s_ref[d, hf]
            ys = lax.dot_general(qw[:, lanes], s.astype(BF16), _NT, preferred_element_type=F32) + yu[:, lanes]
            y_ref[rows, lanes] = ys[:CHUNK]
            sav = jnp.concatenate([ys[CHUNK:], vc[:, lanes]], axis=0).astype(BF16)
            upd = lax.dot_general(sav, bk[:, lanes], _TN, preferred_element_type=F32)
            s_ref[d, hf] = s * pc[:, lanes] + jnp.where(same_head, upd, 0.0)

    def step(j, carry):
        one(0, j, qwf, yuf, bkf, pcf, vf, yf_ref)
        one(1, nc - 1 - j, qwr, yur, bkr, pcr, vr, yr_ref)
        return carry

    lax.fori_loop(0, nc, step, 0)


def _rwkv_scan(loc, nseq, nt_seq, tm):
    _, _, v, qwf, yuf, bkf, pcf, qwr, yur, bkr, pcr = loc
    t = v.shape[0]
    nc = tm // CHUNK
    c = RWKV_WIDTH
    fwd3 = lambda b, i: (b * nt_seq + i, 0, 0)
    rev3 = lambda b, i: (b * nt_seq + nt_seq - 1 - i, 0, 0)
    fwd2 = lambda b, i: (b * nt_seq + i, 0)
    rev2 = lambda b, i: (b * nt_seq + nt_seq - 1 - i, 0)

    def specs(m3, m2):
        return [pl.BlockSpec((nc, 2 * CHUNK, c), m3), pl.BlockSpec((nc, 2 * CHUNK, c), m3),
                pl.BlockSpec((nc, 2 * CHUNK, c), m3), pl.BlockSpec((nc, 8, c), m3), pl.BlockSpec((tm, c), m2)]

    return pl.pallas_call(
        functools.partial(_rwkv_scan_body, nc),
        grid=(nseq, nt_seq),
        in_specs=specs(fwd3, fwd2) + specs(rev3, rev2),
        out_specs=[pl.BlockSpec((tm, c), fwd2), pl.BlockSpec((tm, c), rev2)],
        out_shape=[jax.ShapeDtypeStruct((t, c), F32)] * 2,
        scratch_shapes=[pltpu.VMEM((2, 2, HALF, HALF), F32)],
        compiler_params=_cparams("arbitrary", "arbitrary"),
        name="rwkv_scan",
    )(qwf, yuf, bkf, pcf, v, qwr, yur, bkr, pcr, v)


def _rwkv_post_body(yf_ref, yr_ref, g_ref, bonus_ref, lng_ref, lnb_ref, ones_ref, o_ref):
    y = yf_ref[...] + yr_ref[...]
    ones_bd = ones_ref[...]
    inv_n = 1.0 / RWKV_HEAD_DIM
    mean = _seg_sum(y, ones_bd) * inv_n
    yc = y - mean
    var = _seg_sum(yc * yc, ones_bd) * inv_n
    yn = yc * lax.rsqrt(var + RWKV_GN_EPS) * lng_ref[...] + lnb_ref[...]
    o_ref[...] = (yn + bonus_ref[...]) * g_ref[...]


def _rwkv_post(yf, yr, g, bonus, prm, tm):
    t, c = yf.shape
    hd = np.arange(c) // RWKV_HEAD_DIM
    ones_bd = jnp.asarray(hd[:, None] == hd[None, :], BF16)
    row = lambda i: (i, 0)
    full = lambda i: (0, 0)
    return pl.pallas_call(
        _rwkv_post_body,
        grid=(t // tm,),
        in_specs=[pl.BlockSpec((tm, c), row)] * 4 + [pl.BlockSpec((1, c), full)] * 2 + [pl.BlockSpec((c, c), full)],
        out_specs=pl.BlockSpec((tm, c), row),
        out_shape=jax.ShapeDtypeStruct((t, c), F32),
        compiler_params=_cparams("parallel"),
        name="rwkv_post",
    )(yf, yr, g, bonus, prm["rwkv_ln_g"].reshape(1, c), prm["rwkv_ln_b"].reshape(1, c), ones_bd)


def _rwkv_branch(p_a, nseq, seq_len, prm):
    tm = min(256, seq_len)
    nt_seq = seq_len // tm
    loc = _rwkv_local(p_a, nt_seq, tm, prm)
    yf, yr = _rwkv_scan(loc, nseq, nt_seq, tm)
    return _rwkv_post(yf, yr, loc[0], loc[1], prm, tm)


FFT_N2 = 128
FILTER_EMB_DIM = 33
FILTER_BANDS = (FILTER_EMB_DIM - 1) // 2
FILTER_HIDDEN = 64
FILTER_TARGET = 1e-2
FILTER_FAST_DECAY_PCT = 0.3
FILTER_SLOW_DECAY_PCT = 1.5


def _hyena_prep_body(nt_seq, p_ref, pp_ref, pn_ref, cw_ref, cb_ref, src_ref, x2_ref):
    i = pl.program_id(0) % nt_seq
    p = p_ref[...]
    prev, nxt = _shifted(p, pp_ref[...], pn_ref[...], i == 0, i == nt_seq - 1)
    cw = cw_ref[...]
    u = prev * cw[0:1] + p * cw[1:2] + nxt * cw[2:3] + cb_ref[...]
    c = HYENA_WIDTH
    src_ref[...] = u[:, 0:c] * u[:, 2 * c:3 * c]
    x2_ref[...] = u[:, c:2 * c]


def _hyena_prep(p_b, nt_seq, tm, prm):
    t = p_b.shape[0]
    c = HYENA_WIDTH
    row = lambda i: (i, 0)
    full = lambda i: (0, 0)
    nblk8 = t // 8
    tm8 = tm // 8
    cw = jnp.concatenate([prm["hyena_conv_w"], jnp.zeros((5, C_HYENA_IN), F32)], axis=0)
    return pl.pallas_call(
        functools.partial(_hyena_prep_body, nt_seq),
        grid=(t // tm,),
        in_specs=[pl.BlockSpec((tm, C_HYENA_IN), row),
                  pl.BlockSpec((8, C_HYENA_IN), lambda i: (jnp.maximum(i * tm8 - 1, 0), 0)),
                  pl.BlockSpec((8, C_HYENA_IN), lambda i: (jnp.minimum((i + 1) * tm8, nblk8 - 1), 0)),
                  pl.BlockSpec((8, C_HYENA_IN), full), pl.BlockSpec((1, C_HYENA_IN), full)],
        out_specs=[pl.BlockSpec((tm, c), row)] * 2,
        out_shape=[jax.ShapeDtypeStruct((t, c), F32)] * 2,
        compiler_params=_cparams("parallel"),
        name="hyena_prep",
    )(p_b, p_b, p_b, cw, prm["hyena_conv_b"].reshape(1, -1))


def _hyena_filter_body(seq_len, w1_ref, b1_ref, w2_ref, b2_ref, w3_ref, freq_ref, delta_ref, o_ref):
    rows = o_ref.shape[0]
    n = pl.program_id(0) * rows + lax.broadcasted_iota(I32, (rows, 128), 0)
    lane = lax.broadcasted_iota(I32, (rows, 128), 1)
    pos = jnp.where(n < seq_len, n, 2 * seq_len - n).astype(F32)
    t = pos * (1.0 / (seq_len - 1))
    omega = (2.0 * math.pi) * pos / seq_len
    band_step = (FILTER_BANDS - 1 - 1e-4) / (FILTER_BANDS - 1)
    band_idx = jnp.where(lane <= FILTER_BANDS, lane - 1, lane - 1 - FILTER_BANDS).astype(F32)
    arg = (1e-4 + band_idx * band_step) * omega
    z = jnp.where(lane == 0, t, jnp.where(lane <= FILTER_BANDS, jnp.cos(arg),
                                          jnp.where(lane <= 2 * FILTER_BANDS, -jnp.sin(arg), 0.0)))
    freq = freq_ref[...]
    hid = jnp.sin(freq * (_dot3(z, w1_ref[...]) + b1_ref[...]))
    hid = jnp.sin(freq * (_dot3(hid, w2_ref[...]) + b2_ref[...]))
    filt = _dot3(hid, w3_ref[...])
    nn = n[:, 0:1]
    tt = t[:, 0:1]
    sel = jnp.where(nn < seq_len, filt[:, :HYENA_WIDTH], jnp.where(nn > seq_len, filt[:, HYENA_WIDTH:], 0.0))
    o_ref[...] = sel * jnp.exp(-tt * delta_ref[...])


def _hyena_filter(seq_len, prm):
    rows = min(1024, 2 * seq_len)
    c = HYENA_WIDTH
    fh = FILTER_HIDDEN
    w1 = jnp.concatenate([prm["hyena_f_w1"], jnp.zeros((128 - FILTER_EMB_DIM, fh), F32)], axis=0)
    min_decay = math.log(FILTER_TARGET) / FILTER_SLOW_DECAY_PCT
    max_decay = math.log(FILTER_TARGET) / FILTER_FAST_DECAY_PCT
    deltas = jnp.abs(jnp.linspace(min_decay, max_decay, c, dtype=F32)).reshape(1, c)
    full = lambda i: (0, 0)
    return pl.pallas_call(
        functools.partial(_hyena_filter_body, seq_len),
        grid=(2 * seq_len // rows,),
        in_specs=[pl.BlockSpec((128, fh), full), pl.BlockSpec((1, fh), full), pl.BlockSpec((fh, fh), full),
                  pl.BlockSpec((1, fh), full), pl.BlockSpec((fh, 2 * c), full), pl.BlockSpec((1, fh), full),
                  pl.BlockSpec((1, c), full)],
        out_specs=pl.BlockSpec((rows, c), lambda i: (i, 0)),
        out_shape=jax.ShapeDtypeStruct((2 * seq_len, c), F32),
        compiler_params=_cparams("parallel"),
        name="hyena_filter",
    )(w1, prm["hyena_f_b1"].reshape(1, fh), prm["hyena_f_w2"], prm["hyena_f_b2"].reshape(1, fh),
      prm["hyena_f_w3"], prm["hyena_f_freq"].reshape(1, fh), deltas)


def _lmul_body(m_ref, x_ref, o_ref):
    o_ref[0] = jnp.dot(m_ref[...], x_ref[0].astype(BF16), preferred_element_type=F32)


def _lmul_epilogue_body(m_ref, x_ref, src_ref, x2_ref, skip_ref, o_ref):
    y = jnp.dot(m_ref[...], x_ref[0].astype(BF16), preferred_element_type=F32)
    o_ref[0] = x2_ref[0] * (y + src_ref[0] * skip_ref[...])


def _lmul(m, x, lt, extra=None):
    nfft, r_in, lanes = x.shape
    r_out = m.shape[0]
    xs = pl.BlockSpec((1, r_in, lt), lambda f, j: (f, 0, j))
    os_ = pl.BlockSpec((1, r_out, lt), lambda f, j: (f, 0, j))
    ms = pl.BlockSpec(m.shape, lambda f, j: (0, 0))
    if extra is None:
        body, ins, specs = _lmul_body, (m, x), [ms, xs]
    else:
        src, x2, skip = extra
        body, ins = _lmul_epilogue_body, (m, x, src, x2, skip)
        specs = [ms, xs, os_, os_, pl.BlockSpec((1, lt), lambda f, j: (0, 0))]
    return pl.pallas_call(
        body,
        grid=(nfft, lanes // lt),
        in_specs=specs,
        out_specs=os_,
        out_shape=jax.ShapeDtypeStruct((nfft, r_out, lanes), F32),
        compiler_params=_cparams("parallel", "parallel"),
        name="fft_outer" if extra is None else "fft_outer_out",
    )(*ins)


def _fft_inner_body(kt, conv, g_ref, y_ref, *rest):
    if conv:
        h_ref, o_ref = rest
    else:
        (o_ref,) = rest
    n2 = FFT_N2
    for q in range(kt):
        g = g_ref[q]
        yk = jnp.concatenate([y_ref[0, 0, q], y_ref[0, 1, q]], axis=0).astype(BF16)
        z = jnp.dot(g, yk, preferred_element_type=F32)
        if conv:
            zr, zi = z[:n2], z[n2:]
            hr, hi = h_ref[0, 0, q], h_ref[0, 1, q]
            pr = zr * hr - zi * hi
            pi = zr * hi + zi * hr
            prod = jnp.concatenate([pr, pi], axis=0).astype(BF16)
            z = lax.dot_general(g, prod, _TN, preferred_element_type=F32)
        o_ref[0, 0, q] = z[:n2]
        o_ref[0, 1, q] = z[n2:]


def _fft_inner(g, y, h, kt):
    nfft, _, n1, n2, c = y.shape
    blk = pl.BlockSpec((1, 2, kt, n2, c), lambda f, j: (f, 0, j, 0, 0))
    gs = pl.BlockSpec((kt, 2 * n2, 2 * n2), lambda f, j: (j, 0, 0))
    conv = h is not None
    ins = (g, y, h) if conv else (g, y)
    specs = [gs, blk, pl.BlockSpec((1, 2, kt, n2, c), lambda f, j: (0, 0, j, 0, 0))] if conv else [gs, blk]
    return pl.pallas_call(
        functools.partial(_fft_inner_body, kt, conv),
        grid=(nfft, n1 // kt),
        in_specs=specs,
        out_specs=blk,
        out_shape=jax.ShapeDtypeStruct(y.shape, F32),
        compiler_params=_cparams("parallel", "parallel"),
        name="fft_inner_conv" if conv else "fft_inner",
    )(*ins)


def _dft_tables(seq_len):
    n = 2 * seq_len
    n2 = FFT_N2
    n1 = n // n2
    k1 = jnp.arange(n1, dtype=I32)
    ang1 = (2.0 * math.pi / n1) * ((k1[:, None] * k1[None, :]) % n1).astype(F32)
    c1, s1 = jnp.cos(ang1), jnp.sin(ang1)
    half = n1 // 2
    f_pair = jnp.concatenate([jnp.concatenate([c1[:, :half], s1[:, :half]], 1),
                              jnp.concatenate([-s1[:, :half], c1[:, :half]], 1)], 0)
    f_real_half = jnp.concatenate([c1[:, :half], -s1[:, :half]], 0)
    f_real_full = jnp.concatenate([c1, -s1], 0)
    ci, si = c1[:half] / n, s1[:half] / n
    b_real = jnp.concatenate([ci, -si], 1)
    b_pair = jnp.concatenate([b_real, jnp.concatenate([si, ci], 1)], 0)
    kk = k1[:, None, None] + n1 * jnp.arange(n2, dtype=I32)[None, :, None]
    ang2 = (2.0 * math.pi / n) * ((kk * jnp.arange(n2, dtype=I32)[None, None, :]) % n).astype(F32)
    c2, s2 = jnp.cos(ang2), jnp.sin(ang2)
    g = jnp.concatenate([jnp.concatenate([c2, s2], 2), jnp.concatenate([-s2, c2], 2)], 1)
    cast = lambda a: a.astype(BF16)
    return dict(f_pair=cast(f_pair), f_real_half=cast(f_real_half), f_real_full=cast(f_real_full),
                b_pair=cast(b_pair), b_real=cast(b_real), g=cast(g), n1=n1)


def _hyena_branch(p_b, nseq, seq_len, prm):
    tm = min(512, seq_len)
    c = HYENA_WIDTH
    src, x2 = _hyena_prep(p_b, seq_len // tm, tm, prm)
    tab = _dft_tables(seq_len)
    n1 = tab["n1"]
    n2 = FFT_N2
    lanes = n2 * c
    lt = min(4096, lanes)
    kt = 4
    filt = _hyena_filter(seq_len, prm)
    hy = _lmul(tab["f_real_full"], filt.reshape(1, n1, lanes), lt)
    hspec = _fft_inner(tab["g"], hy.reshape(1, 2, n1, n2, c), None, kt)
    pair = nseq % 2 == 0
    nfft = nseq // 2 if pair else nseq
    rows = n1 if pair else n1 // 2
    xin = src.reshape(nfft, rows, lanes)
    y = _lmul(tab["f_pair"] if pair else tab["f_real_half"], xin, lt)
    w = _fft_inner(tab["g"], y.reshape(nfft, 2, n1, n2, c), hspec, kt)
    skip = jnp.tile(prm["hyena_skip"].reshape(1, c), (1, lt // c))
    out = _lmul(tab["b_pair"] if pair else tab["b_real"], w.reshape(nfft, 2 * n1, lanes), lt,
                extra=(xin, x2.reshape(nfft, rows, lanes), skip))
    return out.reshape(nseq * seq_len, c)


MOE_GROUPS = 4
MOE_EXPERTS_PER_GROUP = 8
MOE_EXPERTS = MOE_GROUPS * MOE_EXPERTS_PER_GROUP
MOE_HIDDEN = D_MODEL // 2
ROUTE_LANES = 128


def _route(logits):
    lane = lax.broadcasted_iota(I32, logits.shape, 1)
    neg = -jnp.inf
    big = ROUTE_LANES
    gl = jnp.where(lane < MOE_GROUPS, logits, neg)
    gmax = jnp.max(gl, axis=-1, keepdims=True)
    grp = jnp.min(jnp.where(gl == gmax, lane, big), axis=-1, keepdims=True)
    p_grp = 1.0 / jnp.sum(jnp.exp(gl - gmax), axis=-1, keepdims=True)
    lo = MOE_GROUPS + grp * MOE_EXPERTS_PER_GROUP
    el = jnp.where((lane >= lo) & (lane < lo + MOE_EXPERTS_PER_GROUP), logits, neg)
    m1 = jnp.max(el, axis=-1, keepdims=True)
    i1 = jnp.min(jnp.where(el == m1, lane, big), axis=-1, keepdims=True)
    el2 = jnp.where(lane == i1, neg, el)
    m2 = jnp.max(el2, axis=-1, keepdims=True)
    i2 = jnp.min(jnp.where(el2 == m2, lane, big), axis=-1, keepdims=True)
    e2 = jnp.exp(m2 - m1)
    g1 = p_grp / (1.0 + e2)
    g2 = p_grp * e2 / (1.0 + e2)
    gates = jnp.where(lane == i1, g1, jnp.where(lane == i2, g2, 0.0))
    return pltpu.roll(gates, ROUTE_LANES - MOE_GROUPS, 1)


def _merge_body(x_ref, ya_ref, yb_ref, pg_ref, wua_ref, wub_ref, wo_ref, g_ref, wr_ref, br_ref,
                x1_ref, xn_ref, gwt_ref):
    ga = jax.nn.sigmoid(pg_ref[:, :D_MODEL])
    gb = jax.nn.sigmoid(pg_ref[:, D_MODEL:])
    merged = ga * _dot(ya_ref[...], wua_ref[...]) + gb * _dot(yb_ref[...], wub_ref[...])
    x1 = x_ref[...] + _dot(merged, wo_ref[...])
    x1_ref[...] = x1
    xn = x1 * lax.rsqrt(jnp.mean(x1 * x1, axis=-1, keepdims=True) + NORM_EPS) * g_ref[...]
    xn_ref[...] = xn.astype(BF16)
    logits = _dot3(xn, wr_ref[...]) + br_ref[...]
    gwt_ref[...] = _route(logits).T


def _merge(x, ya, yb, pg, prm, wts, tm):
    t = x.shape[0]
    row = lambda i: (i, 0)
    full = lambda i: (0, 0)
    wr = jnp.concatenate([prm["moe_w_route_group"], prm["moe_w_route_expert"],
                          jnp.zeros((D_MODEL, ROUTE_LANES - MOE_GROUPS - MOE_EXPERTS), F32)], axis=1)
    br = jnp.concatenate([prm["moe_b_route_group"], prm["moe_b_route_expert"],
                          jnp.zeros((ROUTE_LANES - MOE_GROUPS - MOE_EXPERTS,), F32)]).reshape(1, ROUTE_LANES)
    return pl.pallas_call(
        _merge_body,
        grid=(t // tm,),
        in_specs=[pl.BlockSpec((tm, D_MODEL), row), pl.BlockSpec((tm, RWKV_WIDTH), row),
                  pl.BlockSpec((tm, HYENA_WIDTH), row), pl.BlockSpec((tm, C_GATES), row),
                  pl.BlockSpec((RWKV_WIDTH, D_MODEL), full), pl.BlockSpec((HYENA_WIDTH, D_MODEL), full),
                  pl.BlockSpec((D_MODEL, D_MODEL), full), pl.BlockSpec((1, D_MODEL), full),
                  pl.BlockSpec((D_MODEL, ROUTE_LANES), full), pl.BlockSpec((1, ROUTE_LANES), full)],
        out_specs=[pl.BlockSpec((tm, D_MODEL), row), pl.BlockSpec((tm, D_MODEL), row),
                   pl.BlockSpec((ROUTE_LANES, tm), lambda i: (0, i))],
        out_shape=[jax.ShapeDtypeStruct((t, D_MODEL), F32), jax.ShapeDtypeStruct((t, D_MODEL), BF16),
                   jax.ShapeDtypeStruct((ROUTE_LANES, t), F32)],
        compiler_params=_cparams("parallel"),
        name="merge_route",
    )(x, ya, yb, pg, wts["w_up_a"], wts["w_up_b"], wts["w_out"], prm["norm_ffn_g"].reshape(1, D_MODEL), wr, br)


MOE_ROWS = 128


def _moe_body(final_norm, cnt_ref, xn_ref, gwt_ref, x1_ref, tri_ref, wg_ref, wu_ref, wd_ref, gf_ref, o_ref, acc_ref, rank_ref):
    i = pl.program_id(0)
    e = pl.program_id(1)
    tt = xn_ref.shape[0]

    @pl.when(e == 0)
    def _():
        acc_ref[...] = jnp.zeros(acc_ref.shape, F32)
        sel_all = (gwt_ref[...] > 0.0).astype(BF16)
        rank_ref[...] = jnp.dot(sel_all, tri_ref[...], preferred_element_type=F32)

    gate = gwt_ref[pl.ds(e, 1), :]
    rank = rank_ref[pl.ds(e, 1), :]
    sel = gate > 0.0
    count = cnt_ref[i * MOE_EXPERTS + e]
    slot = lax.broadcasted_iota(I32, (MOE_ROWS, tt), 0).astype(F32)

    def pass_(c, carry):
        hit = sel & (rank == slot + (c * MOE_ROWS).astype(F32))
        onehot = hit.astype(BF16)
        xg = jnp.dot(onehot, xn_ref[...], preferred_element_type=F32).astype(BF16)
        hg = jnp.dot(xg, wg_ref[0], preferred_element_type=F32)
        hu = jnp.dot(xg, wu_ref[0], preferred_element_type=F32)
        hid = (hg * jax.nn.sigmoid(hg)) * hu
        out = jnp.dot(hid.astype(BF16), wd_ref[0], preferred_element_type=F32)
        row_gate = jnp.sum(jnp.where(hit, gate, 0.0), axis=-1, keepdims=True)
        acc_ref[...] += lax.dot_general(onehot, (out * row_gate).astype(BF16), _TN, preferred_element_type=F32)
        return carry

    lax.fori_loop(0, (count + MOE_ROWS - 1) // MOE_ROWS, pass_, 0)

    @pl.when(e == MOE_EXPERTS - 1)
    def _():
        y = x1_ref[...] + acc_ref[...]
        if final_norm:
            y = y * lax.rsqrt(jnp.mean(y * y, axis=-1, keepdims=True) + NORM_EPS) * gf_ref[...]
        o_ref[...] = y


def _moe_final(xn, gwt, x1, wts, norm_final_g, tt):
    t = xn.shape[0]
    final_norm = norm_final_g is not None
    if not final_norm:
        norm_final_g = jnp.ones((D_MODEL,), F32)
    nt = t // tt
    counts = jnp.sum((gwt[:MOE_EXPERTS] > 0.0).reshape(MOE_EXPERTS, nt, tt), axis=-1, dtype=I32).T.reshape(-1)
    idx = np.arange(tt)
    tri = jnp.asarray(idx[:, None] < idx[None, :], BF16)
    grid_spec = pltpu.PrefetchScalarGridSpec(
        num_scalar_prefetch=1,
        grid=(nt, MOE_EXPERTS),
        in_specs=[pl.BlockSpec((tt, D_MODEL), lambda i, e, c: (i, 0)),
                  pl.BlockSpec((ROUTE_LANES, tt), lambda i, e, c: (0, i)),
                  pl.BlockSpec((tt, D_MODEL), lambda i, e, c: (i, 0)),
                  pl.BlockSpec((tt, tt), lambda i, e, c: (0, 0)),
                  pl.BlockSpec((1, D_MODEL, MOE_HIDDEN), lambda i, e, c: (e, 0, 0)),
                  pl.BlockSpec((1, D_MODEL, MOE_HIDDEN), lambda i, e, c: (e, 0, 0)),
                  pl.BlockSpec((1, MOE_HIDDEN, D_MODEL), lambda i, e, c: (e, 0, 0)),
                  pl.BlockSpec((1, D_MODEL), lambda i, e, c: (0, 0))],
        out_specs=pl.BlockSpec((tt, D_MODEL), lambda i, e, c: (i, 0)),
        scratch_shapes=[pltpu.VMEM((tt, D_MODEL), F32), pltpu.VMEM((ROUTE_LANES, tt), F32)],
    )
    return pl.pallas_call(
        functools.partial(_moe_body, final_norm),
        grid_spec=grid_spec,
        out_shape=jax.ShapeDtypeStruct((t, D_MODEL), F32),
        compiler_params=_cparams("parallel", "arbitrary"),
        name="moe_final",
    )(counts, xn, gwt, x1, tri, wts["moe_w_gate"], wts["moe_w_up"], wts["moe_w_down"],
      norm_final_g.reshape(1, D_MODEL))


def _trunk(x, prm, wts, norm_final_g):
    nseq, seq_len, _ = x.shape
    xf = x.reshape(nseq * seq_len, D_MODEL)
    p_a, p_b, p_g = _norm_in_proj(xf, prm["norm_mix_g"], wts["w_in"], min(256, seq_len))
    ya = _rwkv_branch(p_a, nseq, seq_len, prm)
    yb = _hyena_branch(p_b, nseq, seq_len, prm)
    x1, xn, gwt = _merge(xf, ya, yb, p_g, prm, wts, min(512, seq_len))
    out = _moe_final(xn, gwt, x1, wts, norm_final_g, min(1024, seq_len))
    return out.reshape(nseq, seq_len, D_MODEL)


def kernel(x_prompt, x_sample, norm_mix_g, w_in, rwkv_mu_prev, rwkv_mu_next, rwkv_w0, rwkv_w2, rwkv_a0, rwkv_a2, rwkv_g2, rwkv_k_k, rwkv_k_a, rwkv_r_k, rwkv_ln_g, rwkv_ln_b, hyena_conv_w, hyena_conv_b, hyena_f_w1, hyena_f_b1, hyena_f_w2, hyena_f_b2, hyena_f_w3, hyena_f_freq, hyena_skip, w_up_a, w_up_b, w_out, norm_ffn_g, moe_w_route_group, moe_b_route_group, moe_w_route_expert, moe_b_route_expert, moe_w_gate, moe_w_up, moe_w_down, norm_final_g):
    layer = dict(norm_mix_g=norm_mix_g, w_in=w_in, rwkv_mu_prev=rwkv_mu_prev, rwkv_mu_next=rwkv_mu_next,
                 rwkv_w0=rwkv_w0, rwkv_w2=rwkv_w2, rwkv_a0=rwkv_a0, rwkv_a2=rwkv_a2, rwkv_g2=rwkv_g2,
                 rwkv_k_k=rwkv_k_k, rwkv_k_a=rwkv_k_a, rwkv_r_k=rwkv_r_k, rwkv_ln_g=rwkv_ln_g, rwkv_ln_b=rwkv_ln_b,
                 hyena_conv_w=hyena_conv_w, hyena_conv_b=hyena_conv_b, hyena_f_w1=hyena_f_w1, hyena_f_b1=hyena_f_b1,
                 hyena_f_w2=hyena_f_w2, hyena_f_b2=hyena_f_b2, hyena_f_w3=hyena_f_w3, hyena_f_freq=hyena_f_freq,
                 hyena_skip=hyena_skip, w_up_a=w_up_a, w_up_b=w_up_b, w_out=w_out, norm_ffn_g=norm_ffn_g,
                 moe_w_route_group=moe_w_route_group, moe_b_route_group=moe_b_route_group,
                 moe_w_route_expert=moe_w_route_expert, moe_b_route_expert=moe_b_route_expert,
                 moe_w_gate=moe_w_gate, moe_w_up=moe_w_up, moe_w_down=moe_w_down)
    depth = norm_mix_g.shape[0]
    big = ("w_in", "w_up_a", "w_up_b", "w_out", "moe_w_gate", "moe_w_up", "moe_w_down")

    def trunk(x):
        for li in range(depth):
            prm = {k: v[li] for k, v in layer.items()}
            wts = {k: prm[k].astype(BF16) for k in big}
            last = li == depth - 1
            x = _trunk(x, prm, wts, norm_final_g if last else None)
        return x

    return (trunk(x_prompt), trunk(x_sample))
```

```python
import functools
import math

import jax
import jax.numpy as jnp
import numpy as np
from jax import lax
from jax.experimental import pallas as pl
from jax.experimental.pallas import tpu as pltpu

F32 = jnp.float32
BF16 = jnp.bfloat16
I32 = jnp.int32

D_MODEL = 1024
NORM_EPS = 1e-6
RWKV_HEADS = 8
RWKV_HEAD_DIM = 64
RWKV_WIDTH = RWKV_HEADS * RWKV_HEAD_DIM
RWKV_DECAY_RANK = 64
RWKV_ICLR_RANK = 64
RWKV_GATE_RANK = 128
RWKV_GN_EPS = 64e-5
HYENA_WIDTH = D_MODEL // 2
C_RWKV_IN = 3 * RWKV_WIDTH + 2 * RWKV_DECAY_RANK + 2 * RWKV_ICLR_RANK + RWKV_GATE_RANK
C_HYENA_IN = 3 * HYENA_WIDTH
C_GATES = 2 * D_MODEL

CHUNK = 32
STACK = RWKV_HEADS * CHUNK
HALF = RWKV_WIDTH // 2
GROUP_HEADS = 4
GSTACK = GROUP_HEADS * CHUNK

VMEM_LIMIT = 48 * 1024 * 1024
MOE_VMEM_LIMIT = 56 * 1024 * 1024

_NN = (((1,), (0,)), ((), ()))
_NT = (((1,), (1,)), ((), ()))
_TN = (((0,), (0,)), ((), ()))


def _dot(a, b, dims=_NN):
    return lax.dot_general(a.astype(BF16), b.astype(BF16), dims, preferred_element_type=F32)


def _split2(x):
    hi = x.astype(BF16)
    lo = (x - hi.astype(F32)).astype(BF16)
    return hi, lo


def _dot3(a, b, dims=_NN):
    ah, al = _split2(a)
    bh, bl = _split2(b)
    dg = functools.partial(lax.dot_general, dimension_numbers=dims, preferred_element_type=F32)
    return dg(ah, bh) + (dg(ah, bl) + dg(al, bh))


def _dot_exact_lhs(a_bf16, x):
    x1 = x.astype(BF16)
    r1 = x - x1.astype(F32)
    x2 = r1.astype(BF16)
    x3 = (r1 - x2.astype(F32)).astype(BF16)
    dg = functools.partial(lax.dot_general, dimension_numbers=_NN, preferred_element_type=F32)
    return dg(a_bf16, x1) + (dg(a_bf16, x2) + dg(a_bf16, x3))


def _seg_sum(x, ones_bd):
    x1 = x.astype(BF16)
    r1 = x - x1.astype(F32)
    x2 = r1.astype(BF16)
    x3 = (r1 - x2.astype(F32)).astype(BF16)
    dg = functools.partial(lax.dot_general, dimension_numbers=_NN, preferred_element_type=F32)
    return dg(x1, ones_bd) + (dg(x2, ones_bd) + dg(x3, ones_bd))


def _cparams(*sem):
    return pltpu.CompilerParams(dimension_semantics=tuple(sem), vmem_limit_bytes=VMEM_LIMIT)


def _norm_in_proj_body(x_ref, g_ref, wa_ref, wb_ref, wg_ref, pa_ref, pb_ref, pg_ref):
    x = x_ref[...]
    xn = x * lax.rsqrt(jnp.mean(x * x, axis=-1, keepdims=True) + NORM_EPS) * g_ref[...]
    xb = xn.astype(BF16)
    pa_ref[...] = jnp.dot(xb, wa_ref[...], preferred_element_type=F32)
    pb_ref[...] = jnp.dot(xb, wb_ref[...], preferred_element_type=F32)
    pg_ref[...] = jnp.dot(xb, wg_ref[...], preferred_element_type=F32)


def _norm_in_proj(x, g, w_in, tm):
    t = x.shape[0]
    wa = w_in[:, :C_RWKV_IN].astype(BF16)
    wb = w_in[:, C_RWKV_IN:C_RWKV_IN + C_HYENA_IN].astype(BF16)
    wg = w_in[:, C_RWKV_IN + C_HYENA_IN:].astype(BF16)
    full = lambda i: (0, 0)
    row = lambda i: (i, 0)
    return pl.pallas_call(
        _norm_in_proj_body,
        grid=(t // tm,),
        in_specs=[pl.BlockSpec((tm, D_MODEL), row), pl.BlockSpec((1, D_MODEL), full),
                  pl.BlockSpec(wa.shape, full), pl.BlockSpec(wb.shape, full), pl.BlockSpec(wg.shape, full)],
        out_specs=[pl.BlockSpec((tm, C_RWKV_IN), row), pl.BlockSpec((tm, C_HYENA_IN), row),
                   pl.BlockSpec((tm, C_GATES), row)],
        out_shape=[jax.ShapeDtypeStruct((t, C_RWKV_IN), F32), jax.ShapeDtypeStruct((t, C_HYENA_IN), F32),
                   jax.ShapeDtypeStruct((t, C_GATES), F32)],
        compiler_params=_cparams("parallel"),
        name="norm_in_proj",
    )(x, g.reshape(1, D_MODEL), wa, wb, wg)


def _shifted(p, prev_blk, next_blk, is_first, is_last):
    tm = p.shape[0]
    row = lax.broadcasted_iota(I32, p.shape, 0)
    prow = jnp.where(is_first, 0.0, prev_blk[7:8, :])
    nrow = jnp.where(is_last, 0.0, next_blk[0:1, :])
    prev = jnp.where(row == 0, prow, pltpu.roll(p, 1, 0))
    nxt = jnp.where(row == tm - 1, nrow, pltpu.roll(p, tm - 1, 0))
    return prev, nxt


def _stack_heads(x, head_mask):
    return jnp.where(head_mask, jnp.concatenate([x] * GROUP_HEADS, axis=0), 0.0)


def _unstack_heads(z):
    out = z[0:CHUNK]
    for h in range(1, GROUP_HEADS):
        out = out + z[h * CHUNK:(h + 1) * CHUNK]
    return out


def _chunk_local(r, v, kk, lw, k, a, rev):
    ti = lax.broadcasted_iota(I32, (CHUNK, CHUNK), 0)
    si = lax.broadcasted_iota(I32, (CHUNK, CHUNK), 1)
    tri = (ti <= si) if rev else (ti >= si)
    cl = _dot_exact_lhs(tri.astype(BF16), lw)
    tot = cl[0:1] if rev else cl[CHUNK - 1:CHUNK]
    e_neg = jnp.exp(-cl)
    e_tail = jnp.exp(tot - cl)
    beta = kk * a
    alpha_b = -kk * jnp.exp(cl - lw)
    r_b = r * jnp.exp(cl)
    beta_b = beta * e_neg
    k_b = k * e_neg

    srow = lax.broadcasted_iota(I32, (GSTACK, HALF), 0)
    slane = lax.broadcasted_iota(I32, (GSTACK, HALF), 1)
    head_mask = (srow >> 5) == (slane >> 6)
    mrow = lax.broadcasted_iota(I32, (GSTACK, GSTACK), 0)
    mcol = lax.broadcasted_iota(I32, (GSTACK, GSTACK), 1)
    same = (mrow >> 5) == (mcol >> 5)
    if rev:
        strict = same & (mrow < mcol)
        incl = same & (mrow <= mcol)
    else:
        strict = same & (mrow > mcol)
        incl = same & (mrow >= mcol)

    grp = []
    for g in range(RWKV_HEADS // GROUP_HEADS):
        lanes = slice(g * HALF, (g + 1) * HALF)
        sa_ = _stack_heads(alpha_b[:, lanes], head_mask).astype(BF16)
        sr_ = _stack_heads(r_b[:, lanes], head_mask).astype(BF16)
        sv_ = _stack_heads(v[:, lanes], head_mask).astype(BF16)
        rhs = jnp.concatenate([beta_b[:, lanes].astype(BF16)] * GROUP_HEADS
                              + [k_b[:, lanes].astype(BF16)] * GROUP_HEADS, axis=0)
        pm = lax.dot_general(jnp.concatenate([sa_, sr_], axis=0), rhs, _NT, preferred_element_type=F32)
        grp.append(dict(sa=sa_, sv=sv_, lanes=lanes,
                        ab=jnp.where(strict, pm[:GSTACK, :GSTACK], 0.0),
                        ak=jnp.where(strict, pm[:GSTACK, GSTACK:], 0.0),
                        rb=jnp.where(incl, pm[GSTACK:, :GSTACK], 0.0),
                        rk=jnp.where(incl, pm[GSTACK:, GSTACK:], 0.0)))

    zero = jnp.zeros((GSTACK, GSTACK), F32)
    a_ab = jnp.concatenate([jnp.concatenate([grp[0]["ab"], zero], axis=1),
                            jnp.concatenate([zero, grp[1]["ab"]], axis=1)], axis=0)
    drow = lax.broadcasted_iota(I32, (STACK, STACK), 0)
    dcol = lax.broadcasted_iota(I32, (STACK, STACK), 1)
    tinv = jnp.where(drow == dcol, 1.0, 0.0) + a_ab
    apow = a_ab
    for _ in range(int(math.log2(CHUNK)) - 1):
        apow = _dot(apow, apow)
        tinv = tinv + _dot(tinv, apow)

    outs = []
    for g, gd in enumerate(grp):
        tg = tinv[g * GSTACK:(g + 1) * GSTACK, g * GSTACK:(g + 1) * GSTACK]
        akv = _dot(gd["ak"], gd["sv"])
        tw = _dot(tg, jnp.concatenate([gd["sa"].astype(F32), akv], axis=1))
        rbx = _dot(gd["rb"], tw)
        rkv = _dot(gd["rk"], gd["sv"])
        outs.append((r_b[:, gd["lanes"]] + _unstack_heads(rbx[:, :HALF]), _unstack_heads(tw[:, :HALF]),
                     _unstack_heads(rbx[:, HALF:] + rkv), _unstack_heads(tw[:, HALF:])))
    qt, wt, yloc, u = (jnp.concatenate([outs[0][j], outs[1][j]], axis=1) for j in range(4))
    return qt, wt, yloc, u, beta * e_tail, k * e_tail, jnp.exp(tot)


def _rwkv_local_body(nt_seq, p_ref, pp_ref, pn_ref, mup_ref, mun_ref, w0_ref, w2_ref, a0_ref, a2_ref, g2_ref,
                     kk_ref, ka_ref, rk_ref, ones_ref,
                     g_out, bonus_out, v_out, qwf, yuf, bkf, pcf, qwr, yur, bkr, pcr,
                     r_s, v_s, kk_s, lw_s, k_s, a_s):
    i = pl.program_id(0) % nt_seq
    p = p_ref[...]
    tm = p.shape[0]
    prev, nxt = _shifted(p, pp_ref[...], pn_ref[...], i == 0, i == nt_seq - 1)
    ps = p + mup_ref[...] * (prev - p) + mun_ref[...] * (nxt - p)
    c = RWKV_WIDTH
    r = ps[:, 0:c]
    k = ps[:, c:2 * c]
    v = ps[:, 2 * c:3 * c]
    lw = ps[:, 3 * c:3 * c + 128]
    la = ps[:, 3 * c + 128:3 * c + 256]
    lg = ps[:, 3 * c + 256:3 * c + 384]
    ones_bd = ones_ref[...]
    w_raw = w0_ref[...] + _dot3(jnp.tanh(lw), w2_ref[...])
    logw = -jnp.exp(-jnp.logaddexp(-w_raw, 0.0) - 0.5)
    a = jax.nn.sigmoid(a0_ref[...] + _dot3(la, a2_ref[...]))
    g_out[...] = _dot3(jax.nn.sigmoid(lg), g2_ref[...])
    kk0 = k * kk_ref[...]
    kk = kk0 * lax.rsqrt(jnp.maximum(_seg_sum(kk0 * kk0, ones_bd), 1e-24))
    ka = ka_ref[...]
    kdir_f = k * (1.0 + (a[:, :c] - 1.0) * ka)
    kdir_r = k * (1.0 + (a[:, c:] - 1.0) * ka)
    bonus_out[...] = _seg_sum(r * ((kdir_f + kdir_r) * 0.5) * rk_ref[...], ones_bd) * v
    v_out[...] = v
    r_s[...] = r
    v_s[...] = v
    kk_s[...] = kk
    lw_s[...] = logw
    k_s[:, :c] = kdir_f
    k_s[:, c:] = kdir_r
    a_s[...] = a

    def chunk(ci, carry):
        r0 = pl.multiple_of(ci * CHUNK, CHUNK)
        rows = pl.ds(r0, CHUNK)
        rc, vc, kkc = r_s[rows, :], v_s[rows, :], kk_s[rows, :]
        for d, (qw, yu, bk, pc) in enumerate(((qwf, yuf, bkf, pcf), (qwr, yur, bkr, pcr))):
            lanes = slice(d * c, (d + 1) * c)
            qt, wt, yloc, u, bt, kt, pcv = _chunk_local(rc, vc, kkc, lw_s[rows, lanes], k_s[rows, lanes],
                                                        a_s[rows, lanes], rev=(d == 1))
            qw[ci] = jnp.concatenate([qt, wt], axis=0).astype(BF16)
            yu[ci] = jnp.concatenate([yloc, u], axis=0)
            bk[ci] = jnp.concatenate([bt, kt], axis=0).astype(BF16)
            pc[ci] = jnp.broadcast_to(pcv, (8, c))
        return carry

    lax.fori_loop(0, tm // CHUNK, chunk, 0)


def _rwkv_local(p_a, nt_seq, tm, prm):
    t = p_a.shape[0]
    nt = t // tm
    nc = tm // CHUNK
    c = RWKV_WIDTH
    rd, ri = RWKV_DECAY_RANK, RWKV_ICLR_RANK
    z = jnp.zeros((rd, c), F32)
    w2pad = jnp.concatenate([jnp.concatenate([prm["rwkv_w2"][0], z], 0), jnp.concatenate([z, prm["rwkv_w2"][1]], 0)], 1)
    z = jnp.zeros((ri, c), F32)
    a2pad = jnp.concatenate([jnp.concatenate([prm["rwkv_a2"][0], z], 0), jnp.concatenate([z, prm["rwkv_a2"][1]], 0)], 1)
    hd = np.arange(c) // RWKV_HEAD_DIM
    ones_bd = jnp.asarray(hd[:, None] == hd[None, :], BF16)
    full = lambda i: (0, 0)
    row = lambda i: (i, 0)
    row3 = lambda i: (i, 0, 0)
    nblk8 = t // 8
    tm8 = tm // 8
    vec = lambda n: pl.BlockSpec((1, n), full)
    chunk_out = lambda rows, dt: (pl.BlockSpec((nc, rows, c), row3), jax.ShapeDtypeStruct((t // CHUNK, rows, c), dt))
    per_dir = [chunk_out(2 * CHUNK, BF16), chunk_out(2 * CHUNK, F32), chunk_out(2 * CHUNK, BF16), chunk_out(8, F32)]
    outs = [(pl.BlockSpec((tm, c), row), jax.ShapeDtypeStruct((t, c), F32))] * 3 + per_dir + per_dir
    return pl.pallas_call(
        functools.partial(_rwkv_local_body, nt_seq),
        grid=(nt,),
        in_specs=[pl.BlockSpec((tm, C_RWKV_IN), row),
                  pl.BlockSpec((8, C_RWKV_IN), lambda i: (jnp.maximum(i * tm8 - 1, 0), 0)),
                  pl.BlockSpec((8, C_RWKV_IN), lambda i: (jnp.minimum((i + 1) * tm8, nblk8 - 1), 0)),
                  vec(C_RWKV_IN), vec(C_RWKV_IN), vec(2 * c), pl.BlockSpec((128, 2 * c), full),
                  vec(2 * c), pl.BlockSpec((128, 2 * c), full), pl.BlockSpec((RWKV_GATE_RANK, c), full),
                  vec(c), vec(c), vec(c), pl.BlockSpec((c, c), full)],
        out_specs=[o[0] for o in outs],
        out_shape=[o[1] for o in outs],
        scratch_shapes=[pltpu.VMEM((tm, c), F32)] * 3 + [pltpu.VMEM((tm, 2 * c), F32)] * 3,
        compiler_params=_cparams("parallel"),
        name="rwkv_local",
    )(p_a, p_a, p_a, prm["rwkv_mu_prev"].reshape(1, -1), prm["rwkv_mu_next"].reshape(1, -1),
      prm["rwkv_w0"].reshape(1, 2 * c), w2pad, prm["rwkv_a0"].reshape(1, 2 * c), a2pad, prm["rwkv_g2"],
      prm["rwkv_k_k"].reshape(1, c), prm["rwkv_k_a"].reshape(1, c), prm["rwkv_r_k"].reshape(1, c), ones_bd)


def _rwkv_scan_body(nc, qwf, yuf, bkf, pcf, vf, qwr, yur, bkr, pcr, vr, yf_ref, yr_ref, s_ref):
    @pl.when(pl.program_id(1) == 0)
    def _():
        s_ref[...] = jnp.zeros(s_ref.shape, F32)

    brow = lax.broadcasted_iota(I32, (HALF, HALF), 0)
    bcol = lax.broadcasted_iota(I32, (HALF, HALF), 1)
    same_head = (brow >> 6) == (bcol >> 6)

    def one(d, ci, qw_ref, yu_ref, bk_ref, pc_ref, v_ref, y_ref):
        rows = pl.ds(pl.multiple_of(ci * CHUNK, CHUNK), CHUNK)
        qw = qw_ref[ci]
        yu = yu_ref[ci]
        bk = bk_ref[ci]
        pc = pc_ref[ci][0:1]
        vc = v_ref[rows, :]
        for hf in range(2):
            lanes = slice(hf * HALF, (hf + 1) * HALF)
            s = s_ref[d, hf]
            ys = lax.dot_general(qw[:, lanes], s.astype(BF16), _NT, preferred_element_type=F32) + yu[:, lanes]
            y_ref[rows, lanes] = ys[:CHUNK]
            sav = jnp.concatenate([ys[CHUNK:], vc[:, lanes]], axis=0).astype(BF16)
            upd = lax.dot_general(sav, bk[:, lanes], _TN, preferred_element_type=F32)
            s_ref[d, hf] = s * pc[:, lanes] + jnp.where(same_head, upd, 0.0)

    def step(j, carry):
        one(0, j, qwf, yuf, bkf, pcf, vf, yf_ref)
        one(1, nc - 1 - j, qwr, yur, bkr, pcr, vr, yr_ref)
        return carry

    lax.fori_loop(0, nc, step, 0)


def _rwkv_scan(loc, nseq, nt_seq, tm):
    _, _, v, qwf, yuf, bkf, pcf, qwr, yur, bkr, pcr = loc
    t = v.shape[0]
    nc = tm // CHUNK
    c = RWKV_WIDTH
    fwd3 = lambda b, i: (b * nt_seq + i, 0, 0)
    rev3 = lambda b, i: (b * nt_seq + nt_seq - 1 - i, 0, 0)
    fwd2 = lambda b, i: (b * nt_seq + i, 0)
    rev2 = lambda b, i: (b * nt_seq + nt_seq - 1 - i, 0)

    def specs(m3, m2):
        return [pl.BlockSpec((nc, 2 * CHUNK, c), m3), pl.BlockSpec((nc, 2 * CHUNK, c), m3),
                pl.BlockSpec((nc, 2 * CHUNK, c), m3), pl.BlockSpec((nc, 8, c), m3), pl.BlockSpec((tm, c), m2)]

    return pl.pallas_call(
        functools.partial(_rwkv_scan_body, nc),
        grid=(nseq, nt_seq),
        in_specs=specs(fwd3, fwd2) + specs(rev3, rev2),
        out_specs=[pl.BlockSpec((tm, c), fwd2), pl.BlockSpec((tm, c), rev2)],
        out_shape=[jax.ShapeDtypeStruct((t, c), F32)] * 2,
        scratch_shapes=[pltpu.VMEM((2, 2, HALF, HALF), F32)],
        compiler_params=_cparams("arbitrary", "arbitrary"),
        name="rwkv_scan",
    )(qwf, yuf, bkf, pcf, v, qwr, yur, bkr, pcr, v)


def _rwkv_post_body(yf_ref, yr_ref, g_ref, bonus_ref, lng_ref, lnb_ref, ones_ref, o_ref):
    y = yf_ref[...] + yr_ref[...]
    ones_bd = ones_ref[...]
    inv_n = 1.0 / RWKV_HEAD_DIM
    mean = _seg_sum(y, ones_bd) * inv_n
    yc = y - mean
    var = _seg_sum(yc * yc, ones_bd) * inv_n
    yn = yc * lax.rsqrt(var + RWKV_GN_EPS) * lng_ref[...] + lnb_ref[...]
    o_ref[...] = (yn + bonus_ref[...]) * g_ref[...]


def _rwkv_post(yf, yr, g, bonus, prm, tm):
    t, c = yf.shape
    hd = np.arange(c) // RWKV_HEAD_DIM
    ones_bd = jnp.asarray(hd[:, None] == hd[None, :], BF16)
    row = lambda i: (i, 0)
    full = lambda i: (0, 0)
    return pl.pallas_call(
        _rwkv_post_body,
        grid=(t // tm,),
        in_specs=[pl.BlockSpec((tm, c), row)] * 4 + [pl.BlockSpec((1, c), full)] * 2 + [pl.BlockSpec((c, c), full)],
        out_specs=pl.BlockSpec((tm, c), row),
        out_shape=jax.ShapeDtypeStruct((t, c), F32),
        compiler_params=_cparams("parallel"),
        name="rwkv_post",
    )(yf, yr, g, bonus, prm["rwkv_ln_g"].reshape(1, c), prm["rwkv_ln_b"].reshape(1, c), ones_bd)


def _rwkv_branch(p_a, nseq, seq_len, prm):
    tm = min(256, seq_len)
    nt_seq = seq_len // tm
    loc = _rwkv_local(p_a, nt_seq, tm, prm)
    yf, yr = _rwkv_scan(loc, nseq, nt_seq, tm)
    return _rwkv_post(yf, yr, loc[0], loc[1], prm, tm)


FFT_N2 = 128
FILTER_EMB_DIM = 33
FILTER_BANDS = (FILTER_EMB_DIM - 1) // 2
FILTER_HIDDEN = 64
FILTER_TARGET = 1e-2
FILTER_FAST_DECAY_PCT = 0.3
FILTER_SLOW_DECAY_PCT = 1.5


def _hyena_prep_body(nt_seq, p_ref, pp_ref, pn_ref, cw_ref, cb_ref, src_ref, x2_ref):
    i = pl.program_id(0) % nt_seq
    p = p_ref[...]
    prev, nxt = _shifted(p, pp_ref[...], pn_ref[...], i == 0, i == nt_seq - 1)
    cw = cw_ref[...]
    u = prev * cw[0:1] + p * cw[1:2] + nxt * cw[2:3] + cb_ref[...]
    c = HYENA_WIDTH
    src_ref[...] = u[:, 0:c] * u[:, 2 * c:3 * c]
    x2_ref[...] = u[:, c:2 * c]


def _hyena_prep(p_b, nt_seq, tm, prm):
    t = p_b.shape[0]
    c = HYENA_WIDTH
    row = lambda i: (i, 0)
    full = lambda i: (0, 0)
    nblk8 = t // 8
    tm8 = tm // 8
    cw = jnp.concatenate([prm["hyena_conv_w"], jnp.zeros((5, C_HYENA_IN), F32)], axis=0)
    return pl.pallas_call(
        functools.partial(_hyena_prep_body, nt_seq),
        grid=(t // tm,),
        in_specs=[pl.BlockSpec((tm, C_HYENA_IN), row),
                  pl.BlockSpec((8, C_HYENA_IN), lambda i: (jnp.maximum(i * tm8 - 1, 0), 0)),
                  pl.BlockSpec((8, C_HYENA_IN), lambda i: (jnp.minimum((i + 1) * tm8, nblk8 - 1), 0)),
                  pl.BlockSpec((8, C_HYENA_IN), full), pl.BlockSpec((1, C_HYENA_IN), full)],
        out_specs=[pl.BlockSpec((tm, c), row)] * 2,
        out_shape=[jax.ShapeDtypeStruct((t, c), F32)] * 2,
        compiler_params=_cparams("parallel"),
        name="hyena_prep",
    )(p_b, p_b, p_b, cw, prm["hyena_conv_b"].reshape(1, -1))


def _hyena_filter_body(seq_len, w1_ref, b1_ref, w2_ref, b2_ref, w3_ref, freq_ref, delta_ref, o_ref):
    rows = o_ref.shape[0]
    n = pl.program_id(0) * rows + lax.broadcasted_iota(I32, (rows, 128), 0)
    lane = lax.broadcasted_iota(I32, (rows, 128), 1)
    pos = jnp.where(n < seq_len, n, 2 * seq_len - n).astype(F32)
    t = pos * (1.0 / (seq_len - 1))
    omega = (2.0 * math.pi) * pos / seq_len
    band_step = (FILTER_BANDS - 1 - 1e-4) / (FILTER_BANDS - 1)
    band_idx = jnp.where(lane <= FILTER_BANDS, lane - 1, lane - 1 - FILTER_BANDS).astype(F32)
    arg = (1e-4 + band_idx * band_step) * omega
    z = jnp.where(lane == 0, t, jnp.where(lane <= FILTER_BANDS, jnp.cos(arg),
                                          jnp.where(lane <= 2 * FILTER_BANDS, -jnp.sin(arg), 0.0)))
    freq = freq_ref[...]
    hid = jnp.sin(freq * (_dot3(z, w1_ref[...]) + b1_ref[...]))
    hid = jnp.sin(freq * (_dot3(hid, w2_ref[...]) + b2_ref[...]))
    filt = _dot3(hid, w3_ref[...])
    nn = n[:, 0:1]
    tt = t[:, 0:1]
    sel = jnp.where(nn < seq_len, filt[:, :HYENA_WIDTH], jnp.where(nn > seq_len, filt[:, HYENA_WIDTH:], 0.0))
    o_ref[...] = sel * jnp.exp(-tt * delta_ref[...])


def _hyena_filter(seq_len, prm):
    rows = min(1024, 2 * seq_len)
    c = HYENA_WIDTH
    fh = FILTER_HIDDEN
    w1 = jnp.concatenate([prm["hyena_f_w1"], jnp.zeros((128 - FILTER_EMB_DIM, fh), F32)], axis=0)
    min_decay = math.log(FILTER_TARGET) / FILTER_SLOW_DECAY_PCT
    max_decay = math.log(FILTER_TARGET) / FILTER_FAST_DECAY_PCT
    deltas = jnp.abs(jnp.linspace(min_decay, max_decay, c, dtype=F32)).reshape(1, c)
    full = lambda i: (0, 0)
    return pl.pallas_call(
        functools.partial(_hyena_filter_body, seq_len),
        grid=(2 * seq_len // rows,),
        in_specs=[pl.BlockSpec((128, fh), full), pl.BlockSpec((1, fh), full), pl.BlockSpec((fh, fh), full),
                  pl.BlockSpec((1, fh), full), pl.BlockSpec((fh, 2 * c), full), pl.BlockSpec((1, fh), full),
                  pl.BlockSpec((1, c), full)],
        out_specs=pl.BlockSpec((rows, c), lambda i: (i, 0)),
        out_shape=jax.ShapeDtypeStruct((2 * seq_len, c), F32),
        compiler_params=_cparams("parallel"),
        name="hyena_filter",
    )(w1, prm["hyena_f_b1"].reshape(1, fh), prm["hyena_f_w2"], prm["hyena_f_b2"].reshape(1, fh),
      prm["hyena_f_w3"], prm["hyena_f_freq"].reshape(1, fh), deltas)


def _lmul_body(m_ref, x_ref, o_ref):
    o_ref[0] = jnp.dot(m_ref[...], x_ref[0].astype(BF16), preferred_element_type=F32)


def _lmul_epilogue_body(m_ref, x_ref, src_ref, x2_ref, skip_ref, o_ref):
    y = jnp.dot(m_ref[...], x_ref[0].astype(BF16), preferred_element_type=F32)
    o_ref[0] = x2_ref[0] * (y + src_ref[0] * skip_ref[...])


def _lmul(m, x, lt, extra=None):
    nfft, r_in, lanes = x.shape
    r_out = m.shape[0]
    xs = pl.BlockSpec((1, r_in, lt), lambda f, j: (f, 0, j))
    os_ = pl.BlockSpec((1, r_out, lt), lambda f, j: (f, 0, j))
    ms = pl.BlockSpec(m.shape, lambda f, j: (0, 0))
    if extra is None:
        body, ins, specs = _lmul_body, (m, x), [ms, xs]
    else:
        src, x2, skip = extra
        body, ins = _lmul_epilogue_body, (m, x, src, x2, skip)
        specs = [ms, xs, os_, os_, pl.BlockSpec((1, lt), lambda f, j: (0, 0))]
    return pl.pallas_call(
        body,
        grid=(nfft, lanes // lt),
        in_specs=specs,
        out_specs=os_,
        out_shape=jax.ShapeDtypeStruct((nfft, r_out, lanes), F32),
        compiler_params=_cparams("parallel", "parallel"),
        name="fft_outer" if extra is None else "fft_outer_out",
    )(*ins)


def _fft_inner_body(kt, conv, g_ref, y_ref, *rest):
    if conv:
        h_ref, o_ref = rest
    else:
        (o_ref,) = rest
    n2 = FFT_N2
    for q in range(kt):
        g = g_ref[q]
        yk = jnp.concatenate([y_ref[0, 0, q], y_ref[0, 1, q]], axis=0).astype(BF16)
        z = jnp.dot(g, yk, preferred_element_type=F32)
        if conv:
            zr, zi = z[:n2], z[n2:]
            hr, hi = h_ref[0, 0, q], h_ref[0, 1, q]
            pr = zr * hr - zi * hi
            pi = zr * hi + zi * hr
            prod = jnp.concatenate([pr, pi], axis=0).astype(BF16)
            z = lax.dot_general(g, prod, _TN, preferred_element_type=F32)
        o_ref[0, 0, q] = z[:n2]
        o_ref[0, 1, q] = z[n2:]


def _fft_inner(g, y, h, kt):
    nfft, _, n1, n2, c = y.shape
    blk = pl.BlockSpec((1, 2, kt, n2, c), lambda f, j: (f, 0, j, 0, 0))
    gs = pl.BlockSpec((kt, 2 * n2, 2 * n2), lambda f, j: (j, 0, 0))
    conv = h is not None
    ins = (g, y, h) if conv else (g, y)
    specs = [gs, blk, pl.BlockSpec((1, 2, kt, n2, c), lambda f, j: (0, 0, j, 0, 0))] if conv else [gs, blk]
    return pl.pallas_call(
        functools.partial(_fft_inner_body, kt, conv),
        grid=(nfft, n1 // kt),
        in_specs=specs,
        out_specs=blk,
        out_shape=jax.ShapeDtypeStruct(y.shape, F32),
        compiler_params=_cparams("parallel", "parallel"),
        name="fft_inner_conv" if conv else "fft_inner",
    )(*ins)


def _dft_tables(seq_len):
    n = 2 * seq_len
    n2 = FFT_N2
    n1 = n // n2
    k1 = jnp.arange(n1, dtype=I32)
    ang1 = (2.0 * math.pi / n1) * ((k1[:, None] * k1[None, :]) % n1).astype(F32)
    c1, s1 = jnp.cos(ang1), jnp.sin(ang1)
    half = n1 // 2
    f_pair = jnp.concatenate([jnp.concatenate([c1[:, :half], s1[:, :half]], 1),
                              jnp.concatenate([-s1[:, :half], c1[:, :half]], 1)], 0)
    f_real_half = jnp.concatenate([c1[:, :half], -s1[:, :half]], 0)
    f_real_full = jnp.concatenate([c1, -s1], 0)
    ci, si = c1[:half] / n, s1[:half] / n
    b_real = jnp.concatenate([ci, -si], 1)
    b_pair = jnp.concatenate([b_real, jnp.concatenate([si, ci], 1)], 0)
    kk = k1[:, None, None] + n1 * jnp.arange(n2, dtype=I32)[None, :, None]
    ang2 = (2.0 * math.pi / n) * ((kk * jnp.arange(n2, dtype=I32)[None, None, :]) % n).astype(F32)
    c2, s2 = jnp.cos(ang2), jnp.sin(ang2)
    g = jnp.concatenate([jnp.concatenate([c2, s2], 2), jnp.concatenate([-s2, c2], 2)], 1)
    cast = lambda a: a.astype(BF16)
    return dict(f_pair=cast(f_pair), f_real_half=cast(f_real_half), f_real_full=cast(f_real_full),
                b_pair=cast(b_pair), b_real=cast(b_real), g=cast(g), n1=n1)


def _hyena_branch(p_b, nseq, seq_len, prm):
    tm = min(512, seq_len)
    c = HYENA_WIDTH
    src, x2 = _hyena_prep(p_b, seq_len // tm, tm, prm)
    tab = _dft_tables(seq_len)
    n1 = tab["n1"]
    n2 = FFT_N2
    lanes = n2 * c
    lt = min(4096, lanes)
    kt = 4
    filt = _hyena_filter(seq_len, prm)
    hy = _lmul(tab["f_real_full"], filt.reshape(1, n1, lanes), lt)
    hspec = _fft_inner(tab["g"], hy.reshape(1, 2, n1, n2, c), None, kt)
    pair = nseq % 2 == 0
    nfft = nseq // 2 if pair else nseq
    rows = n1 if pair else n1 // 2
    xin = src.reshape(nfft, rows, lanes)
    y = _lmul(tab["f_pair"] if pair else tab["f_real_half"], xin, lt)
    w = _fft_inner(tab["g"], y.reshape(nfft, 2, n1, n2, c), hspec, kt)
    skip = jnp.tile(prm["hyena_skip"].reshape(1, c), (1, lt // c))
    out = _lmul(tab["b_pair"] if pair else tab["b_real"], w.reshape(nfft, 2 * n1, lanes), lt,
                extra=(xin, x2.reshape(nfft, rows, lanes), skip))
    return out.reshape(nseq * seq_len, c)


MOE_GROUPS = 4
MOE_EXPERTS_PER_GROUP = 8
MOE_EXPERTS = MOE_GROUPS * MOE_EXPERTS_PER_GROUP
MOE_HIDDEN = D_MODEL // 2
ROUTE_LANES = 128


def _route(logits):
    lane = lax.broadcasted_iota(I32, logits.shape, 1)
    neg = -jnp.inf
    big = ROUTE_LANES
    gl = jnp.where(lane < MOE_GROUPS, logits, neg)
    gmax = jnp.max(gl, axis=-1, keepdims=True)
    grp = jnp.min(jnp.where(gl == gmax, lane, big), axis=-1, keepdims=True)
    p_grp = 1.0 / jnp.sum(jnp.exp(gl - gmax), axis=-1, keepdims=True)
    lo = MOE_GROUPS + grp * MOE_EXPERTS_PER_GROUP
    el = jnp.where((lane >= lo) & (lane < lo + MOE_EXPERTS_PER_GROUP), logits, neg)
    m1 = jnp.max(el, axis=-1, keepdims=True)
    i1 = jnp.min(jnp.where(el == m1, lane, big), axis=-1, keepdims=True)
    el2 = jnp.where(lane == i1, neg, el)
    m2 = jnp.max(el2, axis=-1, keepdims=True)
    i2 = jnp.min(jnp.where(el2 == m2, lane, big), axis=-1, keepdims=True)
    e2 = jnp.exp(m2 - m1)
    g1 = p_grp / (1.0 + e2)
    g2 = p_grp * e2 / (1.0 + e2)
    gates = jnp.where(lane == i1, g1, jnp.where(lane == i2, g2, 0.0))
    return pltpu.roll(gates, ROUTE_LANES - MOE_GROUPS, 1)


def _merge_body(x_ref, ya_ref, yb_ref, pg_ref, wua_ref, wub_ref, wo_ref, g_ref, wr_ref, br_ref,
                x1_ref, xn_ref, gwt_ref):
    ga = jax.nn.sigmoid(pg_ref[:, :D_MODEL])
    gb = jax.nn.sigmoid(pg_ref[:, D_MODEL:])
    merged = ga * _dot(ya_ref[...], wua_ref[...]) + gb * _dot(yb_ref[...], wub_ref[...])
    x1 = x_ref[...] + _dot(merged, wo_ref[...])
    x1_ref[...] = x1
    xn = x1 * lax.rsqrt(jnp.mean(x1 * x1, axis=-1, keepdims=True) + NORM_EPS) * g_ref[...]
    xn_ref[...] = xn.astype(BF16)
    logits = _dot3(xn, wr_ref[...]) + br_ref[...]
    gwt_ref[...] = _route(logits).T


def _merge(x, ya, yb, pg, prm, wts, tm):
    t = x.shape[0]
    row = lambda i: (i, 0)
    full = lambda i: (0, 0)
    wr = jnp.concatenate([prm["moe_w_route_group"], prm["moe_w_route_expert"],
                          jnp.zeros((D_MODEL, ROUTE_LANES - MOE_GROUPS - MOE_EXPERTS), F32)], axis=1)
    br = jnp.concatenate([prm["moe_b_route_group"], prm["moe_b_route_expert"],
                          jnp.zeros((ROUTE_LANES - MOE_GROUPS - MOE_EXPERTS,), F32)]).reshape(1, ROUTE_LANES)
    return pl.pallas_call(
        _merge_body,
        grid=(t // tm,),
        in_specs=[pl.BlockSpec((tm, D_MODEL), row), pl.BlockSpec((tm, RWKV_WIDTH), row),
                  pl.BlockSpec((tm, HYENA_WIDTH), row), pl.BlockSpec((tm, C_GATES), row),
                  pl.BlockSpec((RWKV_WIDTH, D_MODEL), full), pl.BlockSpec((HYENA_WIDTH, D_MODEL), full),
                  pl.BlockSpec((D_MODEL, D_MODEL), full), pl.BlockSpec((1, D_MODEL), full),
                  pl.BlockSpec((D_MODEL, ROUTE_LANES), full), pl.BlockSpec((1, ROUTE_LANES), full)],
        out_specs=[pl.BlockSpec((tm, D_MODEL), row), pl.BlockSpec((tm, D_MODEL), row),
                   pl.BlockSpec((ROUTE_LANES, tm), lambda i: (0, i))],
        out_shape=[jax.ShapeDtypeStruct((t, D_MODEL), F32), jax.ShapeDtypeStruct((t, D_MODEL), BF16),
                   jax.ShapeDtypeStruct((ROUTE_LANES, t), F32)],
        compiler_params=_cparams("parallel"),
        name="merge_route",
    )(x, ya, yb, pg, wts["w_up_a"], wts["w_up_b"], wts["w_out"], prm["norm_ffn_g"].reshape(1, D_MODEL), wr, br)


MOE_PAD = 16
MOE_PASS = 64
MOE_BLK = 256


def _moe_slots(tt):
    ns = 2 * tt + MOE_EXPERTS * MOE_PAD
    return (ns + MOE_BLK - 1) // MOE_BLK * MOE_BLK


def _moe_body(final_norm, cnt_ref, start_ref, xn_ref, gwt_ref, x1_ref, tri_ref, pcol_ref, wg_ref, wu_ref, wd_ref,
              gf_ref, o_ref, xg_ref, og_ref, drow_ref, dcol_ref):
    i = pl.program_id(0)
    e = pl.program_id(1)
    tt = xn_ref.shape[0]
    ns = _moe_slots(tt)

    @pl.when(e == 0)
    def _():
        gw = gwt_ref[0:MOE_EXPERTS, :]
        sel = gw > 0.0
        rank = jnp.dot(sel.astype(BF16), tri_ref[...], preferred_element_type=F32)
        dest = pcol_ref[:, 0:1] + rank
        d_lo = jnp.min(jnp.where(sel, dest, 1e9), axis=0, keepdims=True)
        d_hi = jnp.max(jnp.where(sel, dest, -1.0), axis=0, keepdims=True)
        g_lo = jnp.sum(jnp.where(sel & (dest == d_lo), gw, 0.0), axis=0, keepdims=True)
        g_hi = jnp.sum(jnp.where(sel & (dest == d_hi), gw, 0.0), axis=0, keepdims=True)
        single = d_hi == d_lo
        g_hi = jnp.where(single, 0.0, g_hi)
        d_hi = jnp.where(single, -1.0, d_hi)
        rows8 = jnp.concatenate([d_lo, d_hi, g_lo, g_hi, jnp.zeros((4, tt), F32)], axis=0)
        drow_ref[...] = rows8
        dcol_ref[...] = jnp.concatenate([rows8, jnp.zeros((ROUTE_LANES - 8, tt), F32)], axis=0).T
        og_ref[...] = jnp.zeros(og_ref.shape, BF16)
        xg_ref[ns:, :] = jnp.zeros((MOE_PASS, D_MODEL), BF16)
        for blk in range(ns // MOE_BLK):
            r = (blk * MOE_BLK + lax.broadcasted_iota(I32, (MOE_BLK, tt), 0)).astype(F32)
            onehot = ((r == d_lo) | (r == d_hi)).astype(BF16)
            xg_ref[blk * MOE_BLK:(blk + 1) * MOE_BLK, :] = jnp.dot(
                onehot, xn_ref[...], preferred_element_type=F32).astype(BF16)

    count = cnt_ref[i * MOE_EXPERTS + e]
    start = start_ref[i * MOE_EXPERTS + e]
    d_lo, d_hi = drow_ref[0:1, :], drow_ref[1:2, :]
    g_lo, g_hi = drow_ref[2:3, :], drow_ref[3:4, :]

    def pass_(c, carry):
        r0 = pl.multiple_of(start + c * MOE_PASS, MOE_PAD)
        rows = pl.ds(r0, MOE_PASS)
        xg = xg_ref[rows, :]
        hg = jnp.dot(xg, wg_ref[0], preferred_element_type=F32)
        hu = jnp.dot(xg, wu_ref[0], preferred_element_type=F32)
        hid = (hg * jax.nn.sigmoid(hg)) * hu
        out = jnp.dot(hid.astype(BF16), wd_ref[0], preferred_element_type=F32)
        r = (r0 + lax.broadcasted_iota(I32, (MOE_PASS, tt), 0)).astype(F32)
        row_gate = jnp.sum(jnp.where(r == d_lo, g_lo, 0.0) + jnp.where(r == d_hi, g_hi, 0.0), axis=-1, keepdims=True)
        og_ref[rows, :] = (out * row_gate).astype(BF16)
        return carry

    lax.fori_loop(0, (count + MOE_PASS - 1) // MOE_PASS, pass_, 0)

    @pl.when(e == MOE_EXPERTS - 1)
    def _():
        y = x1_ref[...]
        c_lo, c_hi = dcol_ref[:, 0:1], dcol_ref[:, 1:2]
        for blk in range(ns // MOE_BLK):
            r = (blk * MOE_BLK + lax.broadcasted_iota(I32, (tt, MOE_BLK), 1)).astype(F32)
            onehot = ((r == c_lo) | (r == c_hi)).astype(BF16)
            y = y + jnp.dot(onehot, og_ref[blk * MOE_BLK:(blk + 1) * MOE_BLK, :], preferred_element_type=F32)
        if final_norm:
            y = y * lax.rsqrt(jnp.mean(y * y, axis=-1, keepdims=True) + NORM_EPS) * gf_ref[...]
        o_ref[...] = y


def _moe_final(xn, gwt, x1, wts, norm_final_g, tt):
    t = xn.shape[0]
    final_norm = norm_final_g is not None
    if not final_norm:
        norm_final_g = jnp.ones((D_MODEL,), F32)
    nt = t // tt
    counts = jnp.sum((gwt[:MOE_EXPERTS] > 0.0).reshape(MOE_EXPERTS, nt, tt), axis=-1, dtype=I32).T
    padded = (counts + MOE_PAD - 1) // MOE_PAD * MOE_PAD
    starts = jnp.cumsum(padded, axis=1) - padded
    pcol = jnp.broadcast_to(starts.astype(F32).reshape(nt * MOE_EXPERTS, 1), (nt * MOE_EXPERTS, ROUTE_LANES))
    idx = np.arange(tt)
    tri = jnp.asarray(idx[:, None] < idx[None, :], BF16)
    nsb = _moe_slots(tt) + MOE_PASS
    grid_spec = pltpu.PrefetchScalarGridSpec(
        num_scalar_prefetch=2,
        grid=(nt, MOE_EXPERTS),
        in_specs=[pl.BlockSpec((tt, D_MODEL), lambda i, e, c, s: (i, 0)),
                  pl.BlockSpec((ROUTE_LANES, tt), lambda i, e, c, s: (0, i)),
                  pl.BlockSpec((tt, D_MODEL), lambda i, e, c, s: (i, 0)),
                  pl.BlockSpec((tt, tt), lambda i, e, c, s: (0, 0)),
                  pl.BlockSpec((MOE_EXPERTS, ROUTE_LANES), lambda i, e, c, s: (i, 0)),
                  pl.BlockSpec((1, D_MODEL, MOE_HIDDEN), lambda i, e, c, s: (e, 0, 0)),
                  pl.BlockSpec((1, D_MODEL, MOE_HIDDEN), lambda i, e, c, s: (e, 0, 0)),
                  pl.BlockSpec((1, MOE_HIDDEN, D_MODEL), lambda i, e, c, s: (e, 0, 0)),
                  pl.BlockSpec((1, D_MODEL), lambda i, e, c, s: (0, 0))],
        out_specs=pl.BlockSpec((tt, D_MODEL), lambda i, e, c, s: (i, 0)),
        scratch_shapes=[pltpu.VMEM((nsb, D_MODEL), BF16), pltpu.VMEM((nsb, D_MODEL), BF16),
                        pltpu.VMEM((8, tt), F32), pltpu.VMEM((tt, ROUTE_LANES), F32)],
    )
    return pl.pallas_call(
        functools.partial(_moe_body, final_norm),
        grid_spec=grid_spec,
        out_shape=jax.ShapeDtypeStruct((t, D_MODEL), F32),
        compiler_params=pltpu.CompilerParams(dimension_semantics=("parallel", "arbitrary"),
                                             vmem_limit_bytes=MOE_VMEM_LIMIT),
        name="moe_final",
    )(counts.reshape(-1), starts.reshape(-1), xn, gwt, x1, tri, pcol, wts["moe_w_gate"], wts["moe_w_up"],
      wts["moe_w_down"], norm_final_g.reshape(1, D_MODEL))


def _trunk(x, prm, wts, norm_final_g):
    nseq, seq_len, _ = x.shape
    xf = x.reshape(nseq * seq_len, D_MODEL)
    p_a, p_b, p_g = _norm_in_proj(xf, prm["norm_mix_g"], wts["w_in"], min(256, seq_len))
    ya = _rwkv_branch(p_a, nseq, seq_len, prm)
    yb = _hyena_branch(p_b, nseq, seq_len, prm)
    x1, xn, gwt = _merge(xf, ya, yb, p_g, prm, wts, min(512, seq_len))
    out = _moe_final(xn, gwt, x1, wts, norm_final_g, min(1024, seq_len))
    return out.reshape(nseq, seq_len, D_MODEL)


def kernel(x_prompt, x_sample, norm_mix_g, w_in, rwkv_mu_prev, rwkv_mu_next, rwkv_w0, rwkv_w2, rwkv_a0, rwkv_a2, rwkv_g2, rwkv_k_k, rwkv_k_a, rwkv_r_k, rwkv_ln_g, rwkv_ln_b, hyena_conv_w, hyena_conv_b, hyena_f_w1, hyena_f_b1, hyena_f_w2, hyena_f_b2, hyena_f_w3, hyena_f_freq, hyena_skip, w_up_a, w_up_b, w_out, norm_ffn_g, moe_w_route_group, moe_b_route_group, moe_w_route_expert, moe_b_route_expert, moe_w_gate, moe_w_up, moe_w_down, norm_final_g):
    layer = dict(norm_mix_g=norm_mix_g, w_in=w_in, rwkv_mu_prev=rwkv_mu_prev, rwkv_mu_next=rwkv_mu_next,
                 rwkv_w0=rwkv_w0, rwkv_w2=rwkv_w2, rwkv_a0=rwkv_a0, rwkv_a2=rwkv_a2, rwkv_g2=rwkv_g2,
                 rwkv_k_k=rwkv_k_k, rwkv_k_a=rwkv_k_a, rwkv_r_k=rwkv_r_k, rwkv_ln_g=rwkv_ln_g, rwkv_ln_b=rwkv_ln_b,
                 hyena_conv_w=hyena_conv_w, hyena_conv_b=hyena_conv_b, hyena_f_w1=hyena_f_w1, hyena_f_b1=hyena_f_b1,
                 hyena_f_w2=hyena_f_w2, hyena_f_b2=hyena_f_b2, hyena_f_w3=hyena_f_w3, hyena_f_freq=hyena_f_freq,
                 hyena_skip=hyena_skip, w_up_a=w_up_a, w_up_b=w_up_b, w_out=w_out, norm_ffn_g=norm_ffn_g,
                 moe_w_route_group=moe_w_route_group, moe_b_route_group=moe_b_route_group,
                 moe_w_route_expert=moe_w_route_expert, moe_b_route_expert=moe_b_route_expert,
                 moe_w_gate=moe_w_gate, moe_w_up=moe_w_up, moe_w_down=moe_w_down)
    depth = norm_mix_g.shape[0]
    big = ("w_in", "w_up_a", "w_up_b", "w_out", "moe_w_gate", "moe_w_up", "moe_w_down")

    def trunk(x):
        for li in range(depth):
            prm = {k: v[li] for k, v in layer.items()}
            wts = {k: prm[k].astype(BF16) for k in big}
            last = li == depth - 1
            x = _trunk(x, prm, wts, norm_final_g if last else None)
        return x

    return (trunk(x_prompt), trunk(x_sample))
```

```python
import functools
import math

import jax
import jax.numpy as jnp
import numpy as np
from jax import lax
from jax.experimental import pallas as pl
from jax.experimental.pallas import tpu as pltpu

F32 = jnp.float32
BF16 = jnp.bfloat16
I32 = jnp.int32

D_MODEL = 1024
NORM_EPS = 1e-6
RWKV_HEADS = 8
RWKV_HEAD_DIM = 64
RWKV_WIDTH = RWKV_HEADS * RWKV_HEAD_DIM
RWKV_DECAY_RANK = 64
RWKV_ICLR_RANK = 64
RWKV_GATE_RANK = 128
RWKV_GN_EPS = 64e-5
HYENA_WIDTH = D_MODEL // 2
C_RWKV_IN = 3 * RWKV_WIDTH + 2 * RWKV_DECAY_RANK + 2 * RWKV_ICLR_RANK + RWKV_GATE_RANK
C_HYENA_IN = 3 * HYENA_WIDTH
C_GATES = 2 * D_MODEL

CHUNK = 32
STACK = RWKV_HEADS * CHUNK
HALF = RWKV_WIDTH // 2
GROUP_HEADS = 4
GSTACK = GROUP_HEADS * CHUNK
LOCAL_CHUNKS_PER_ITER = 2

VMEM_LIMIT = 48 * 1024 * 1024
MOE_VMEM_LIMIT = 56 * 1024 * 1024

_NN = (((1,), (0,)), ((), ()))
_NT = (((1,), (1,)), ((), ()))
_TN = (((0,), (0,)), ((), ()))


def _dot(a, b, dims=_NN):
    return lax.dot_general(a.astype(BF16), b.astype(BF16), dims, preferred_element_type=F32)


def _split2(x):
    hi = x.astype(BF16)
    lo = (x - hi.astype(F32)).astype(BF16)
    return hi, lo


def _dot3(a, b, dims=_NN):
    ah, al = _split2(a)
    bh, bl = _split2(b)
    dg = functools.partial(lax.dot_general, dimension_numbers=dims, preferred_element_type=F32)
    return dg(ah, bh) + (dg(ah, bl) + dg(al, bh))


def _dot_exact_lhs(a_bf16, x):
    x1 = x.astype(BF16)
    r1 = x - x1.astype(F32)
    x2 = r1.astype(BF16)
    x3 = (r1 - x2.astype(F32)).astype(BF16)
    dg = functools.partial(lax.dot_general, dimension_numbers=_NN, preferred_element_type=F32)
    return dg(a_bf16, x1) + (dg(a_bf16, x2) + dg(a_bf16, x3))


def _seg_sum(x, ones_bd):
    x1 = x.astype(BF16)
    r1 = x - x1.astype(F32)
    x2 = r1.astype(BF16)
    x3 = (r1 - x2.astype(F32)).astype(BF16)
    dg = functools.partial(lax.dot_general, dimension_numbers=_NN, preferred_element_type=F32)
    return dg(x1, ones_bd) + (dg(x2, ones_bd) + dg(x3, ones_bd))


def _cparams(*sem):
    return pltpu.CompilerParams(dimension_semantics=tuple(sem), vmem_limit_bytes=VMEM_LIMIT)


def _norm_in_proj_body(x_ref, g_ref, wa_ref, wb_ref, wg_ref, pa_ref, pb_ref, pg_ref):
    x = x_ref[...]
    xn = x * lax.rsqrt(jnp.mean(x * x, axis=-1, keepdims=True) + NORM_EPS) * g_ref[...]
    xb = xn.astype(BF16)
    pa_ref[...] = jnp.dot(xb, wa_ref[...], preferred_element_type=F32)
    pb_ref[...] = jnp.dot(xb, wb_ref[...], preferred_element_type=F32)
    pg_ref[...] = jnp.dot(xb, wg_ref[...], preferred_element_type=F32)


def _norm_in_proj(x, g, w_in, tm):
    t = x.shape[0]
    wa = w_in[:, :C_RWKV_IN].astype(BF16)
    wb = w_in[:, C_RWKV_IN:C_RWKV_IN + C_HYENA_IN].astype(BF16)
    wg = w_in[:, C_RWKV_IN + C_HYENA_IN:].astype(BF16)
    full = lambda i: (0, 0)
    row = lambda i: (i, 0)
    return pl.pallas_call(
        _norm_in_proj_body,
        grid=(t // tm,),
        in_specs=[pl.BlockSpec((tm, D_MODEL), row), pl.BlockSpec((1, D_MODEL), full),
                  pl.BlockSpec(wa.shape, full), pl.BlockSpec(wb.shape, full), pl.BlockSpec(wg.shape, full)],
        out_specs=[pl.BlockSpec((tm, C_RWKV_IN), row), pl.BlockSpec((tm, C_HYENA_IN), row),
                   pl.BlockSpec((tm, C_GATES), row)],
        out_shape=[jax.ShapeDtypeStruct((t, C_RWKV_IN), F32), jax.ShapeDtypeStruct((t, C_HYENA_IN), F32),
                   jax.ShapeDtypeStruct((t, C_GATES), F32)],
        compiler_params=_cparams("parallel"),
        name="norm_in_proj",
    )(x, g.reshape(1, D_MODEL), wa, wb, wg)


def _shifted(p, prev_blk, next_blk, is_first, is_last):
    tm = p.shape[0]
    row = lax.broadcasted_iota(I32, p.shape, 0)
    prow = jnp.where(is_first, 0.0, prev_blk[7:8, :])
    nrow = jnp.where(is_last, 0.0, next_blk[0:1, :])
    prev = jnp.where(row == 0, prow, pltpu.roll(p, 1, 0))
    nxt = jnp.where(row == tm - 1, nrow, pltpu.roll(p, tm - 1, 0))
    return prev, nxt


def _stack_heads(x, head_mask):
    return jnp.where(head_mask, jnp.concatenate([x] * GROUP_HEADS, axis=0), 0.0)


def _unstack_heads(z):
    out = z[0:CHUNK]
    for h in range(1, GROUP_HEADS):
        out = out + z[h * CHUNK:(h + 1) * CHUNK]
    return out


def _chunks_local(chains):
    ngrp = RWKV_HEADS // GROUP_HEADS
    ti = lax.broadcasted_iota(I32, (CHUNK, CHUNK), 0)
    si = lax.broadcasted_iota(I32, (CHUNK, CHUNK), 1)
    srow = lax.broadcasted_iota(I32, (GSTACK, HALF), 0)
    slane = lax.broadcasted_iota(I32, (GSTACK, HALF), 1)
    head_mask = (srow >> 5) == (slane >> 6)
    mrow = lax.broadcasted_iota(I32, (GSTACK, GSTACK), 0)
    mcol = lax.broadcasted_iota(I32, (GSTACK, GSTACK), 1)
    same = (mrow >> 5) == (mcol >> 5)
    masks = {False: (same & (mrow > mcol), same & (mrow >= mcol)),
             True: (same & (mrow < mcol), same & (mrow <= mcol))}
    tris = {False: (ti >= si).astype(BF16), True: (ti <= si).astype(BF16)}

    cls = [_dot_exact_lhs(tris[rev], lw) for (_, _, _, lw, _, _, rev) in chains]
    st = []
    for (r, v, kk, lw, k, a, rev), cl in zip(chains, cls):
        tot = cl[0:1] if rev else cl[CHUNK - 1:CHUNK]
        e_neg = jnp.exp(-cl)
        e_tail = jnp.exp(tot - cl)
        beta = kk * a
        st.append(dict(rev=rev, v=v, alpha_b=-kk * jnp.exp(cl - lw), r_b=r * jnp.exp(cl), beta_b=beta * e_neg,
                       k_b=k * e_neg, bt=beta * e_tail, kt=k * e_tail, pc=jnp.exp(tot)))

    for s in st:
        s["sa"], s["sv"], s["lhs"], s["rhs"] = [], [], [], []
        for g in range(ngrp):
            lanes = slice(g * HALF, (g + 1) * HALF)
            sa_ = _stack_heads(s["alpha_b"][:, lanes], head_mask).astype(BF16)
            sr_ = _stack_heads(s["r_b"][:, lanes], head_mask).astype(BF16)
            s["sa"].append(sa_)
            s["sv"].append(_stack_heads(s["v"][:, lanes], head_mask).astype(BF16))
            s["lhs"].append(jnp.concatenate([sa_, sr_], axis=0))
            s["rhs"].append(jnp.concatenate([s["beta_b"][:, lanes].astype(BF16)] * GROUP_HEADS
                                            + [s["k_b"][:, lanes].astype(BF16)] * GROUP_HEADS, axis=0))
    pms = [[lax.dot_general(s["lhs"][g], s["rhs"][g], _NT, preferred_element_type=F32) for g in range(ngrp)]
           for s in st]
    zero = jnp.zeros((GSTACK, GSTACK), F32)
    drow = lax.broadcasted_iota(I32, (STACK, STACK), 0)
    dcol = lax.broadcasted_iota(I32, (STACK, STACK), 1)
    eye = jnp.where(drow == dcol, 1.0, 0.0)
    for s, pm in zip(st, pms):
        strict, incl = masks[s["rev"]]
        s["ak"] = [jnp.where(strict, pm[g][:GSTACK, GSTACK:], 0.0).astype(BF16) for g in range(ngrp)]
        s["rb"] = [jnp.where(incl, pm[g][GSTACK:, :GSTACK], 0.0).astype(BF16) for g in range(ngrp)]
        s["rk"] = [jnp.where(incl, pm[g][GSTACK:, GSTACK:], 0.0).astype(BF16) for g in range(ngrp)]
        ab = [jnp.where(strict, pm[g][:GSTACK, :GSTACK], 0.0) for g in range(ngrp)]
        s["apow"] = jnp.concatenate([jnp.concatenate([ab[0], zero], axis=1),
                                     jnp.concatenate([zero, ab[1]], axis=1)], axis=0)
        s["tinv"] = eye + s["apow"]

    for _ in range(int(math.log2(CHUNK)) - 1):
        sq = [_dot(s["apow"], s["apow"]) for s in st]
        for s, x in zip(st, sq):
            s["apow"] = x
        pr = [_dot(s["tinv"], s["apow"]) for s in st]
        for s, x in zip(st, pr):
            s["tinv"] = s["tinv"] + x

    akv = [[_dot(s["ak"][g], s["sv"][g]) for g in range(ngrp)] for s in st]
    rkv = [[_dot(s["rk"][g], s["sv"][g]) for g in range(ngrp)] for s in st]
    tw = [[_dot(s["tinv"][g * GSTACK:(g + 1) * GSTACK, g * GSTACK:(g + 1) * GSTACK],
                jnp.concatenate([s["sa"][g].astype(F32), akv[c][g]], axis=1)) for g in range(ngrp)]
          for c, s in enumerate(st)]
    rbx = [[_dot(s["rb"][g], tw[c][g]) for g in range(ngrp)] for c, s in enumerate(st)]
    outs = []
    for c, s in enumerate(st):
        cat = lambda f: jnp.concatenate([f(g) for g in range(ngrp)], axis=1)
        qt = s["r_b"] + cat(lambda g: _unstack_heads(rbx[c][g][:, :HALF]))
        wt = cat(lambda g: _unstack_heads(tw[c][g][:, :HALF]))
        yloc = cat(lambda g: _unstack_heads(rbx[c][g][:, HALF:] + rkv[c][g]))
        u = cat(lambda g: _unstack_heads(tw[c][g][:, HALF:]))
        outs.append((qt, wt, yloc, u, s["bt"], s["kt"], s["pc"]))
    return outs


def _rwkv_local_body(nt_seq, p_ref, pp_ref, pn_ref, mup_ref, mun_ref, w0_ref, w2_ref, a0_ref, a2_ref, g2_ref,
                     kk_ref, ka_ref, rk_ref, ones_ref,
                     g_out, bonus_out, v_out, qwf, yuf, bkf, pcf, qwr, yur, bkr, pcr,
                     r_s, v_s, kk_s, lw_s, k_s, a_s):
    i = pl.program_id(0) % nt_seq
    p = p_ref[...]
    tm = p.shape[0]
    prev, nxt = _shifted(p, pp_ref[...], pn_ref[...], i == 0, i == nt_seq - 1)
    ps = p + mup_ref[...] * (prev - p) + mun_ref[...] * (nxt - p)
    c = RWKV_WIDTH
    r = ps[:, 0:c]
    k = ps[:, c:2 * c]
    v = ps[:, 2 * c:3 * c]
    lw = ps[:, 3 * c:3 * c + 128]
    la = ps[:, 3 * c + 128:3 * c + 256]
    lg = ps[:, 3 * c + 256:3 * c + 384]
    ones_bd = ones_ref[...]
    w_raw = w0_ref[...] + _dot3(jnp.tanh(lw), w2_ref[...])
    logw = -jnp.exp(-jnp.logaddexp(-w_raw, 0.0) - 0.5)
    a = jax.nn.sigmoid(a0_ref[...] + _dot3(la, a2_ref[...]))
    g_out[...] = _dot3(jax.nn.sigmoid(lg), g2_ref[...])
    kk0 = k * kk_ref[...]
    kk = kk0 * lax.rsqrt(jnp.maximum(_seg_sum(kk0 * kk0, ones_bd), 1e-24))
    ka = ka_ref[...]
    kdir_f = k * (1.0 + (a[:, :c] - 1.0) * ka)
    kdir_r = k * (1.0 + (a[:, c:] - 1.0) * ka)
    bonus_out[...] = _seg_sum(r * ((kdir_f + kdir_r) * 0.5) * rk_ref[...], ones_bd) * v
    v_out[...] = v
    r_s[...] = r
    v_s[...] = v
    kk_s[...] = kk
    lw_s[...] = logw
    k_s[:, :c] = kdir_f
    k_s[:, c:] = kdir_r
    a_s[...] = a

    out_refs = ((qwf, yuf, bkf, pcf), (qwr, yur, bkr, pcr))

    def chunk(cj, carry):
        chains, where = [], []
        for uu in range(LOCAL_CHUNKS_PER_ITER):
            ci = cj * LOCAL_CHUNKS_PER_ITER + uu
            rows = pl.ds(pl.multiple_of(ci * CHUNK, CHUNK), CHUNK)
            rc, vc, kkc = r_s[rows, :], v_s[rows, :], kk_s[rows, :]
            for d in range(2):
                lanes = slice(d * c, (d + 1) * c)
                chains.append((rc, vc, kkc, lw_s[rows, lanes], k_s[rows, lanes], a_s[rows, lanes], d == 1))
                where.append((ci, d))
        for (ci, d), (qt, wt, yloc, u, bt, kt, pcv) in zip(where, _chunks_local(chains)):
            qw, yu, bk, pc = out_refs[d]
            qw[ci] = jnp.concatenate([qt, wt], axis=0).astype(BF16)
            yu[ci] = jnp.concatenate([yloc, u], axis=0)
            bk[ci] = jnp.concatenate([bt, kt], axis=0).astype(BF16)
            pc[ci] = jnp.broadcast_to(pcv, (8, c))
        return carry

    lax.fori_loop(0, tm // (CHUNK * LOCAL_CHUNKS_PER_ITER), chunk, 0)


def _rwkv_local(p_a, nt_seq, tm, prm):
    t = p_a.shape[0]
    nt = t // tm
    nc = tm // CHUNK
    c = RWKV_WIDTH
    rd, ri = RWKV_DECAY_RANK, RWKV_ICLR_RANK
    z = jnp.zeros((rd, c), F32)
    w2pad = jnp.concatenate([jnp.concatenate([prm["rwkv_w2"][0], z], 0), jnp.concatenate([z, prm["rwkv_w2"][1]], 0)], 1)
    z = jnp.zeros((ri, c), F32)
    a2pad = jnp.concatenate([jnp.concatenate([prm["rwkv_a2"][0], z], 0), jnp.concatenate([z, prm["rwkv_a2"][1]], 0)], 1)
    hd = np.arange(c) // RWKV_HEAD_DIM
    ones_bd = jnp.asarray(hd[:, None] == hd[None, :], BF16)
    full = lambda i: (0, 0)
    row = lambda i: (i, 0)
    row3 = lambda i: (i, 0, 0)
    nblk8 = t // 8
    tm8 = tm // 8
    vec = lambda n: pl.BlockSpec((1, n), full)
    chunk_out = lambda rows, dt: (pl.BlockSpec((nc, rows, c), row3), jax.ShapeDtypeStruct((t // CHUNK, rows, c), dt))
    per_dir = [chunk_out(2 * CHUNK, BF16), chunk_out(2 * CHUNK, F32), chunk_out(2 * CHUNK, BF16), chunk_out(8, F32)]
    outs = [(pl.BlockSpec((tm, c), row), jax.ShapeDtypeStruct((t, c), F32))] * 3 + per_dir + per_dir
    return pl.pallas_call(
        functools.partial(_rwkv_local_body, nt_seq),
        grid=(nt,),
        in_specs=[pl.BlockSpec((tm, C_RWKV_IN), row),
                  pl.BlockSpec((8, C_RWKV_IN), lambda i: (jnp.maximum(i * tm8 - 1, 0), 0)),
                  pl.BlockSpec((8, C_RWKV_IN), lambda i: (jnp.minimum((i + 1) * tm8, nblk8 - 1), 0)),
                  vec(C_RWKV_IN), vec(C_RWKV_IN), vec(2 * c), pl.BlockSpec((128, 2 * c), full),
                  vec(2 * c), pl.BlockSpec((128, 2 * c), full), pl.BlockSpec((RWKV_GATE_RANK, c), full),
                  vec(c), vec(c), vec(c), pl.BlockSpec((c, c), full)],
        out_specs=[o[0] for o in outs],
        out_shape=[o[1] for o in outs],
        scratch_shapes=[pltpu.VMEM((tm, c), F32)] * 3 + [pltpu.VMEM((tm, 2 * c), F32)] * 3,
        compiler_params=_cparams("parallel"),
        name="rwkv_local",
    )(p_a, p_a, p_a, prm["rwkv_mu_prev"].reshape(1, -1), prm["rwkv_mu_next"].reshape(1, -1),
      prm["rwkv_w0"].reshape(1, 2 * c), w2pad, prm["rwkv_a0"].reshape(1, 2 * c), a2pad, prm["rwkv_g2"],
      prm["rwkv_k_k"].reshape(1, c), prm["rwkv_k_a"].reshape(1, c), prm["rwkv_r_k"].reshape(1, c), ones_bd)


def _rwkv_scan_body(nc, qwf, yuf, bkf, pcf, vf, qwr, yur, bkr, pcr, vr, yf_ref, yr_ref, s_ref):
    @pl.when(pl.program_id(1) == 0)
    def _():
        s_ref[...] = jnp.zeros(s_ref.shape, F32)

    brow = lax.broadcasted_iota(I32, (HALF, HALF), 0)
    bcol = lax.broadcasted_iota(I32, (HALF, HALF), 1)
    same_head = (brow >> 6) == (bcol >> 6)

    dirs = ((qwf, yuf, bkf, pcf, vf, yf_ref), (qwr, yur, bkr, pcr, vr, yr_ref))

    def step(j, carry):
        ch = []
        for d, (qw_ref, yu_ref, bk_ref, pc_ref, v_ref, y_ref) in enumerate(dirs):
            ci = j if d == 0 else nc - 1 - j
            rows = pl.ds(pl.multiple_of(ci * CHUNK, CHUNK), CHUNK)
            qw, yu, bk, pc, vc = qw_ref[ci], yu_ref[ci], bk_ref[ci], pc_ref[ci][0:1], v_ref[rows, :]
            for hf in range(2):
                lanes = slice(hf * HALF, (hf + 1) * HALF)
                ch.append(dict(d=d, hf=hf, rows=rows, lanes=lanes, y_ref=y_ref, qw=qw[:, lanes], yu=yu[:, lanes],
                               bk=bk[:, lanes], pc=pc[:, lanes], vc=vc[:, lanes]))
        for c in ch:
            c["s"] = s_ref[c["d"], c["hf"]]
        ys = [lax.dot_general(c["qw"], c["s"].astype(BF16), _NT, preferred_element_type=F32) + c["yu"] for c in ch]
        for c, y in zip(ch, ys):
            c["y_ref"][c["rows"], c["lanes"]] = y[:CHUNK]
        sav = [jnp.concatenate([y[CHUNK:], c["vc"]], axis=0).astype(BF16) for c, y in zip(ch, ys)]
        upd = [lax.dot_general(x, c["bk"], _TN, preferred_element_type=F32) for c, x in zip(ch, sav)]
        for c, u in zip(ch, upd):
            s_ref[c["d"], c["hf"]] = c["s"] * c["pc"] + jnp.where(same_head, u, 0.0)
        return carry

    lax.fori_loop(0, nc, step, 0)


def _rwkv_scan(loc, nseq, nt_seq, tm):
    _, _, v, qwf, yuf, bkf, pcf, qwr, yur, bkr, pcr = loc
    t = v.shape[0]
    nc = tm // CHUNK
    c = RWKV_WIDTH
    fwd3 = lambda b, i: (b * nt_seq + i, 0, 0)
    rev3 = lambda b, i: (b * nt_seq + nt_seq - 1 - i, 0, 0)
    fwd2 = lambda b, i: (b * nt_seq + i, 0)
    rev2 = lambda b, i: (b * nt_seq + nt_seq - 1 - i, 0)

    def specs(m3, m2):
        return [pl.BlockSpec((nc, 2 * CHUNK, c), m3), pl.BlockSpec((nc, 2 * CHUNK, c), m3),
                pl.BlockSpec((nc, 2 * CHUNK, c), m3), pl.BlockSpec((nc, 8, c), m3), pl.BlockSpec((tm, c), m2)]

    return pl.pallas_call(
        functools.partial(_rwkv_scan_body, nc),
        grid=(nseq, nt_seq),
        in_specs=specs(fwd3, fwd2) + specs(rev3, rev2),
        out_specs=[pl.BlockSpec((tm, c), fwd2), pl.BlockSpec((tm, c), rev2)],
        out_shape=[jax.ShapeDtypeStruct((t, c), F32)] * 2,
        scratch_shapes=[pltpu.VMEM((2, 2, HALF, HALF), F32)],
        compiler_params=_cparams("arbitrary", "arbitrary"),
        name="rwkv_scan",
    )(qwf, yuf, bkf, pcf, v, qwr, yur, bkr, pcr, v)


def _rwkv_post_body(yf_ref, yr_ref, g_ref, bonus_ref, lng_ref, lnb_ref, ones_ref, o_ref):
    y = yf_ref[...] + yr_ref[...]
    ones_bd = ones_ref[...]
    inv_n = 1.0 / RWKV_HEAD_DIM
    mean = _seg_sum(y, ones_bd) * inv_n
    yc = y - mean
    var = _seg_sum(yc * yc, ones_bd) * inv_n
    yn = yc * lax.rsqrt(var + RWKV_GN_EPS) * lng_ref[...] + lnb_ref[...]
    o_ref[...] = (yn + bonus_ref[...]) * g_ref[...]


def _rwkv_post(yf, yr, g, bonus, prm, tm):
    t, c = yf.shape
    hd = np.arange(c) // RWKV_HEAD_DIM
    ones_bd = jnp.asarray(hd[:, None] == hd[None, :], BF16)
    row = lambda i: (i, 0)
    full = lambda i: (0, 0)
    return pl.pallas_call(
        _rwkv_post_body,
        grid=(t // tm,),
        in_specs=[pl.BlockSpec((tm, c), row)] * 4 + [pl.BlockSpec((1, c), full)] * 2 + [pl.BlockSpec((c, c), full)],
        out_specs=pl.BlockSpec((tm, c), row),
        out_shape=jax.ShapeDtypeStruct((t, c), F32),
        compiler_params=_cparams("parallel"),
        name="rwkv_post",
    )(yf, yr, g, bonus, prm["rwkv_ln_g"].reshape(1, c), prm["rwkv_ln_b"].reshape(1, c), ones_bd)


def _rwkv_branch(p_a, nseq, seq_len, prm):
    tm = min(256, seq_len)
    nt_seq = seq_len // tm
    loc = _rwkv_local(p_a, nt_seq, tm, prm)
    yf, yr = _rwkv_scan(loc, nseq, nt_seq, tm)
    return _rwkv_post(yf, yr, loc[0], loc[1], prm, tm)


FFT_N2 = 128
FILTER_EMB_DIM = 33
FILTER_BANDS = (FILTER_EMB_DIM - 1) // 2
FILTER_HIDDEN = 64
FILTER_TARGET = 1e-2
FILTER_FAST_DECAY_PCT = 0.3
FILTER_SLOW_DECAY_PCT = 1.5


def _hyena_prep_body(nt_seq, p_ref, pp_ref, pn_ref, cw_ref, cb_ref, src_ref, x2_ref):
    i = pl.program_id(0) % nt_seq
    p = p_ref[...]
    prev, nxt = _shifted(p, pp_ref[...], pn_ref[...], i == 0, i == nt_seq - 1)
    cw = cw_ref[...]
    u = prev * cw[0:1] + p * cw[1:2] + nxt * cw[2:3] + cb_ref[...]
    c = HYENA_WIDTH
    src_ref[...] = u[:, 0:c] * u[:, 2 * c:3 * c]
    x2_ref[...] = u[:, c:2 * c]


def _hyena_prep(p_b, nt_seq, tm, prm):
    t = p_b.shape[0]
    c = HYENA_WIDTH
    row = lambda i: (i, 0)
    full = lambda i: (0, 0)
    nblk8 = t // 8
    tm8 = tm // 8
    cw = jnp.concatenate([prm["hyena_conv_w"], jnp.zeros((5, C_HYENA_IN), F32)], axis=0)
    return pl.pallas_call(
        functools.partial(_hyena_prep_body, nt_seq),
        grid=(t // tm,),
        in_specs=[pl.BlockSpec((tm, C_HYENA_IN), row),
                  pl.BlockSpec((8, C_HYENA_IN), lambda i: (jnp.maximum(i * tm8 - 1, 0), 0)),
                  pl.BlockSpec((8, C_HYENA_IN), lambda i: (jnp.minimum((i + 1) * tm8, nblk8 - 1), 0)),
                  pl.BlockSpec((8, C_HYENA_IN), full), pl.BlockSpec((1, C_HYENA_IN), full)],
        out_specs=[pl.BlockSpec((tm, c), row)] * 2,
        out_shape=[jax.ShapeDtypeStruct((t, c), F32)] * 2,
        compiler_params=_cparams("parallel"),
        name="hyena_prep",
    )(p_b, p_b, p_b, cw, prm["hyena_conv_b"].reshape(1, -1))


def _hyena_filter_body(seq_len, w1_ref, b1_ref, w2_ref, b2_ref, w3_ref, freq_ref, delta_ref, o_ref):
    rows = o_ref.shape[0]
    n = pl.program_id(0) * rows + lax.broadcasted_iota(I32, (rows, 128), 0)
    lane = lax.broadcasted_iota(I32, (rows, 128), 1)
    pos = jnp.where(n < seq_len, n, 2 * seq_len - n).astype(F32)
    t = pos * (1.0 / (seq_len - 1))
    omega = (2.0 * math.pi) * pos / seq_len
    band_step = (FILTER_BANDS - 1 - 1e-4) / (FILTER_BANDS - 1)
    band_idx = jnp.where(lane <= FILTER_BANDS, lane - 1, lane - 1 - FILTER_BANDS).astype(F32)
    arg = (1e-4 + band_idx * band_step) * omega
    z = jnp.where(lane == 0, t, jnp.where(lane <= FILTER_BANDS, jnp.cos(arg),
                                          jnp.where(lane <= 2 * FILTER_BANDS, -jnp.sin(arg), 0.0)))
    freq = freq_ref[...]
    hid = jnp.sin(freq * (_dot3(z, w1_ref[...]) + b1_ref[...]))
    hid = jnp.sin(freq * (_dot3(hid, w2_ref[...]) + b2_ref[...]))
    filt = _dot3(hid, w3_ref[...])
    nn = n[:, 0:1]
    tt = t[:, 0:1]
    sel = jnp.where(nn < seq_len, filt[:, :HYENA_WIDTH], jnp.where(nn > seq_len, filt[:, HYENA_WIDTH:], 0.0))
    o_ref[...] = sel * jnp.exp(-tt * delta_ref[...])


def _hyena_filter(seq_len, prm):
    rows = min(1024, 2 * seq_len)
    c = HYENA_WIDTH
    fh = FILTER_HIDDEN
    w1 = jnp.concatenate([prm["hyena_f_w1"], jnp.zeros((128 - FILTER_EMB_DIM, fh), F32)], axis=0)
    min_decay = math.log(FILTER_TARGET) / FILTER_SLOW_DECAY_PCT
    max_decay = math.log(FILTER_TARGET) / FILTER_FAST_DECAY_PCT
    deltas = jnp.abs(jnp.linspace(min_decay, max_decay, c, dtype=F32)).reshape(1, c)
    full = lambda i: (0, 0)
    return pl.pallas_call(
        functools.partial(_hyena_filter_body, seq_len),
        grid=(2 * seq_len // rows,),
        in_specs=[pl.BlockSpec((128, fh), full), pl.BlockSpec((1, fh), full), pl.BlockSpec((fh, fh), full),
                  pl.BlockSpec((1, fh), full), pl.BlockSpec((fh, 2 * c), full), pl.BlockSpec((1, fh), full),
                  pl.BlockSpec((1, c), full)],
        out_specs=pl.BlockSpec((rows, c), lambda i: (i, 0)),
        out_shape=jax.ShapeDtypeStruct((2 * seq_len, c), F32),
        compiler_params=_cparams("parallel"),
        name="hyena_filter",
    )(w1, prm["hyena_f_b1"].reshape(1, fh), prm["hyena_f_w2"], prm["hyena_f_b2"].reshape(1, fh),
      prm["hyena_f_w3"], prm["hyena_f_freq"].reshape(1, fh), deltas)


def _lmul_body(m_ref, x_ref, o_ref):
    o_ref[0] = jnp.dot(m_ref[...], x_ref[0].astype(BF16), preferred_element_type=F32)


def _lmul_epilogue_body(m_ref, x_ref, src_ref, x2_ref, skip_ref, o_ref):
    y = jnp.dot(m_ref[...], x_ref[0].astype(BF16), preferred_element_type=F32)
    o_ref[0] = x2_ref[0] * (y + src_ref[0] * skip_ref[...])


def _lmul(m, x, lt, extra=None):
    nfft, r_in, lanes = x.shape
    r_out = m.shape[0]
    xs = pl.BlockSpec((1, r_in, lt), lambda f, j: (f, 0, j))
    os_ = pl.BlockSpec((1, r_out, lt), lambda f, j: (f, 0, j))
    ms = pl.BlockSpec(m.shape, lambda f, j: (0, 0))
    if extra is None:
        body, ins, specs = _lmul_body, (m, x), [ms, xs]
    else:
        src, x2, skip = extra
        body, ins = _lmul_epilogue_body, (m, x, src, x2, skip)
        specs = [ms, xs, os_, os_, pl.BlockSpec((1, lt), lambda f, j: (0, 0))]
    return pl.pallas_call(
        body,
        grid=(nfft, lanes // lt),
        in_specs=specs,
        out_specs=os_,
        out_shape=jax.ShapeDtypeStruct((nfft, r_out, lanes), F32),
        compiler_params=_cparams("parallel", "parallel"),
        name="fft_outer" if extra is None else "fft_outer_out",
    )(*ins)


def _fft_inner_body(kt, conv, g_ref, y_ref, *rest):
    if conv:
        h_ref, o_ref = rest
    else:
        (o_ref,) = rest
    n2 = FFT_N2
    for q in range(kt):
        g = g_ref[q]
        yk = jnp.concatenate([y_ref[0, 0, q], y_ref[0, 1, q]], axis=0).astype(BF16)
        z = jnp.dot(g, yk, preferred_element_type=F32)
        if conv:
            zr, zi = z[:n2], z[n2:]
            hr, hi = h_ref[0, 0, q], h_ref[0, 1, q]
            pr = zr * hr - zi * hi
            pi = zr * hi + zi * hr
            prod = jnp.concatenate([pr, pi], axis=0).astype(BF16)
            z = lax.dot_general(g, prod, _TN, preferred_element_type=F32)
        o_ref[0, 0, q] = z[:n2]
        o_ref[0, 1, q] = z[n2:]


def _fft_inner(g, y, h, kt):
    nfft, _, n1, n2, c = y.shape
    blk = pl.BlockSpec((1, 2, kt, n2, c), lambda f, j: (f, 0, j, 0, 0))
    gs = pl.BlockSpec((kt, 2 * n2, 2 * n2), lambda f, j: (j, 0, 0))
    conv = h is not None
    ins = (g, y, h) if conv else (g, y)
    specs = [gs, blk, pl.BlockSpec((1, 2, kt, n2, c), lambda f, j: (0, 0, j, 0, 0))] if conv else [gs, blk]
    return pl.pallas_call(
        functools.partial(_fft_inner_body, kt, conv),
        grid=(nfft, n1 // kt),
        in_specs=specs,
        out_specs=blk,
        out_shape=jax.ShapeDtypeStruct(y.shape, F32),
        compiler_params=_cparams("parallel", "parallel"),
        name="fft_inner_conv" if conv else "fft_inner",
    )(*ins)


def _dft_tables(seq_len):
    n = 2 * seq_len
    n2 = FFT_N2
    n1 = n // n2
    k1 = jnp.arange(n1, dtype=I32)
    ang1 = (2.0 * math.pi / n1) * ((k1[:, None] * k1[None, :]) % n1).astype(F32)
    c1, s1 = jnp.cos(ang1), jnp.sin(ang1)
    half = n1 // 2
    f_pair = jnp.concatenate([jnp.concatenate([c1[:, :half], s1[:, :half]], 1),
                              jnp.concatenate([-s1[:, :half], c1[:, :half]], 1)], 0)
    f_real_half = jnp.concatenate([c1[:, :half], -s1[:, :half]], 0)
    f_real_full = jnp.concatenate([c1, -s1], 0)
    ci, si = c1[:half] / n, s1[:half] / n
    b_real = jnp.concatenate([ci, -si], 1)
    b_pair = jnp.concatenate([b_real, jnp.concatenate([si, ci], 1)], 0)
    kk = k1[:, None, None] + n1 * jnp.arange(n2, dtype=I32)[None, :, None]
    ang2 = (2.0 * math.pi / n) * ((kk * jnp.arange(n2, dtype=I32)[None, None, :]) % n).astype(F32)
    c2, s2 = jnp.cos(ang2), jnp.sin(ang2)
    g = jnp.concatenate([jnp.concatenate([c2, s2], 2), jnp.concatenate([-s2, c2], 2)], 1)
    cast = lambda a: a.astype(BF16)
    return dict(f_pair=cast(f_pair), f_real_half=cast(f_real_half), f_real_full=cast(f_real_full),
                b_pair=cast(b_pair), b_real=cast(b_real), g=cast(g), n1=n1)


def _hyena_branch(p_b, nseq, seq_len, prm):
    tm = min(512, seq_len)
    c = HYENA_WIDTH
    src, x2 = _hyena_prep(p_b, seq_len // tm, tm, prm)
    tab = _dft_tables(seq_len)
    n1 = tab["n1"]
    n2 = FFT_N2
    lanes = n2 * c
    lt = min(4096, lanes)
    kt = 4
    filt = _hyena_filter(seq_len, prm)
    hy = _lmul(tab["f_real_full"], filt.reshape(1, n1, lanes), lt)
    hspec = _fft_inner(tab["g"], hy.reshape(1, 2, n1, n2, c), None, kt)
    pair = nseq % 2 == 0
    nfft = nseq // 2 if pair else nseq
    rows = n1 if pair else n1 // 2
    xin = src.reshape(nfft, rows, lanes)
    y = _lmul(tab["f_pair"] if pair else tab["f_real_half"], xin, lt)
    w = _fft_inner(tab["g"], y.reshape(nfft, 2, n1, n2, c), hspec, kt)
    skip = jnp.tile(prm["hyena_skip"].reshape(1, c), (1, lt // c))
    out = _lmul(tab["b_pair"] if pair else tab["b_real"], w.reshape(nfft, 2 * n1, lanes), lt,
                extra=(xin, x2.reshape(nfft, rows, lanes), skip))
    return out.reshape(nseq * seq_len, c)


MOE_GROUPS = 4
MOE_EXPERTS_PER_GROUP = 8
MOE_EXPERTS = MOE_GROUPS * MOE_EXPERTS_PER_GROUP
MOE_HIDDEN = D_MODEL // 2
ROUTE_LANES = 128


def _route(logits):
    lane = lax.broadcasted_iota(I32, logits.shape, 1)
    neg = -jnp.inf
    big = ROUTE_LANES
    gl = jnp.where(lane < MOE_GROUPS, logits, neg)
    gmax = jnp.max(gl, axis=-1, keepdims=True)
    grp = jnp.min(jnp.where(gl == gmax, lane, big), axis=-1, keepdims=True)
    p_grp = 1.0 / jnp.sum(jnp.exp(gl - gmax), axis=-1, keepdims=True)
    lo = MOE_GROUPS + grp * MOE_EXPERTS_PER_GROUP
    el = jnp.where((lane >= lo) & (lane < lo + MOE_EXPERTS_PER_GROUP), logits, neg)
    m1 = jnp.max(el, axis=-1, keepdims=True)
    i1 = jnp.min(jnp.where(el == m1, lane, big), axis=-1, keepdims=True)
    el2 = jnp.where(lane == i1, neg, el)
    m2 = jnp.max(el2, axis=-1, keepdims=True)
    i2 = jnp.min(jnp.where(el2 == m2, lane, big), axis=-1, keepdims=True)
    e2 = jnp.exp(m2 - m1)
    g1 = p_grp / (1.0 + e2)
    g2 = p_grp * e2 / (1.0 + e2)
    gates = jnp.where(lane == i1, g1, jnp.where(lane == i2, g2, 0.0))
    return pltpu.roll(gates, ROUTE_LANES - MOE_GROUPS, 1)


def _merge_body(x_ref, ya_ref, yb_ref, pg_ref, wua_ref, wub_ref, wo_ref, g_ref, wr_ref, br_ref,
                x1_ref, xn_ref, gwt_ref):
    ga = jax.nn.sigmoid(pg_ref[:, :D_MODEL])
    gb = jax.nn.sigmoid(pg_ref[:, D_MODEL:])
    merged = ga * _dot(ya_ref[...], wua_ref[...]) + gb * _dot(yb_ref[...], wub_ref[...])
    x1 = x_ref[...] + _dot(merged, wo_ref[...])
    x1_ref[...] = x1
    xn = x1 * lax.rsqrt(jnp.mean(x1 * x1, axis=-1, keepdims=True) + NORM_EPS) * g_ref[...]
    xn_ref[...] = xn.astype(BF16)
    logits = _dot3(xn, wr_ref[...]) + br_ref[...]
    gwt_ref[...] = _route(logits).T


def _merge(x, ya, yb, pg, prm, wts, tm):
    t = x.shape[0]
    row = lambda i: (i, 0)
    full = lambda i: (0, 0)
    wr = jnp.concatenate([prm["moe_w_route_group"], prm["moe_w_route_expert"],
                          jnp.zeros((D_MODEL, ROUTE_LANES - MOE_GROUPS - MOE_EXPERTS), F32)], axis=1)
    br = jnp.concatenate([prm["moe_b_route_group"], prm["moe_b_route_expert"],
                          jnp.zeros((ROUTE_LANES - MOE_GROUPS - MOE_EXPERTS,), F32)]).reshape(1, ROUTE_LANES)
    return pl.pallas_call(
        _merge_body,
        grid=(t // tm,),
        in_specs=[pl.BlockSpec((tm, D_MODEL), row), pl.BlockSpec((tm, RWKV_WIDTH), row),
                  pl.BlockSpec((tm, HYENA_WIDTH), row), pl.BlockSpec((tm, C_GATES), row),
                  pl.BlockSpec((RWKV_WIDTH, D_MODEL), full), pl.BlockSpec((HYENA_WIDTH, D_MODEL), full),
                  pl.BlockSpec((D_MODEL, D_MODEL), full), pl.BlockSpec((1, D_MODEL), full),
                  pl.BlockSpec((D_MODEL, ROUTE_LANES), full), pl.BlockSpec((1, ROUTE_LANES), full)],
        out_specs=[pl.BlockSpec((tm, D_MODEL), row), pl.BlockSpec((tm, D_MODEL), row),
                   pl.BlockSpec((ROUTE_LANES, tm), lambda i: (0, i))],
        out_shape=[jax.ShapeDtypeStruct((t, D_MODEL), F32), jax.ShapeDtypeStruct((t, D_MODEL), BF16),
                   jax.ShapeDtypeStruct((ROUTE_LANES, t), F32)],
        compiler_params=_cparams("parallel"),
        name="merge_route",
    )(x, ya, yb, pg, wts["w_up_a"], wts["w_up_b"], wts["w_out"], prm["norm_ffn_g"].reshape(1, D_MODEL), wr, br)


MOE_PAD = 16
MOE_PASS = 64
MOE_BLK = 256
MOE_STEP_EXPERTS = 2


def _moe_slots(tt):
    ns = 2 * tt + MOE_EXPERTS * MOE_PAD
    return (ns + MOE_BLK - 1) // MOE_BLK * MOE_BLK


def _moe_body(final_norm, cnt_ref, start_ref, xn_ref, gwt_ref, x1_ref, tri_ref, pcol_ref, wg_ref, wu_ref, wd_ref,
              gf_ref, o_ref, xg_ref, og_ref, drow_ref, dcol_ref):
    i = pl.program_id(0)
    e = pl.program_id(1)
    tt = xn_ref.shape[0]
    ns = _moe_slots(tt)

    @pl.when(e == 0)
    def _():
        gw = gwt_ref[0:MOE_EXPERTS, :]
        sel = gw > 0.0
        rank = jnp.dot(sel.astype(BF16), tri_ref[...], preferred_element_type=F32)
        dest = pcol_ref[:, 0:1] + rank
        d_lo = jnp.min(jnp.where(sel, dest, 1e9), axis=0, keepdims=True)
        d_hi = jnp.max(jnp.where(sel, dest, -1.0), axis=0, keepdims=True)
        g_lo = jnp.sum(jnp.where(sel & (dest == d_lo), gw, 0.0), axis=0, keepdims=True)
        g_hi = jnp.sum(jnp.where(sel & (dest == d_hi), gw, 0.0), axis=0, keepdims=True)
        single = d_hi == d_lo
        g_hi = jnp.where(single, 0.0, g_hi)
        d_hi = jnp.where(single, -1.0, d_hi)
        rows8 = jnp.concatenate([d_lo, d_hi, g_lo, g_hi, jnp.zeros((4, tt), F32)], axis=0)
        drow_ref[...] = rows8
        dcol_ref[...] = jnp.concatenate([rows8, jnp.zeros((ROUTE_LANES - 8, tt), F32)], axis=0).T
        og_ref[...] = jnp.zeros(og_ref.shape, BF16)
        xg_ref[ns:, :] = jnp.zeros((MOE_PASS, D_MODEL), BF16)
        for blk in range(ns // MOE_BLK):
            r = (blk * MOE_BLK + lax.broadcasted_iota(I32, (MOE_BLK, tt), 0)).astype(F32)
            onehot = ((r == d_lo) | (r == d_hi)).astype(BF16)
            xg_ref[blk * MOE_BLK:(blk + 1) * MOE_BLK, :] = jnp.dot(
                onehot, xn_ref[...], preferred_element_type=F32).astype(BF16)

    d_lo, d_hi = drow_ref[0:1, :], drow_ref[1:2, :]
    g_lo, g_hi = drow_ref[2:3, :], drow_ref[3:4, :]
    counts = [cnt_ref[i * MOE_EXPERTS + e * MOE_STEP_EXPERTS + q] for q in range(MOE_STEP_EXPERTS)]
    starts = [start_ref[i * MOE_EXPERTS + e * MOE_STEP_EXPERTS + q] for q in range(MOE_STEP_EXPERTS)]
    limits = [s + (n + MOE_PAD - 1) // MOE_PAD * MOE_PAD for s, n in zip(starts, counts)]
    npass = [(n + MOE_PASS - 1) // MOE_PASS for n in counts]

    def pass_(c, carry):
        r0s = [pl.multiple_of(jnp.minimum(s + c * MOE_PASS, ns), MOE_PAD) for s in starts]
        xgs = [xg_ref[pl.ds(r0, MOE_PASS), :] for r0 in r0s]
        hgs = [jnp.dot(x, wg_ref[q], preferred_element_type=F32) for q, x in enumerate(xgs)]
        hus = [jnp.dot(x, wu_ref[q], preferred_element_type=F32) for q, x in enumerate(xgs)]
        hids = [((hg * jax.nn.sigmoid(hg)) * hu).astype(BF16) for hg, hu in zip(hgs, hus)]
        outs = [jnp.dot(h, wd_ref[q], preferred_element_type=F32) for q, h in enumerate(hids)]
        for r0, lim, out in zip(r0s, limits, outs):
            ri = r0 + lax.broadcasted_iota(I32, (MOE_PASS, tt), 0)
            r = ri.astype(F32)
            row_gate = jnp.sum(jnp.where(r == d_lo, g_lo, 0.0) + jnp.where(r == d_hi, g_hi, 0.0),
                               axis=-1, keepdims=True)
            rows = pl.ds(r0, MOE_PASS)
            own = ri[:, 0:1] < lim
            og_ref[rows, :] = jnp.where(own, (out * row_gate).astype(BF16), og_ref[rows, :])
        return carry

    lax.fori_loop(0, functools.reduce(jnp.maximum, npass), pass_, 0)

    @pl.when(e == MOE_EXPERTS // MOE_STEP_EXPERTS - 1)
    def _():
        y = x1_ref[...]
        c_lo, c_hi = dcol_ref[:, 0:1], dcol_ref[:, 1:2]
        for blk in range(ns // MOE_BLK):
            r = (blk * MOE_BLK + lax.broadcasted_iota(I32, (tt, MOE_BLK), 1)).astype(F32)
            onehot = ((r == c_lo) | (r == c_hi)).astype(BF16)
            y = y + jnp.dot(onehot, og_ref[blk * MOE_BLK:(blk + 1) * MOE_BLK, :], preferred_element_type=F32)
        if final_norm:
            y = y * lax.rsqrt(jnp.mean(y * y, axis=-1, keepdims=True) + NORM_EPS) * gf_ref[...]
        o_ref[...] = y


def _moe_final(xn, gwt, x1, wts, norm_final_g, tt):
    t = xn.shape[0]
    final_norm = norm_final_g is not None
    if not final_norm:
        norm_final_g = jnp.ones((D_MODEL,), F32)
    nt = t // tt
    counts = jnp.sum((gwt[:MOE_EXPERTS] > 0.0).reshape(MOE_EXPERTS, nt, tt), axis=-1, dtype=I32).T
    padded = (counts + MOE_PAD - 1) // MOE_PAD * MOE_PAD
    starts = jnp.cumsum(padded, axis=1) - padded
    pcol = jnp.broadcast_to(starts.astype(F32).reshape(nt * MOE_EXPERTS, 1), (nt * MOE_EXPERTS, ROUTE_LANES))
    idx = np.arange(tt)
    tri = jnp.asarray(idx[:, None] < idx[None, :], BF16)
    nsb = _moe_slots(tt) + MOE_PASS
    grid_spec = pltpu.PrefetchScalarGridSpec(
        num_scalar_prefetch=2,
        grid=(nt, MOE_EXPERTS // MOE_STEP_EXPERTS),
        in_specs=[pl.BlockSpec((tt, D_MODEL), lambda i, e, c, s: (i, 0)),
                  pl.BlockSpec((ROUTE_LANES, tt), lambda i, e, c, s: (0, i)),
                  pl.BlockSpec((tt, D_MODEL), lambda i, e, c, s: (i, 0)),
                  pl.BlockSpec((tt, tt), lambda i, e, c, s: (0, 0)),
                  pl.BlockSpec((MOE_EXPERTS, ROUTE_LANES), lambda i, e, c, s: (i, 0)),
                  pl.BlockSpec((MOE_STEP_EXPERTS, D_MODEL, MOE_HIDDEN), lambda i, e, c, s: (e, 0, 0)),
                  pl.BlockSpec((MOE_STEP_EXPERTS, D_MODEL, MOE_HIDDEN), lambda i, e, c, s: (e, 0, 0)),
                  pl.BlockSpec((MOE_STEP_EXPERTS, MOE_HIDDEN, D_MODEL), lambda i, e, c, s: (e, 0, 0)),
                  pl.BlockSpec((1, D_MODEL), lambda i, e, c, s: (0, 0))],
        out_specs=pl.BlockSpec((tt, D_MODEL), lambda i, e, c, s: (i, 0)),
        scratch_shapes=[pltpu.VMEM((nsb, D_MODEL), BF16), pltpu.VMEM((nsb, D_MODEL), BF16),
                        pltpu.VMEM((8, tt), F32), pltpu.VMEM((tt, ROUTE_LANES), F32)],
    )
    return pl.pallas_call(
        functools.partial(_moe_body, final_norm),
        grid_spec=grid_spec,
        out_shape=jax.ShapeDtypeStruct((t, D_MODEL), F32),
        compiler_params=pltpu.CompilerParams(dimension_semantics=("parallel", "arbitrary"),
                                             vmem_limit_bytes=MOE_VMEM_LIMIT),
        name="moe_final",
    )(counts.reshape(-1), starts.reshape(-1), xn, gwt, x1, tri, pcol, wts["moe_w_gate"], wts["moe_w_up"],
      wts["moe_w_down"], norm_final_g.reshape(1, D_MODEL))


def _trunk(x, prm, wts, norm_final_g):
    nseq, seq_len, _ = x.shape
    xf = x.reshape(nseq * seq_len, D_MODEL)
    p_a, p_b, p_g = _norm_in_proj(xf, prm["norm_mix_g"], wts["w_in"], min(256, seq_len))
    ya = _rwkv_branch(p_a, nseq, seq_len, prm)
    yb = _hyena_branch(p_b, nseq, seq_len, prm)
    x1, xn, gwt = _merge(xf, ya, yb, p_g, prm, wts, min(512, seq_len))
    out = _moe_final(xn, gwt, x1, wts, norm_final_g, min(1024, seq_len))
    return out.reshape(nseq, seq_len, D_MODEL)


def kernel(x_prompt, x_sample, norm_mix_g, w_in, rwkv_mu_prev, rwkv_mu_next, rwkv_w0, rwkv_w2, rwkv_a0, rwkv_a2, rwkv_g2, rwkv_k_k, rwkv_k_a, rwkv_r_k, rwkv_ln_g, rwkv_ln_b, hyena_conv_w, hyena_conv_b, hyena_f_w1, hyena_f_b1, hyena_f_w2, hyena_f_b2, hyena_f_w3, hyena_f_freq, hyena_skip, w_up_a, w_up_b, w_out, norm_ffn_g, moe_w_route_group, moe_b_route_group, moe_w_route_expert, moe_b_route_expert, moe_w_gate, moe_w_up, moe_w_down, norm_final_g):
    layer = dict(norm_mix_g=norm_mix_g, w_in=w_in, rwkv_mu_prev=rwkv_mu_prev, rwkv_mu_next=rwkv_mu_next,
                 rwkv_w0=rwkv_w0, rwkv_w2=rwkv_w2, rwkv_a0=rwkv_a0, rwkv_a2=rwkv_a2, rwkv_g2=rwkv_g2,
                 rwkv_k_k=rwkv_k_k, rwkv_k_a=rwkv_k_a, rwkv_r_k=rwkv_r_k, rwkv_ln_g=rwkv_ln_g, rwkv_ln_b=rwkv_ln_b,
                 hyena_conv_w=hyena_conv_w, hyena_conv_b=hyena_conv_b, hyena_f_w1=hyena_f_w1, hyena_f_b1=hyena_f_b1,
                 hyena_f_w2=hyena_f_w2, hyena_f_b2=hyena_f_b2, hyena_f_w3=hyena_f_w3, hyena_f_freq=hyena_f_freq,
                 hyena_skip=hyena_skip, w_up_a=w_up_a, w_up_b=w_up_b, w_out=w_out, norm_ffn_g=norm_ffn_g,
                 moe_w_route_group=moe_w_route_group, moe_b_route_group=moe_b_route_group,
                 moe_w_route_expert=moe_w_route_expert, moe_b_route_expert=moe_b_route_expert,
                 moe_w_gate=moe_w_gate, moe_w_up=moe_w_up, moe_w_down=moe_w_down)
    depth = norm_mix_g.shape[0]
    big = ("w_in", "w_up_a", "w_up_b", "w_out", "moe_w_gate", "moe_w_up", "moe_w_down")

    def trunk(x):
        for li in range(depth):
            prm = {k: v[li] for k, v in layer.items()}
            wts = {k: prm[k].astype(BF16) for k in big}
            last = li == depth - 1
            x = _trunk(x, prm, wts, norm_final_g if last else None)
        return x

    return (trunk(x_prompt), trunk(x_sample))
```

```python
import functools
import math

import jax
import jax.numpy as jnp
import numpy as np
from jax import lax
from jax.experimental import pallas as pl
from jax.experimental.pallas import tpu as pltpu

F32 = jnp.float32
BF16 = jnp.bfloat16
I32 = jnp.int32

D_MODEL = 1024
NORM_EPS = 1e-6
RWKV_HEADS = 8
RWKV_HEAD_DIM = 64
RWKV_WIDTH = RWKV_HEADS * RWKV_HEAD_DIM
RWKV_DECAY_RANK = 64
RWKV_ICLR_RANK = 64
RWKV_GATE_RANK = 128
RWKV_GN_EPS = 64e-5
HYENA_WIDTH = D_MODEL // 2
C_RWKV_IN = 3 * RWKV_WIDTH + 2 * RWKV_DECAY_RANK + 2 * RWKV_ICLR_RANK + RWKV_GATE_RANK
C_HYENA_IN = 3 * HYENA_WIDTH
C_GATES = 2 * D_MODEL

CHUNK = 32
STACK = RWKV_HEADS * CHUNK
HALF = RWKV_WIDTH // 2
GROUP_HEADS = 4
GSTACK = GROUP_HEADS * CHUNK
LOCAL_CHUNKS_PER_ITER = 4

VMEM_LIMIT = 48 * 1024 * 1024
MOE_VMEM_LIMIT = 56 * 1024 * 1024

_NN = (((1,), (0,)), ((), ()))
_NT = (((1,), (1,)), ((), ()))
_TN = (((0,), (0,)), ((), ()))


def _dot(a, b, dims=_NN):
    return lax.dot_general(a.astype(BF16), b.astype(BF16), dims, preferred_element_type=F32)


def _split2(x):
    hi = x.astype(BF16)
    lo = (x - hi.astype(F32)).astype(BF16)
    return hi, lo


def _dot3(a, b, dims=_NN):
    ah, al = _split2(a)
    bh, bl = _split2(b)
    dg = functools.partial(lax.dot_general, dimension_numbers=dims, preferred_element_type=F32)
    return dg(ah, bh) + (dg(ah, bl) + dg(al, bh))


def _dot_exact_lhs(a_bf16, x):
    x1 = x.astype(BF16)
    r1 = x - x1.astype(F32)
    x2 = r1.astype(BF16)
    x3 = (r1 - x2.astype(F32)).astype(BF16)
    dg = functools.partial(lax.dot_general, dimension_numbers=_NN, preferred_element_type=F32)
    return dg(a_bf16, x1) + (dg(a_bf16, x2) + dg(a_bf16, x3))


def _seg_sum(x, ones_bd):
    x1 = x.astype(BF16)
    r1 = x - x1.astype(F32)
    x2 = r1.astype(BF16)
    x3 = (r1 - x2.astype(F32)).astype(BF16)
    dg = functools.partial(lax.dot_general, dimension_numbers=_NN, preferred_element_type=F32)
    return dg(x1, ones_bd) + (dg(x2, ones_bd) + dg(x3, ones_bd))


def _cparams(*sem):
    return pltpu.CompilerParams(dimension_semantics=tuple(sem), vmem_limit_bytes=VMEM_LIMIT)


def _norm_in_proj_body(x_ref, g_ref, wa_ref, wb_ref, wg_ref, pa_ref, pb_ref, pg_ref):
    x = x_ref[...]
    xn = x * lax.rsqrt(jnp.mean(x * x, axis=-1, keepdims=True) + NORM_EPS) * g_ref[...]
    xb = xn.astype(BF16)
    pa_ref[...] = jnp.dot(xb, wa_ref[...], preferred_element_type=F32)
    pb_ref[...] = jnp.dot(xb, wb_ref[...], preferred_element_type=F32)
    pg_ref[...] = jnp.dot(xb, wg_ref[...], preferred_element_type=F32)


def _norm_in_proj(x, g, w_in, tm):
    t = x.shape[0]
    wa = w_in[:, :C_RWKV_IN].astype(BF16)
    wb = w_in[:, C_RWKV_IN:C_RWKV_IN + C_HYENA_IN].astype(BF16)
    wg = w_in[:, C_RWKV_IN + C_HYENA_IN:].astype(BF16)
    full = lambda i: (0, 0)
    row = lambda i: (i, 0)
    return pl.pallas_call(
        _norm_in_proj_body,
        grid=(t // tm,),
        in_specs=[pl.BlockSpec((tm, D_MODEL), row), pl.BlockSpec((1, D_MODEL), full),
                  pl.BlockSpec(wa.shape, full), pl.BlockSpec(wb.shape, full), pl.BlockSpec(wg.shape, full)],
        out_specs=[pl.BlockSpec((tm, C_RWKV_IN), row), pl.BlockSpec((tm, C_HYENA_IN), row),
                   pl.BlockSpec((tm, C_GATES), row)],
        out_shape=[jax.ShapeDtypeStruct((t, C_RWKV_IN), F32), jax.ShapeDtypeStruct((t, C_HYENA_IN), F32),
                   jax.ShapeDtypeStruct((t, C_GATES), F32)],
        compiler_params=_cparams("parallel"),
        name="norm_in_proj",
    )(x, g.reshape(1, D_MODEL), wa, wb, wg)


def _shifted(p, prev_blk, next_blk, is_first, is_last):
    tm = p.shape[0]
    row = lax.broadcasted_iota(I32, p.shape, 0)
    prow = jnp.where(is_first, 0.0, prev_blk[7:8, :])
    nrow = jnp.where(is_last, 0.0, next_blk[0:1, :])
    prev = jnp.where(row == 0, prow, pltpu.roll(p, 1, 0))
    nxt = jnp.where(row == tm - 1, nrow, pltpu.roll(p, tm - 1, 0))
    return prev, nxt


def _stack_heads(x, head_mask):
    return jnp.where(head_mask, jnp.concatenate([x] * GROUP_HEADS, axis=0), 0.0)


def _unstack_heads(z):
    out = z[0:CHUNK]
    for h in range(1, GROUP_HEADS):
        out = out + z[h * CHUNK:(h + 1) * CHUNK]
    return out


def _chunks_local(chains):
    ngrp = RWKV_HEADS // GROUP_HEADS
    ti = lax.broadcasted_iota(I32, (CHUNK, CHUNK), 0)
    si = lax.broadcasted_iota(I32, (CHUNK, CHUNK), 1)
    srow = lax.broadcasted_iota(I32, (GSTACK, HALF), 0)
    slane = lax.broadcasted_iota(I32, (GSTACK, HALF), 1)
    head_mask = (srow >> 5) == (slane >> 6)
    mrow = lax.broadcasted_iota(I32, (GSTACK, GSTACK), 0)
    mcol = lax.broadcasted_iota(I32, (GSTACK, GSTACK), 1)
    same = (mrow >> 5) == (mcol >> 5)
    masks = {False: (same & (mrow > mcol), same & (mrow >= mcol)),
             True: (same & (mrow < mcol), same & (mrow <= mcol))}
    tris = {False: (ti >= si).astype(BF16), True: (ti <= si).astype(BF16)}

    cls = [_dot_exact_lhs(tris[rev], lw) for (_, _, _, lw, _, _, rev) in chains]
    st = []
    for (r, v, kk, lw, k, a, rev), cl in zip(chains, cls):
        tot = cl[0:1] if rev else cl[CHUNK - 1:CHUNK]
        e_neg = jnp.exp(-cl)
        e_tail = jnp.exp(tot - cl)
        beta = kk * a
        st.append(dict(rev=rev, v=v, alpha_b=-kk * jnp.exp(cl - lw), r_b=r * jnp.exp(cl), beta_b=beta * e_neg,
                       k_b=k * e_neg, bt=beta * e_tail, kt=k * e_tail, pc=jnp.exp(tot)))

    for s in st:
        s["sa"], s["sv"], s["lhs"], s["rhs"] = [], [], [], []
        for g in range(ngrp):
            lanes = slice(g * HALF, (g + 1) * HALF)
            sa_ = _stack_heads(s["alpha_b"][:, lanes], head_mask).astype(BF16)
            sr_ = _stack_heads(s["r_b"][:, lanes], head_mask).astype(BF16)
            s["sa"].append(sa_)
            s["sv"].append(_stack_heads(s["v"][:, lanes], head_mask).astype(BF16))
            s["lhs"].append(jnp.concatenate([sa_, sr_], axis=0))
            s["rhs"].append(jnp.concatenate([s["beta_b"][:, lanes].astype(BF16)] * GROUP_HEADS
                                            + [s["k_b"][:, lanes].astype(BF16)] * GROUP_HEADS, axis=0))
    pms = [[lax.dot_general(s["lhs"][g], s["rhs"][g], _NT, preferred_element_type=F32) for g in range(ngrp)]
           for s in st]
    zero = jnp.zeros((GSTACK, GSTACK), F32)
    drow = lax.broadcasted_iota(I32, (STACK, STACK), 0)
    dcol = lax.broadcasted_iota(I32, (STACK, STACK), 1)
    eye = jnp.where(drow == dcol, 1.0, 0.0)
    for s, pm in zip(st, pms):
        strict, incl = masks[s["rev"]]
        s["ak"] = [jnp.where(strict, pm[g][:GSTACK, GSTACK:], 0.0).astype(BF16) for g in range(ngrp)]
        s["rb"] = [jnp.where(incl, pm[g][GSTACK:, :GSTACK], 0.0).astype(BF16) for g in range(ngrp)]
        s["rk"] = [jnp.where(incl, pm[g][GSTACK:, GSTACK:], 0.0).astype(BF16) for g in range(ngrp)]
        ab = [jnp.where(strict, pm[g][:GSTACK, :GSTACK], 0.0) for g in range(ngrp)]
        s["apow"] = jnp.concatenate([jnp.concatenate([ab[0], zero], axis=1),
                                     jnp.concatenate([zero, ab[1]], axis=1)], axis=0)
        s["tinv"] = eye + s["apow"]

    for _ in range(int(math.log2(CHUNK)) - 1):
        sq = [_dot(s["apow"], s["apow"]) for s in st]
        for s, x in zip(st, sq):
            s["apow"] = x
        pr = [_dot(s["tinv"], s["apow"]) for s in st]
        for s, x in zip(st, pr):
            s["tinv"] = s["tinv"] + x

    akv = [[_dot(s["ak"][g], s["sv"][g]) for g in range(ngrp)] for s in st]
    rkv = [[_dot(s["rk"][g], s["sv"][g]) for g in range(ngrp)] for s in st]
    tw = [[_dot(s["tinv"][g * GSTACK:(g + 1) * GSTACK, g * GSTACK:(g + 1) * GSTACK],
                jnp.concatenate([s["sa"][g].astype(F32), akv[c][g]], axis=1)) for g in range(ngrp)]
          for c, s in enumerate(st)]
    rbx = [[_dot(s["rb"][g], tw[c][g]) for g in range(ngrp)] for c, s in enumerate(st)]
    outs = []
    for c, s in enumerate(st):
        cat = lambda f: jnp.concatenate([f(g) for g in range(ngrp)], axis=1)
        qt = s["r_b"] + cat(lambda g: _unstack_heads(rbx[c][g][:, :HALF]))
        wt = cat(lambda g: _unstack_heads(tw[c][g][:, :HALF]))
        yloc = cat(lambda g: _unstack_heads(rbx[c][g][:, HALF:] + rkv[c][g]))
        u = cat(lambda g: _unstack_heads(tw[c][g][:, HALF:]))
        outs.append((qt, wt, yloc, u, s["bt"], s["kt"], s["pc"]))
    return outs


def _rwkv_local_body(nt_seq, p_ref, pp_ref, pn_ref, mup_ref, mun_ref, w0_ref, w2_ref, a0_ref, a2_ref, g2_ref,
                     kk_ref, ka_ref, rk_ref, ones_ref,
                     g_out, bonus_out, v_out, qwf, yuf, bkf, pcf, qwr, yur, bkr, pcr,
                     r_s, v_s, kk_s, lw_s, k_s, a_s):
    i = pl.program_id(0) % nt_seq
    p = p_ref[...]
    tm = p.shape[0]
    prev, nxt = _shifted(p, pp_ref[...], pn_ref[...], i == 0, i == nt_seq - 1)
    ps = p + mup_ref[...] * (prev - p) + mun_ref[...] * (nxt - p)
    c = RWKV_WIDTH
    r = ps[:, 0:c]
    k = ps[:, c:2 * c]
    v = ps[:, 2 * c:3 * c]
    lw = ps[:, 3 * c:3 * c + 128]
    la = ps[:, 3 * c + 128:3 * c + 256]
    lg = ps[:, 3 * c + 256:3 * c + 384]
    ones_bd = ones_ref[...]
    w_raw = w0_ref[...] + _dot3(jnp.tanh(lw), w2_ref[...])
    logw = -jnp.exp(-jnp.logaddexp(-w_raw, 0.0) - 0.5)
    a = jax.nn.sigmoid(a0_ref[...] + _dot3(la, a2_ref[...]))
    g_out[...] = _dot3(jax.nn.sigmoid(lg), g2_ref[...])
    kk0 = k * kk_ref[...]
    kk = kk0 * lax.rsqrt(jnp.maximum(_seg_sum(kk0 * kk0, ones_bd), 1e-24))
    ka = ka_ref[...]
    kdir_f = k * (1.0 + (a[:, :c] - 1.0) * ka)
    kdir_r = k * (1.0 + (a[:, c:] - 1.0) * ka)
    bonus_out[...] = _seg_sum(r * ((kdir_f + kdir_r) * 0.5) * rk_ref[...], ones_bd) * v
    v_out[...] = v
    r_s[...] = r
    v_s[...] = v
    kk_s[...] = kk
    lw_s[...] = logw
    k_s[:, :c] = kdir_f
    k_s[:, c:] = kdir_r
    a_s[...] = a

    out_refs = ((qwf, yuf, bkf, pcf), (qwr, yur, bkr, pcr))

    def chunk(cj, carry):
        chains, where = [], []
        for uu in range(LOCAL_CHUNKS_PER_ITER):
            ci = cj * LOCAL_CHUNKS_PER_ITER + uu
            rows = pl.ds(pl.multiple_of(ci * CHUNK, CHUNK), CHUNK)
            rc, vc, kkc = r_s[rows, :], v_s[rows, :], kk_s[rows, :]
            for d in range(2):
                lanes = slice(d * c, (d + 1) * c)
                chains.append((rc, vc, kkc, lw_s[rows, lanes], k_s[rows, lanes], a_s[rows, lanes], d == 1))
                where.append((ci, d))
        for (ci, d), (qt, wt, yloc, u, bt, kt, pcv) in zip(where, _chunks_local(chains)):
            qw, yu, bk, pc = out_refs[d]
            qw[ci] = jnp.concatenate([qt, wt], axis=0).astype(BF16)
            yu[ci] = jnp.concatenate([yloc, u], axis=0)
            bk[ci] = jnp.concatenate([bt, kt], axis=0).astype(BF16)
            pc[ci] = jnp.broadcast_to(pcv, (8, c))
        return carry

    lax.fori_loop(0, tm // (CHUNK * LOCAL_CHUNKS_PER_ITER), chunk, 0)


def _rwkv_local(p_a, nt_seq, tm, prm):
    t = p_a.shape[0]
    nt = t // tm
    nc = tm // CHUNK
    c = RWKV_WIDTH
    rd, ri = RWKV_DECAY_RANK, RWKV_ICLR_RANK
    z = jnp.zeros((rd, c), F32)
    w2pad = jnp.concatenate([jnp.concatenate([prm["rwkv_w2"][0], z], 0), jnp.concatenate([z, prm["rwkv_w2"][1]], 0)], 1)
    z = jnp.zeros((ri, c), F32)
    a2pad = jnp.concatenate([jnp.concatenate([prm["rwkv_a2"][0], z], 0), jnp.concatenate([z, prm["rwkv_a2"][1]], 0)], 1)
    hd = np.arange(c) // RWKV_HEAD_DIM
    ones_bd = jnp.asarray(hd[:, None] == hd[None, :], BF16)
    full = lambda i: (0, 0)
    row = lambda i: (i, 0)
    row3 = lambda i: (i, 0, 0)
    nblk8 = t // 8
    tm8 = tm // 8
    vec = lambda n: pl.BlockSpec((1, n), full)
    chunk_out = lambda rows, dt: (pl.BlockSpec((nc, rows, c), row3), jax.ShapeDtypeStruct((t // CHUNK, rows, c), dt))
    per_dir = [chunk_out(2 * CHUNK, BF16), chunk_out(2 * CHUNK, F32), chunk_out(2 * CHUNK, BF16), chunk_out(8, F32)]
    outs = [(pl.BlockSpec((tm, c), row), jax.ShapeDtypeStruct((t, c), F32))] * 3 + per_dir + per_dir
    return pl.pallas_call(
        functools.partial(_rwkv_local_body, nt_seq),
        grid=(nt,),
        in_specs=[pl.BlockSpec((tm, C_RWKV_IN), row),
                  pl.BlockSpec((8, C_RWKV_IN), lambda i: (jnp.maximum(i * tm8 - 1, 0), 0)),
                  pl.BlockSpec((8, C_RWKV_IN), lambda i: (jnp.minimum((i + 1) * tm8, nblk8 - 1), 0)),
                  vec(C_RWKV_IN), vec(C_RWKV_IN), vec(2 * c), pl.BlockSpec((128, 2 * c), full),
                  vec(2 * c), pl.BlockSpec((128, 2 * c), full), pl.BlockSpec((RWKV_GATE_RANK, c), full),
                  vec(c), vec(c), vec(c), pl.BlockSpec((c, c), full)],
        out_specs=[o[0] for o in outs],
        out_shape=[o[1] for o in outs],
        scratch_shapes=[pltpu.VMEM((tm, c), F32)] * 3 + [pltpu.VMEM((tm, 2 * c), F32)] * 3,
        compiler_params=_cparams("parallel"),
        name="rwkv_local",
    )(p_a, p_a, p_a, prm["rwkv_mu_prev"].reshape(1, -1), prm["rwkv_mu_next"].reshape(1, -1),
      prm["rwkv_w0"].reshape(1, 2 * c), w2pad, prm["rwkv_a0"].reshape(1, 2 * c), a2pad, prm["rwkv_g2"],
      prm["rwkv_k_k"].reshape(1, c), prm["rwkv_k_a"].reshape(1, c), prm["rwkv_r_k"].reshape(1, c), ones_bd)


def _rwkv_scan_body(nc, qwf, yuf, bkf, pcf, vf, qwr, yur, bkr, pcr, vr, yf_ref, yr_ref, s_ref):
    @pl.when(pl.program_id(1) == 0)
    def _():
        s_ref[...] = jnp.zeros(s_ref.shape, F32)

    brow = lax.broadcasted_iota(I32, (HALF, HALF), 0)
    bcol = lax.broadcasted_iota(I32, (HALF, HALF), 1)
    same_head = (brow >> 6) == (bcol >> 6)

    dirs = ((qwf, yuf, bkf, pcf, vf, yf_ref), (qwr, yur, bkr, pcr, vr, yr_ref))

    def step(j, carry):
        ch = []
        for d, (qw_ref, yu_ref, bk_ref, pc_ref, v_ref, y_ref) in enumerate(dirs):
            ci = j if d == 0 else nc - 1 - j
            rows = pl.ds(pl.multiple_of(ci * CHUNK, CHUNK), CHUNK)
            qw, yu, bk, pc, vc = qw_ref[ci], yu_ref[ci], bk_ref[ci], pc_ref[ci][0:1], v_ref[rows, :]
            for hf in range(2):
                lanes = slice(hf * HALF, (hf + 1) * HALF)
                ch.append(dict(d=d, hf=hf, rows=rows, lanes=lanes, y_ref=y_ref, qw=qw[:, lanes], yu=yu[:, lanes],
                               bk=bk[:, lanes], pc=pc[:, lanes], vc=vc[:, lanes]))
        for c in ch:
            c["s"] = s_ref[c["d"], c["hf"]]
        ys = [lax.dot_general(c["qw"], c["s"].astype(BF16), _NT, preferred_element_type=F32) + c["yu"] for c in ch]
        for c, y in zip(ch, ys):
            c["y_ref"][c["rows"], c["lanes"]] = y[:CHUNK]
        sav = [jnp.concatenate([y[CHUNK:], c["vc"]], axis=0).astype(BF16) for c, y in zip(ch, ys)]
        upd = [lax.dot_general(x, c["bk"], _TN, preferred_element_type=F32) for c, x in zip(ch, sav)]
        for c, u in zip(ch, upd):
            s_ref[c["d"], c["hf"]] = c["s"] * c["pc"] + jnp.where(same_head, u, 0.0)
        return carry

    lax.fori_loop(0, nc, step, 0)


def _rwkv_scan(loc, nseq, nt_seq, tm):
    _, _, v, qwf, yuf, bkf, pcf, qwr, yur, bkr, pcr = loc
    t = v.shape[0]
    nc = tm // CHUNK
    c = RWKV_WIDTH
    fwd3 = lambda b, i: (b * nt_seq + i, 0, 0)
    rev3 = lambda b, i: (b * nt_seq + nt_seq - 1 - i, 0, 0)
    fwd2 = lambda b, i: (b * nt_seq + i, 0)
    rev2 = lambda b, i: (b * nt_seq + nt_seq - 1 - i, 0)

    def specs(m3, m2):
        return [pl.BlockSpec((nc, 2 * CHUNK, c), m3), pl.BlockSpec((nc, 2 * CHUNK, c), m3),
                pl.BlockSpec((nc, 2 * CHUNK, c), m3), pl.BlockSpec((nc, 8, c), m3), pl.BlockSpec((tm, c), m2)]

    return pl.pallas_call(
        functools.partial(_rwkv_scan_body, nc),
        grid=(nseq, nt_seq),
        in_specs=specs(fwd3, fwd2) + specs(rev3, rev2),
        out_specs=[pl.BlockSpec((tm, c), fwd2), pl.BlockSpec((tm, c), rev2)],
        out_shape=[jax.ShapeDtypeStruct((t, c), F32)] * 2,
        scratch_shapes=[pltpu.VMEM((2, 2, HALF, HALF), F32)],
        compiler_params=_cparams("arbitrary", "arbitrary"),
        name="rwkv_scan",
    )(qwf, yuf, bkf, pcf, v, qwr, yur, bkr, pcr, v)


def _rwkv_out(yf, yr, g, bonus, ln_g, ln_b, ones_bd):
    y = yf + yr
    inv_n = 1.0 / RWKV_HEAD_DIM
    mean = _seg_sum(y, ones_bd) * inv_n
    yc = y - mean
    var = _seg_sum(yc * yc, ones_bd) * inv_n
    yn = yc * lax.rsqrt(var + RWKV_GN_EPS) * ln_g + ln_b
    return (yn + bonus) * g


def _rwkv_branch(p_a, nseq, seq_len, prm):
    tm = min(256, seq_len)
    nt_seq = seq_len // tm
    loc = _rwkv_local(p_a, nt_seq, tm, prm)
    yf, yr = _rwkv_scan(loc, nseq, nt_seq, tm)
    return yf, yr, loc[0], loc[1]


FFT_N2 = 128
FILTER_EMB_DIM = 33
FILTER_BANDS = (FILTER_EMB_DIM - 1) // 2
FILTER_HIDDEN = 64
FILTER_TARGET = 1e-2
FILTER_FAST_DECAY_PCT = 0.3
FILTER_SLOW_DECAY_PCT = 1.5


def _hyena_prep_body(nt_seq, p_ref, pp_ref, pn_ref, cw_ref, cb_ref, src_ref, x2_ref):
    i = pl.program_id(0) % nt_seq
    p = p_ref[...]
    prev, nxt = _shifted(p, pp_ref[...], pn_ref[...], i == 0, i == nt_seq - 1)
    cw = cw_ref[...]
    u = prev * cw[0:1] + p * cw[1:2] + nxt * cw[2:3] + cb_ref[...]
    c = HYENA_WIDTH
    src_ref[...] = u[:, 0:c] * u[:, 2 * c:3 * c]
    x2_ref[...] = u[:, c:2 * c]


def _hyena_prep(p_b, nt_seq, tm, prm):
    t = p_b.shape[0]
    c = HYENA_WIDTH
    row = lambda i: (i, 0)
    full = lambda i: (0, 0)
    nblk8 = t // 8
    tm8 = tm // 8
    cw = jnp.concatenate([prm["hyena_conv_w"], jnp.zeros((5, C_HYENA_IN), F32)], axis=0)
    return pl.pallas_call(
        functools.partial(_hyena_prep_body, nt_seq),
        grid=(t // tm,),
        in_specs=[pl.BlockSpec((tm, C_HYENA_IN), row),
                  pl.BlockSpec((8, C_HYENA_IN), lambda i: (jnp.maximum(i * tm8 - 1, 0), 0)),
                  pl.BlockSpec((8, C_HYENA_IN), lambda i: (jnp.minimum((i + 1) * tm8, nblk8 - 1), 0)),
                  pl.BlockSpec((8, C_HYENA_IN), full), pl.BlockSpec((1, C_HYENA_IN), full)],
        out_specs=[pl.BlockSpec((tm, c), row)] * 2,
        out_shape=[jax.ShapeDtypeStruct((t, c), F32)] * 2,
        compiler_params=_cparams("parallel"),
        name="hyena_prep",
    )(p_b, p_b, p_b, cw, prm["hyena_conv_b"].reshape(1, -1))


def _hyena_filter_body(seq_len, w1_ref, b1_ref, w2_ref, b2_ref, w3_ref, freq_ref, delta_ref, o_ref):
    rows = o_ref.shape[0]
    n = pl.program_id(0) * rows + lax.broadcasted_iota(I32, (rows, 128), 0)
    lane = lax.broadcasted_iota(I32, (rows, 128), 1)
    pos = jnp.where(n < seq_len, n, 2 * seq_len - n).astype(F32)
    t = pos * (1.0 / (seq_len - 1))
    omega = (2.0 * math.pi) * pos / seq_len
    band_step = (FILTER_BANDS - 1 - 1e-4) / (FILTER_BANDS - 1)
    band_idx = jnp.where(lane <= FILTER_BANDS, lane - 1, lane - 1 - FILTER_BANDS).astype(F32)
    arg = (1e-4 + band_idx * band_step) * omega
    z = jnp.where(lane == 0, t, jnp.where(lane <= FILTER_BANDS, jnp.cos(arg),
                                          jnp.where(lane <= 2 * FILTER_BANDS, -jnp.sin(arg), 0.0)))
    freq = freq_ref[...]
    hid = jnp.sin(freq * (_dot3(z, w1_ref[...]) + b1_ref[...]))
    hid = jnp.sin(freq * (_dot3(hid, w2_ref[...]) + b2_ref[...]))
    filt = _dot3(hid, w3_ref[...])
    nn = n[:, 0:1]
    tt = t[:, 0:1]
    sel = jnp.where(nn < seq_len, filt[:, :HYENA_WIDTH], jnp.where(nn > seq_len, filt[:, HYENA_WIDTH:], 0.0))
    o_ref[...] = sel * jnp.exp(-tt * delta_ref[...])


def _hyena_filter(seq_len, prm):
    rows = min(1024, 2 * seq_len)
    c = HYENA_WIDTH
    fh = FILTER_HIDDEN
    w1 = jnp.concatenate([prm["hyena_f_w1"], jnp.zeros((128 - FILTER_EMB_DIM, fh), F32)], axis=0)
    min_decay = math.log(FILTER_TARGET) / FILTER_SLOW_DECAY_PCT
    max_decay = math.log(FILTER_TARGET) / FILTER_FAST_DECAY_PCT
    deltas = jnp.abs(jnp.linspace(min_decay, max_decay, c, dtype=F32)).reshape(1, c)
    full = lambda i: (0, 0)
    return pl.pallas_call(
        functools.partial(_hyena_filter_body, seq_len),
        grid=(2 * seq_len // rows,),
        in_specs=[pl.BlockSpec((128, fh), full), pl.BlockSpec((1, fh), full), pl.BlockSpec((fh, fh), full),
                  pl.BlockSpec((1, fh), full), pl.BlockSpec((fh, 2 * c), full), pl.BlockSpec((1, fh), full),
                  pl.BlockSpec((1, c), full)],
        out_specs=pl.BlockSpec((rows, c), lambda i: (i, 0)),
        out_shape=jax.ShapeDtypeStruct((2 * seq_len, c), F32),
        compiler_params=_cparams("parallel"),
        name="hyena_filter",
    )(w1, prm["hyena_f_b1"].reshape(1, fh), prm["hyena_f_w2"], prm["hyena_f_b2"].reshape(1, fh),
      prm["hyena_f_w3"], prm["hyena_f_freq"].reshape(1, fh), deltas)


FFT_N2_TILE = 16
FFT_LANE_TILE = 256


def _lmul_rows(m_ref, x_ref):
    m = m_ref[...]
    xt = pltpu.einshape("rjc->jrc", x_ref[0].astype(F32))
    ys = [jnp.dot(m, xt[j].astype(BF16), preferred_element_type=F32) for j in range(FFT_N2_TILE)]
    return pltpu.einshape("jrc->rjc", jnp.stack(ys, axis=0))


def _lmul_body(m_ref, x_ref, o_ref):
    o_ref[0] = _lmul_rows(m_ref, x_ref).astype(o_ref.dtype)


def _lmul_epilogue_body(m_ref, x_ref, src_ref, x2_ref, skip_ref, o_ref):
    o_ref[0] = x2_ref[0] * (_lmul_rows(m_ref, x_ref) + src_ref[0] * skip_ref[...])


def _lmul(m, x, extra=None):
    nfft, r_in, n2, c = x.shape
    r_out = m.shape[0]
    ct = FFT_LANE_TILE
    xs = pl.BlockSpec((1, r_in, FFT_N2_TILE, ct), lambda f, j, l: (f, 0, j, l))
    os_ = pl.BlockSpec((1, r_out, FFT_N2_TILE, ct), lambda f, j, l: (f, 0, j, l))
    ms = pl.BlockSpec(m.shape, lambda f, j, l: (0, 0))
    if extra is None:
        body, ins, specs, out_dtype = _lmul_body, (m, x), [ms, xs], BF16
    else:
        src, x2, skip = extra
        body, ins, out_dtype = _lmul_epilogue_body, (m, x, src, x2, skip), F32
        specs = [ms, xs, os_, os_, pl.BlockSpec((1, ct), lambda f, j, l: (0, l))]
    return pl.pallas_call(
        body,
        grid=(nfft, n2 // FFT_N2_TILE, c // ct),
        in_specs=specs,
        out_specs=os_,
        out_shape=jax.ShapeDtypeStruct((nfft, r_out, n2, c), out_dtype),
        compiler_params=_cparams("parallel", "parallel", "parallel"),
        name="fft_outer" if extra is None else "fft_outer_out",
    )(*ins)


def _fft_inner_body(kt, conv, g_ref, y_ref, *rest):
    if conv:
        h_ref, o_ref = rest
    else:
        (o_ref,) = rest
    n2 = FFT_N2
    for q in range(kt):
        g = g_ref[q]
        yk = jnp.concatenate([y_ref[0, 0, q], y_ref[0, 1, q]], axis=0).astype(BF16)
        z = jnp.dot(g, yk, preferred_element_type=F32)
        if conv:
            zr, zi = z[:n2], z[n2:]
            hr, hi = h_ref[0, 0, q], h_ref[0, 1, q]
            pr = zr * hr - zi * hi
            pi = zr * hi + zi * hr
            prod = jnp.concatenate([pr, pi], axis=0).astype(BF16)
            z = lax.dot_general(g, prod, _TN, preferred_element_type=F32)
        o_ref[0, 0, q] = z[:n2].astype(o_ref.dtype)
        o_ref[0, 1, q] = z[n2:].astype(o_ref.dtype)


def _fft_inner(g, y, h, kt):
    nfft, _, n1, n2, c = y.shape
    blk = pl.BlockSpec((1, 2, kt, n2, c), lambda f, j: (f, 0, j, 0, 0))
    gs = pl.BlockSpec((kt, 2 * n2, 2 * n2), lambda f, j: (j, 0, 0))
    conv = h is not None
    ins = (g, y, h) if conv else (g, y)
    specs = [gs, blk, pl.BlockSpec((1, 2, kt, n2, c), lambda f, j: (0, 0, j, 0, 0))] if conv else [gs, blk]
    return pl.pallas_call(
        functools.partial(_fft_inner_body, kt, conv),
        grid=(nfft, n1 // kt),
        in_specs=specs,
        out_specs=blk,
        out_shape=jax.ShapeDtypeStruct(y.shape, BF16 if conv else F32),
        compiler_params=_cparams("parallel", "parallel"),
        name="fft_inner_conv" if conv else "fft_inner",
    )(*ins)


def _dft_tables(seq_len):
    n = 2 * seq_len
    n2 = FFT_N2
    n1 = n // n2
    k1 = jnp.arange(n1, dtype=I32)
    ang1 = (2.0 * math.pi / n1) * ((k1[:, None] * k1[None, :]) % n1).astype(F32)
    c1, s1 = jnp.cos(ang1), jnp.sin(ang1)
    half = n1 // 2
    f_pair = jnp.concatenate([jnp.concatenate([c1[:, :half], s1[:, :half]], 1),
                              jnp.concatenate([-s1[:, :half], c1[:, :half]], 1)], 0)
    f_real_half = jnp.concatenate([c1[:, :half], -s1[:, :half]], 0)
    f_real_full = jnp.concatenate([c1, -s1], 0)
    ci, si = c1[:half] / n, s1[:half] / n
    b_real = jnp.concatenate([ci, -si], 1)
    b_pair = jnp.concatenate([b_real, jnp.concatenate([si, ci], 1)], 0)
    kk = k1[:, None, None] + n1 * jnp.arange(n2, dtype=I32)[None, :, None]
    ang2 = (2.0 * math.pi / n) * ((kk * jnp.arange(n2, dtype=I32)[None, None, :]) % n).astype(F32)
    c2, s2 = jnp.cos(ang2), jnp.sin(ang2)
    g = jnp.concatenate([jnp.concatenate([c2, s2], 2), jnp.concatenate([-s2, c2], 2)], 1)
    cast = lambda a: a.astype(BF16)
    return dict(f_pair=cast(f_pair), f_real_half=cast(f_real_half), f_real_full=cast(f_real_full),
                b_pair=cast(b_pair), b_real=cast(b_real), g=cast(g), n1=n1)


def _hyena_branch(p_b, nseq, seq_len, prm):
    tm = min(512, seq_len)
    c = HYENA_WIDTH
    src, x2 = _hyena_prep(p_b, seq_len // tm, tm, prm)
    tab = _dft_tables(seq_len)
    n1 = tab["n1"]
    n2 = FFT_N2
    kt = 4
    filt = _hyena_filter(seq_len, prm)
    hy = _lmul(tab["f_real_full"], filt.reshape(1, n1, n2, c))
    hspec = _fft_inner(tab["g"], hy.reshape(1, 2, n1, n2, c), None, kt)
    pair = nseq % 2 == 0
    nfft = nseq // 2 if pair else nseq
    rows = n1 if pair else n1 // 2
    xin = src.reshape(nfft, rows, n2, c)
    y = _lmul(tab["f_pair"] if pair else tab["f_real_half"], xin)
    w = _fft_inner(tab["g"], y.reshape(nfft, 2, n1, n2, c), hspec, kt)
    out = _lmul(tab["b_pair"] if pair else tab["b_real"], w.reshape(nfft, 2 * n1, n2, c),
                extra=(xin, x2.reshape(nfft, rows, n2, c), prm["hyena_skip"].reshape(1, c)))
    return out.reshape(nseq * seq_len, c)


MOE_GROUPS = 4
MOE_EXPERTS_PER_GROUP = 8
MOE_EXPERTS = MOE_GROUPS * MOE_EXPERTS_PER_GROUP
MOE_HIDDEN = D_MODEL // 2
ROUTE_LANES = 128


def _route(logits):
    lane = lax.broadcasted_iota(I32, logits.shape, 1)
    neg = -jnp.inf
    big = ROUTE_LANES
    gl = jnp.where(lane < MOE_GROUPS, logits, neg)
    gmax = jnp.max(gl, axis=-1, keepdims=True)
    grp = jnp.min(jnp.where(gl == gmax, lane, big), axis=-1, keepdims=True)
    p_grp = 1.0 / jnp.sum(jnp.exp(gl - gmax), axis=-1, keepdims=True)
    lo = MOE_GROUPS + grp * MOE_EXPERTS_PER_GROUP
    el = jnp.where((lane >= lo) & (lane < lo + MOE_EXPERTS_PER_GROUP), logits, neg)
    m1 = jnp.max(el, axis=-1, keepdims=True)
    i1 = jnp.min(jnp.where(el == m1, lane, big), axis=-1, keepdims=True)
    el2 = jnp.where(lane == i1, neg, el)
    m2 = jnp.max(el2, axis=-1, keepdims=True)
    i2 = jnp.min(jnp.where(el2 == m2, lane, big), axis=-1, keepdims=True)
    e2 = jnp.exp(m2 - m1)
    g1 = p_grp / (1.0 + e2)
    g2 = p_grp * e2 / (1.0 + e2)
    gates = jnp.where(lane == i1, g1, jnp.where(lane == i2, g2, 0.0))
    return pltpu.roll(gates, ROUTE_LANES - MOE_GROUPS, 1)


def _merge_body(x_ref, yf_ref, yr_ref, rg_ref, bonus_ref, lng_ref, lnb_ref, ones_ref, yb_ref, pg_ref,
                wua_ref, wub_ref, wo_ref, g_ref, wr_ref, br_ref, x1_ref, xn_ref, gwt_ref):
    ya = _rwkv_out(yf_ref[...], yr_ref[...], rg_ref[...], bonus_ref[...], lng_ref[...], lnb_ref[...], ones_ref[...])
    ga = jax.nn.sigmoid(pg_ref[:, :D_MODEL])
    gb = jax.nn.sigmoid(pg_ref[:, D_MODEL:])
    merged = ga * _dot(ya, wua_ref[...]) + gb * _dot(yb_ref[...], wub_ref[...])
    x1 = x_ref[...] + _dot(merged, wo_ref[...])
    x1_ref[...] = x1
    xn = x1 * lax.rsqrt(jnp.mean(x1 * x1, axis=-1, keepdims=True) + NORM_EPS) * g_ref[...]
    xn_ref[...] = xn.astype(BF16)
    logits = _dot3(xn, wr_ref[...]) + br_ref[...]
    gwt_ref[...] = _route(logits).T


def _merge(x, rwkv, yb, pg, prm, wts, tm):
    t = x.shape[0]
    c = RWKV_WIDTH
    row = lambda i: (i, 0)
    full = lambda i: (0, 0)
    hd = np.arange(c) // RWKV_HEAD_DIM
    ones_bd = jnp.asarray(hd[:, None] == hd[None, :], BF16)
    wr = jnp.concatenate([prm["moe_w_route_group"], prm["moe_w_route_expert"],
                          jnp.zeros((D_MODEL, ROUTE_LANES - MOE_GROUPS - MOE_EXPERTS), F32)], axis=1)
    br = jnp.concatenate([prm["moe_b_route_group"], prm["moe_b_route_expert"],
                          jnp.zeros((ROUTE_LANES - MOE_GROUPS - MOE_EXPERTS,), F32)]).reshape(1, ROUTE_LANES)
    return pl.pallas_call(
        _merge_body,
        grid=(t // tm,),
        in_specs=[pl.BlockSpec((tm, D_MODEL), row)] + [pl.BlockSpec((tm, c), row)] * 4
                 + [pl.BlockSpec((1, c), full)] * 2 + [pl.BlockSpec((c, c), full),
                  pl.BlockSpec((tm, HYENA_WIDTH), row), pl.BlockSpec((tm, C_GATES), row),
                  pl.BlockSpec((RWKV_WIDTH, D_MODEL), full), pl.BlockSpec((HYENA_WIDTH, D_MODEL), full),
                  pl.BlockSpec((D_MODEL, D_MODEL), full), pl.BlockSpec((1, D_MODEL), full),
                  pl.BlockSpec((D_MODEL, ROUTE_LANES), full), pl.BlockSpec((1, ROUTE_LANES), full)],
        out_specs=[pl.BlockSpec((tm, D_MODEL), row), pl.BlockSpec((tm, D_MODEL), row),
                   pl.BlockSpec((ROUTE_LANES, tm), lambda i: (0, i))],
        out_shape=[jax.ShapeDtypeStruct((t, D_MODEL), F32), jax.ShapeDtypeStruct((t, D_MODEL), BF16),
                   jax.ShapeDtypeStruct((ROUTE_LANES, t), F32)],
        compiler_params=_cparams("parallel"),
        name="merge_route",
    )(x, *rwkv, prm["rwkv_ln_g"].reshape(1, c), prm["rwkv_ln_b"].reshape(1, c), ones_bd, yb, pg,
      wts["w_up_a"], wts["w_up_b"], wts["w_out"], prm["norm_ffn_g"].reshape(1, D_MODEL), wr, br)


MOE_PAD = 16
MOE_PASS = 64
MOE_BLK = 256
MOE_STEP_EXPERTS = 2


def _moe_slots(tt):
    ns = 2 * tt + MOE_EXPERTS * MOE_PAD
    return (ns + MOE_BLK - 1) // MOE_BLK * MOE_BLK


def _moe_body(final_norm, cnt_ref, start_ref, xn_ref, gwt_ref, x1_ref, tri_ref, pcol_ref, wg_ref, wu_ref, wd_ref,
              gf_ref, o_ref, xg_ref, og_ref, drow_ref, dcol_ref):
    i = pl.program_id(0)
    e = pl.program_id(1)
    tt = xn_ref.shape[0]
    ns = _moe_slots(tt)

    @pl.when(e == 0)
    def _():
        gw = gwt_ref[0:MOE_EXPERTS, :]
        sel = gw > 0.0
        rank = jnp.dot(sel.astype(BF16), tri_ref[...], preferred_element_type=F32)
        dest = pcol_ref[:, 0:1] + rank
        d_lo = jnp.min(jnp.where(sel, dest, 1e9), axis=0, keepdims=True)
        d_hi = jnp.max(jnp.where(sel, dest, -1.0), axis=0, keepdims=True)
        g_lo = jnp.sum(jnp.where(sel & (dest == d_lo), gw, 0.0), axis=0, keepdims=True)
        g_hi = jnp.sum(jnp.where(sel & (dest == d_hi), gw, 0.0), axis=0, keepdims=True)
        single = d_hi == d_lo
        g_hi = jnp.where(single, 0.0, g_hi)
        d_hi = jnp.where(single, -1.0, d_hi)
        rows8 = jnp.concatenate([d_lo, d_hi, g_lo, g_hi, jnp.zeros((4, tt), F32)], axis=0)
        drow_ref[...] = rows8
        dcol_ref[...] = jnp.concatenate([rows8, jnp.zeros((ROUTE_LANES - 8, tt), F32)], axis=0).T
        og_ref[...] = jnp.zeros(og_ref.shape, BF16)
        xg_ref[ns:, :] = jnp.zeros((MOE_PASS, D_MODEL), BF16)
        for blk in range(ns // MOE_BLK):
            r = (blk * MOE_BLK + lax.broadcasted_iota(I32, (MOE_BLK, tt), 0)).astype(F32)
            onehot = ((r == d_lo) | (r == d_hi)).astype(BF16)
            xg_ref[blk * MOE_BLK:(blk + 1) * MOE_BLK, :] = jnp.dot(
                onehot, xn_ref[...], preferred_element_type=F32).astype(BF16)

    d_lo, d_hi = drow_ref[0:1, :], drow_ref[1:2, :]
    g_lo, g_hi = drow_ref[2:3, :], drow_ref[3:4, :]
    counts = [cnt_ref[i * MOE_EXPERTS + e * MOE_STEP_EXPERTS + q] for q in range(MOE_STEP_EXPERTS)]
    starts = [start_ref[i * MOE_EXPERTS + e * MOE_STEP_EXPERTS + q] for q in range(MOE_STEP_EXPERTS)]
    limits = [s + (n + MOE_PAD - 1) // MOE_PAD * MOE_PAD for s, n in zip(starts, counts)]
    npass = [(n + MOE_PASS - 1) // MOE_PASS for n in counts]

    def pass_(c, carry):
        r0s = [pl.multiple_of(jnp.minimum(s + c * MOE_PASS, ns), MOE_PAD) for s in starts]
        xgs = [xg_ref[pl.ds(r0, MOE_PASS), :] for r0 in r0s]
        hgs = [jnp.dot(x, wg_ref[q], preferred_element_type=F32) for q, x in enumerate(xgs)]
        hus = [jnp.dot(x, wu_ref[q], preferred_element_type=F32) for q, x in enumerate(xgs)]
        hids = [((hg * jax.nn.sigmoid(hg)) * hu).astype(BF16) for hg, hu in zip(hgs, hus)]
        outs = [jnp.dot(h, wd_ref[q], preferred_element_type=F32) for q, h in enumerate(hids)]
        for r0, lim, out in zip(r0s, limits, outs):
            ri = r0 + lax.broadcasted_iota(I32, (MOE_PASS, tt), 0)
            r = ri.astype(F32)
            row_gate = jnp.sum(jnp.where(r == d_lo, g_lo, 0.0) + jnp.where(r == d_hi, g_hi, 0.0),
                               axis=-1, keepdims=True)
            rows = pl.ds(r0, MOE_PASS)
            own = ri[:, 0:1] < lim
            og_ref[rows, :] = jnp.where(own, (out * row_gate).astype(BF16), og_ref[rows, :])
        return carry

    lax.fori_loop(0, functools.reduce(jnp.maximum, npass), pass_, 0)

    @pl.when(e == MOE_EXPERTS // MOE_STEP_EXPERTS - 1)
    def _():
        y = x1_ref[...]
        c_lo, c_hi = dcol_ref[:, 0:1], dcol_ref[:, 1:2]
        for blk in range(ns // MOE_BLK):
            r = (blk * MOE_BLK + lax.broadcasted_iota(I32, (tt, MOE_BLK), 1)).astype(F32)
            onehot = ((r == c_lo) | (r == c_hi)).astype(BF16)
            y = y + jnp.dot(onehot, og_ref[blk * MOE_BLK:(blk + 1) * MOE_BLK, :], preferred_element_type=F32)
        if final_norm:
            y = y * lax.rsqrt(jnp.mean(y * y, axis=-1, keepdims=True) + NORM_EPS) * gf_ref[...]
        o_ref[...] = y


def _moe_final(xn, gwt, x1, wts, norm_final_g, tt):
    t = xn.shape[0]
    final_norm = norm_final_g is not None
    if not final_norm:
        norm_final_g = jnp.ones((D_MODEL,), F32)
    nt = t // tt
    counts = jnp.sum((gwt[:MOE_EXPERTS] > 0.0).reshape(MOE_EXPERTS, nt, tt), axis=-1, dtype=I32).T
    padded = (counts + MOE_PAD - 1) // MOE_PAD * MOE_PAD
    starts = jnp.cumsum(padded, axis=1) - padded
    pcol = jnp.broadcast_to(starts.astype(F32).reshape(nt * MOE_EXPERTS, 1), (nt * MOE_EXPERTS, ROUTE_LANES))
    idx = np.arange(tt)
    tri = jnp.asarray(idx[:, None] < idx[None, :], BF16)
    nsb = _moe_slots(tt) + MOE_PASS
    grid_spec = pltpu.PrefetchScalarGridSpec(
        num_scalar_prefetch=2,
        grid=(nt, MOE_EXPERTS // MOE_STEP_EXPERTS),
        in_specs=[pl.BlockSpec((tt, D_MODEL), lambda i, e, c, s: (i, 0)),
                  pl.BlockSpec((ROUTE_LANES, tt), lambda i, e, c, s: (0, i)),
                  pl.BlockSpec((tt, D_MODEL), lambda i, e, c, s: (i, 0)),
                  pl.BlockSpec((tt, tt), lambda i, e, c, s: (0, 0)),
                  pl.BlockSpec((MOE_EXPERTS, ROUTE_LANES), lambda i, e, c, s: (i, 0)),
                  pl.BlockSpec((MOE_STEP_EXPERTS, D_MODEL, MOE_HIDDEN), lambda i, e, c, s: (e, 0, 0)),
                  pl.BlockSpec((MOE_STEP_EXPERTS, D_MODEL, MOE_HIDDEN), lambda i, e, c, s: (e, 0, 0)),
                  pl.BlockSpec((MOE_STEP_EXPERTS, MOE_HIDDEN, D_MODEL), lambda i, e, c, s: (e, 0, 0)),
                  pl.BlockSpec((1, D_MODEL), lambda i, e, c, s: (0, 0))],
        out_specs=pl.BlockSpec((tt, D_MODEL), lambda i, e, c, s: (i, 0)),
        scratch_shapes=[pltpu.VMEM((nsb, D_MODEL), BF16), pltpu.VMEM((nsb, D_MODEL), BF16),
                        pltpu.VMEM((8, tt), F32), pltpu.VMEM((tt, ROUTE_LANES), F32)],
    )
    return pl.pallas_call(
        functools.partial(_moe_body, final_norm),
        grid_spec=grid_spec,
        out_shape=jax.ShapeDtypeStruct((t, D_MODEL), F32),
        compiler_params=pltpu.CompilerParams(dimension_semantics=("parallel", "arbitrary"),
                                             vmem_limit_bytes=MOE_VMEM_LIMIT),
        name="moe_final",
    )(counts.reshape(-1), starts.reshape(-1), xn, gwt, x1, tri, pcol, wts["moe_w_gate"], wts["moe_w_up"],
      wts["moe_w_down"], norm_final_g.reshape(1, D_MODEL))


def _trunk(x, prm, wts, norm_final_g):
    nseq, seq_len, _ = x.shape
    xf = x.reshape(nseq * seq_len, D_MODEL)
    p_a, p_b, p_g = _norm_in_proj(xf, prm["norm_mix_g"], wts["w_in"], min(256, seq_len))
    rwkv = _rwkv_branch(p_a, nseq, seq_len, prm)
    yb = _hyena_branch(p_b, nseq, seq_len, prm)
    x1, xn, gwt = _merge(xf, rwkv, yb, p_g, prm, wts, min(512, seq_len))
    out = _moe_final(xn, gwt, x1, wts, norm_final_g, min(1024, seq_len))
    return out.reshape(nseq, seq_len, D_MODEL)


def kernel(x_prompt, x_sample, norm_mix_g, w_in, rwkv_mu_prev, rwkv_mu_next, rwkv_w0, rwkv_w2, rwkv_a0, rwkv_a2, rwkv_g2, rwkv_k_k, rwkv_k_a, rwkv_r_k, rwkv_ln_g, rwkv_ln_b, hyena_conv_w, hyena_conv_b, hyena_f_w1, hyena_f_b1, hyena_f_w2, hyena_f_b2, hyena_f_w3, hyena_f_freq, hyena_skip, w_up_a, w_up_b, w_out, norm_ffn_g, moe_w_route_group, moe_b_route_group, moe_w_route_expert, moe_b_route_expert, moe_w_gate, moe_w_up, moe_w_down, norm_final_g):
    layer = dict(norm_mix_g=norm_mix_g, w_in=w_in, rwkv_mu_prev=rwkv_mu_prev, rwkv_mu_next=rwkv_mu_next,
                 rwkv_w0=rwkv_w0, rwkv_w2=rwkv_w2, rwkv_a0=rwkv_a0, rwkv_a2=rwkv_a2, rwkv_g2=rwkv_g2,
                 rwkv_k_k=rwkv_k_k, rwkv_k_a=rwkv_k_a, rwkv_r_k=rwkv_r_k, rwkv_ln_g=rwkv_ln_g, rwkv_ln_b=rwkv_ln_b,
                 hyena_conv_w=hyena_conv_w, hyena_conv_b=hyena_conv_b, hyena_f_w1=hyena_f_w1, hyena_f_b1=hyena_f_b1,
                 hyena_f_w2=hyena_f_w2, hyena_f_b2=hyena_f_b2, hyena_f_w3=hyena_f_w3, hyena_f_freq=hyena_f_freq,
                 hyena_skip=hyena_skip, w_up_a=w_up_a, w_up_b=w_up_b, w_out=w_out, norm_ffn_g=norm_ffn_g,
                 moe_w_route_group=moe_w_route_group, moe_b_route_group=moe_b_route_group,
                 moe_w_route_expert=moe_w_route_expert, moe_b_route_expert=moe_b_route_expert,
                 moe_w_gate=moe_w_gate, moe_w_up=moe_w_up, moe_w_down=moe_w_down)
    depth = norm_mix_g.shape[0]
    big = ("w_in", "w_up_a", "w_up_b", "w_out", "moe_w_gate", "moe_w_up", "moe_w_down")

    def trunk(x):
        for li in range(depth):
            prm = {k: v[li] for k, v in layer.items()}
            wts = {k: prm[k].astype(BF16) for k in big}
            last = li == depth - 1
            x = _trunk(x, prm, wts, norm_final_g if last else None)
        return x

    return (trunk(x_prompt), trunk(x_sample))
```

```python
import functools
import math

import jax
import jax.numpy as jnp
import numpy as np
from jax import lax
from jax.experimental import pallas as pl
from jax.experimental.pallas import tpu as pltpu

F32 = jnp.float32
BF16 = jnp.bfloat16
I32 = jnp.int32

D_MODEL = 1024
NORM_EPS = 1e-6
RWKV_HEADS = 8
RWKV_HEAD_DIM = 64
RWKV_WIDTH = RWKV_HEADS * RWKV_HEAD_DIM
RWKV_DECAY_RANK = 64
RWKV_ICLR_RANK = 64
RWKV_GATE_RANK = 128
RWKV_GN_EPS = 64e-5
HYENA_WIDTH = D_MODEL // 2
C_RWKV_IN = 3 * RWKV_WIDTH + 2 * RWKV_DECAY_RANK + 2 * RWKV_ICLR_RANK + RWKV_GATE_RANK
C_HYENA_IN = 3 * HYENA_WIDTH
C_GATES = 2 * D_MODEL

CHUNK = 32
STACK = RWKV_HEADS * CHUNK
HALF = RWKV_WIDTH // 2
GROUP_HEADS = 4
GSTACK = GROUP_HEADS * CHUNK
LOCAL_CHUNKS_PER_ITER = 4

VMEM_LIMIT = 48 * 1024 * 1024

_NN = (((1,), (0,)), ((), ()))
_NT = (((1,), (1,)), ((), ()))
_TN = (((0,), (0,)), ((), ()))


def _dot(a, b, dims=_NN):
    return lax.dot_general(a.astype(BF16), b.astype(BF16), dims, preferred_element_type=F32)


def _split2(x):
    hi = x.astype(BF16)
    lo = (x - hi.astype(F32)).astype(BF16)
    return hi, lo


def _dot3(a, b, dims=_NN):
    ah, al = _split2(a)
    bh, bl = _split2(b)
    dg = functools.partial(lax.dot_general, dimension_numbers=dims, preferred_element_type=F32)
    return dg(ah, bh) + (dg(ah, bl) + dg(al, bh))


def _dot_exact_lhs(a_bf16, x):
    x1 = x.astype(BF16)
    r1 = x - x1.astype(F32)
    x2 = r1.astype(BF16)
    x3 = (r1 - x2.astype(F32)).astype(BF16)
    dg = functools.partial(lax.dot_general, dimension_numbers=_NN, preferred_element_type=F32)
    return dg(a_bf16, x1) + (dg(a_bf16, x2) + dg(a_bf16, x3))


def _seg_sum(x, ones_bd):
    x1 = x.astype(BF16)
    r1 = x - x1.astype(F32)
    x2 = r1.astype(BF16)
    x3 = (r1 - x2.astype(F32)).astype(BF16)
    dg = functools.partial(lax.dot_general, dimension_numbers=_NN, preferred_element_type=F32)
    return dg(x1, ones_bd) + (dg(x2, ones_bd) + dg(x3, ones_bd))


def _cparams(*sem):
    return pltpu.CompilerParams(dimension_semantics=tuple(sem), vmem_limit_bytes=VMEM_LIMIT)


def _norm_in_proj_body(x_ref, g_ref, wa_ref, wb_ref, wg_ref, pa_ref, pb_ref, pg_ref):
    x = x_ref[...]
    xn = x * lax.rsqrt(jnp.mean(x * x, axis=-1, keepdims=True) + NORM_EPS) * g_ref[...]
    xb = xn.astype(BF16)
    pa_ref[...] = jnp.dot(xb, wa_ref[...], preferred_element_type=F32)
    pb_ref[...] = jnp.dot(xb, wb_ref[...], preferred_element_type=F32)
    pg_ref[...] = jnp.dot(xb, wg_ref[...], preferred_element_type=F32)


def _norm_in_proj(x, g, w_in, tm):
    t = x.shape[0]
    wa = w_in[:, :C_RWKV_IN].astype(BF16)
    wb = w_in[:, C_RWKV_IN:C_RWKV_IN + C_HYENA_IN].astype(BF16)
    wg = w_in[:, C_RWKV_IN + C_HYENA_IN:].astype(BF16)
    full = lambda i: (0, 0)
    row = lambda i: (i, 0)
    return pl.pallas_call(
        _norm_in_proj_body,
        grid=(t // tm,),
        in_specs=[pl.BlockSpec((tm, D_MODEL), row), pl.BlockSpec((1, D_MODEL), full),
                  pl.BlockSpec(wa.shape, full), pl.BlockSpec(wb.shape, full), pl.BlockSpec(wg.shape, full)],
        out_specs=[pl.BlockSpec((tm, C_RWKV_IN), row), pl.BlockSpec((tm, C_HYENA_IN), row),
                   pl.BlockSpec((tm, C_GATES), row)],
        out_shape=[jax.ShapeDtypeStruct((t, C_RWKV_IN), F32), jax.ShapeDtypeStruct((t, C_HYENA_IN), F32),
                   jax.ShapeDtypeStruct((t, C_GATES), F32)],
        compiler_params=_cparams("parallel"),
        name="norm_in_proj",
    )(x, g.reshape(1, D_MODEL), wa, wb, wg)


def _shifted(p, prev_blk, next_blk, is_first, is_last):
    tm = p.shape[0]
    row = lax.broadcasted_iota(I32, p.shape, 0)
    prow = jnp.where(is_first, 0.0, prev_blk[7:8, :])
    nrow = jnp.where(is_last, 0.0, next_blk[0:1, :])
    prev = jnp.where(row == 0, prow, pltpu.roll(p, 1, 0))
    nxt = jnp.where(row == tm - 1, nrow, pltpu.roll(p, tm - 1, 0))
    return prev, nxt


def _stack_heads(x, head_mask):
    return jnp.where(head_mask, jnp.concatenate([x] * GROUP_HEADS, axis=0), 0.0)


def _unstack_heads(z):
    out = z[0:CHUNK]
    for h in range(1, GROUP_HEADS):
        out = out + z[h * CHUNK:(h + 1) * CHUNK]
    return out


def _chunks_local(chains):
    ngrp = RWKV_HEADS // GROUP_HEADS
    ti = lax.broadcasted_iota(I32, (CHUNK, CHUNK), 0)
    si = lax.broadcasted_iota(I32, (CHUNK, CHUNK), 1)
    srow = lax.broadcasted_iota(I32, (GSTACK, HALF), 0)
    slane = lax.broadcasted_iota(I32, (GSTACK, HALF), 1)
    head_mask = (srow >> 5) == (slane >> 6)
    mrow = lax.broadcasted_iota(I32, (GSTACK, GSTACK), 0)
    mcol = lax.broadcasted_iota(I32, (GSTACK, GSTACK), 1)
    same = (mrow >> 5) == (mcol >> 5)
    masks = {False: (same & (mrow > mcol), same & (mrow >= mcol)),
             True: (same & (mrow < mcol), same & (mrow <= mcol))}
    tris = {False: (ti >= si).astype(BF16), True: (ti <= si).astype(BF16)}

    cls = [_dot_exact_lhs(tris[rev], lw) for (_, _, _, lw, _, _, rev) in chains]
    st = []
    for (r, v, kk, lw, k, a, rev), cl in zip(chains, cls):
        tot = cl[0:1] if rev else cl[CHUNK - 1:CHUNK]
        e_neg = jnp.exp(-cl)
        e_tail = jnp.exp(tot - cl)
        beta = kk * a
        st.append(dict(rev=rev, v=v, alpha_b=-kk * jnp.exp(cl - lw), r_b=r * jnp.exp(cl), beta_b=beta * e_neg,
                       k_b=k * e_neg, bt=beta * e_tail, kt=k * e_tail, pc=jnp.exp(tot)))

    for s in st:
        s["sa"], s["sv"], s["lhs"], s["rhs"] = [], [], [], []
        for g in range(ngrp):
            lanes = slice(g * HALF, (g + 1) * HALF)
            sa_ = _stack_heads(s["alpha_b"][:, lanes], head_mask).astype(BF16)
            sr_ = _stack_heads(s["r_b"][:, lanes], head_mask).astype(BF16)
            s["sa"].append(sa_)
            s["sv"].append(_stack_heads(s["v"][:, lanes], head_mask).astype(BF16))
            s["lhs"].append(jnp.concatenate([sa_, sr_], axis=0))
            s["rhs"].append(jnp.concatenate([s["beta_b"][:, lanes].astype(BF16)] * GROUP_HEADS
                                            + [s["k_b"][:, lanes].astype(BF16)] * GROUP_HEADS, axis=0))
    pms = [[lax.dot_general(s["lhs"][g], s["rhs"][g], _NT, preferred_element_type=F32) for g in range(ngrp)]
           for s in st]
    zero = jnp.zeros((GSTACK, GSTACK), F32)
    drow = lax.broadcasted_iota(I32, (STACK, STACK), 0)
    dcol = lax.broadcasted_iota(I32, (STACK, STACK), 1)
    eye = jnp.where(drow == dcol, 1.0, 0.0)
    for s, pm in zip(st, pms):
        strict, incl = masks[s["rev"]]
        s["ak"] = [jnp.where(strict, pm[g][:GSTACK, GSTACK:], 0.0).astype(BF16) for g in range(ngrp)]
        s["rb"] = [jnp.where(incl, pm[g][GSTACK:, :GSTACK], 0.0).astype(BF16) for g in range(ngrp)]
        s["rk"] = [jnp.where(incl, pm[g][GSTACK:, GSTACK:], 0.0).astype(BF16) for g in range(ngrp)]
        ab = [jnp.where(strict, pm[g][:GSTACK, :GSTACK], 0.0) for g in range(ngrp)]
        s["apow"] = jnp.concatenate([jnp.concatenate([ab[0], zero], axis=1),
                                     jnp.concatenate([zero, ab[1]], axis=1)], axis=0)
        s["tinv"] = eye + s["apow"]

    for _ in range(int(math.log2(CHUNK)) - 1):
        sq = [_dot(s["apow"], s["apow"]) for s in st]
        for s, x in zip(st, sq):
            s["apow"] = x
        pr = [_dot(s["tinv"], s["apow"]) for s in st]
        for s, x in zip(st, pr):
            s["tinv"] = s["tinv"] + x

    akv = [[_dot(s["ak"][g], s["sv"][g]) for g in range(ngrp)] for s in st]
    rkv = [[_dot(s["rk"][g], s["sv"][g]) for g in range(ngrp)] for s in st]
    tw = [[_dot(s["tinv"][g * GSTACK:(g + 1) * GSTACK, g * GSTACK:(g + 1) * GSTACK],
                jnp.concatenate([s["sa"][g].astype(F32), akv[c][g]], axis=1)) for g in range(ngrp)]
          for c, s in enumerate(st)]
    rbx = [[_dot(s["rb"][g], tw[c][g]) for g in range(ngrp)] for c, s in enumerate(st)]
    outs = []
    for c, s in enumerate(st):
        cat = lambda f: jnp.concatenate([f(g) for g in range(ngrp)], axis=1)
        qt = s["r_b"] + cat(lambda g: _unstack_heads(rbx[c][g][:, :HALF]))
        wt = cat(lambda g: _unstack_heads(tw[c][g][:, :HALF]))
        yloc = cat(lambda g: _unstack_heads(rbx[c][g][:, HALF:] + rkv[c][g]))
        u = cat(lambda g: _unstack_heads(tw[c][g][:, HALF:]))
        outs.append((qt, wt, yloc, u, s["bt"], s["kt"], s["pc"]))
    return outs


def _rwkv_local_body(nt_seq, p_ref, pp_ref, pn_ref, mup_ref, mun_ref, w0_ref, w2_ref, a0_ref, a2_ref, g2_ref,
                     kk_ref, ka_ref, rk_ref, ones_ref,
                     g_out, bonus_out, v_out, qwf, yuf, bkf, pcf, qwr, yur, bkr, pcr,
                     r_s, v_s, kk_s, lw_s, k_s, a_s):
    i = pl.program_id(0) % nt_seq
    p = p_ref[...]
    tm = p.shape[0]
    prev, nxt = _shifted(p, pp_ref[...], pn_ref[...], i == 0, i == nt_seq - 1)
    ps = p + mup_ref[...] * (prev - p) + mun_ref[...] * (nxt - p)
    c = RWKV_WIDTH
    r = ps[:, 0:c]
    k = ps[:, c:2 * c]
    v = ps[:, 2 * c:3 * c]
    lw = ps[:, 3 * c:3 * c + 128]
    la = ps[:, 3 * c + 128:3 * c + 256]
    lg = ps[:, 3 * c + 256:3 * c + 384]
    ones_bd = ones_ref[...]
    w_raw = w0_ref[...] + _dot3(jnp.tanh(lw), w2_ref[...])
    logw = -jnp.exp(-jnp.logaddexp(-w_raw, 0.0) - 0.5)
    a = jax.nn.sigmoid(a0_ref[...] + _dot3(la, a2_ref[...]))
    g_out[...] = _dot3(jax.nn.sigmoid(lg), g2_ref[...])
    kk0 = k * kk_ref[...]
    kk = kk0 * lax.rsqrt(jnp.maximum(_seg_sum(kk0 * kk0, ones_bd), 1e-24))
    ka = ka_ref[...]
    kdir_f = k * (1.0 + (a[:, :c] - 1.0) * ka)
    kdir_r = k * (1.0 + (a[:, c:] - 1.0) * ka)
    bonus_out[...] = _seg_sum(r * ((kdir_f + kdir_r) * 0.5) * rk_ref[...], ones_bd) * v
    v_out[...] = v
    r_s[...] = r
    v_s[...] = v
    kk_s[...] = kk
    lw_s[...] = logw
    k_s[:, :c] = kdir_f
    k_s[:, c:] = kdir_r
    a_s[...] = a

    out_refs = ((qwf, yuf, bkf, pcf), (qwr, yur, bkr, pcr))

    def chunk(cj, carry):
        chains, where = [], []
        for uu in range(LOCAL_CHUNKS_PER_ITER):
            ci = cj * LOCAL_CHUNKS_PER_ITER + uu
            rows = pl.ds(pl.multiple_of(ci * CHUNK, CHUNK), CHUNK)
            rc, vc, kkc = r_s[rows, :], v_s[rows, :], kk_s[rows, :]
            for d in range(2):
                lanes = slice(d * c, (d + 1) * c)
                chains.append((rc, vc, kkc, lw_s[rows, lanes], k_s[rows, lanes], a_s[rows, lanes], d == 1))
                where.append((ci, d))
        for (ci, d), (qt, wt, yloc, u, bt, kt, pcv) in zip(where, _chunks_local(chains)):
            qw, yu, bk, pc = out_refs[d]
            qw[ci] = jnp.concatenate([qt, wt], axis=0).astype(BF16)
            yu[ci] = jnp.concatenate([yloc, u], axis=0)
            bk[ci] = jnp.concatenate([bt, kt], axis=0).astype(BF16)
            pc[ci] = jnp.broadcast_to(pcv, (8, c))
        return carry

    lax.fori_loop(0, tm // (CHUNK * LOCAL_CHUNKS_PER_ITER), chunk, 0)


def _rwkv_local(p_a, nt_seq, tm, prm):
    t = p_a.shape[0]
    nt = t // tm
    nc = tm // CHUNK
    c = RWKV_WIDTH
    rd, ri = RWKV_DECAY_RANK, RWKV_ICLR_RANK
    z = jnp.zeros((rd, c), F32)
    w2pad = jnp.concatenate([jnp.concatenate([prm["rwkv_w2"][0], z], 0), jnp.concatenate([z, prm["rwkv_w2"][1]], 0)], 1)
    z = jnp.zeros((ri, c), F32)
    a2pad = jnp.concatenate([jnp.concatenate([prm["rwkv_a2"][0], z], 0), jnp.concatenate([z, prm["rwkv_a2"][1]], 0)], 1)
    hd = np.arange(c) // RWKV_HEAD_DIM
    ones_bd = jnp.asarray(hd[:, None] == hd[None, :], BF16)
    full = lambda i: (0, 0)
    row = lambda i: (i, 0)
    row3 = lambda i: (i, 0, 0)
    nblk8 = t // 8
    tm8 = tm // 8
    vec = lambda n: pl.BlockSpec((1, n), full)
    chunk_out = lambda rows, dt: (pl.BlockSpec((nc, rows, c), row3), jax.ShapeDtypeStruct((t // CHUNK, rows, c), dt))
    per_dir = [chunk_out(2 * CHUNK, BF16), chunk_out(2 * CHUNK, F32), chunk_out(2 * CHUNK, BF16), chunk_out(8, F32)]
    outs = [(pl.BlockSpec((tm, c), row), jax.ShapeDtypeStruct((t, c), F32))] * 3 + per_dir + per_dir
    return pl.pallas_call(
        functools.partial(_rwkv_local_body, nt_seq),
        grid=(nt,),
        in_specs=[pl.BlockSpec((tm, C_RWKV_IN), row),
                  pl.BlockSpec((8, C_RWKV_IN), lambda i: (jnp.maximum(i * tm8 - 1, 0), 0)),
                  pl.BlockSpec((8, C_RWKV_IN), lambda i: (jnp.minimum((i + 1) * tm8, nblk8 - 1), 0)),
                  vec(C_RWKV_IN), vec(C_RWKV_IN), vec(2 * c), pl.BlockSpec((128, 2 * c), full),
                  vec(2 * c), pl.BlockSpec((128, 2 * c), full), pl.BlockSpec((RWKV_GATE_RANK, c), full),
                  vec(c), vec(c), vec(c), pl.BlockSpec((c, c), full)],
        out_specs=[o[0] for o in outs],
        out_shape=[o[1] for o in outs],
        scratch_shapes=[pltpu.VMEM((tm, c), F32)] * 3 + [pltpu.VMEM((tm, 2 * c), F32)] * 3,
        compiler_params=_cparams("parallel"),
        name="rwkv_local",
    )(p_a, p_a, p_a, prm["rwkv_mu_prev"].reshape(1, -1), prm["rwkv_mu_next"].reshape(1, -1),
      prm["rwkv_w0"].reshape(1, 2 * c), w2pad, prm["rwkv_a0"].reshape(1, 2 * c), a2pad, prm["rwkv_g2"],
      prm["rwkv_k_k"].reshape(1, c), prm["rwkv_k_a"].reshape(1, c), prm["rwkv_r_k"].reshape(1, c), ones_bd)


def _rwkv_scan_body(nc, qwf, yuf, bkf, pcf, vf, qwr, yur, bkr, pcr, vr, yf_ref, yr_ref, s_ref):
    @pl.when(pl.program_id(1) == 0)
    def _():
        s_ref[...] = jnp.zeros(s_ref.shape, F32)

    brow = lax.broadcasted_iota(I32, (HALF, HALF), 0)
    bcol = lax.broadcasted_iota(I32, (HALF, HALF), 1)
    same_head = (brow >> 6) == (bcol >> 6)

    dirs = ((qwf, yuf, bkf, pcf, vf, yf_ref), (qwr, yur, bkr, pcr, vr, yr_ref))

    def step(j, carry):
        ch = []
        for d, (qw_ref, yu_ref, bk_ref, pc_ref, v_ref, y_ref) in enumerate(dirs):
            ci = j if d == 0 else nc - 1 - j
            rows = pl.ds(pl.multiple_of(ci * CHUNK, CHUNK), CHUNK)
            qw, yu, bk, pc, vc = qw_ref[ci], yu_ref[ci], bk_ref[ci], pc_ref[ci][0:1], v_ref[rows, :]
            for hf in range(2):
                lanes = slice(hf * HALF, (hf + 1) * HALF)
                ch.append(dict(d=d, hf=hf, rows=rows, lanes=lanes, y_ref=y_ref, qw=qw[:, lanes], yu=yu[:, lanes],
                               bk=bk[:, lanes], pc=pc[:, lanes], vc=vc[:, lanes]))
        for c in ch:
            c["s"] = s_ref[c["d"], c["hf"]]
        ys = [lax.dot_general(c["qw"], c["s"].astype(BF16), _NT, preferred_element_type=F32) + c["yu"] for c in ch]
        for c, y in zip(ch, ys):
            c["y_ref"][c["rows"], c["lanes"]] = y[:CHUNK]
        sav = [jnp.concatenate([y[CHUNK:], c["vc"]], axis=0).astype(BF16) for c, y in zip(ch, ys)]
        upd = [lax.dot_general(x, c["bk"], _TN, preferred_element_type=F32) for c, x in zip(ch, sav)]
        for c, u in zip(ch, upd):
            s_ref[c["d"], c["hf"]] = c["s"] * c["pc"] + jnp.where(same_head, u, 0.0)
        return carry

    lax.fori_loop(0, nc, step, 0)


def _rwkv_scan(loc, nseq, nt_seq, tm):
    _, _, v, qwf, yuf, bkf, pcf, qwr, yur, bkr, pcr = loc
    t = v.shape[0]
    nc = tm // CHUNK
    c = RWKV_WIDTH
    fwd3 = lambda b, i: (b * nt_seq + i, 0, 0)
    rev3 = lambda b, i: (b * nt_seq + nt_seq - 1 - i, 0, 0)
    fwd2 = lambda b, i: (b * nt_seq + i, 0)
    rev2 = lambda b, i: (b * nt_seq + nt_seq - 1 - i, 0)

    def specs(m3, m2):
        return [pl.BlockSpec((nc, 2 * CHUNK, c), m3), pl.BlockSpec((nc, 2 * CHUNK, c), m3),
                pl.BlockSpec((nc, 2 * CHUNK, c), m3), pl.BlockSpec((nc, 8, c), m3), pl.BlockSpec((tm, c), m2)]

    return pl.pallas_call(
        functools.partial(_rwkv_scan_body, nc),
        grid=(nseq, nt_seq),
        in_specs=specs(fwd3, fwd2) + specs(rev3, rev2),
        out_specs=[pl.BlockSpec((tm, c), fwd2), pl.BlockSpec((tm, c), rev2)],
        out_shape=[jax.ShapeDtypeStruct((t, c), F32)] * 2,
        scratch_shapes=[pltpu.VMEM((2, 2, HALF, HALF), F32)],
        compiler_params=_cparams("arbitrary", "arbitrary"),
        name="rwkv_scan",
    )(qwf, yuf, bkf, pcf, v, qwr, yur, bkr, pcr, v)


def _rwkv_out(yf, yr, g, bonus, ln_g, ln_b, ones_bd):
    y = yf + yr
    inv_n = 1.0 / RWKV_HEAD_DIM
    mean = _seg_sum(y, ones_bd) * inv_n
    yc = y - mean
    var = _seg_sum(yc * yc, ones_bd) * inv_n
    yn = yc * lax.rsqrt(var + RWKV_GN_EPS) * ln_g + ln_b
    return (yn + bonus) * g


def _rwkv_branch(p_a, nseq, seq_len, prm):
    tm = min(256, seq_len)
    nt_seq = seq_len // tm
    loc = _rwkv_local(p_a, nt_seq, tm, prm)
    yf, yr = _rwkv_scan(loc, nseq, nt_seq, tm)
    return yf, yr, loc[0], loc[1]


FFT_N2 = 128
FILTER_EMB_DIM = 33
FILTER_BANDS = (FILTER_EMB_DIM - 1) // 2
FILTER_HIDDEN = 64
FILTER_TARGET = 1e-2
FILTER_FAST_DECAY_PCT = 0.3
FILTER_SLOW_DECAY_PCT = 1.5


def _hyena_prep_body(nt_seq, p_ref, pp_ref, pn_ref, cw_ref, cb_ref, src_ref, x2_ref):
    i = pl.program_id(0) % nt_seq
    p = p_ref[...]
    prev, nxt = _shifted(p, pp_ref[...], pn_ref[...], i == 0, i == nt_seq - 1)
    cw = cw_ref[...]
    u = prev * cw[0:1] + p * cw[1:2] + nxt * cw[2:3] + cb_ref[...]
    c = HYENA_WIDTH
    src_ref[...] = u[:, 0:c] * u[:, 2 * c:3 * c]
    x2_ref[...] = u[:, c:2 * c]


def _hyena_prep(p_b, nt_seq, tm, prm):
    t = p_b.shape[0]
    c = HYENA_WIDTH
    row = lambda i: (i, 0)
    full = lambda i: (0, 0)
    nblk8 = t // 8
    tm8 = tm // 8
    cw = jnp.concatenate([prm["hyena_conv_w"], jnp.zeros((5, C_HYENA_IN), F32)], axis=0)
    return pl.pallas_call(
        functools.partial(_hyena_prep_body, nt_seq),
        grid=(t // tm,),
        in_specs=[pl.BlockSpec((tm, C_HYENA_IN), row),
                  pl.BlockSpec((8, C_HYENA_IN), lambda i: (jnp.maximum(i * tm8 - 1, 0), 0)),
                  pl.BlockSpec((8, C_HYENA_IN), lambda i: (jnp.minimum((i + 1) * tm8, nblk8 - 1), 0)),
                  pl.BlockSpec((8, C_HYENA_IN), full), pl.BlockSpec((1, C_HYENA_IN), full)],
        out_specs=[pl.BlockSpec((tm, c), row)] * 2,
        out_shape=[jax.ShapeDtypeStruct((t, c), F32)] * 2,
        compiler_params=_cparams("parallel"),
        name="hyena_prep",
    )(p_b, p_b, p_b, cw, prm["hyena_conv_b"].reshape(1, -1))


def _hyena_filter_body(seq_len, w1_ref, b1_ref, w2_ref, b2_ref, w3_ref, freq_ref, delta_ref, o_ref):
    rows = o_ref.shape[0]
    n = pl.program_id(0) * rows + lax.broadcasted_iota(I32, (rows, 128), 0)
    lane = lax.broadcasted_iota(I32, (rows, 128), 1)
    pos = jnp.where(n < seq_len, n, 2 * seq_len - n).astype(F32)
    t = pos * (1.0 / (seq_len - 1))
    omega = (2.0 * math.pi) * pos / seq_len
    band_step = (FILTER_BANDS - 1 - 1e-4) / (FILTER_BANDS - 1)
    band_idx = jnp.where(lane <= FILTER_BANDS, lane - 1, lane - 1 - FILTER_BANDS).astype(F32)
    arg = (1e-4 + band_idx * band_step) * omega
    z = jnp.where(lane == 0, t, jnp.where(lane <= FILTER_BANDS, jnp.cos(arg),
                                          jnp.where(lane <= 2 * FILTER_BANDS, -jnp.sin(arg), 0.0)))
    freq = freq_ref[...]
    hid = jnp.sin(freq * (_dot3(z, w1_ref[...]) + b1_ref[...]))
    hid = jnp.sin(freq * (_dot3(hid, w2_ref[...]) + b2_ref[...]))
    filt = _dot3(hid, w3_ref[...])
    nn = n[:, 0:1]
    tt = t[:, 0:1]
    sel = jnp.where(nn < seq_len, filt[:, :HYENA_WIDTH], jnp.where(nn > seq_len, filt[:, HYENA_WIDTH:], 0.0))
    o_ref[...] = sel * jnp.exp(-tt * delta_ref[...])


def _hyena_filter(seq_len, prm):
    rows = min(1024, 2 * seq_len)
    c = HYENA_WIDTH
    fh = FILTER_HIDDEN
    w1 = jnp.concatenate([prm["hyena_f_w1"], jnp.zeros((128 - FILTER_EMB_DIM, fh), F32)], axis=0)
    min_decay = math.log(FILTER_TARGET) / FILTER_SLOW_DECAY_PCT
    max_decay = math.log(FILTER_TARGET) / FILTER_FAST_DECAY_PCT
    deltas = jnp.abs(jnp.linspace(min_decay, max_decay, c, dtype=F32)).reshape(1, c)
    full = lambda i: (0, 0)
    return pl.pallas_call(
        functools.partial(_hyena_filter_body, seq_len),
        grid=(2 * seq_len // rows,),
        in_specs=[pl.BlockSpec((128, fh), full), pl.BlockSpec((1, fh), full), pl.BlockSpec((fh, fh), full),
                  pl.BlockSpec((1, fh), full), pl.BlockSpec((fh, 2 * c), full), pl.BlockSpec((1, fh), full),
                  pl.BlockSpec((1, c), full)],
        out_specs=pl.BlockSpec((rows, c), lambda i: (i, 0)),
        out_shape=jax.ShapeDtypeStruct((2 * seq_len, c), F32),
        compiler_params=_cparams("parallel"),
        name="hyena_filter",
    )(w1, prm["hyena_f_b1"].reshape(1, fh), prm["hyena_f_w2"], prm["hyena_f_b2"].reshape(1, fh),
      prm["hyena_f_w3"], prm["hyena_f_freq"].reshape(1, fh), deltas)


FFT_N2_TILE = 16
FFT_LANE_TILE = 256


def _lmul_rows(m_ref, x_ref):
    m = m_ref[...]
    xt = pltpu.einshape("rjc->jrc", x_ref[0].astype(F32))
    ys = [jnp.dot(m, xt[j].astype(BF16), preferred_element_type=F32) for j in range(FFT_N2_TILE)]
    return pltpu.einshape("jrc->rjc", jnp.stack(ys, axis=0))


def _lmul_body(m_ref, x_ref, o_ref):
    o_ref[0] = _lmul_rows(m_ref, x_ref).astype(o_ref.dtype)


def _lmul_epilogue_body(m_ref, x_ref, src_ref, x2_ref, skip_ref, o_ref):
    o_ref[0] = x2_ref[0] * (_lmul_rows(m_ref, x_ref) + src_ref[0] * skip_ref[...])


def _lmul(m, x, extra=None):
    nfft, r_in, n2, c = x.shape
    r_out = m.shape[0]
    ct = FFT_LANE_TILE
    xs = pl.BlockSpec((1, r_in, FFT_N2_TILE, ct), lambda f, j, l: (f, 0, j, l))
    os_ = pl.BlockSpec((1, r_out, FFT_N2_TILE, ct), lambda f, j, l: (f, 0, j, l))
    ms = pl.BlockSpec(m.shape, lambda f, j, l: (0, 0))
    if extra is None:
        body, ins, specs, out_dtype = _lmul_body, (m, x), [ms, xs], BF16
    else:
        src, x2, skip = extra
        body, ins, out_dtype = _lmul_epilogue_body, (m, x, src, x2, skip), F32
        specs = [ms, xs, os_, os_, pl.BlockSpec((1, ct), lambda f, j, l: (0, l))]
    return pl.pallas_call(
        body,
        grid=(nfft, n2 // FFT_N2_TILE, c // ct),
        in_specs=specs,
        out_specs=os_,
        out_shape=jax.ShapeDtypeStruct((nfft, r_out, n2, c), out_dtype),
        compiler_params=_cparams("parallel", "parallel", "parallel"),
        name="fft_outer" if extra is None else "fft_outer_out",
    )(*ins)


def _fft_inner_body(kt, conv, g_ref, y_ref, *rest):
    if conv:
        h_ref, o_ref = rest
    else:
        (o_ref,) = rest
    n2 = FFT_N2
    for q in range(kt):
        g = g_ref[q]
        yk = jnp.concatenate([y_ref[0, 0, q], y_ref[0, 1, q]], axis=0).astype(BF16)
        z = jnp.dot(g, yk, preferred_element_type=F32)
        if conv:
            zr, zi = z[:n2], z[n2:]
            hr, hi = h_ref[0, 0, q], h_ref[0, 1, q]
            pr = zr * hr - zi * hi
            pi = zr * hi + zi * hr
            prod = jnp.concatenate([pr, pi], axis=0).astype(BF16)
            z = lax.dot_general(g, prod, _TN, preferred_element_type=F32)
        o_ref[0, 0, q] = z[:n2].astype(o_ref.dtype)
        o_ref[0, 1, q] = z[n2:].astype(o_ref.dtype)


def _fft_inner(g, y, h, kt):
    nfft, _, n1, n2, c = y.shape
    blk = pl.BlockSpec((1, 2, kt, n2, c), lambda f, j: (f, 0, j, 0, 0))
    gs = pl.BlockSpec((kt, 2 * n2, 2 * n2), lambda f, j: (j, 0, 0))
    conv = h is not None
    ins = (g, y, h) if conv else (g, y)
    specs = [gs, blk, pl.BlockSpec((1, 2, kt, n2, c), lambda f, j: (0, 0, j, 0, 0))] if conv else [gs, blk]
    return pl.pallas_call(
        functools.partial(_fft_inner_body, kt, conv),
        grid=(nfft, n1 // kt),
        in_specs=specs,
        out_specs=blk,
        out_shape=jax.ShapeDtypeStruct(y.shape, BF16 if conv else F32),
        compiler_params=_cparams("parallel", "parallel"),
        name="fft_inner_conv" if conv else "fft_inner",
    )(*ins)


def _dft_tables(seq_len):
    n = 2 * seq_len
    n2 = FFT_N2
    n1 = n // n2
    k1 = jnp.arange(n1, dtype=I32)
    ang1 = (2.0 * math.pi / n1) * ((k1[:, None] * k1[None, :]) % n1).astype(F32)
    c1, s1 = jnp.cos(ang1), jnp.sin(ang1)
    half = n1 // 2
    f_pair = jnp.concatenate([jnp.concatenate([c1[:, :half], s1[:, :half]], 1),
                              jnp.concatenate([-s1[:, :half], c1[:, :half]], 1)], 0)
    f_real_half = jnp.concatenate([c1[:, :half], -s1[:, :half]], 0)
    f_real_full = jnp.concatenate([c1, -s1], 0)
    ci, si = c1[:half] / n, s1[:half] / n
    b_real = jnp.concatenate([ci, -si], 1)
    b_pair = jnp.concatenate([b_real, jnp.concatenate([si, ci], 1)], 0)
    kk = k1[:, None, None] + n1 * jnp.arange(n2, dtype=I32)[None, :, None]
    ang2 = (2.0 * math.pi / n) * ((kk * jnp.arange(n2, dtype=I32)[None, None, :]) % n).astype(F32)
    c2, s2 = jnp.cos(ang2), jnp.sin(ang2)
    g = jnp.concatenate([jnp.concatenate([c2, s2], 2), jnp.concatenate([-s2, c2], 2)], 1)
    cast = lambda a: a.astype(BF16)
    return dict(f_pair=cast(f_pair), f_real_half=cast(f_real_half), f_real_full=cast(f_real_full),
                b_pair=cast(b_pair), b_real=cast(b_real), g=cast(g), n1=n1)


def _hyena_branch(p_b, nseq, seq_len, prm):
    tm = min(512, seq_len)
    c = HYENA_WIDTH
    src, x2 = _hyena_prep(p_b, seq_len // tm, tm, prm)
    tab = _dft_tables(seq_len)
    n1 = tab["n1"]
    n2 = FFT_N2
    kt = 4
    filt = _hyena_filter(seq_len, prm)
    hy = _lmul(tab["f_real_full"], filt.reshape(1, n1, n2, c))
    hspec = _fft_inner(tab["g"], hy.reshape(1, 2, n1, n2, c), None, kt)
    pair = nseq % 2 == 0
    nfft = nseq // 2 if pair else nseq
    rows = n1 if pair else n1 // 2
    xin = src.reshape(nfft, rows, n2, c)
    y = _lmul(tab["f_pair"] if pair else tab["f_real_half"], xin)
    w = _fft_inner(tab["g"], y.reshape(nfft, 2, n1, n2, c), hspec, kt)
    out = _lmul(tab["b_pair"] if pair else tab["b_real"], w.reshape(nfft, 2 * n1, n2, c),
                extra=(xin, x2.reshape(nfft, rows, n2, c), prm["hyena_skip"].reshape(1, c)))
    return out.reshape(nseq * seq_len, c)


MOE_GROUPS = 4
MOE_EXPERTS_PER_GROUP = 8
MOE_EXPERTS = MOE_GROUPS * MOE_EXPERTS_PER_GROUP
MOE_HIDDEN = D_MODEL // 2
ROUTE_LANES = 128


def _route(logits):
    lane = lax.broadcasted_iota(I32, logits.shape, 1)
    neg = -jnp.inf
    big = ROUTE_LANES
    gl = jnp.where(lane < MOE_GROUPS, logits, neg)
    gmax = jnp.max(gl, axis=-1, keepdims=True)
    grp = jnp.min(jnp.where(gl == gmax, lane, big), axis=-1, keepdims=True)
    p_grp = 1.0 / jnp.sum(jnp.exp(gl - gmax), axis=-1, keepdims=True)
    lo = MOE_GROUPS + grp * MOE_EXPERTS_PER_GROUP
    el = jnp.where((lane >= lo) & (lane < lo + MOE_EXPERTS_PER_GROUP), logits, neg)
    m1 = jnp.max(el, axis=-1, keepdims=True)
    i1 = jnp.min(jnp.where(el == m1, lane, big), axis=-1, keepdims=True)
    el2 = jnp.where(lane == i1, neg, el)
    m2 = jnp.max(el2, axis=-1, keepdims=True)
    i2 = jnp.min(jnp.where(el2 == m2, lane, big), axis=-1, keepdims=True)
    e2 = jnp.exp(m2 - m1)
    g1 = p_grp / (1.0 + e2)
    g2 = p_grp * e2 / (1.0 + e2)
    gates = jnp.where(lane == i1, g1, jnp.where(lane == i2, g2, 0.0))
    return pltpu.roll(gates, ROUTE_LANES - MOE_GROUPS, 1)


def _merge_body(x_ref, yf_ref, yr_ref, rg_ref, bonus_ref, lng_ref, lnb_ref, ones_ref, yb_ref, pg_ref,
                wua_ref, wub_ref, wo_ref, g_ref, wr_ref, br_ref, x1_ref, xn_ref, gwt_ref):
    ya = _rwkv_out(yf_ref[...], yr_ref[...], rg_ref[...], bonus_ref[...], lng_ref[...], lnb_ref[...], ones_ref[...])
    ga = jax.nn.sigmoid(pg_ref[:, :D_MODEL])
    gb = jax.nn.sigmoid(pg_ref[:, D_MODEL:])
    merged = ga * _dot(ya, wua_ref[...]) + gb * _dot(yb_ref[...], wub_ref[...])
    x1 = x_ref[...] + _dot(merged, wo_ref[...])
    x1_ref[...] = x1
    xn = x1 * lax.rsqrt(jnp.mean(x1 * x1, axis=-1, keepdims=True) + NORM_EPS) * g_ref[...]
    xn_ref[...] = xn.astype(BF16)
    logits = _dot3(xn, wr_ref[...]) + br_ref[...]
    gwt_ref[...] = _route(logits).T


def _merge(x, rwkv, yb, pg, prm, wts, tm):
    t = x.shape[0]
    c = RWKV_WIDTH
    row = lambda i: (i, 0)
    full = lambda i: (0, 0)
    hd = np.arange(c) // RWKV_HEAD_DIM
    ones_bd = jnp.asarray(hd[:, None] == hd[None, :], BF16)
    wr = jnp.concatenate([prm["moe_w_route_group"], prm["moe_w_route_expert"],
                          jnp.zeros((D_MODEL, ROUTE_LANES - MOE_GROUPS - MOE_EXPERTS), F32)], axis=1)
    br = jnp.concatenate([prm["moe_b_route_group"], prm["moe_b_route_expert"],
                          jnp.zeros((ROUTE_LANES - MOE_GROUPS - MOE_EXPERTS,), F32)]).reshape(1, ROUTE_LANES)
    return pl.pallas_call(
        _merge_body,
        grid=(t // tm,),
        in_specs=[pl.BlockSpec((tm, D_MODEL), row)] + [pl.BlockSpec((tm, c), row)] * 4
                 + [pl.BlockSpec((1, c), full)] * 2 + [pl.BlockSpec((c, c), full),
                  pl.BlockSpec((tm, HYENA_WIDTH), row), pl.BlockSpec((tm, C_GATES), row),
                  pl.BlockSpec((RWKV_WIDTH, D_MODEL), full), pl.BlockSpec((HYENA_WIDTH, D_MODEL), full),
                  pl.BlockSpec((D_MODEL, D_MODEL), full), pl.BlockSpec((1, D_MODEL), full),
                  pl.BlockSpec((D_MODEL, ROUTE_LANES), full), pl.BlockSpec((1, ROUTE_LANES), full)],
        out_specs=[pl.BlockSpec((tm, D_MODEL), row), pl.BlockSpec((tm, D_MODEL), row),
                   pl.BlockSpec((ROUTE_LANES, tm), lambda i: (0, i))],
        out_shape=[jax.ShapeDtypeStruct((t, D_MODEL), F32), jax.ShapeDtypeStruct((t, D_MODEL), BF16),
                   jax.ShapeDtypeStruct((ROUTE_LANES, t), F32)],
        compiler_params=_cparams("parallel"),
        name="merge_route",
    )(x, *rwkv, prm["rwkv_ln_g"].reshape(1, c), prm["rwkv_ln_b"].reshape(1, c), ones_bd, yb, pg,
      wts["w_up_a"], wts["w_up_b"], wts["w_out"], prm["norm_ffn_g"].reshape(1, D_MODEL), wr, br)


MOE_PAD = 16
MOE_BLK = 256
MOE_FFN_ROWS = 256


def _moe_slots(tt):
    ns = 2 * tt + MOE_EXPERTS * MOE_PAD
    return (ns + MOE_BLK - 1) // MOE_BLK * MOE_BLK


def _moe_run_copies(i, lstart_ref, npiece_ref, goff_ref, local_refs, global_refs, sems, to_global, wait):
    for e in range(MOE_EXPERTS):
        ls = lstart_ref[i * MOE_EXPERTS + e]
        go = goff_ref[i * MOE_EXPERTS + e]

        def piece(p, carry, ls=ls, go=go):
            lrows = pl.ds(pl.multiple_of(ls + p * MOE_PAD, MOE_PAD), MOE_PAD)
            grows = pl.ds(pl.multiple_of(go + p * MOE_PAD, MOE_PAD), MOE_PAD)
            for loc, glob, sem in zip(local_refs, global_refs, sems):
                src, dst = (loc.at[lrows, :], glob.at[grows, :]) if to_global else (glob.at[grows, :], loc.at[lrows, :])
                cp = pltpu.make_async_copy(src, dst, sem)
                if wait:
                    cp.wait()
                else:
                    cp.start()
            return carry

        lax.fori_loop(0, npiece_ref[i * MOE_EXPERTS + e], piece, 0)


def _moe_compact_body(lstart_ref, npiece_ref, goff_ref, xn_ref, gwt_ref, tri_ref, pcol_ref, xg0_ref, gg0_ref,
                      xg_out, gg_out, dcol_ref, xg_loc, gg_loc, sems):
    del xg0_ref, gg0_ref
    i = pl.program_id(0)
    tt = xn_ref.shape[0]
    ns = _moe_slots(tt)
    gw = gwt_ref[0:MOE_EXPERTS, :]
    sel = gw > 0.0
    rank = jnp.dot(sel.astype(BF16), tri_ref[...], preferred_element_type=F32)
    dest = pcol_ref[:, 0:1] + rank
    d_lo = jnp.min(jnp.where(sel, dest, 1e9), axis=0, keepdims=True)
    d_hi = jnp.max(jnp.where(sel, dest, -1.0), axis=0, keepdims=True)
    g_lo = jnp.sum(jnp.where(sel & (dest == d_lo), gw, 0.0), axis=0, keepdims=True)
    g_hi = jnp.sum(jnp.where(sel & (dest == d_hi), gw, 0.0), axis=0, keepdims=True)
    single = d_hi == d_lo
    g_hi = jnp.where(single, 0.0, g_hi)
    d_hi = jnp.where(single, -1.0, d_hi)
    rows8 = jnp.concatenate([d_lo, d_hi, g_lo, g_hi, jnp.zeros((4, tt), F32)], axis=0)
    dcol_ref[...] = jnp.concatenate([rows8, jnp.zeros((ROUTE_LANES - 8, tt), F32)], axis=0).T
    for blk in range(ns // MOE_BLK):
        r = (blk * MOE_BLK + lax.broadcasted_iota(I32, (MOE_BLK, tt), 0)).astype(F32)
        lo, hi = r == d_lo, r == d_hi
        rows = slice(blk * MOE_BLK, (blk + 1) * MOE_BLK)
        xg_loc[rows, :] = jnp.dot((lo | hi).astype(BF16), xn_ref[...], preferred_element_type=F32).astype(BF16)
        row_gate = jnp.sum(jnp.where(lo, g_lo, 0.0) + jnp.where(hi, g_hi, 0.0), axis=-1, keepdims=True)
        gg_loc[rows, :] = jnp.broadcast_to(row_gate, (MOE_BLK, ROUTE_LANES))
    copies = functools.partial(_moe_run_copies, i, lstart_ref, npiece_ref, goff_ref, (xg_loc, gg_loc),
                               (xg_out, gg_out), (sems.at[0], sems.at[1]), True)
    copies(False)
    copies(True)


def _moe_ffn_body(bexp_ref, nused_ref, xg_ref, gg_ref, wg_ref, wu_ref, wd_ref, og_ref):
    del bexp_ref
    used = pl.program_id(0) < nused_ref[0]

    @pl.when(jnp.logical_not(used))
    def _():
        og_ref[...] = jnp.zeros(og_ref.shape, BF16)

    @pl.when(used)
    def _():
        xg = xg_ref[...]
        hg = jnp.dot(xg, wg_ref[0], preferred_element_type=F32)
        hu = jnp.dot(xg, wu_ref[0], preferred_element_type=F32)
        hid = ((hg * jax.nn.sigmoid(hg)) * hu).astype(BF16)
        out = jnp.dot(hid, wd_ref[0], preferred_element_type=F32)
        og_ref[...] = (out * gg_ref[:, 0:1]).astype(BF16)


def _moe_scatter_body(final_norm, lstart_ref, npiece_ref, goff_ref, og_ref, x1_ref, dcol_ref, gf_ref, o_ref,
                      og_loc, sems):
    i = pl.program_id(0)
    tt = x1_ref.shape[0]
    ns = _moe_slots(tt)

    @pl.when(i == 0)
    def _():
        og_loc[...] = jnp.zeros(og_loc.shape, BF16)

    copies = functools.partial(_moe_run_copies, i, lstart_ref, npiece_ref, goff_ref, (og_loc,), (og_ref,),
                               (sems.at[0],), False)
    copies(False)
    copies(True)
    y = x1_ref[...]
    c_lo, c_hi = dcol_ref[:, 0:1], dcol_ref[:, 1:2]
    for blk in range(ns // MOE_BLK):
        r = (blk * MOE_BLK + lax.broadcasted_iota(I32, (tt, MOE_BLK), 1)).astype(F32)
        onehot = ((r == c_lo) | (r == c_hi)).astype(BF16)
        y = y + jnp.dot(onehot, og_loc[blk * MOE_BLK:(blk + 1) * MOE_BLK, :], preferred_element_type=F32)
    if final_norm:
        y = y * lax.rsqrt(jnp.mean(y * y, axis=-1, keepdims=True) + NORM_EPS) * gf_ref[...]
    o_ref[...] = y


def _moe_grouped(xn, gwt, x1, wts, norm_final_g, tt):
    t = xn.shape[0]
    final_norm = norm_final_g is not None
    if not final_norm:
        norm_final_g = jnp.ones((D_MODEL,), F32)
    nt = t // tt
    ns = _moe_slots(tt)
    ne = MOE_EXPERTS
    counts = jnp.sum((gwt[:ne] > 0.0).reshape(ne, nt, tt), axis=-1, dtype=I32).T
    padded = (counts + MOE_PAD - 1) // MOE_PAD * MOE_PAD
    lstart = jnp.cumsum(padded, axis=1) - padded
    region = (jnp.sum(padded, axis=0) + MOE_FFN_ROWS - 1) // MOE_FFN_ROWS * MOE_FFN_ROWS
    gend = jnp.cumsum(region)
    goff = (gend - region)[None, :] + jnp.cumsum(padded, axis=0) - padded
    ng = (2 * t + nt * ne * MOE_PAD + ne * MOE_FFN_ROWS + MOE_FFN_ROWS - 1) // MOE_FFN_ROWS * MOE_FFN_ROWS
    nb = ng // MOE_FFN_ROWS
    bexp = jnp.minimum(jnp.searchsorted(gend, jnp.arange(nb, dtype=I32) * MOE_FFN_ROWS, side="right"),
                       ne - 1).astype(I32)
    nused = (gend[-1] // MOE_FFN_ROWS).astype(I32).reshape(1)
    pcol = jnp.broadcast_to(lstart.astype(F32).reshape(nt * ne, 1), (nt * ne, ROUTE_LANES))
    idx = np.arange(tt)
    tri = jnp.asarray(idx[:, None] < idx[None, :], BF16)
    sched = (lstart.reshape(-1), (padded // MOE_PAD).reshape(-1), goff.reshape(-1).astype(I32))
    any_spec = pl.BlockSpec(memory_space=pl.ANY)

    xg, gg, dcol = pl.pallas_call(
        _moe_compact_body,
        grid_spec=pltpu.PrefetchScalarGridSpec(
            num_scalar_prefetch=3,
            grid=(nt,),
            in_specs=[pl.BlockSpec((tt, D_MODEL), lambda i, *_: (i, 0)),
                      pl.BlockSpec((ROUTE_LANES, tt), lambda i, *_: (0, i)),
                      pl.BlockSpec((tt, tt), lambda i, *_: (0, 0)),
                      pl.BlockSpec((ne, ROUTE_LANES), lambda i, *_: (i, 0)),
                      any_spec, any_spec],
            out_specs=[any_spec, any_spec, pl.BlockSpec((tt, ROUTE_LANES), lambda i, *_: (i, 0))],
            scratch_shapes=[pltpu.VMEM((ns, D_MODEL), BF16), pltpu.VMEM((ns, ROUTE_LANES), F32),
                            pltpu.SemaphoreType.DMA((2,))],
        ),
        out_shape=[jax.ShapeDtypeStruct((ng, D_MODEL), BF16), jax.ShapeDtypeStruct((ng, ROUTE_LANES), F32),
                   jax.ShapeDtypeStruct((t, ROUTE_LANES), F32)],
        input_output_aliases={7: 0, 8: 1},
        compiler_params=_cparams("arbitrary"),
        name="moe_compact",
    )(*sched, xn, gwt, tri, pcol, jnp.zeros((ng, D_MODEL), BF16), jnp.zeros((ng, ROUTE_LANES), F32))

    og = pl.pallas_call(
        _moe_ffn_body,
        grid_spec=pltpu.PrefetchScalarGridSpec(
            num_scalar_prefetch=2,
            grid=(nb,),
            in_specs=[pl.BlockSpec((MOE_FFN_ROWS, D_MODEL), lambda b, *_: (b, 0)),
                      pl.BlockSpec((MOE_FFN_ROWS, ROUTE_LANES), lambda b, *_: (b, 0)),
                      pl.BlockSpec((1, D_MODEL, MOE_HIDDEN), lambda b, be, nu: (be[b], 0, 0)),
                      pl.BlockSpec((1, D_MODEL, MOE_HIDDEN), lambda b, be, nu: (be[b], 0, 0)),
                      pl.BlockSpec((1, MOE_HIDDEN, D_MODEL), lambda b, be, nu: (be[b], 0, 0))],
            out_specs=pl.BlockSpec((MOE_FFN_ROWS, D_MODEL), lambda b, *_: (b, 0)),
        ),
        out_shape=jax.ShapeDtypeStruct((ng, D_MODEL), BF16),
        compiler_params=_cparams("arbitrary"),
        name="moe_ffn",
    )(bexp, nused, xg, gg, wts["moe_w_gate"], wts["moe_w_up"], wts["moe_w_down"])

    return pl.pallas_call(
        functools.partial(_moe_scatter_body, final_norm),
        grid_spec=pltpu.PrefetchScalarGridSpec(
            num_scalar_prefetch=3,
            grid=(nt,),
            in_specs=[any_spec,
                      pl.BlockSpec((tt, D_MODEL), lambda i, *_: (i, 0)),
                      pl.BlockSpec((tt, ROUTE_LANES), lambda i, *_: (i, 0)),
                      pl.BlockSpec((1, D_MODEL), lambda i, *_: (0, 0))],
            out_specs=pl.BlockSpec((tt, D_MODEL), lambda i, *_: (i, 0)),
            scratch_shapes=[pltpu.VMEM((ns, D_MODEL), BF16), pltpu.SemaphoreType.DMA((1,))],
        ),
        out_shape=jax.ShapeDtypeStruct((t, D_MODEL), F32),
        compiler_params=_cparams("arbitrary"),
        name="moe_scatter",
    )(*sched, og, x1, dcol, norm_final_g.reshape(1, D_MODEL))


def _trunk(x, prm, wts, norm_final_g):
    nseq, seq_len, _ = x.shape
    xf = x.reshape(nseq * seq_len, D_MODEL)
    p_a, p_b, p_g = _norm_in_proj(xf, prm["norm_mix_g"], wts["w_in"], min(256, seq_len))
    rwkv = _rwkv_branch(p_a, nseq, seq_len, prm)
    yb = _hyena_branch(p_b, nseq, seq_len, prm)
    x1, xn, gwt = _merge(xf, rwkv, yb, p_g, prm, wts, min(512, seq_len))
    out = _moe_grouped(xn, gwt, x1, wts, norm_final_g, min(1024, seq_len))
    return out.reshape(nseq, seq_len, D_MODEL)


def kernel(x_prompt, x_sample, norm_mix_g, w_in, rwkv_mu_prev, rwkv_mu_next, rwkv_w0, rwkv_w2, rwkv_a0, rwkv_a2, rwkv_g2, rwkv_k_k, rwkv_k_a, rwkv_r_k, rwkv_ln_g, rwkv_ln_b, hyena_conv_w, hyena_conv_b, hyena_f_w1, hyena_f_b1, hyena_f_w2, hyena_f_b2, hyena_f_w3, hyena_f_freq, hyena_skip, w_up_a, w_up_b, w_out, norm_ffn_g, moe_w_route_group, moe_b_route_group, moe_w_route_expert, moe_b_route_expert, moe_w_gate, moe_w_up, moe_w_down, norm_final_g):
    layer = dict(norm_mix_g=norm_mix_g, w_in=w_in, rwkv_mu_prev=rwkv_mu_prev, rwkv_mu_next=rwkv_mu_next,
                 rwkv_w0=rwkv_w0, rwkv_w2=rwkv_w2, rwkv_a0=rwkv_a0, rwkv_a2=rwkv_a2, rwkv_g2=rwkv_g2,
                 rwkv_k_k=rwkv_k_k, rwkv_k_a=rwkv_k_a, rwkv_r_k=rwkv_r_k, rwkv_ln_g=rwkv_ln_g, rwkv_ln_b=rwkv_ln_b,
                 hyena_conv_w=hyena_conv_w, hyena_conv_b=hyena_conv_b, hyena_f_w1=hyena_f_w1, hyena_f_b1=hyena_f_b1,
                 hyena_f_w2=hyena_f_w2, hyena_f_b2=hyena_f_b2, hyena_f_w3=hyena_f_w3, hyena_f_freq=hyena_f_freq,
                 hyena_skip=hyena_skip, w_up_a=w_up_a, w_up_b=w_up_b, w_out=w_out, norm_ffn_g=norm_ffn_g,
                 moe_w_route_group=moe_w_route_group, moe_b_route_group=moe_b_route_group,
                 moe_w_route_expert=moe_w_route_expert, moe_b_route_expert=moe_b_route_expert,
                 moe_w_gate=moe_w_gate, moe_w_up=moe_w_up, moe_w_down=moe_w_down)
    depth = norm_mix_g.shape[0]
    big = ("w_in", "w_up_a", "w_up_b", "w_out", "moe_w_gate", "moe_w_up", "moe_w_down")

    def trunk(x):
        for li in range(depth):
            prm = {k: v[li] for k, v in layer.items()}
            wts = {k: prm[k].astype(BF16) for k in big}
            last = li == depth - 1
            x = _trunk(x, prm, wts, norm_final_g if last else None)
        return x

    return (trunk(x_prompt), trunk(x_sample))
```

```python
import functools
import math

import jax
import jax.numpy as jnp
import numpy as np
from jax import lax
from jax.experimental import pallas as pl
from jax.experimental.pallas import tpu as pltpu

F32 = jnp.float32
BF16 = jnp.bfloat16
I32 = jnp.int32

D_MODEL = 1024
NORM_EPS = 1e-6
RWKV_HEADS = 8
RWKV_HEAD_DIM = 64
RWKV_WIDTH = RWKV_HEADS * RWKV_HEAD_DIM
RWKV_DECAY_RANK = 64
RWKV_ICLR_RANK = 64
RWKV_GATE_RANK = 128
RWKV_GN_EPS = 64e-5
HYENA_WIDTH = D_MODEL // 2
C_RWKV_IN = 3 * RWKV_WIDTH + 2 * RWKV_DECAY_RANK + 2 * RWKV_ICLR_RANK + RWKV_GATE_RANK
C_HYENA_IN = 3 * HYENA_WIDTH
C_GATES = 2 * D_MODEL

CHUNK = 32
STACK = RWKV_HEADS * CHUNK
HALF = RWKV_WIDTH // 2
GROUP_HEADS = 4
GSTACK = GROUP_HEADS * CHUNK
LOCAL_CHUNKS_PER_ITER = 4

VMEM_LIMIT = 48 * 1024 * 1024

_NN = (((1,), (0,)), ((), ()))
_NT = (((1,), (1,)), ((), ()))
_TN = (((0,), (0,)), ((), ()))


def _dot(a, b, dims=_NN):
    return lax.dot_general(a.astype(BF16), b.astype(BF16), dims, preferred_element_type=F32)


def _split2(x):
    hi = x.astype(BF16)
    lo = (x - hi.astype(F32)).astype(BF16)
    return hi, lo


def _dot3(a, b, dims=_NN):
    ah, al = _split2(a)
    bh, bl = _split2(b)
    dg = functools.partial(lax.dot_general, dimension_numbers=dims, preferred_element_type=F32)
    return dg(ah, bh) + (dg(ah, bl) + dg(al, bh))


def _dot_exact_lhs(a_bf16, x):
    x1 = x.astype(BF16)
    r1 = x - x1.astype(F32)
    x2 = r1.astype(BF16)
    x3 = (r1 - x2.astype(F32)).astype(BF16)
    dg = functools.partial(lax.dot_general, dimension_numbers=_NN, preferred_element_type=F32)
    return dg(a_bf16, x1) + (dg(a_bf16, x2) + dg(a_bf16, x3))


def _seg_sum(x, ones_bd):
    x1, x2 = _split2(x)
    dg = functools.partial(lax.dot_general, dimension_numbers=_NN, preferred_element_type=F32)
    return dg(x1, ones_bd) + dg(x2, ones_bd)


def _cparams(*sem):
    return pltpu.CompilerParams(dimension_semantics=tuple(sem), vmem_limit_bytes=VMEM_LIMIT)


def _norm_in_proj_body(x_ref, g_ref, wa_ref, wb_ref, wg_ref, pa_ref, pb_ref, pg_ref):
    x = x_ref[...]
    xn = x * lax.rsqrt(jnp.mean(x * x, axis=-1, keepdims=True) + NORM_EPS) * g_ref[...]
    xb = xn.astype(BF16)
    pa_ref[...] = jnp.dot(xb, wa_ref[...], preferred_element_type=F32)
    pb_ref[...] = jnp.dot(xb, wb_ref[...], preferred_element_type=F32)
    pg_ref[...] = jnp.dot(xb, wg_ref[...], preferred_element_type=F32)


def _norm_in_proj(x, g, w_in, tm):
    t = x.shape[0]
    wa = w_in[:, :C_RWKV_IN].astype(BF16)
    wb = w_in[:, C_RWKV_IN:C_RWKV_IN + C_HYENA_IN].astype(BF16)
    wg = w_in[:, C_RWKV_IN + C_HYENA_IN:].astype(BF16)
    full = lambda i: (0, 0)
    row = lambda i: (i, 0)
    return pl.pallas_call(
        _norm_in_proj_body,
        grid=(t // tm,),
        in_specs=[pl.BlockSpec((tm, D_MODEL), row), pl.BlockSpec((1, D_MODEL), full),
                  pl.BlockSpec(wa.shape, full), pl.BlockSpec(wb.shape, full), pl.BlockSpec(wg.shape, full)],
        out_specs=[pl.BlockSpec((tm, C_RWKV_IN), row), pl.BlockSpec((tm, C_HYENA_IN), row),
                   pl.BlockSpec((tm, C_GATES), row)],
        out_shape=[jax.ShapeDtypeStruct((t, C_RWKV_IN), F32), jax.ShapeDtypeStruct((t, C_HYENA_IN), F32),
                   jax.ShapeDtypeStruct((t, C_GATES), F32)],
        compiler_params=_cparams("parallel"),
        name="norm_in_proj",
    )(x, g.reshape(1, D_MODEL), wa, wb, wg)


def _shifted(p, prev_blk, next_blk, is_first, is_last):
    tm = p.shape[0]
    row = lax.broadcasted_iota(I32, p.shape, 0)
    prow = jnp.where(is_first, 0.0, prev_blk[7:8, :])
    nrow = jnp.where(is_last, 0.0, next_blk[0:1, :])
    prev = jnp.where(row == 0, prow, pltpu.roll(p, 1, 0))
    nxt = jnp.where(row == tm - 1, nrow, pltpu.roll(p, tm - 1, 0))
    return prev, nxt


def _stack_heads(x, head_mask):
    return jnp.where(head_mask, jnp.concatenate([x] * GROUP_HEADS, axis=0), 0.0)


def _unstack_heads(z):
    out = z[0:CHUNK]
    for h in range(1, GROUP_HEADS):
        out = out + z[h * CHUNK:(h + 1) * CHUNK]
    return out


def _chunks_local(chains):
    ngrp = RWKV_HEADS // GROUP_HEADS
    ti = lax.broadcasted_iota(I32, (CHUNK, CHUNK), 0)
    si = lax.broadcasted_iota(I32, (CHUNK, CHUNK), 1)
    srow = lax.broadcasted_iota(I32, (GSTACK, HALF), 0)
    slane = lax.broadcasted_iota(I32, (GSTACK, HALF), 1)
    head_mask = (srow >> 5) == (slane >> 6)
    mrow = lax.broadcasted_iota(I32, (GSTACK, GSTACK), 0)
    mcol = lax.broadcasted_iota(I32, (GSTACK, GSTACK), 1)
    same = (mrow >> 5) == (mcol >> 5)
    masks = {False: (same & (mrow > mcol), same & (mrow >= mcol)),
             True: (same & (mrow < mcol), same & (mrow <= mcol))}
    tris = {False: (ti >= si).astype(BF16), True: (ti <= si).astype(BF16)}

    cls = [_dot_exact_lhs(tris[rev], lw) for (_, _, _, lw, _, _, rev) in chains]
    st = []
    for (r, v, kk, lw, k, a, rev), cl in zip(chains, cls):
        tot = cl[0:1] if rev else cl[CHUNK - 1:CHUNK]
        e_neg = jnp.exp(-cl)
        e_tail = jnp.exp(tot - cl)
        beta = kk * a
        st.append(dict(rev=rev, v=v, alpha_b=-kk * jnp.exp(cl - lw), r_b=r * jnp.exp(cl), beta_b=beta * e_neg,
                       k_b=k * e_neg, bt=beta * e_tail, kt=k * e_tail, pc=jnp.exp(tot)))

    for s in st:
        s["sa"], s["sv"], s["lhs"], s["rhs"] = [], [], [], []
        for g in range(ngrp):
            lanes = slice(g * HALF, (g + 1) * HALF)
            sa_ = _stack_heads(s["alpha_b"][:, lanes], head_mask).astype(BF16)
            sr_ = _stack_heads(s["r_b"][:, lanes], head_mask).astype(BF16)
            s["sa"].append(sa_)
            s["sv"].append(_stack_heads(s["v"][:, lanes], head_mask).astype(BF16))
            s["lhs"].append(jnp.concatenate([sa_, sr_], axis=0))
            s["rhs"].append(jnp.concatenate([s["beta_b"][:, lanes].astype(BF16)] * GROUP_HEADS
                                            + [s["k_b"][:, lanes].astype(BF16)] * GROUP_HEADS, axis=0))
    pms = [[lax.dot_general(s["lhs"][g], s["rhs"][g], _NT, preferred_element_type=F32) for g in range(ngrp)]
           for s in st]
    zero = jnp.zeros((GSTACK, GSTACK), F32)
    drow = lax.broadcasted_iota(I32, (STACK, STACK), 0)
    dcol = lax.broadcasted_iota(I32, (STACK, STACK), 1)
    eye = jnp.where(drow == dcol, 1.0, 0.0)
    for s, pm in zip(st, pms):
        strict, incl = masks[s["rev"]]
        s["ak"] = [jnp.where(strict, pm[g][:GSTACK, GSTACK:], 0.0).astype(BF16) for g in range(ngrp)]
        s["rb"] = [jnp.where(incl, pm[g][GSTACK:, :GSTACK], 0.0).astype(BF16) for g in range(ngrp)]
        s["rk"] = [jnp.where(incl, pm[g][GSTACK:, GSTACK:], 0.0).astype(BF16) for g in range(ngrp)]
        ab = [jnp.where(strict, pm[g][:GSTACK, :GSTACK], 0.0) for g in range(ngrp)]
        s["apow"] = jnp.concatenate([jnp.concatenate([ab[0], zero], axis=1),
                                     jnp.concatenate([zero, ab[1]], axis=1)], axis=0)
        s["tinv"] = eye + s["apow"]

    for _ in range(int(math.log2(CHUNK)) - 1):
        sq = [_dot(s["apow"], s["apow"]) for s in st]
        for s, x in zip(st, sq):
            s["apow"] = x
        pr = [_dot(s["tinv"], s["apow"]) for s in st]
        for s, x in zip(st, pr):
            s["tinv"] = s["tinv"] + x

    akv = [[_dot(s["ak"][g], s["sv"][g]) for g in range(ngrp)] for s in st]
    rkv = [[_dot(s["rk"][g], s["sv"][g]) for g in range(ngrp)] for s in st]
    tw = [[_dot(s["tinv"][g * GSTACK:(g + 1) * GSTACK, g * GSTACK:(g + 1) * GSTACK],
                jnp.concatenate([s["sa"][g].astype(F32), akv[c][g]], axis=1)) for g in range(ngrp)]
          for c, s in enumerate(st)]
    rbx = [[_dot(s["rb"][g], tw[c][g]) for g in range(ngrp)] for c, s in enumerate(st)]
    outs = []
    for c, s in enumerate(st):
        cat = lambda f: jnp.concatenate([f(g) for g in range(ngrp)], axis=1)
        qt = s["r_b"] + cat(lambda g: _unstack_heads(rbx[c][g][:, :HALF]))
        wt = cat(lambda g: _unstack_heads(tw[c][g][:, :HALF]))
        yloc = cat(lambda g: _unstack_heads(rbx[c][g][:, HALF:] + rkv[c][g]))
        u = cat(lambda g: _unstack_heads(tw[c][g][:, HALF:]))
        outs.append((qt, wt, yloc, u, s["bt"], s["kt"], s["pc"]))
    return outs


def _rwkv_local_body(nt_seq, p_ref, pp_ref, pn_ref, mup_ref, mun_ref, w0_ref, w2_ref, a0_ref, a2_ref, g2_ref,
                     kk_ref, ka_ref, rk_ref, ones_ref,
                     g_out, bonus_out, v_out, qwf, yuf, bkf, pcf, qwr, yur, bkr, pcr,
                     r_s, v_s, kk_s, lw_s, k_s, a_s):
    i = pl.program_id(0) % nt_seq
    p = p_ref[...]
    tm = p.shape[0]
    prev, nxt = _shifted(p, pp_ref[...], pn_ref[...], i == 0, i == nt_seq - 1)
    ps = p + mup_ref[...] * (prev - p) + mun_ref[...] * (nxt - p)
    c = RWKV_WIDTH
    r = ps[:, 0:c]
    k = ps[:, c:2 * c]
    v = ps[:, 2 * c:3 * c]
    lw = ps[:, 3 * c:3 * c + 128]
    la = ps[:, 3 * c + 128:3 * c + 256]
    lg = ps[:, 3 * c + 256:3 * c + 384]
    ones_bd = ones_ref[...]
    w_raw = w0_ref[...] + _dot3(jnp.tanh(lw), w2_ref[...])
    logw = -jnp.exp(-jnp.logaddexp(-w_raw, 0.0) - 0.5)
    a = jax.nn.sigmoid(a0_ref[...] + _dot3(la, a2_ref[...]))
    g_out[...] = _dot3(jax.nn.sigmoid(lg), g2_ref[...])
    kk0 = k * kk_ref[...]
    kk = kk0 * lax.rsqrt(jnp.maximum(_seg_sum(kk0 * kk0, ones_bd), 1e-24))
    ka = ka_ref[...]
    kdir_f = k * (1.0 + (a[:, :c] - 1.0) * ka)
    kdir_r = k * (1.0 + (a[:, c:] - 1.0) * ka)
    bonus_out[...] = _seg_sum(r * ((kdir_f + kdir_r) * 0.5) * rk_ref[...], ones_bd) * v
    v_out[...] = v
    r_s[...] = r
    v_s[...] = v
    kk_s[...] = kk
    lw_s[...] = logw
    k_s[:, :c] = kdir_f
    k_s[:, c:] = kdir_r
    a_s[...] = a

    out_refs = ((qwf, yuf, bkf, pcf), (qwr, yur, bkr, pcr))

    def chunk(cj, carry):
        chains, where = [], []
        for uu in range(LOCAL_CHUNKS_PER_ITER):
            ci = cj * LOCAL_CHUNKS_PER_ITER + uu
            rows = pl.ds(pl.multiple_of(ci * CHUNK, CHUNK), CHUNK)
            rc, vc, kkc = r_s[rows, :], v_s[rows, :], kk_s[rows, :]
            for d in range(2):
                lanes = slice(d * c, (d + 1) * c)
                chains.append((rc, vc, kkc, lw_s[rows, lanes], k_s[rows, lanes], a_s[rows, lanes], d == 1))
                where.append((ci, d))
        for (ci, d), (qt, wt, yloc, u, bt, kt, pcv) in zip(where, _chunks_local(chains)):
            qw, yu, bk, pc = out_refs[d]
            qw[ci] = jnp.concatenate([qt, wt], axis=0).astype(BF16)
            yu[ci] = jnp.concatenate([yloc, u], axis=0)
            bk[ci] = jnp.concatenate([bt, kt], axis=0).astype(BF16)
            pc[ci] = jnp.broadcast_to(pcv, (8, c))
        return carry

    lax.fori_loop(0, tm // (CHUNK * LOCAL_CHUNKS_PER_ITER), chunk, 0)


def _rwkv_local(p_a, nt_seq, tm, prm):
    t = p_a.shape[0]
    nt = t // tm
    nc = tm // CHUNK
    c = RWKV_WIDTH
    rd, ri = RWKV_DECAY_RANK, RWKV_ICLR_RANK
    z = jnp.zeros((rd, c), F32)
    w2pad = jnp.concatenate([jnp.concatenate([prm["rwkv_w2"][0], z], 0), jnp.concatenate([z, prm["rwkv_w2"][1]], 0)], 1)
    z = jnp.zeros((ri, c), F32)
    a2pad = jnp.concatenate([jnp.concatenate([prm["rwkv_a2"][0], z], 0), jnp.concatenate([z, prm["rwkv_a2"][1]], 0)], 1)
    hd = np.arange(c) // RWKV_HEAD_DIM
    ones_bd = jnp.asarray(hd[:, None] == hd[None, :], BF16)
    full = lambda i: (0, 0)
    row = lambda i: (i, 0)
    row3 = lambda i: (i, 0, 0)
    nblk8 = t // 8
    tm8 = tm // 8
    vec = lambda n: pl.BlockSpec((1, n), full)
    chunk_out = lambda rows, dt: (pl.BlockSpec((nc, rows, c), row3), jax.ShapeDtypeStruct((t // CHUNK, rows, c), dt))
    per_dir = [chunk_out(2 * CHUNK, BF16), chunk_out(2 * CHUNK, F32), chunk_out(2 * CHUNK, BF16), chunk_out(8, F32)]
    outs = [(pl.BlockSpec((tm, c), row), jax.ShapeDtypeStruct((t, c), F32))] * 3 + per_dir + per_dir
    return pl.pallas_call(
        functools.partial(_rwkv_local_body, nt_seq),
        grid=(nt,),
        in_specs=[pl.BlockSpec((tm, C_RWKV_IN), row),
                  pl.BlockSpec((8, C_RWKV_IN), lambda i: (jnp.maximum(i * tm8 - 1, 0), 0)),
                  pl.BlockSpec((8, C_RWKV_IN), lambda i: (jnp.minimum((i + 1) * tm8, nblk8 - 1), 0)),
                  vec(C_RWKV_IN), vec(C_RWKV_IN), vec(2 * c), pl.BlockSpec((128, 2 * c), full),
                  vec(2 * c), pl.BlockSpec((128, 2 * c), full), pl.BlockSpec((RWKV_GATE_RANK, c), full),
                  vec(c), vec(c), vec(c), pl.BlockSpec((c, c), full)],
        out_specs=[o[0] for o in outs],
        out_shape=[o[1] for o in outs],
        scratch_shapes=[pltpu.VMEM((tm, c), F32)] * 3 + [pltpu.VMEM((tm, 2 * c), F32)] * 3,
        compiler_params=_cparams("parallel"),
        name="rwkv_local",
    )(p_a, p_a, p_a, prm["rwkv_mu_prev"].reshape(1, -1), prm["rwkv_mu_next"].reshape(1, -1),
      prm["rwkv_w0"].reshape(1, 2 * c), w2pad, prm["rwkv_a0"].reshape(1, 2 * c), a2pad, prm["rwkv_g2"],
      prm["rwkv_k_k"].reshape(1, c), prm["rwkv_k_a"].reshape(1, c), prm["rwkv_r_k"].reshape(1, c), ones_bd)


SCAN_TILE = 128


def _rwkv_scan_body(nc, nseq, qwf, yuf, bkf, pcf, vf, qwr, yur, bkr, pcr, vr, yf_ref, yr_ref, s_ref):
    @pl.when(pl.program_id(0) == 0)
    def _():
        s_ref[...] = jnp.zeros(s_ref.shape, F32)

    brow = lax.broadcasted_iota(I32, (HALF, HALF), 0)
    bcol = lax.broadcasted_iota(I32, (HALF, HALF), 1)
    same_head = (brow >> 6) == (bcol >> 6)

    dirs = ((qwf, yuf, bkf, pcf, vf, yf_ref), (qwr, yur, bkr, pcr, vr, yr_ref))

    def step(j, carry):
        ch = []
        for b in range(nseq):
            for d, (qw_ref, yu_ref, bk_ref, pc_ref, v_ref, y_ref) in enumerate(dirs):
                ci = j if d == 0 else nc - 1 - j
                rows = pl.ds(pl.multiple_of(ci * CHUNK, CHUNK), CHUNK)
                qw, yu, bk = qw_ref[b, ci], yu_ref[b, ci], bk_ref[b, ci]
                pc, vc = pc_ref[b, ci][0:1], v_ref[b, rows, :]
                for hf in range(2):
                    lanes = slice(hf * HALF, (hf + 1) * HALF)
                    ch.append(dict(sidx=(b, d, hf), b=b, rows=rows, lanes=lanes, y_ref=y_ref, qw=qw[:, lanes],
                                   yu=yu[:, lanes], bk=bk[:, lanes], pc=pc[:, lanes], vc=vc[:, lanes]))
        for c in ch:
            c["s"] = s_ref[c["sidx"]]
        ys = [lax.dot_general(c["qw"], c["s"].astype(BF16), _NT, preferred_element_type=F32) + c["yu"] for c in ch]
        for c, y in zip(ch, ys):
            c["y_ref"][c["b"], c["rows"], c["lanes"]] = y[:CHUNK]
        sav = [jnp.concatenate([y[CHUNK:], c["vc"]], axis=0).astype(BF16) for c, y in zip(ch, ys)]
        upd = [lax.dot_general(x, c["bk"], _TN, preferred_element_type=F32) for c, x in zip(ch, sav)]
        for c, u in zip(ch, upd):
            s_ref[c["sidx"]] = c["s"] * c["pc"] + jnp.where(same_head, u, 0.0)
        return carry

    lax.fori_loop(0, nc, step, 0)


def _rwkv_scan(loc, nseq, seq_len):
    _, _, v, qwf, yuf, bkf, pcf, qwr, yur, bkr, pcr = loc
    tm = min(SCAN_TILE, seq_len)
    nt_seq = seq_len // tm
    nc = tm // CHUNK
    c = RWKV_WIDTH
    ncs = seq_len // CHUNK
    fwd = lambda i: (0, i, 0, 0)
    rev = lambda i: (0, nt_seq - 1 - i, 0, 0)
    fwd_v = lambda i: (0, i, 0)
    rev_v = lambda i: (0, nt_seq - 1 - i, 0)

    def specs(m4, m3):
        return [pl.BlockSpec((nseq, nc, 2 * CHUNK, c), m4), pl.BlockSpec((nseq, nc, 2 * CHUNK, c), m4),
                pl.BlockSpec((nseq, nc, 2 * CHUNK, c), m4), pl.BlockSpec((nseq, nc, 8, c), m4),
                pl.BlockSpec((nseq, tm, c), m3)]

    per_seq = lambda a: a.reshape((nseq, ncs) + a.shape[1:])
    v3 = v.reshape(nseq, seq_len, c)
    yf, yr = pl.pallas_call(
        functools.partial(_rwkv_scan_body, nc, nseq),
        grid=(nt_seq,),
        in_specs=specs(fwd, fwd_v) + specs(rev, rev_v),
        out_specs=[pl.BlockSpec((nseq, tm, c), fwd_v), pl.BlockSpec((nseq, tm, c), rev_v)],
        out_shape=[jax.ShapeDtypeStruct((nseq, seq_len, c), F32)] * 2,
        scratch_shapes=[pltpu.VMEM((nseq, 2, 2, HALF, HALF), F32)],
        compiler_params=_cparams("arbitrary"),
        name="rwkv_scan",
    )(per_seq(qwf), per_seq(yuf), per_seq(bkf), per_seq(pcf), v3,
      per_seq(qwr), per_seq(yur), per_seq(bkr), per_seq(pcr), v3)
    return yf.reshape(nseq * seq_len, c), yr.reshape(nseq * seq_len, c)


def _rwkv_out(yf, yr, g, bonus, ln_g, ln_b, ones_bd):
    y = yf + yr
    inv_n = 1.0 / RWKV_HEAD_DIM
    mean = _seg_sum(y, ones_bd) * inv_n
    yc = y - mean
    var = _seg_sum(yc * yc, ones_bd) * inv_n
    yn = yc * lax.rsqrt(var + RWKV_GN_EPS) * ln_g + ln_b
    return (yn + bonus) * g


def _rwkv_branch(p_a, nseq, seq_len, prm):
    tm = min(256, seq_len)
    nt_seq = seq_len // tm
    loc = _rwkv_local(p_a, nt_seq, tm, prm)
    yf, yr = _rwkv_scan(loc, nseq, seq_len)
    return yf, yr, loc[0], loc[1]


FFT_N2 = 128
FILTER_EMB_DIM = 33
FILTER_BANDS = (FILTER_EMB_DIM - 1) // 2
FILTER_HIDDEN = 64
FILTER_TARGET = 1e-2
FILTER_FAST_DECAY_PCT = 0.3
FILTER_SLOW_DECAY_PCT = 1.5


def _hyena_prep_body(nt_seq, p_ref, pp_ref, pn_ref, cw_ref, cb_ref, src_ref, x2_ref):
    i = pl.program_id(0) % nt_seq
    p = p_ref[...]
    prev, nxt = _shifted(p, pp_ref[...], pn_ref[...], i == 0, i == nt_seq - 1)
    cw = cw_ref[...]
    u = prev * cw[0:1] + p * cw[1:2] + nxt * cw[2:3] + cb_ref[...]
    c = HYENA_WIDTH
    src_ref[...] = u[:, 0:c] * u[:, 2 * c:3 * c]
    x2_ref[...] = u[:, c:2 * c]


def _hyena_prep(p_b, nt_seq, tm, prm):
    t = p_b.shape[0]
    c = HYENA_WIDTH
    row = lambda i: (i, 0)
    full = lambda i: (0, 0)
    nblk8 = t // 8
    tm8 = tm // 8
    cw = jnp.concatenate([prm["hyena_conv_w"], jnp.zeros((5, C_HYENA_IN), F32)], axis=0)
    return pl.pallas_call(
        functools.partial(_hyena_prep_body, nt_seq),
        grid=(t // tm,),
        in_specs=[pl.BlockSpec((tm, C_HYENA_IN), row),
                  pl.BlockSpec((8, C_HYENA_IN), lambda i: (jnp.maximum(i * tm8 - 1, 0), 0)),
                  pl.BlockSpec((8, C_HYENA_IN), lambda i: (jnp.minimum((i + 1) * tm8, nblk8 - 1), 0)),
                  pl.BlockSpec((8, C_HYENA_IN), full), pl.BlockSpec((1, C_HYENA_IN), full)],
        out_specs=[pl.BlockSpec((tm, c), row)] * 2,
        out_shape=[jax.ShapeDtypeStruct((t, c), F32)] * 2,
        compiler_params=_cparams("parallel"),
        name="hyena_prep",
    )(p_b, p_b, p_b, cw, prm["hyena_conv_b"].reshape(1, -1))


def _hyena_filter_body(seq_len, w1_ref, b1_ref, w2_ref, b2_ref, w3_ref, freq_ref, delta_ref, o_ref):
    rows = o_ref.shape[0]
    n = pl.program_id(0) * rows + lax.broadcasted_iota(I32, (rows, 128), 0)
    lane = lax.broadcasted_iota(I32, (rows, 128), 1)
    pos = jnp.where(n < seq_len, n, 2 * seq_len - n).astype(F32)
    t = pos * (1.0 / (seq_len - 1))
    omega = (2.0 * math.pi) * pos / seq_len
    band_step = (FILTER_BANDS - 1 - 1e-4) / (FILTER_BANDS - 1)
    band_idx = jnp.where(lane <= FILTER_BANDS, lane - 1, lane - 1 - FILTER_BANDS).astype(F32)
    arg = (1e-4 + band_idx * band_step) * omega
    z = jnp.where(lane == 0, t, jnp.where(lane <= FILTER_BANDS, jnp.cos(arg),
                                          jnp.where(lane <= 2 * FILTER_BANDS, -jnp.sin(arg), 0.0)))
    freq = freq_ref[...]
    hid = jnp.sin(freq * (_dot3(z, w1_ref[...]) + b1_ref[...]))
    hid = jnp.sin(freq * (_dot3(hid, w2_ref[...]) + b2_ref[...]))
    filt = _dot3(hid, w3_ref[...])
    nn = n[:, 0:1]
    tt = t[:, 0:1]
    sel = jnp.where(nn < seq_len, filt[:, :HYENA_WIDTH], jnp.where(nn > seq_len, filt[:, HYENA_WIDTH:], 0.0))
    o_ref[...] = sel * jnp.exp(-tt * delta_ref[...])


def _hyena_filter(seq_len, prm):
    rows = min(1024, 2 * seq_len)
    c = HYENA_WIDTH
    fh = FILTER_HIDDEN
    w1 = jnp.concatenate([prm["hyena_f_w1"], jnp.zeros((128 - FILTER_EMB_DIM, fh), F32)], axis=0)
    min_decay = math.log(FILTER_TARGET) / FILTER_SLOW_DECAY_PCT
    max_decay = math.log(FILTER_TARGET) / FILTER_FAST_DECAY_PCT
    deltas = jnp.abs(jnp.linspace(min_decay, max_decay, c, dtype=F32)).reshape(1, c)
    full = lambda i: (0, 0)
    return pl.pallas_call(
        functools.partial(_hyena_filter_body, seq_len),
        grid=(2 * seq_len // rows,),
        in_specs=[pl.BlockSpec((128, fh), full), pl.BlockSpec((1, fh), full), pl.BlockSpec((fh, fh), full),
                  pl.BlockSpec((1, fh), full), pl.BlockSpec((fh, 2 * c), full), pl.BlockSpec((1, fh), full),
                  pl.BlockSpec((1, c), full)],
        out_specs=pl.BlockSpec((rows, c), lambda i: (i, 0)),
        out_shape=jax.ShapeDtypeStruct((2 * seq_len, c), F32),
        compiler_params=_cparams("parallel"),
        name="hyena_filter",
    )(w1, prm["hyena_f_b1"].reshape(1, fh), prm["hyena_f_w2"], prm["hyena_f_b2"].reshape(1, fh),
      prm["hyena_f_w3"], prm["hyena_f_freq"].reshape(1, fh), deltas)


FFT_N2_TILE = 16
FFT_LANE_TILE = 256


def _lmul_rows(m_ref, x_ref):
    m = m_ref[...]
    xt = pltpu.einshape("rjc->jrc", x_ref[0].astype(F32))
    ys = [jnp.dot(m, xt[j].astype(BF16), preferred_element_type=F32) for j in range(FFT_N2_TILE)]
    return pltpu.einshape("jrc->rjc", jnp.stack(ys, axis=0))


def _lmul_body(m_ref, x_ref, o_ref):
    o_ref[0] = _lmul_rows(m_ref, x_ref).astype(o_ref.dtype)


def _lmul_epilogue_body(m_ref, x_ref, src_ref, x2_ref, skip_ref, o_ref):
    o_ref[0] = x2_ref[0] * (_lmul_rows(m_ref, x_ref) + src_ref[0] * skip_ref[...])


def _lmul(m, x, extra=None):
    nfft, r_in, n2, c = x.shape
    r_out = m.shape[0]
    ct = FFT_LANE_TILE
    xs = pl.BlockSpec((1, r_in, FFT_N2_TILE, ct), lambda f, j, l: (f, 0, j, l))
    os_ = pl.BlockSpec((1, r_out, FFT_N2_TILE, ct), lambda f, j, l: (f, 0, j, l))
    ms = pl.BlockSpec(m.shape, lambda f, j, l: (0, 0))
    if extra is None:
        body, ins, specs, out_dtype = _lmul_body, (m, x), [ms, xs], BF16
    else:
        src, x2, skip = extra
        body, ins, out_dtype = _lmul_epilogue_body, (m, x, src, x2, skip), F32
        specs = [ms, xs, os_, os_, pl.BlockSpec((1, ct), lambda f, j, l: (0, l))]
    return pl.pallas_call(
        body,
        grid=(nfft, n2 // FFT_N2_TILE, c // ct),
        in_specs=specs,
        out_specs=os_,
        out_shape=jax.ShapeDtypeStruct((nfft, r_out, n2, c), out_dtype),
        compiler_params=_cparams("parallel", "parallel", "parallel"),
        name="fft_outer" if extra is None else "fft_outer_out",
    )(*ins)


def _fft_inner_body(kt, conv, g_ref, y_ref, *rest):
    if conv:
        h_ref, o_ref = rest
    else:
        (o_ref,) = rest
    n2 = FFT_N2
    for q in range(kt):
        g = g_ref[q]
        yk = jnp.concatenate([y_ref[0, 0, q], y_ref[0, 1, q]], axis=0).astype(BF16)
        z = jnp.dot(g, yk, preferred_element_type=F32)
        if conv:
            zr, zi = z[:n2], z[n2:]
            hr, hi = h_ref[0, 0, q], h_ref[0, 1, q]
            pr = zr * hr - zi * hi
            pi = zr * hi + zi * hr
            prod = jnp.concatenate([pr, pi], axis=0).astype(BF16)
            z = lax.dot_general(g, prod, _TN, preferred_element_type=F32)
        o_ref[0, 0, q] = z[:n2].astype(o_ref.dtype)
        o_ref[0, 1, q] = z[n2:].astype(o_ref.dtype)


def _fft_inner(g, y, h, kt):
    nfft, _, n1, n2, c = y.shape
    blk = pl.BlockSpec((1, 2, kt, n2, c), lambda f, j: (f, 0, j, 0, 0))
    gs = pl.BlockSpec((kt, 2 * n2, 2 * n2), lambda f, j: (j, 0, 0))
    conv = h is not None
    ins = (g, y, h) if conv else (g, y)
    specs = [gs, blk, pl.BlockSpec((1, 2, kt, n2, c), lambda f, j: (0, 0, j, 0, 0))] if conv else [gs, blk]
    return pl.pallas_call(
        functools.partial(_fft_inner_body, kt, conv),
        grid=(nfft, n1 // kt),
        in_specs=specs,
        out_specs=blk,
        out_shape=jax.ShapeDtypeStruct(y.shape, BF16 if conv else F32),
        compiler_params=_cparams("parallel", "parallel"),
        name="fft_inner_conv" if conv else "fft_inner",
    )(*ins)


def _dft_tables(seq_len):
    n = 2 * seq_len
    n2 = FFT_N2
    n1 = n // n2
    k1 = jnp.arange(n1, dtype=I32)
    ang1 = (2.0 * math.pi / n1) * ((k1[:, None] * k1[None, :]) % n1).astype(F32)
    c1, s1 = jnp.cos(ang1), jnp.sin(ang1)
    half = n1 // 2
    f_pair = jnp.concatenate([jnp.concatenate([c1[:, :half], s1[:, :half]], 1),
                              jnp.concatenate([-s1[:, :half], c1[:, :half]], 1)], 0)
    f_real_half = jnp.concatenate([c1[:, :half], -s1[:, :half]], 0)
    f_real_full = jnp.concatenate([c1, -s1], 0)
    ci, si = c1[:half] / n, s1[:half] / n
    b_real = jnp.concatenate([ci, -si], 1)
    b_pair = jnp.concatenate([b_real, jnp.concatenate([si, ci], 1)], 0)
    kk = k1[:, None, None] + n1 * jnp.arange(n2, dtype=I32)[None, :, None]
    ang2 = (2.0 * math.pi / n) * ((kk * jnp.arange(n2, dtype=I32)[None, None, :]) % n).astype(F32)
    c2, s2 = jnp.cos(ang2), jnp.sin(ang2)
    g = jnp.concatenate([jnp.concatenate([c2, s2], 2), jnp.concatenate([-s2, c2], 2)], 1)
    cast = lambda a: a.astype(BF16)
    return dict(f_pair=cast(f_pair), f_real_half=cast(f_real_half), f_real_full=cast(f_real_full),
                b_pair=cast(b_pair), b_real=cast(b_real), g=cast(g), n1=n1)


def _hyena_branch(p_b, nseq, seq_len, prm):
    tm = min(512, seq_len)
    c = HYENA_WIDTH
    src, x2 = _hyena_prep(p_b, seq_len // tm, tm, prm)
    tab = _dft_tables(seq_len)
    n1 = tab["n1"]
    n2 = FFT_N2
    kt = 4
    filt = _hyena_filter(seq_len, prm)
    hy = _lmul(tab["f_real_full"], filt.reshape(1, n1, n2, c))
    hspec = _fft_inner(tab["g"], hy.reshape(1, 2, n1, n2, c), None, kt)
    pair = nseq % 2 == 0
    nfft = nseq // 2 if pair else nseq
    rows = n1 if pair else n1 // 2
    xin = src.reshape(nfft, rows, n2, c)
    y = _lmul(tab["f_pair"] if pair else tab["f_real_half"], xin)
    w = _fft_inner(tab["g"], y.reshape(nfft, 2, n1, n2, c), hspec, kt)
    out = _lmul(tab["b_pair"] if pair else tab["b_real"], w.reshape(nfft, 2 * n1, n2, c),
                extra=(xin, x2.reshape(nfft, rows, n2, c), prm["hyena_skip"].reshape(1, c)))
    return out.reshape(nseq * seq_len, c)


MOE_GROUPS = 4
MOE_EXPERTS_PER_GROUP = 8
MOE_EXPERTS = MOE_GROUPS * MOE_EXPERTS_PER_GROUP
MOE_HIDDEN = D_MODEL // 2
ROUTE_LANES = 128


def _route(logits):
    lane = lax.broadcasted_iota(I32, logits.shape, 1)
    neg = -jnp.inf
    big = ROUTE_LANES
    gl = jnp.where(lane < MOE_GROUPS, logits, neg)
    gmax = jnp.max(gl, axis=-1, keepdims=True)
    grp = jnp.min(jnp.where(gl == gmax, lane, big), axis=-1, keepdims=True)
    p_grp = 1.0 / jnp.sum(jnp.exp(gl - gmax), axis=-1, keepdims=True)
    lo = MOE_GROUPS + grp * MOE_EXPERTS_PER_GROUP
    el = jnp.where((lane >= lo) & (lane < lo + MOE_EXPERTS_PER_GROUP), logits, neg)
    m1 = jnp.max(el, axis=-1, keepdims=True)
    i1 = jnp.min(jnp.where(el == m1, lane, big), axis=-1, keepdims=True)
    el2 = jnp.where(lane == i1, neg, el)
    m2 = jnp.max(el2, axis=-1, keepdims=True)
    i2 = jnp.min(jnp.where(el2 == m2, lane, big), axis=-1, keepdims=True)
    e2 = jnp.exp(m2 - m1)
    g1 = p_grp / (1.0 + e2)
    g2 = p_grp * e2 / (1.0 + e2)
    gates = jnp.where(lane == i1, g1, jnp.where(lane == i2, g2, 0.0))
    return pltpu.roll(gates, ROUTE_LANES - MOE_GROUPS, 1)


def _merge_body(x_ref, yf_ref, yr_ref, rg_ref, bonus_ref, lng_ref, lnb_ref, ones_ref, yb_ref, pg_ref,
                wua_ref, wub_ref, wo_ref, g_ref, wr_ref, br_ref, x1_ref, xn_ref, gwt_ref):
    ya = _rwkv_out(yf_ref[...], yr_ref[...], rg_ref[...], bonus_ref[...], lng_ref[...], lnb_ref[...], ones_ref[...])
    ga = jax.nn.sigmoid(pg_ref[:, :D_MODEL])
    gb = jax.nn.sigmoid(pg_ref[:, D_MODEL:])
    merged = ga * _dot(ya, wua_ref[...]) + gb * _dot(yb_ref[...], wub_ref[...])
    x1 = x_ref[...] + _dot(merged, wo_ref[...])
    x1_ref[...] = x1
    xn = x1 * lax.rsqrt(jnp.mean(x1 * x1, axis=-1, keepdims=True) + NORM_EPS) * g_ref[...]
    xn_ref[...] = xn.astype(BF16)
    logits = _dot3(xn, wr_ref[...]) + br_ref[...]
    gwt_ref[...] = _route(logits).T


def _merge(x, rwkv, yb, pg, prm, wts, tm):
    t = x.shape[0]
    c = RWKV_WIDTH
    row = lambda i: (i, 0)
    full = lambda i: (0, 0)
    hd = np.arange(c) // RWKV_HEAD_DIM
    ones_bd = jnp.asarray(hd[:, None] == hd[None, :], BF16)
    wr = jnp.concatenate([prm["moe_w_route_group"], prm["moe_w_route_expert"],
                          jnp.zeros((D_MODEL, ROUTE_LANES - MOE_GROUPS - MOE_EXPERTS), F32)], axis=1)
    br = jnp.concatenate([prm["moe_b_route_group"], prm["moe_b_route_expert"],
                          jnp.zeros((ROUTE_LANES - MOE_GROUPS - MOE_EXPERTS,), F32)]).reshape(1, ROUTE_LANES)
    return pl.pallas_call(
        _merge_body,
        grid=(t // tm,),
        in_specs=[pl.BlockSpec((tm, D_MODEL), row)] + [pl.BlockSpec((tm, c), row)] * 4
                 + [pl.BlockSpec((1, c), full)] * 2 + [pl.BlockSpec((c, c), full),
                  pl.BlockSpec((tm, HYENA_WIDTH), row), pl.BlockSpec((tm, C_GATES), row),
                  pl.BlockSpec((RWKV_WIDTH, D_MODEL), full), pl.BlockSpec((HYENA_WIDTH, D_MODEL), full),
                  pl.BlockSpec((D_MODEL, D_MODEL), full), pl.BlockSpec((1, D_MODEL), full),
                  pl.BlockSpec((D_MODEL, ROUTE_LANES), full), pl.BlockSpec((1, ROUTE_LANES), full)],
        out_specs=[pl.BlockSpec((tm, D_MODEL), row), pl.BlockSpec((tm, D_MODEL), row),
                   pl.BlockSpec((ROUTE_LANES, tm), lambda i: (0, i))],
        out_shape=[jax.ShapeDtypeStruct((t, D_MODEL), F32), jax.ShapeDtypeStruct((t, D_MODEL), BF16),
                   jax.ShapeDtypeStruct((ROUTE_LANES, t), F32)],
        compiler_params=_cparams("parallel"),
        name="merge_route",
    )(x, *rwkv, prm["rwkv_ln_g"].reshape(1, c), prm["rwkv_ln_b"].reshape(1, c), ones_bd, yb, pg,
      wts["w_up_a"], wts["w_up_b"], wts["w_out"], prm["norm_ffn_g"].reshape(1, D_MODEL), wr, br)


MOE_PAD = 16
MOE_BLK = 256
MOE_FFN_ROWS = 256


def _moe_slots(tt):
    ns = 2 * tt + MOE_EXPERTS * MOE_PAD
    return (ns + MOE_BLK - 1) // MOE_BLK * MOE_BLK


def _moe_run_copies(i, lstart_ref, npiece_ref, goff_ref, local_refs, global_refs, sems, to_global, wait):
    for e in range(MOE_EXPERTS):
        ls = lstart_ref[i * MOE_EXPERTS + e]
        go = goff_ref[i * MOE_EXPERTS + e]

        def piece(p, carry, ls=ls, go=go):
            lrows = pl.ds(pl.multiple_of(ls + p * MOE_PAD, MOE_PAD), MOE_PAD)
            grows = pl.ds(pl.multiple_of(go + p * MOE_PAD, MOE_PAD), MOE_PAD)
            for loc, glob, sem in zip(local_refs, global_refs, sems):
                src, dst = (loc.at[lrows, :], glob.at[grows, :]) if to_global else (glob.at[grows, :], loc.at[lrows, :])
                cp = pltpu.make_async_copy(src, dst, sem)
                if wait:
                    cp.wait()
                else:
                    cp.start()
            return carry

        lax.fori_loop(0, npiece_ref[i * MOE_EXPERTS + e], piece, 0)


def _moe_compact_body(lstart_ref, npiece_ref, goff_ref, xn_ref, gwt_ref, tri_ref, pcol_ref, xg0_ref, gg0_ref,
                      xg_out, gg_out, dcol_ref, xg_loc, gg_loc, sems):
    del xg0_ref, gg0_ref
    i = pl.program_id(0)
    tt = xn_ref.shape[0]
    ns = _moe_slots(tt)
    gw = gwt_ref[0:MOE_EXPERTS, :]
    sel = gw > 0.0
    rank = jnp.dot(sel.astype(BF16), tri_ref[...], preferred_element_type=F32)
    dest = pcol_ref[:, 0:1] + rank
    d_lo = jnp.min(jnp.where(sel, dest, 1e9), axis=0, keepdims=True)
    d_hi = jnp.max(jnp.where(sel, dest, -1.0), axis=0, keepdims=True)
    g_lo = jnp.sum(jnp.where(sel & (dest == d_lo), gw, 0.0), axis=0, keepdims=True)
    g_hi = jnp.sum(jnp.where(sel & (dest == d_hi), gw, 0.0), axis=0, keepdims=True)
    single = d_hi == d_lo
    g_hi = jnp.where(single, 0.0, g_hi)
    d_hi = jnp.where(single, -1.0, d_hi)
    rows8 = jnp.concatenate([d_lo, d_hi, g_lo, g_hi, jnp.zeros((4, tt), F32)], axis=0)
    dcol_ref[...] = jnp.concatenate([rows8, jnp.zeros((ROUTE_LANES - 8, tt), F32)], axis=0).T
    for blk in range(ns // MOE_BLK):
        r = (blk * MOE_BLK + lax.broadcasted_iota(I32, (MOE_BLK, tt), 0)).astype(F32)
        lo, hi = r == d_lo, r == d_hi
        rows = slice(blk * MOE_BLK, (blk + 1) * MOE_BLK)
        xg_loc[rows, :] = jnp.dot((lo | hi).astype(BF16), xn_ref[...], preferred_element_type=F32).astype(BF16)
        row_gate = jnp.sum(jnp.where(lo, g_lo, 0.0) + jnp.where(hi, g_hi, 0.0), axis=-1, keepdims=True)
        gg_loc[rows, :] = jnp.broadcast_to(row_gate, (MOE_BLK, ROUTE_LANES))
    copies = functools.partial(_moe_run_copies, i, lstart_ref, npiece_ref, goff_ref, (xg_loc, gg_loc),
                               (xg_out, gg_out), (sems.at[0], sems.at[1]), True)
    copies(False)
    copies(True)


def _moe_ffn_body(bexp_ref, nused_ref, xg_ref, gg_ref, wg_ref, wu_ref, wd_ref, og_ref):
    del bexp_ref
    used = pl.program_id(0) < nused_ref[0]

    @pl.when(jnp.logical_not(used))
    def _():
        og_ref[...] = jnp.zeros(og_ref.shape, BF16)

    @pl.when(used)
    def _():
        xg = xg_ref[...]
        hg = jnp.dot(xg, wg_ref[0], preferred_element_type=F32)
        hu = jnp.dot(xg, wu_ref[0], preferred_element_type=F32)
        hid = ((hg * jax.nn.sigmoid(hg)) * hu).astype(BF16)
        out = jnp.dot(hid, wd_ref[0], preferred_element_type=F32)
        og_ref[...] = (out * gg_ref[:, 0:1]).astype(BF16)


def _moe_scatter_body(final_norm, lstart_ref, npiece_ref, goff_ref, og_ref, x1_ref, dcol_ref, gf_ref, o_ref,
                      og_loc, sems):
    i = pl.program_id(0)
    tt = x1_ref.shape[0]
    ns = _moe_slots(tt)

    @pl.when(i == 0)
    def _():
        og_loc[...] = jnp.zeros(og_loc.shape, BF16)

    copies = functools.partial(_moe_run_copies, i, lstart_ref, npiece_ref, goff_ref, (og_loc,), (og_ref,),
                               (sems.at[0],), False)
    copies(False)
    copies(True)
    y = x1_ref[...]
    c_lo, c_hi = dcol_ref[:, 0:1], dcol_ref[:, 1:2]
    for blk in range(ns // MOE_BLK):
        r = (blk * MOE_BLK + lax.broadcasted_iota(I32, (tt, MOE_BLK), 1)).astype(F32)
        onehot = ((r == c_lo) | (r == c_hi)).astype(BF16)
        y = y + jnp.dot(onehot, og_loc[blk * MOE_BLK:(blk + 1) * MOE_BLK, :], preferred_element_type=F32)
    if final_norm:
        y = y * lax.rsqrt(jnp.mean(y * y, axis=-1, keepdims=True) + NORM_EPS) * gf_ref[...]
    o_ref[...] = y


def _moe_grouped(xn, gwt, x1, wts, norm_final_g, tt):
    t = xn.shape[0]
    final_norm = norm_final_g is not None
    if not final_norm:
        norm_final_g = jnp.ones((D_MODEL,), F32)
    nt = t // tt
    ns = _moe_slots(tt)
    ne = MOE_EXPERTS
    counts = jnp.sum((gwt[:ne] > 0.0).reshape(ne, nt, tt), axis=-1, dtype=I32).T
    padded = (counts + MOE_PAD - 1) // MOE_PAD * MOE_PAD
    lstart = jnp.cumsum(padded, axis=1) - padded
    region = (jnp.sum(padded, axis=0) + MOE_FFN_ROWS - 1) // MOE_FFN_ROWS * MOE_FFN_ROWS
    gend = jnp.cumsum(region)
    goff = (gend - region)[None, :] + jnp.cumsum(padded, axis=0) - padded
    ng = (2 * t + nt * ne * MOE_PAD + ne * MOE_FFN_ROWS + MOE_FFN_ROWS - 1) // MOE_FFN_ROWS * MOE_FFN_ROWS
    nb = ng // MOE_FFN_ROWS
    block_row = jnp.arange(nb, dtype=I32)[:, None] * MOE_FFN_ROWS
    bexp = jnp.minimum(jnp.sum(block_row >= gend[None, :], axis=1, dtype=I32), ne - 1)
    nused = (gend[-1] // MOE_FFN_ROWS).astype(I32).reshape(1)
    pcol = jnp.broadcast_to(lstart.astype(F32).reshape(nt * ne, 1), (nt * ne, ROUTE_LANES))
    idx = np.arange(tt)
    tri = jnp.asarray(idx[:, None] < idx[None, :], BF16)
    sched = (lstart.reshape(-1), (padded // MOE_PAD).reshape(-1), goff.reshape(-1).astype(I32))
    any_spec = pl.BlockSpec(memory_space=pl.ANY)

    xg, gg, dcol = pl.pallas_call(
        _moe_compact_body,
        grid_spec=pltpu.PrefetchScalarGridSpec(
            num_scalar_prefetch=3,
            grid=(nt,),
            in_specs=[pl.BlockSpec((tt, D_MODEL), lambda i, *_: (i, 0)),
                      pl.BlockSpec((ROUTE_LANES, tt), lambda i, *_: (0, i)),
                      pl.BlockSpec((tt, tt), lambda i, *_: (0, 0)),
                      pl.BlockSpec((ne, ROUTE_LANES), lambda i, *_: (i, 0)),
                      any_spec, any_spec],
            out_specs=[any_spec, any_spec, pl.BlockSpec((tt, ROUTE_LANES), lambda i, *_: (i, 0))],
            scratch_shapes=[pltpu.VMEM((ns, D_MODEL), BF16), pltpu.VMEM((ns, ROUTE_LANES), F32),
                            pltpu.SemaphoreType.DMA((2,))],
        ),
        out_shape=[jax.ShapeDtypeStruct((ng, D_MODEL), BF16), jax.ShapeDtypeStruct((ng, ROUTE_LANES), F32),
                   jax.ShapeDtypeStruct((t, ROUTE_LANES), F32)],
        input_output_aliases={7: 0, 8: 1},
        compiler_params=_cparams("arbitrary"),
        name="moe_compact",
    )(*sched, xn, gwt, tri, pcol, jnp.zeros((ng, D_MODEL), BF16), jnp.zeros((ng, ROUTE_LANES), F32))

    og = pl.pallas_call(
        _moe_ffn_body,
        grid_spec=pltpu.PrefetchScalarGridSpec(
            num_scalar_prefetch=2,
            grid=(nb,),
            in_specs=[pl.BlockSpec((MOE_FFN_ROWS, D_MODEL), lambda b, *_: (b, 0)),
                      pl.BlockSpec((MOE_FFN_ROWS, ROUTE_LANES), lambda b, *_: (b, 0)),
                      pl.BlockSpec((1, D_MODEL, MOE_HIDDEN), lambda b, be, nu: (be[b], 0, 0)),
                      pl.BlockSpec((1, D_MODEL, MOE_HIDDEN), lambda b, be, nu: (be[b], 0, 0)),
                      pl.BlockSpec((1, MOE_HIDDEN, D_MODEL), lambda b, be, nu: (be[b], 0, 0))],
            out_specs=pl.BlockSpec((MOE_FFN_ROWS, D_MODEL), lambda b, *_: (b, 0)),
        ),
        out_shape=jax.ShapeDtypeStruct((ng, D_MODEL), BF16),
        compiler_params=_cparams("arbitrary"),
        name="moe_ffn",
    )(bexp, nused, xg, gg, wts["moe_w_gate"], wts["moe_w_up"], wts["moe_w_down"])

    return pl.pallas_call(
        functools.partial(_moe_scatter_body, final_norm),
        grid_spec=pltpu.PrefetchScalarGridSpec(
            num_scalar_prefetch=3,
            grid=(nt,),
            in_specs=[any_spec,
                      pl.BlockSpec((tt, D_MODEL), lambda i, *_: (i, 0)),
                      pl.BlockSpec((tt, ROUTE_LANES), lambda i, *_: (i, 0)),
                      pl.BlockSpec((1, D_MODEL), lambda i, *_: (0, 0))],
            out_specs=pl.BlockSpec((tt, D_MODEL), lambda i, *_: (i, 0)),
            scratch_shapes=[pltpu.VMEM((ns, D_MODEL), BF16), pltpu.SemaphoreType.DMA((1,))],
        ),
        out_shape=jax.ShapeDtypeStruct((t, D_MODEL), F32),
        compiler_params=_cparams("arbitrary"),
        name="moe_scatter",
    )(*sched, og, x1, dcol, norm_final_g.reshape(1, D_MODEL))


def _trunk(x, prm, wts, norm_final_g):
    nseq, seq_len, _ = x.shape
    xf = x.reshape(nseq * seq_len, D_MODEL)
    p_a, p_b, p_g = _norm_in_proj(xf, prm["norm_mix_g"], wts["w_in"], min(256, seq_len))
    rwkv = _rwkv_branch(p_a, nseq, seq_len, prm)
    yb = _hyena_branch(p_b, nseq, seq_len, prm)
    x1, xn, gwt = _merge(xf, rwkv, yb, p_g, prm, wts, min(512, seq_len))
    out = _moe_grouped(xn, gwt, x1, wts, norm_final_g, min(1024, seq_len))
    return out.reshape(nseq, seq_len, D_MODEL)


def kernel(x_prompt, x_sample, norm_mix_g, w_in, rwkv_mu_prev, rwkv_mu_next, rwkv_w0, rwkv_w2, rwkv_a0, rwkv_a2, rwkv_g2, rwkv_k_k, rwkv_k_a, rwkv_r_k, rwkv_ln_g, rwkv_ln_b, hyena_conv_w, hyena_conv_b, hyena_f_w1, hyena_f_b1, hyena_f_w2, hyena_f_b2, hyena_f_w3, hyena_f_freq, hyena_skip, w_up_a, w_up_b, w_out, norm_ffn_g, moe_w_route_group, moe_b_route_group, moe_w_route_expert, moe_b_route_expert, moe_w_gate, moe_w_up, moe_w_down, norm_final_g):
    layer = dict(norm_mix_g=norm_mix_g, w_in=w_in, rwkv_mu_prev=rwkv_mu_prev, rwkv_mu_next=rwkv_mu_next,
                 rwkv_w0=rwkv_w0, rwkv_w2=rwkv_w2, rwkv_a0=rwkv_a0, rwkv_a2=rwkv_a2, rwkv_g2=rwkv_g2,
                 rwkv_k_k=rwkv_k_k, rwkv_k_a=rwkv_k_a, rwkv_r_k=rwkv_r_k, rwkv_ln_g=rwkv_ln_g, rwkv_ln_b=rwkv_ln_b,
                 hyena_conv_w=hyena_conv_w, hyena_conv_b=hyena_conv_b, hyena_f_w1=hyena_f_w1, hyena_f_b1=hyena_f_b1,
                 hyena_f_w2=hyena_f_w2, hyena_f_b2=hyena_f_b2, hyena_f_w3=hyena_f_w3, hyena_f_freq=hyena_f_freq,
                 hyena_skip=hyena_skip, w_up_a=w_up_a, w_up_b=w_up_b, w_out=w_out, norm_ffn_g=norm_ffn_g,
                 moe_w_route_group=moe_w_route_group, moe_b_route_group=moe_b_route_group,
                 moe_w_route_expert=moe_w_route_expert, moe_b_route_expert=moe_b_route_expert,
                 moe_w_gate=moe_w_gate, moe_w_up=moe_w_up, moe_w_down=moe_w_down)
    depth = norm_mix_g.shape[0]
    big = ("w_in", "w_up_a", "w_up_b", "w_out", "moe_w_gate", "moe_w_up", "moe_w_down")

    def trunk(x):
        for li in range(depth):
            prm = {k: v[li] for k, v in layer.items()}
            wts = {k: prm[k].astype(BF16) for k in big}
            last = li == depth - 1
            x = _trunk(x, prm, wts, norm_final_g if last else None)
        return x

    return (trunk(x_prompt), trunk(x_sample))
```

```python
import functools
import math

import jax
import jax.numpy as jnp
import numpy as np
from jax import lax
from jax.experimental import pallas as pl
from jax.experimental.pallas import tpu as pltpu

F32 = jnp.float32
BF16 = jnp.bfloat16
I32 = jnp.int32

D_MODEL = 1024
NORM_EPS = 1e-6
RWKV_HEADS = 8
RWKV_HEAD_DIM = 64
RWKV_WIDTH = RWKV_HEADS * RWKV_HEAD_DIM
RWKV_DECAY_RANK = 64
RWKV_ICLR_RANK = 64
RWKV_GATE_RANK = 128
RWKV_GN_EPS = 64e-5
HYENA_WIDTH = D_MODEL // 2
C_RWKV_IN = 3 * RWKV_WIDTH + 2 * RWKV_DECAY_RANK + 2 * RWKV_ICLR_RANK + RWKV_GATE_RANK
C_HYENA_IN = 3 * HYENA_WIDTH
C_GATES = 2 * D_MODEL

CHUNK = 32
STACK = RWKV_HEADS * CHUNK
HALF = RWKV_WIDTH // 2
GROUP_HEADS = 4
GSTACK = GROUP_HEADS * CHUNK
LOCAL_CHUNKS_PER_ITER = 4

VMEM_LIMIT = 48 * 1024 * 1024

_NN = (((1,), (0,)), ((), ()))
_NT = (((1,), (1,)), ((), ()))
_TN = (((0,), (0,)), ((), ()))


def _dot(a, b, dims=_NN):
    return lax.dot_general(a.astype(BF16), b.astype(BF16), dims, preferred_element_type=F32)


def _split2(x):
    hi = x.astype(BF16)
    lo = (x - hi.astype(F32)).astype(BF16)
    return hi, lo


def _dot3(a, b, dims=_NN):
    ah, al = _split2(a)
    bh, bl = _split2(b)
    dg = functools.partial(lax.dot_general, dimension_numbers=dims, preferred_element_type=F32)
    return dg(ah, bh) + (dg(ah, bl) + dg(al, bh))


def _dot_exact_lhs(a_bf16, x):
    x1 = x.astype(BF16)
    r1 = x - x1.astype(F32)
    x2 = r1.astype(BF16)
    x3 = (r1 - x2.astype(F32)).astype(BF16)
    dg = functools.partial(lax.dot_general, dimension_numbers=_NN, preferred_element_type=F32)
    return dg(a_bf16, x1) + (dg(a_bf16, x2) + dg(a_bf16, x3))


def _seg_sum(x, ones_bd):
    x1, x2 = _split2(x)
    dg = functools.partial(lax.dot_general, dimension_numbers=_NN, preferred_element_type=F32)
    halves = []
    for hf in range(2):
        lanes = slice(hf * HALF, (hf + 1) * HALF)
        ones_h = ones_bd[lanes, lanes]
        halves.append(dg(x1[:, lanes], ones_h) + dg(x2[:, lanes], ones_h))
    return jnp.concatenate(halves, axis=1)


def _cparams(*sem):
    return pltpu.CompilerParams(dimension_semantics=tuple(sem), vmem_limit_bytes=VMEM_LIMIT)


def _norm_in_proj_body(x_ref, g_ref, wa_ref, wb_ref, wg_ref, pa_ref, pb_ref, pg_ref):
    x = x_ref[...]
    xn = x * lax.rsqrt(jnp.mean(x * x, axis=-1, keepdims=True) + NORM_EPS) * g_ref[...]
    xb = xn.astype(BF16)
    pa_ref[...] = jnp.dot(xb, wa_ref[...], preferred_element_type=F32)
    pb_ref[...] = jnp.dot(xb, wb_ref[...], preferred_element_type=F32)
    pg_ref[...] = jnp.dot(xb, wg_ref[...], preferred_element_type=F32)


def _norm_in_proj(x, g, w_in, tm):
    t = x.shape[0]
    wa = w_in[:, :C_RWKV_IN].astype(BF16)
    wb = w_in[:, C_RWKV_IN:C_RWKV_IN + C_HYENA_IN].astype(BF16)
    wg = w_in[:, C_RWKV_IN + C_HYENA_IN:].astype(BF16)
    full = lambda i: (0, 0)
    row = lambda i: (i, 0)
    return pl.pallas_call(
        _norm_in_proj_body,
        grid=(t // tm,),
        in_specs=[pl.BlockSpec((tm, D_MODEL), row), pl.BlockSpec((1, D_MODEL), full),
                  pl.BlockSpec(wa.shape, full), pl.BlockSpec(wb.shape, full), pl.BlockSpec(wg.shape, full)],
        out_specs=[pl.BlockSpec((tm, C_RWKV_IN), row), pl.BlockSpec((tm, C_HYENA_IN), row),
                   pl.BlockSpec((tm, C_GATES), row)],
        out_shape=[jax.ShapeDtypeStruct((t, C_RWKV_IN), F32), jax.ShapeDtypeStruct((t, C_HYENA_IN), F32),
                   jax.ShapeDtypeStruct((t, C_GATES), F32)],
        compiler_params=_cparams("parallel"),
        name="norm_in_proj",
    )(x, g.reshape(1, D_MODEL), wa, wb, wg)


def _shifted(p, prev_blk, next_blk, is_first, is_last):
    tm = p.shape[0]
    row = lax.broadcasted_iota(I32, p.shape, 0)
    prow = jnp.where(is_first, 0.0, prev_blk[7:8, :])
    nrow = jnp.where(is_last, 0.0, next_blk[0:1, :])
    prev = jnp.where(row == 0, prow, pltpu.roll(p, 1, 0))
    nxt = jnp.where(row == tm - 1, nrow, pltpu.roll(p, tm - 1, 0))
    return prev, nxt


def _stack_heads(x, head_mask):
    return jnp.where(head_mask, jnp.concatenate([x] * GROUP_HEADS, axis=0), 0.0)


def _unstack_heads(z):
    out = z[0:CHUNK]
    for h in range(1, GROUP_HEADS):
        out = out + z[h * CHUNK:(h + 1) * CHUNK]
    return out


def _chunks_local(chains, fillers=()):
    fillers = list(fillers)

    def fill():
        if fillers:
            fillers.pop(0)()

    ngrp = RWKV_HEADS // GROUP_HEADS
    ti = lax.broadcasted_iota(I32, (CHUNK, CHUNK), 0)
    si = lax.broadcasted_iota(I32, (CHUNK, CHUNK), 1)
    srow = lax.broadcasted_iota(I32, (GSTACK, HALF), 0)
    slane = lax.broadcasted_iota(I32, (GSTACK, HALF), 1)
    head_mask = (srow >> 5) == (slane >> 6)
    mrow = lax.broadcasted_iota(I32, (GSTACK, GSTACK), 0)
    mcol = lax.broadcasted_iota(I32, (GSTACK, GSTACK), 1)
    same = (mrow >> 5) == (mcol >> 5)
    masks = {False: (same & (mrow > mcol), same & (mrow >= mcol)),
             True: (same & (mrow < mcol), same & (mrow <= mcol))}
    tris = {False: (ti >= si).astype(BF16), True: (ti <= si).astype(BF16)}

    cls = [_dot_exact_lhs(tris[rev], lw) for (_, _, _, lw, _, _, rev) in chains]
    st = []
    for (r, v, kk, lw, k, a, rev), cl in zip(chains, cls):
        tot = cl[0:1] if rev else cl[CHUNK - 1:CHUNK]
        e_neg = jnp.exp(-cl)
        e_tail = jnp.exp(tot - cl)
        beta = kk * a
        st.append(dict(rev=rev, v=v, alpha_b=-kk * jnp.exp(cl - lw), r_b=r * jnp.exp(cl), beta_b=beta * e_neg,
                       k_b=k * e_neg, bt=beta * e_tail, kt=k * e_tail, pc=jnp.exp(tot)))

    for s in st:
        s["sa"], s["sv"], s["lhs"], s["rhs"] = [], [], [], []
        for g in range(ngrp):
            lanes = slice(g * HALF, (g + 1) * HALF)
            sa_ = _stack_heads(s["alpha_b"][:, lanes], head_mask).astype(BF16)
            sr_ = _stack_heads(s["r_b"][:, lanes], head_mask).astype(BF16)
            s["sa"].append(sa_)
            s["sv"].append(_stack_heads(s["v"][:, lanes], head_mask).astype(BF16))
            s["lhs"].append(jnp.concatenate([sa_, sr_], axis=0))
            s["rhs"].append(jnp.concatenate([s["beta_b"][:, lanes].astype(BF16)] * GROUP_HEADS
                                            + [s["k_b"][:, lanes].astype(BF16)] * GROUP_HEADS, axis=0))
    pms = [[lax.dot_general(s["lhs"][g], s["rhs"][g], _NT, preferred_element_type=F32) for g in range(ngrp)]
           for s in st]
    zero = jnp.zeros((GSTACK, GSTACK), F32)
    drow = lax.broadcasted_iota(I32, (STACK, STACK), 0)
    dcol = lax.broadcasted_iota(I32, (STACK, STACK), 1)
    eye = jnp.where(drow == dcol, 1.0, 0.0)
    for s, pm in zip(st, pms):
        strict, incl = masks[s["rev"]]
        s["ak"] = [jnp.where(strict, pm[g][:GSTACK, GSTACK:], 0.0).astype(BF16) for g in range(ngrp)]
        s["rb"] = [jnp.where(incl, pm[g][GSTACK:, :GSTACK], 0.0).astype(BF16) for g in range(ngrp)]
        s["rk"] = [jnp.where(incl, pm[g][GSTACK:, GSTACK:], 0.0).astype(BF16) for g in range(ngrp)]
        ab = [jnp.where(strict, pm[g][:GSTACK, :GSTACK], 0.0) for g in range(ngrp)]
        s["apow"] = jnp.concatenate([jnp.concatenate([ab[0], zero], axis=1),
                                     jnp.concatenate([zero, ab[1]], axis=1)], axis=0)
        s["tinv"] = eye + s["apow"]

    for _ in range(int(math.log2(CHUNK)) - 1):
        sq = [_dot(s["apow"], s["apow"]) for s in st]
        fill()
        for s, x in zip(st, sq):
            s["apow"] = x
        pr = [_dot(s["tinv"], s["apow"]) for s in st]
        fill()
        for s, x in zip(st, pr):
            s["tinv"] = s["tinv"] + x
    while fillers:
        fill()

    akv = [[_dot(s["ak"][g], s["sv"][g]) for g in range(ngrp)] for s in st]
    rkv = [[_dot(s["rk"][g], s["sv"][g]) for g in range(ngrp)] for s in st]
    tw = [[_dot(s["tinv"][g * GSTACK:(g + 1) * GSTACK, g * GSTACK:(g + 1) * GSTACK],
                jnp.concatenate([s["sa"][g].astype(F32), akv[c][g]], axis=1)) for g in range(ngrp)]
          for c, s in enumerate(st)]
    rbx = [[_dot(s["rb"][g], tw[c][g]) for g in range(ngrp)] for c, s in enumerate(st)]
    outs = []
    for c, s in enumerate(st):
        cat = lambda f: jnp.concatenate([f(g) for g in range(ngrp)], axis=1)
        qt = s["r_b"] + cat(lambda g: _unstack_heads(rbx[c][g][:, :HALF]))
        wt = cat(lambda g: _unstack_heads(tw[c][g][:, :HALF]))
        yloc = cat(lambda g: _unstack_heads(rbx[c][g][:, HALF:] + rkv[c][g]))
        u = cat(lambda g: _unstack_heads(tw[c][g][:, HALF:]))
        outs.append((qt, wt, yloc, u, s["bt"], s["kt"], s["pc"]))
    return outs


def _rwkv_local_body(nt_seq, p_ref, pp_ref, pn_ref, mup_ref, mun_ref, w0_ref, w2_ref, a0_ref, a2_ref, g2_ref,
                     kk_ref, ka_ref, rk_ref, ones_ref,
                     g_out, bonus_out, v_out, qwf, yuf, bkf, pcf, qwr, yur, bkr, pcr,
                     r_s, v_s, kk_s, lw_s, k_s, a_s, ps_s):
    i = pl.program_id(0) % nt_seq
    p = p_ref[...]
    tm = p.shape[0]
    prev, nxt = _shifted(p, pp_ref[...], pn_ref[...], i == 0, i == nt_seq - 1)
    ps_s[...] = p + mup_ref[...] * (prev - p) + mun_ref[...] * (nxt - p)
    c = RWKV_WIDTH
    span = CHUNK * LOCAL_CHUNKS_PER_ITER

    def param_steps(h):
        rows = slice(h * span, (h + 1) * span)
        st = {}

        def split():
            ps = ps_s[rows, :]
            st["r"], st["k"], st["v"] = ps[:, 0:c], ps[:, c:2 * c], ps[:, 2 * c:3 * c]
            st["lw"], st["la"], st["lg"] = (ps[:, 3 * c + j * 128:3 * c + (j + 1) * 128] for j in range(3))
            v_out[rows, :] = st["v"]
            r_s[rows, :] = st["r"]
            v_s[rows, :] = st["v"]

        def decay_proj():
            st["w_raw"] = w0_ref[...] + _dot3(jnp.tanh(st["lw"]), w2_ref[...])

        def decay():
            lw_s[rows, :] = -jnp.exp(-jnp.logaddexp(-st["w_raw"], 0.0) - 0.5)

        def iclr():
            st["a"] = jax.nn.sigmoid(a0_ref[...] + _dot3(st["la"], a2_ref[...]))
            a_s[rows, :] = st["a"]

        def gate():
            g_out[rows, :] = _dot(jax.nn.sigmoid(st["lg"]), g2_ref[...])

        def key_norm():
            kk0 = st["k"] * kk_ref[...]
            kk_s[rows, :] = kk0 * lax.rsqrt(jnp.maximum(_seg_sum(kk0 * kk0, ones_ref[...]), 1e-24))

        def key_dir():
            ka = ka_ref[...]
            st["kf"] = st["k"] * (1.0 + (st["a"][:, :c] - 1.0) * ka)
            st["kr"] = st["k"] * (1.0 + (st["a"][:, c:] - 1.0) * ka)
            k_s[rows, :c] = st["kf"]
            k_s[rows, c:] = st["kr"]

        def bonus():
            rk = st["r"] * ((st["kf"] + st["kr"]) * 0.5) * rk_ref[...]
            bonus_out[rows, :] = _seg_sum(rk, ones_ref[...]) * st["v"]

        return [split, decay_proj, decay, iclr, gate, key_norm, key_dir, bonus]

    out_refs = ((qwf, yuf, bkf, pcf), (qwr, yur, bkr, pcr))
    nspan = tm // span
    for step in param_steps(0):
        step()
    for h in range(nspan):
        chains, where = [], []
        for uu in range(LOCAL_CHUNKS_PER_ITER):
            ci = h * LOCAL_CHUNKS_PER_ITER + uu
            rows = slice(ci * CHUNK, (ci + 1) * CHUNK)
            rc, vc, kkc = r_s[rows, :], v_s[rows, :], kk_s[rows, :]
            for d in range(2):
                lanes = slice(d * c, (d + 1) * c)
                chains.append((rc, vc, kkc, lw_s[rows, lanes], k_s[rows, lanes], a_s[rows, lanes], d == 1))
                where.append((ci, d))
        fillers = param_steps(h + 1) if h + 1 < nspan else []
        for (ci, d), (qt, wt, yloc, u, bt, kt, pcv) in zip(where, _chunks_local(chains, fillers)):
            qw, yu, bk, pc = out_refs[d]
            qw[ci] = jnp.concatenate([qt, wt], axis=0).astype(BF16)
            yu[ci] = jnp.concatenate([yloc, u], axis=0)
            bk[ci] = jnp.concatenate([bt, kt], axis=0).astype(BF16)
            pc[ci] = jnp.broadcast_to(pcv, (8, c))


def _rwkv_local(p_a, nt_seq, tm, prm):
    t = p_a.shape[0]
    nt = t // tm
    nc = tm // CHUNK
    c = RWKV_WIDTH
    rd, ri = RWKV_DECAY_RANK, RWKV_ICLR_RANK
    z = jnp.zeros((rd, c), F32)
    w2pad = jnp.concatenate([jnp.concatenate([prm["rwkv_w2"][0], z], 0), jnp.concatenate([z, prm["rwkv_w2"][1]], 0)], 1)
    z = jnp.zeros((ri, c), F32)
    a2pad = jnp.concatenate([jnp.concatenate([prm["rwkv_a2"][0], z], 0), jnp.concatenate([z, prm["rwkv_a2"][1]], 0)], 1)
    hd = np.arange(c) // RWKV_HEAD_DIM
    ones_bd = jnp.asarray(hd[:, None] == hd[None, :], BF16)
    full = lambda i: (0, 0)
    row = lambda i: (i, 0)
    row3 = lambda i: (i, 0, 0)
    nblk8 = t // 8
    tm8 = tm // 8
    vec = lambda n: pl.BlockSpec((1, n), full)
    chunk_out = lambda rows, dt: (pl.BlockSpec((nc, rows, c), row3), jax.ShapeDtypeStruct((t // CHUNK, rows, c), dt))
    per_dir = [chunk_out(2 * CHUNK, BF16), chunk_out(2 * CHUNK, F32), chunk_out(2 * CHUNK, BF16), chunk_out(8, F32)]
    outs = [(pl.BlockSpec((tm, c), row), jax.ShapeDtypeStruct((t, c), F32))] * 3 + per_dir + per_dir
    return pl.pallas_call(
        functools.partial(_rwkv_local_body, nt_seq),
        grid=(nt,),
        in_specs=[pl.BlockSpec((tm, C_RWKV_IN), row),
                  pl.BlockSpec((8, C_RWKV_IN), lambda i: (jnp.maximum(i * tm8 - 1, 0), 0)),
                  pl.BlockSpec((8, C_RWKV_IN), lambda i: (jnp.minimum((i + 1) * tm8, nblk8 - 1), 0)),
                  vec(C_RWKV_IN), vec(C_RWKV_IN), vec(2 * c), pl.BlockSpec((128, 2 * c), full),
                  vec(2 * c), pl.BlockSpec((128, 2 * c), full), pl.BlockSpec((RWKV_GATE_RANK, c), full),
                  vec(c), vec(c), vec(c), pl.BlockSpec((c, c), full)],
        out_specs=[o[0] for o in outs],
        out_shape=[o[1] for o in outs],
        scratch_shapes=[pltpu.VMEM((tm, c), F32)] * 3 + [pltpu.VMEM((tm, 2 * c), F32)] * 3
                       + [pltpu.VMEM((tm, C_RWKV_IN), F32)],
        compiler_params=_cparams("parallel"),
        name="rwkv_local",
    )(p_a, p_a, p_a, prm["rwkv_mu_prev"].reshape(1, -1), prm["rwkv_mu_next"].reshape(1, -1),
      prm["rwkv_w0"].reshape(1, 2 * c), w2pad, prm["rwkv_a0"].reshape(1, 2 * c), a2pad, prm["rwkv_g2"],
      prm["rwkv_k_k"].reshape(1, c), prm["rwkv_k_a"].reshape(1, c), prm["rwkv_r_k"].reshape(1, c), ones_bd)


SCAN_TILE = 128


def _rwkv_scan_body(nc, nseq, qwf, yuf, bkf, pcf, vf, qwr, yur, bkr, pcr, vr, yf_ref, yr_ref, s_ref):
    @pl.when(pl.program_id(0) == 0)
    def _():
        s_ref[...] = jnp.zeros(s_ref.shape, F32)

    brow = lax.broadcasted_iota(I32, (HALF, HALF), 0)
    bcol = lax.broadcasted_iota(I32, (HALF, HALF), 1)
    same_head = (brow >> 6) == (bcol >> 6)

    dirs = ((qwf, yuf, bkf, pcf, vf, yf_ref), (qwr, yur, bkr, pcr, vr, yr_ref))

    def step(j, carry):
        ch = []
        for b in range(nseq):
            for d, (qw_ref, yu_ref, bk_ref, pc_ref, v_ref, y_ref) in enumerate(dirs):
                ci = j if d == 0 else nc - 1 - j
                rows = pl.ds(pl.multiple_of(ci * CHUNK, CHUNK), CHUNK)
                qw, yu, bk = qw_ref[b, ci], yu_ref[b, ci], bk_ref[b, ci]
                pc, vc = pc_ref[b, ci][0:1], v_ref[b, rows, :]
                for hf in range(2):
                    lanes = slice(hf * HALF, (hf + 1) * HALF)
                    ch.append(dict(sidx=(b, d, hf), b=b, rows=rows, lanes=lanes, y_ref=y_ref, qw=qw[:, lanes],
                                   yu=yu[:, lanes], bk=bk[:, lanes], pc=pc[:, lanes], vc=vc[:, lanes]))
        for c in ch:
            c["s"] = s_ref[c["sidx"]]
        ys = [lax.dot_general(c["qw"], c["s"].astype(BF16), _NT, preferred_element_type=F32) + c["yu"] for c in ch]
        for c, y in zip(ch, ys):
            c["y_ref"][c["b"], c["rows"], c["lanes"]] = y[:CHUNK]
        sav = [jnp.concatenate([y[CHUNK:], c["vc"]], axis=0).astype(BF16) for c, y in zip(ch, ys)]
        upd = [lax.dot_general(x, c["bk"], _TN, preferred_element_type=F32) for c, x in zip(ch, sav)]
        for c, u in zip(ch, upd):
            s_ref[c["sidx"]] = c["s"] * c["pc"] + jnp.where(same_head, u, 0.0)
        return carry

    lax.fori_loop(0, nc, step, 0)


def _rwkv_scan(loc, nseq, seq_len):
    _, _, v, qwf, yuf, bkf, pcf, qwr, yur, bkr, pcr = loc
    tm = min(SCAN_TILE, seq_len)
    nt_seq = seq_len // tm
    nc = tm // CHUNK
    c = RWKV_WIDTH
    ncs = seq_len // CHUNK
    fwd = lambda i: (0, i, 0, 0)
    rev = lambda i: (0, nt_seq - 1 - i, 0, 0)
    fwd_v = lambda i: (0, i, 0)
    rev_v = lambda i: (0, nt_seq - 1 - i, 0)

    def specs(m4, m3):
        return [pl.BlockSpec((nseq, nc, 2 * CHUNK, c), m4), pl.BlockSpec((nseq, nc, 2 * CHUNK, c), m4),
                pl.BlockSpec((nseq, nc, 2 * CHUNK, c), m4), pl.BlockSpec((nseq, nc, 8, c), m4),
                pl.BlockSpec((nseq, tm, c), m3)]

    per_seq = lambda a: a.reshape((nseq, ncs) + a.shape[1:])
    v3 = v.reshape(nseq, seq_len, c)
    yf, yr = pl.pallas_call(
        functools.partial(_rwkv_scan_body, nc, nseq),
        grid=(nt_seq,),
        in_specs=specs(fwd, fwd_v) + specs(rev, rev_v),
        out_specs=[pl.BlockSpec((nseq, tm, c), fwd_v), pl.BlockSpec((nseq, tm, c), rev_v)],
        out_shape=[jax.ShapeDtypeStruct((nseq, seq_len, c), F32)] * 2,
        scratch_shapes=[pltpu.VMEM((nseq, 2, 2, HALF, HALF), F32)],
        compiler_params=_cparams("arbitrary"),
        name="rwkv_scan",
    )(per_seq(qwf), per_seq(yuf), per_seq(bkf), per_seq(pcf), v3,
      per_seq(qwr), per_seq(yur), per_seq(bkr), per_seq(pcr), v3)
    return yf.reshape(nseq * seq_len, c), yr.reshape(nseq * seq_len, c)


def _rwkv_out(yf, yr, g, bonus, ln_g, ln_b, ones_bd):
    y = yf + yr
    inv_n = 1.0 / RWKV_HEAD_DIM
    mean = _seg_sum(y, ones_bd) * inv_n
    yc = y - mean
    var = _seg_sum(yc * yc, ones_bd) * inv_n
    yn = yc * lax.rsqrt(var + RWKV_GN_EPS) * ln_g + ln_b
    return (yn + bonus) * g


def _rwkv_branch(p_a, nseq, seq_len, prm):
    tm = min(256, seq_len)
    nt_seq = seq_len // tm
    loc = _rwkv_local(p_a, nt_seq, tm, prm)
    yf, yr = _rwkv_scan(loc, nseq, seq_len)
    return yf, yr, loc[0], loc[1]


FFT_N2 = 128
FILTER_EMB_DIM = 33
FILTER_BANDS = (FILTER_EMB_DIM - 1) // 2
FILTER_HIDDEN = 64
FILTER_TARGET = 1e-2
FILTER_FAST_DECAY_PCT = 0.3
FILTER_SLOW_DECAY_PCT = 1.5


def _hyena_prep_body(nt_seq, p_ref, pp_ref, pn_ref, cw_ref, cb_ref, src_ref, x2_ref):
    i = pl.program_id(0) % nt_seq
    p = p_ref[...]
    prev, nxt = _shifted(p, pp_ref[...], pn_ref[...], i == 0, i == nt_seq - 1)
    cw = cw_ref[...]
    u = prev * cw[0:1] + p * cw[1:2] + nxt * cw[2:3] + cb_ref[...]
    c = HYENA_WIDTH
    src_ref[...] = u[:, 0:c] * u[:, 2 * c:3 * c]
    x2_ref[...] = u[:, c:2 * c]


def _hyena_prep(p_b, nt_seq, tm, prm):
    t = p_b.shape[0]
    c = HYENA_WIDTH
    row = lambda i: (i, 0)
    full = lambda i: (0, 0)
    nblk8 = t // 8
    tm8 = tm // 8
    cw = jnp.concatenate([prm["hyena_conv_w"], jnp.zeros((5, C_HYENA_IN), F32)], axis=0)
    return pl.pallas_call(
        functools.partial(_hyena_prep_body, nt_seq),
        grid=(t // tm,),
        in_specs=[pl.BlockSpec((tm, C_HYENA_IN), row),
                  pl.BlockSpec((8, C_HYENA_IN), lambda i: (jnp.maximum(i * tm8 - 1, 0), 0)),
                  pl.BlockSpec((8, C_HYENA_IN), lambda i: (jnp.minimum((i + 1) * tm8, nblk8 - 1), 0)),
                  pl.BlockSpec((8, C_HYENA_IN), full), pl.BlockSpec((1, C_HYENA_IN), full)],
        out_specs=[pl.BlockSpec((tm, c), row)] * 2,
        out_shape=[jax.ShapeDtypeStruct((t, c), F32)] * 2,
        compiler_params=_cparams("parallel"),
        name="hyena_prep",
    )(p_b, p_b, p_b, cw, prm["hyena_conv_b"].reshape(1, -1))


def _hyena_filter_body(seq_len, w1_ref, b1_ref, w2_ref, b2_ref, w3_ref, freq_ref, delta_ref, o_ref):
    rows = o_ref.shape[0]
    n = pl.program_id(0) * rows + lax.broadcasted_iota(I32, (rows, 128), 0)
    lane = lax.broadcasted_iota(I32, (rows, 128), 1)
    pos = jnp.where(n < seq_len, n, 2 * seq_len - n).astype(F32)
    t = pos * (1.0 / (seq_len - 1))
    omega = (2.0 * math.pi) * pos / seq_len
    band_step = (FILTER_BANDS - 1 - 1e-4) / (FILTER_BANDS - 1)
    band_idx = jnp.where(lane <= FILTER_BANDS, lane - 1, lane - 1 - FILTER_BANDS).astype(F32)
    arg = (1e-4 + band_idx * band_step) * omega
    phase = jnp.where(lane <= FILTER_BANDS, 0.5 * math.pi, math.pi)
    z = jnp.where(lane == 0, t, jnp.where(lane <= 2 * FILTER_BANDS, jnp.sin(arg + phase), 0.0))
    half = rows // 2
    freq = freq_ref[...]
    hid = jnp.sin(freq * (_dot3(z[:half], w1_ref[0]) + _dot3(z[half:], w1_ref[1]) + b1_ref[...]))
    hid = jnp.sin(freq * (_dot3(hid, w2_ref[...]) + b2_ref[...]))
    filt = jnp.concatenate([_dot3(hid, w3_ref[0]), _dot3(hid, w3_ref[1])], axis=0)
    nn = n[:, 0:1]
    tt = t[:, 0:1]
    sel = jnp.where(nn < seq_len, filt[:, :HYENA_WIDTH], jnp.where(nn > seq_len, filt[:, HYENA_WIDTH:], 0.0))
    o_ref[...] = sel * jnp.exp(-tt * delta_ref[...])


def _hyena_filter(seq_len, prm):
    rows = min(1024, 2 * seq_len)
    c = HYENA_WIDTH
    fh = FILTER_HIDDEN
    w1 = jnp.concatenate([prm["hyena_f_w1"], jnp.zeros((128 - FILTER_EMB_DIM, fh), F32)], axis=0)
    z1 = jnp.zeros_like(w1)
    w1p = jnp.stack([jnp.concatenate([w1, z1], axis=1), jnp.concatenate([z1, w1], axis=1)])
    w2 = prm["hyena_f_w2"]
    z2 = jnp.zeros_like(w2)
    w2p = jnp.concatenate([jnp.concatenate([w2, z2], axis=1), jnp.concatenate([z2, w2], axis=1)], axis=0)
    w3 = prm["hyena_f_w3"]
    z3 = jnp.zeros_like(w3)
    w3p = jnp.stack([jnp.concatenate([w3, z3], axis=0), jnp.concatenate([z3, w3], axis=0)])
    twice = lambda a: jnp.tile(a.reshape(1, fh), (1, 2))
    min_decay = math.log(FILTER_TARGET) / FILTER_SLOW_DECAY_PCT
    max_decay = math.log(FILTER_TARGET) / FILTER_FAST_DECAY_PCT
    deltas = jnp.abs(jnp.linspace(min_decay, max_decay, c, dtype=F32)).reshape(1, c)
    full = lambda i: (0, 0)
    full3 = lambda i: (0, 0, 0)
    return pl.pallas_call(
        functools.partial(_hyena_filter_body, seq_len),
        grid=(2 * seq_len // rows,),
        in_specs=[pl.BlockSpec((2, 128, 2 * fh), full3), pl.BlockSpec((1, 2 * fh), full),
                  pl.BlockSpec((2 * fh, 2 * fh), full), pl.BlockSpec((1, 2 * fh), full),
                  pl.BlockSpec((2, 2 * fh, 2 * c), full3), pl.BlockSpec((1, 2 * fh), full),
                  pl.BlockSpec((1, c), full)],
        out_specs=pl.BlockSpec((rows, c), lambda i: (i, 0)),
        out_shape=jax.ShapeDtypeStruct((2 * seq_len, c), F32),
        compiler_params=_cparams("parallel"),
        name="hyena_filter",
    )(w1p, twice(prm["hyena_f_b1"]), w2p, twice(prm["hyena_f_b2"]), w3p, twice(prm["hyena_f_freq"]), deltas)


FFT_N2_TILE = 16
FFT_LANE_TILE = 256


def _lmul_rows(m_ref, x_ref):
    m = m_ref[...]
    xt = pltpu.einshape("rjc->jrc", x_ref[0].astype(F32))
    ys = [jnp.dot(m, xt[j].astype(BF16), preferred_element_type=F32) for j in range(FFT_N2_TILE)]
    return pltpu.einshape("jrc->rjc", jnp.stack(ys, axis=0))


def _lmul_body(m_ref, x_ref, o_ref):
    o_ref[0] = _lmul_rows(m_ref, x_ref).astype(o_ref.dtype)


def _lmul_epilogue_body(m_ref, x_ref, src_ref, x2_ref, skip_ref, o_ref):
    o_ref[0] = x2_ref[0] * (_lmul_rows(m_ref, x_ref) + src_ref[0] * skip_ref[...])


def _lmul(m, x, extra=None):
    nfft, r_in, n2, c = x.shape
    r_out = m.shape[0]
    ct = FFT_LANE_TILE
    xs = pl.BlockSpec((1, r_in, FFT_N2_TILE, ct), lambda f, j, l: (f, 0, j, l))
    os_ = pl.BlockSpec((1, r_out, FFT_N2_TILE, ct), lambda f, j, l: (f, 0, j, l))
    ms = pl.BlockSpec(m.shape, lambda f, j, l: (0, 0))
    if extra is None:
        body, ins, specs, out_dtype = _lmul_body, (m, x), [ms, xs], BF16
    else:
        src, x2, skip = extra
        body, ins, out_dtype = _lmul_epilogue_body, (m, x, src, x2, skip), F32
        specs = [ms, xs, os_, os_, pl.BlockSpec((1, ct), lambda f, j, l: (0, l))]
    return pl.pallas_call(
        body,
        grid=(nfft, n2 // FFT_N2_TILE, c // ct),
        in_specs=specs,
        out_specs=os_,
        out_shape=jax.ShapeDtypeStruct((nfft, r_out, n2, c), out_dtype),
        compiler_params=_cparams("parallel", "parallel", "parallel"),
        name="fft_outer" if extra is None else "fft_outer_out",
    )(*ins)


def _fft_inner_body(kt, conv, g_ref, y_ref, *rest):
    if conv:
        h_ref, o_ref = rest
    else:
        (o_ref,) = rest
    n2 = FFT_N2
    for q in range(kt):
        g = g_ref[q]
        yk = jnp.concatenate([y_ref[0, 0, q], y_ref[0, 1, q]], axis=0).astype(BF16)
        z = jnp.dot(g, yk, preferred_element_type=F32)
        if conv:
            zr, zi = z[:n2], z[n2:]
            hr, hi = h_ref[0, 0, q], h_ref[0, 1, q]
            pr = zr * hr - zi * hi
            pi = zr * hi + zi * hr
            prod = jnp.concatenate([pr, pi], axis=0).astype(BF16)
            z = lax.dot_general(g, prod, _TN, preferred_element_type=F32)
        o_ref[0, 0, q] = z[:n2].astype(o_ref.dtype)
        o_ref[0, 1, q] = z[n2:].astype(o_ref.dtype)


def _fft_inner(g, y, h, kt):
    nfft, _, n1, n2, c = y.shape
    blk = pl.BlockSpec((1, 2, kt, n2, c), lambda f, j: (f, 0, j, 0, 0))
    gs = pl.BlockSpec((kt, 2 * n2, 2 * n2), lambda f, j: (j, 0, 0))
    conv = h is not None
    ins = (g, y, h) if conv else (g, y)
    specs = [gs, blk, pl.BlockSpec((1, 2, kt, n2, c), lambda f, j: (0, 0, j, 0, 0))] if conv else [gs, blk]
    return pl.pallas_call(
        functools.partial(_fft_inner_body, kt, conv),
        grid=(nfft, n1 // kt),
        in_specs=specs,
        out_specs=blk,
        out_shape=jax.ShapeDtypeStruct(y.shape, BF16 if conv else F32),
        compiler_params=_cparams("parallel", "parallel"),
        name="fft_inner_conv" if conv else "fft_inner",
    )(*ins)


def _dft_tables(seq_len):
    n = 2 * seq_len
    n2 = FFT_N2
    n1 = n // n2
    k1 = jnp.arange(n1, dtype=I32)
    ang1 = (2.0 * math.pi / n1) * ((k1[:, None] * k1[None, :]) % n1).astype(F32)
    c1, s1 = jnp.cos(ang1), jnp.sin(ang1)
    half = n1 // 2
    f_pair = jnp.concatenate([jnp.concatenate([c1[:, :half], s1[:, :half]], 1),
                              jnp.concatenate([-s1[:, :half], c1[:, :half]], 1)], 0)
    f_real_half = jnp.concatenate([c1[:, :half], -s1[:, :half]], 0)
    f_real_full = jnp.concatenate([c1, -s1], 0)
    ci, si = c1[:half] / n, s1[:half] / n
    b_real = jnp.concatenate([ci, -si], 1)
    b_pair = jnp.concatenate([b_real, jnp.concatenate([si, ci], 1)], 0)
    kk = k1[:, None, None] + n1 * jnp.arange(n2, dtype=I32)[None, :, None]
    ang2 = (2.0 * math.pi / n) * ((kk * jnp.arange(n2, dtype=I32)[None, None, :]) % n).astype(F32)
    c2, s2 = jnp.cos(ang2), jnp.sin(ang2)
    g = jnp.concatenate([jnp.concatenate([c2, s2], 2), jnp.concatenate([-s2, c2], 2)], 1)
    cast = lambda a: a.astype(BF16)
    return dict(f_pair=cast(f_pair), f_real_half=cast(f_real_half), f_real_full=cast(f_real_full),
                b_pair=cast(b_pair), b_real=cast(b_real), g=cast(g), n1=n1)


def _hyena_branch(p_b, nseq, seq_len, prm):
    tm = min(512, seq_len)
    c = HYENA_WIDTH
    src, x2 = _hyena_prep(p_b, seq_len // tm, tm, prm)
    tab = _dft_tables(seq_len)
    n1 = tab["n1"]
    n2 = FFT_N2
    kt = 4
    filt = _hyena_filter(seq_len, prm)
    hy = _lmul(tab["f_real_full"], filt.reshape(1, n1, n2, c))
    hspec = _fft_inner(tab["g"], hy.reshape(1, 2, n1, n2, c), None, kt)
    pair = nseq % 2 == 0
    nfft = nseq // 2 if pair else nseq
    rows = n1 if pair else n1 // 2
    xin = src.reshape(nfft, rows, n2, c)
    y = _lmul(tab["f_pair"] if pair else tab["f_real_half"], xin)
    w = _fft_inner(tab["g"], y.reshape(nfft, 2, n1, n2, c), hspec, kt)
    out = _lmul(tab["b_pair"] if pair else tab["b_real"], w.reshape(nfft, 2 * n1, n2, c),
                extra=(xin, x2.reshape(nfft, rows, n2, c), prm["hyena_skip"].reshape(1, c)))
    return out.reshape(nseq * seq_len, c)


MOE_GROUPS = 4
MOE_EXPERTS_PER_GROUP = 8
MOE_EXPERTS = MOE_GROUPS * MOE_EXPERTS_PER_GROUP
MOE_HIDDEN = D_MODEL // 2
ROUTE_LANES = 128


def _route(logits):
    lane = lax.broadcasted_iota(I32, logits.shape, 1)
    neg = -jnp.inf
    big = ROUTE_LANES
    gl = jnp.where(lane < MOE_GROUPS, logits, neg)
    gmax = jnp.max(gl, axis=-1, keepdims=True)
    grp = jnp.min(jnp.where(gl == gmax, lane, big), axis=-1, keepdims=True)
    p_grp = 1.0 / jnp.sum(jnp.exp(gl - gmax), axis=-1, keepdims=True)
    lo = MOE_GROUPS + grp * MOE_EXPERTS_PER_GROUP
    el = jnp.where((lane >= lo) & (lane < lo + MOE_EXPERTS_PER_GROUP), logits, neg)
    m1 = jnp.max(el, axis=-1, keepdims=True)
    i1 = jnp.min(jnp.where(el == m1, lane, big), axis=-1, keepdims=True)
    el2 = jnp.where(lane == i1, neg, el)
    m2 = jnp.max(el2, axis=-1, keepdims=True)
    i2 = jnp.min(jnp.where(el2 == m2, lane, big), axis=-1, keepdims=True)
    e2 = jnp.exp(m2 - m1)
    g1 = p_grp / (1.0 + e2)
    g2 = p_grp * e2 / (1.0 + e2)
    gates = jnp.where(lane == i1, g1, jnp.where(lane == i2, g2, 0.0))
    return pltpu.roll(gates, ROUTE_LANES - MOE_GROUPS, 1)


def _merge_body(x_ref, yf_ref, yr_ref, rg_ref, bonus_ref, lng_ref, lnb_ref, ones_ref, yb_ref, pg_ref,
                wua_ref, wub_ref, wo_ref, g_ref, wr_ref, br_ref, x1_ref, xn_ref, gwt_ref):
    ya = _rwkv_out(yf_ref[...], yr_ref[...], rg_ref[...], bonus_ref[...], lng_ref[...], lnb_ref[...], ones_ref[...])
    ga = jax.nn.sigmoid(pg_ref[:, :D_MODEL])
    gb = jax.nn.sigmoid(pg_ref[:, D_MODEL:])
    merged = ga * _dot(ya, wua_ref[...]) + gb * _dot(yb_ref[...], wub_ref[...])
    x1 = x_ref[...] + _dot(merged, wo_ref[...])
    x1_ref[...] = x1
    xn = x1 * lax.rsqrt(jnp.mean(x1 * x1, axis=-1, keepdims=True) + NORM_EPS) * g_ref[...]
    xn_ref[...] = xn.astype(BF16)
    logits = _dot3(xn, wr_ref[...]) + br_ref[...]
    gwt_ref[...] = _route(logits).T


def _merge(x, rwkv, yb, pg, prm, wts, tm):
    t = x.shape[0]
    c = RWKV_WIDTH
    row = lambda i: (i, 0)
    full = lambda i: (0, 0)
    hd = np.arange(c) // RWKV_HEAD_DIM
    ones_bd = jnp.asarray(hd[:, None] == hd[None, :], BF16)
    wr = jnp.concatenate([prm["moe_w_route_group"], prm["moe_w_route_expert"],
                          jnp.zeros((D_MODEL, ROUTE_LANES - MOE_GROUPS - MOE_EXPERTS), F32)], axis=1)
    br = jnp.concatenate([prm["moe_b_route_group"], prm["moe_b_route_expert"],
                          jnp.zeros((ROUTE_LANES - MOE_GROUPS - MOE_EXPERTS,), F32)]).reshape(1, ROUTE_LANES)
    return pl.pallas_call(
        _merge_body,
        grid=(t // tm,),
        in_specs=[pl.BlockSpec((tm, D_MODEL), row)] + [pl.BlockSpec((tm, c), row)] * 4
                 + [pl.BlockSpec((1, c), full)] * 2 + [pl.BlockSpec((c, c), full),
                  pl.BlockSpec((tm, HYENA_WIDTH), row), pl.BlockSpec((tm, C_GATES), row),
                  pl.BlockSpec((RWKV_WIDTH, D_MODEL), full), pl.BlockSpec((HYENA_WIDTH, D_MODEL), full),
                  pl.BlockSpec((D_MODEL, D_MODEL), full), pl.BlockSpec((1, D_MODEL), full),
                  pl.BlockSpec((D_MODEL, ROUTE_LANES), full), pl.BlockSpec((1, ROUTE_LANES), full)],
        out_specs=[pl.BlockSpec((tm, D_MODEL), row), pl.BlockSpec((tm, D_MODEL), row),
                   pl.BlockSpec((ROUTE_LANES, tm), lambda i: (0, i))],
        out_shape=[jax.ShapeDtypeStruct((t, D_MODEL), F32), jax.ShapeDtypeStruct((t, D_MODEL), BF16),
                   jax.ShapeDtypeStruct((ROUTE_LANES, t), F32)],
        compiler_params=_cparams("parallel"),
        name="merge_route",
    )(x, *rwkv, prm["rwkv_ln_g"].reshape(1, c), prm["rwkv_ln_b"].reshape(1, c), ones_bd, yb, pg,
      wts["w_up_a"], wts["w_up_b"], wts["w_out"], prm["norm_ffn_g"].reshape(1, D_MODEL), wr, br)


MOE_PAD = 16
MOE_BLK = 256
MOE_FFN_ROWS = 256


def _moe_slots(tt):
    ns = 2 * tt + MOE_EXPERTS * MOE_PAD
    return (ns + MOE_BLK - 1) // MOE_BLK * MOE_BLK


def _moe_run_copies(i, lstart_ref, npiece_ref, goff_ref, local_refs, global_refs, sems, to_global, wait):
    for e in range(MOE_EXPERTS):
        ls = lstart_ref[i * MOE_EXPERTS + e]
        go = goff_ref[i * MOE_EXPERTS + e]

        def piece(p, carry, ls=ls, go=go):
            lrows = pl.ds(pl.multiple_of(ls + p * MOE_PAD, MOE_PAD), MOE_PAD)
            grows = pl.ds(pl.multiple_of(go + p * MOE_PAD, MOE_PAD), MOE_PAD)
            for loc, glob, sem in zip(local_refs, global_refs, sems):
                src, dst = (loc.at[lrows, :], glob.at[grows, :]) if to_global else (glob.at[grows, :], loc.at[lrows, :])
                cp = pltpu.make_async_copy(src, dst, sem)
                if wait:
                    cp.wait()
                else:
                    cp.start()
            return carry

        lax.fori_loop(0, npiece_ref[i * MOE_EXPERTS + e], piece, 0)


def _moe_compact_body(lstart_ref, npiece_ref, goff_ref, xn_ref, gwt_ref, tri_ref, pcol_ref, xg0_ref, gg0_ref,
                      xg_out, gg_out, dcol_ref, xg_loc, gg_loc, sems):
    del xg0_ref, gg0_ref
    i = pl.program_id(0)
    tt = xn_ref.shape[0]
    ns = _moe_slots(tt)
    gw = gwt_ref[0:MOE_EXPERTS, :]
    sel = gw > 0.0
    rank = jnp.dot(sel.astype(BF16), tri_ref[...], preferred_element_type=F32)
    dest = pcol_ref[:, 0:1] + rank
    d_lo = jnp.min(jnp.where(sel, dest, 1e9), axis=0, keepdims=True)
    d_hi = jnp.max(jnp.where(sel, dest, -1.0), axis=0, keepdims=True)
    g_lo = jnp.sum(jnp.where(sel & (dest == d_lo), gw, 0.0), axis=0, keepdims=True)
    g_hi = jnp.sum(jnp.where(sel & (dest == d_hi), gw, 0.0), axis=0, keepdims=True)
    single = d_hi == d_lo
    g_hi = jnp.where(single, 0.0, g_hi)
    d_hi = jnp.where(single, -1.0, d_hi)
    rows8 = jnp.concatenate([d_lo, d_hi, g_lo, g_hi, jnp.zeros((4, tt), F32)], axis=0)
    dcol_ref[...] = jnp.concatenate([rows8, jnp.zeros((ROUTE_LANES - 8, tt), F32)], axis=0).T
    for blk in range(ns // MOE_BLK):
        r = (blk * MOE_BLK + lax.broadcasted_iota(I32, (MOE_BLK, tt), 0)).astype(F32)
        lo, hi = r == d_lo, r == d_hi
        rows = slice(blk * MOE_BLK, (blk + 1) * MOE_BLK)
        xg_loc[rows, :] = jnp.dot((lo | hi).astype(BF16), xn_ref[...], preferred_element_type=F32).astype(BF16)
        row_gate = jnp.sum(jnp.where(lo, g_lo, 0.0) + jnp.where(hi, g_hi, 0.0), axis=-1, keepdims=True)
        gg_loc[rows, :] = jnp.broadcast_to(row_gate, (MOE_BLK, ROUTE_LANES))
    copies = functools.partial(_moe_run_copies, i, lstart_ref, npiece_ref, goff_ref, (xg_loc, gg_loc),
                               (xg_out, gg_out), (sems.at[0], sems.at[1]), True)
    copies(False)
    copies(True)


def _moe_ffn_body(bexp_ref, nused_ref, xg_ref, gg_ref, wg_ref, wu_ref, wd_ref, og_ref):
    del bexp_ref
    used = pl.program_id(0) < nused_ref[0]

    @pl.when(jnp.logical_not(used))
    def _():
        og_ref[...] = jnp.zeros(og_ref.shape, BF16)

    @pl.when(used)
    def _():
        xg = xg_ref[...]
        hg = jnp.dot(xg, wg_ref[0], preferred_element_type=F32)
        hu = jnp.dot(xg, wu_ref[0], preferred_element_type=F32)
        hid = ((hg * jax.nn.sigmoid(hg)) * hu).astype(BF16)
        out = jnp.dot(hid, wd_ref[0], preferred_element_type=F32)
        og_ref[...] = (out * gg_ref[:, 0:1]).astype(BF16)


def _moe_scatter_body(final_norm, lstart_ref, npiece_ref, goff_ref, og_ref, x1_ref, dcol_ref, gf_ref, o_ref,
                      og_loc, sems):
    i = pl.program_id(0)
    tt = x1_ref.shape[0]
    ns = _moe_slots(tt)

    @pl.when(i == 0)
    def _():
        og_loc[...] = jnp.zeros(og_loc.shape, BF16)

    copies = functools.partial(_moe_run_copies, i, lstart_ref, npiece_ref, goff_ref, (og_loc,), (og_ref,),
                               (sems.at[0],), False)
    copies(False)
    copies(True)
    y = x1_ref[...]
    c_lo, c_hi = dcol_ref[:, 0:1], dcol_ref[:, 1:2]
    for blk in range(ns // MOE_BLK):
        r = (blk * MOE_BLK + lax.broadcasted_iota(I32, (tt, MOE_BLK), 1)).astype(F32)
        onehot = ((r == c_lo) | (r == c_hi)).astype(BF16)
        y = y + jnp.dot(onehot, og_loc[blk * MOE_BLK:(blk + 1) * MOE_BLK, :], preferred_element_type=F32)
    if final_norm:
        y = y * lax.rsqrt(jnp.mean(y * y, axis=-1, keepdims=True) + NORM_EPS) * gf_ref[...]
    o_ref[...] = y


def _moe_grouped(xn, gwt, x1, wts, norm_final_g, tt):
    t = xn.shape[0]
    final_norm = norm_final_g is not None
    if not final_norm:
        norm_final_g = jnp.ones((D_MODEL,), F32)
    nt = t // tt
    ns = _moe_slots(tt)
    ne = MOE_EXPERTS
    counts = jnp.sum((gwt[:ne] > 0.0).reshape(ne, nt, tt), axis=-1, dtype=I32).T
    padded = (counts + MOE_PAD - 1) // MOE_PAD * MOE_PAD
    lstart = jnp.cumsum(padded, axis=1) - padded
    region = (jnp.sum(padded, axis=0) + MOE_FFN_ROWS - 1) // MOE_FFN_ROWS * MOE_FFN_ROWS
    gend = jnp.cumsum(region)
    goff = (gend - region)[None, :] + jnp.cumsum(padded, axis=0) - padded
    ng = (2 * t + nt * ne * MOE_PAD + ne * MOE_FFN_ROWS + MOE_FFN_ROWS - 1) // MOE_FFN_ROWS * MOE_FFN_ROWS
    nb = ng // MOE_FFN_ROWS
    block_row = jnp.arange(nb, dtype=I32)[:, None] * MOE_FFN_ROWS
    bexp = jnp.minimum(jnp.sum(block_row >= gend[None, :], axis=1, dtype=I32), ne - 1)
    nused = (gend[-1] // MOE_FFN_ROWS).astype(I32).reshape(1)
    pcol = jnp.broadcast_to(lstart.astype(F32).reshape(nt * ne, 1), (nt * ne, ROUTE_LANES))
    idx = np.arange(tt)
    tri = jnp.asarray(idx[:, None] < idx[None, :], BF16)
    sched = (lstart.reshape(-1), (padded // MOE_PAD).reshape(-1), goff.reshape(-1).astype(I32))
    any_spec = pl.BlockSpec(memory_space=pl.ANY)

    xg, gg, dcol = pl.pallas_call(
        _moe_compact_body,
        grid_spec=pltpu.PrefetchScalarGridSpec(
            num_scalar_prefetch=3,
            grid=(nt,),
            in_specs=[pl.BlockSpec((tt, D_MODEL), lambda i, *_: (i, 0)),
                      pl.BlockSpec((ROUTE_LANES, tt), lambda i, *_: (0, i)),
                      pl.BlockSpec((tt, tt), lambda i, *_: (0, 0)),
                      pl.BlockSpec((ne, ROUTE_LANES), lambda i, *_: (i, 0)),
                      any_spec, any_spec],
            out_specs=[any_spec, any_spec, pl.BlockSpec((tt, ROUTE_LANES), lambda i, *_: (i, 0))],
            scratch_shapes=[pltpu.VMEM((ns, D_MODEL), BF16), pltpu.VMEM((ns, ROUTE_LANES), F32),
                            pltpu.SemaphoreType.DMA((2,))],
        ),
        out_shape=[jax.ShapeDtypeStruct((ng, D_MODEL), BF16), jax.ShapeDtypeStruct((ng, ROUTE_LANES), F32),
                   jax.ShapeDtypeStruct((t, ROUTE_LANES), F32)],
        input_output_aliases={7: 0, 8: 1},
        compiler_params=_cparams("arbitrary"),
        name="moe_compact",
    )(*sched, xn, gwt, tri, pcol, jnp.zeros((ng, D_MODEL), BF16), jnp.zeros((ng, ROUTE_LANES), F32))

    og = pl.pallas_call(
        _moe_ffn_body,
        grid_spec=pltpu.PrefetchScalarGridSpec(
            num_scalar_prefetch=2,
            grid=(nb,),
            in_specs=[pl.BlockSpec((MOE_FFN_ROWS, D_MODEL), lambda b, *_: (b, 0)),
                      pl.BlockSpec((MOE_FFN_ROWS, ROUTE_LANES), lambda b, *_: (b, 0)),
                      pl.BlockSpec((1, D_MODEL, MOE_HIDDEN), lambda b, be, nu: (be[b], 0, 0)),
                      pl.BlockSpec((1, D_MODEL, MOE_HIDDEN), lambda b, be, nu: (be[b], 0, 0)),
                      pl.BlockSpec((1, MOE_HIDDEN, D_MODEL), lambda b, be, nu: (be[b], 0, 0))],
            out_specs=pl.BlockSpec((MOE_FFN_ROWS, D_MODEL), lambda b, *_: (b, 0)),
        ),
        out_shape=jax.ShapeDtypeStruct((ng, D_MODEL), BF16),
        compiler_params=_cparams("arbitrary"),
        name="moe_ffn",
    )(bexp, nused, xg, gg, wts["moe_w_gate"], wts["moe_w_up"], wts["moe_w_down"])

    return pl.pallas_call(
        functools.partial(_moe_scatter_body, final_norm),
        grid_spec=pltpu.PrefetchScalarGridSpec(
            num_scalar_prefetch=3,
            grid=(nt,),
            in_specs=[any_spec,
                      pl.BlockSpec((tt, D_MODEL), lambda i, *_: (i, 0)),
                      pl.BlockSpec((tt, ROUTE_LANES), lambda i, *_: (i, 0)),
                      pl.BlockSpec((1, D_MODEL), lambda i, *_: (0, 0))],
            out_specs=pl.BlockSpec((tt, D_MODEL), lambda i, *_: (i, 0)),
            scratch_shapes=[pltpu.VMEM((ns, D_MODEL), BF16), pltpu.SemaphoreType.DMA((1,))],
        ),
        out_shape=jax.ShapeDtypeStruct((t, D_MODEL), F32),
        compiler_params=_cparams("arbitrary"),
        name="moe_scatter",
    )(*sched, og, x1, dcol, norm_final_g.reshape(1, D_MODEL))


def _trunk(x, prm, wts, norm_final_g):
    nseq, seq_len, _ = x.shape
    xf = x.reshape(nseq * seq_len, D_MODEL)
    p_a, p_b, p_g = _norm_in_proj(xf, prm["norm_mix_g"], wts["w_in"], min(256, seq_len))
    rwkv = _rwkv_branch(p_a, nseq, seq_len, prm)
    yb = _hyena_branch(p_b, nseq, seq_len, prm)
    x1, xn, gwt = _merge(xf, rwkv, yb, p_g, prm, wts, min(512, seq_len))
    out = _moe_grouped(xn, gwt, x1, wts, norm_final_g, min(1024, seq_len))
    return out.reshape(nseq, seq_len, D_MODEL)


def kernel(x_prompt, x_sample, norm_mix_g, w_in, rwkv_mu_prev, rwkv_mu_next, rwkv_w0, rwkv_w2, rwkv_a0, rwkv_a2, rwkv_g2, rwkv_k_k, rwkv_k_a, rwkv_r_k, rwkv_ln_g, rwkv_ln_b, hyena_conv_w, hyena_conv_b, hyena_f_w1, hyena_f_b1, hyena_f_w2, hyena_f_b2, hyena_f_w3, hyena_f_freq, hyena_skip, w_up_a, w_up_b, w_out, norm_ffn_g, moe_w_route_group, moe_b_route_group, moe_w_route_expert, moe_b_route_expert, moe_w_gate, moe_w_up, moe_w_down, norm_final_g):
    layer = dict(norm_mix_g=norm_mix_g, w_in=w_in, rwkv_mu_prev=rwkv_mu_prev, rwkv_mu_next=rwkv_mu_next,
                 rwkv_w0=rwkv_w0, rwkv_w2=rwkv_w2, rwkv_a0=rwkv_a0, rwkv_a2=rwkv_a2, rwkv_g2=rwkv_g2,
                 rwkv_k_k=rwkv_k_k, rwkv_k_a=rwkv_k_a, rwkv_r_k=rwkv_r_k, rwkv_ln_g=rwkv_ln_g, rwkv_ln_b=rwkv_ln_b,
                 hyena_conv_w=hyena_conv_w, hyena_conv_b=hyena_conv_b, hyena_f_w1=hyena_f_w1, hyena_f_b1=hyena_f_b1,
                 hyena_f_w2=hyena_f_w2, hyena_f_b2=hyena_f_b2, hyena_f_w3=hyena_f_w3, hyena_f_freq=hyena_f_freq,
                 hyena_skip=hyena_skip, w_up_a=w_up_a, w_up_b=w_up_b, w_out=w_out, norm_ffn_g=norm_ffn_g,
                 moe_w_route_group=moe_w_route_group, moe_b_route_group=moe_b_route_group,
                 moe_w_route_expert=moe_w_route_expert, moe_b_route_expert=moe_b_route_expert,
                 moe_w_gate=moe_w_gate, moe_w_up=moe_w_up, moe_w_down=moe_w_down)
    depth = norm_mix_g.shape[0]
    big = ("w_in", "w_up_a", "w_up_b", "w_out", "moe_w_gate", "moe_w_up", "moe_w_down")

    def trunk(x):
        for li in range(depth):
            prm = {k: v[li] for k, v in layer.items()}
            wts = {k: prm[k].astype(BF16) for k in big}
            last = li == depth - 1
            x = _trunk(x, prm, wts, norm_final_g if last else None)
        return x

    return (trunk(x_prompt), trunk(x_sample))
```

```python
import functools
import math

import jax
import jax.numpy as jnp
import numpy as np
from jax import lax
from jax.experimental import pallas as pl
from jax.experimental.pallas import tpu as pltpu

F32 = jnp.float32
BF16 = jnp.bfloat16
I32 = jnp.int32

D_MODEL = 1024
NORM_EPS = 1e-6
RWKV_HEADS = 8
RWKV_HEAD_DIM = 64
RWKV_WIDTH = RWKV_HEADS * RWKV_HEAD_DIM
RWKV_DECAY_RANK = 64
RWKV_ICLR_RANK = 64
RWKV_GATE_RANK = 128
RWKV_GN_EPS = 64e-5
HYENA_WIDTH = D_MODEL // 2
C_RWKV_IN = 3 * RWKV_WIDTH + 2 * RWKV_DECAY_RANK + 2 * RWKV_ICLR_RANK + RWKV_GATE_RANK
C_HYENA_IN = 3 * HYENA_WIDTH
C_GATES = 2 * D_MODEL

CHUNK = 32
STACK = RWKV_HEADS * CHUNK
HALF = RWKV_WIDTH // 2
GROUP_HEADS = 4
GSTACK = GROUP_HEADS * CHUNK
LOCAL_CHUNKS_PER_ITER = 4

VMEM_LIMIT = 48 * 1024 * 1024

_NN = (((1,), (0,)), ((), ()))
_NT = (((1,), (1,)), ((), ()))
_TN = (((0,), (0,)), ((), ()))


def _dot(a, b, dims=_NN):
    return lax.dot_general(a.astype(BF16), b.astype(BF16), dims, preferred_element_type=F32)


def _split2(x):
    hi = x.astype(BF16)
    lo = (x - hi.astype(F32)).astype(BF16)
    return hi, lo


def _dot3(a, b, dims=_NN):
    ah, al = _split2(a)
    bh, bl = _split2(b)
    dg = functools.partial(lax.dot_general, dimension_numbers=dims, preferred_element_type=F32)
    return dg(ah, bh) + (dg(ah, bl) + dg(al, bh))


def _dot2(a, b, dims=_NN):
    ah, al = _split2(a)
    bh = b.astype(BF16)
    dg = functools.partial(lax.dot_general, dimension_numbers=dims, preferred_element_type=F32)
    return dg(ah, bh) + dg(al, bh)


def _dot_exact_lhs(a_bf16, x):
    x1 = x.astype(BF16)
    r1 = x - x1.astype(F32)
    x2 = r1.astype(BF16)
    x3 = (r1 - x2.astype(F32)).astype(BF16)
    dg = functools.partial(lax.dot_general, dimension_numbers=_NN, preferred_element_type=F32)
    return dg(a_bf16, x1) + (dg(a_bf16, x2) + dg(a_bf16, x3))


def _seg_sum(x, ones_bd):
    x1, x2 = _split2(x)
    dg = functools.partial(lax.dot_general, dimension_numbers=_NN, preferred_element_type=F32)
    halves = []
    for hf in range(2):
        lanes = slice(hf * HALF, (hf + 1) * HALF)
        ones_h = ones_bd[lanes, lanes]
        halves.append(dg(x1[:, lanes], ones_h) + dg(x2[:, lanes], ones_h))
    return jnp.concatenate(halves, axis=1)


def _cparams(*sem):
    return pltpu.CompilerParams(dimension_semantics=tuple(sem), vmem_limit_bytes=VMEM_LIMIT)


def _norm_in_proj_body(x_ref, g_ref, wa_ref, wb_ref, wg_ref, pa_ref, pb_ref, pg_ref):
    x = x_ref[...]
    xn = x * lax.rsqrt(jnp.mean(x * x, axis=-1, keepdims=True) + NORM_EPS) * g_ref[...]
    xb = xn.astype(BF16)
    pa_ref[...] = jnp.dot(xb, wa_ref[...], preferred_element_type=F32)
    pb_ref[...] = jnp.dot(xb, wb_ref[...], preferred_element_type=F32)
    pg_ref[...] = jnp.dot(xb, wg_ref[...], preferred_element_type=F32)


def _norm_in_proj(x, g, w_in, tm):
    t = x.shape[0]
    wa = w_in[:, :C_RWKV_IN].astype(BF16)
    wb = w_in[:, C_RWKV_IN:C_RWKV_IN + C_HYENA_IN].astype(BF16)
    wg = w_in[:, C_RWKV_IN + C_HYENA_IN:].astype(BF16)
    full = lambda i: (0, 0)
    row = lambda i: (i, 0)
    return pl.pallas_call(
        _norm_in_proj_body,
        grid=(t // tm,),
        in_specs=[pl.BlockSpec((tm, D_MODEL), row), pl.BlockSpec((1, D_MODEL), full),
                  pl.BlockSpec(wa.shape, full), pl.BlockSpec(wb.shape, full), pl.BlockSpec(wg.shape, full)],
        out_specs=[pl.BlockSpec((tm, C_RWKV_IN), row), pl.BlockSpec((tm, C_HYENA_IN), row),
                   pl.BlockSpec((tm, C_GATES), row)],
        out_shape=[jax.ShapeDtypeStruct((t, C_RWKV_IN), F32), jax.ShapeDtypeStruct((t, C_HYENA_IN), F32),
                   jax.ShapeDtypeStruct((t, C_GATES), F32)],
        compiler_params=_cparams("parallel"),
        name="norm_in_proj",
    )(x, g.reshape(1, D_MODEL), wa, wb, wg)


def _shifted(p, prev_blk, next_blk, is_first, is_last):
    tm = p.shape[0]
    row = lax.broadcasted_iota(I32, p.shape, 0)
    prow = jnp.where(is_first, 0.0, prev_blk[7:8, :])
    nrow = jnp.where(is_last, 0.0, next_blk[0:1, :])
    prev = jnp.where(row == 0, prow, pltpu.roll(p, 1, 0))
    nxt = jnp.where(row == tm - 1, nrow, pltpu.roll(p, tm - 1, 0))
    return prev, nxt


def _stack_heads(x, head_mask):
    return jnp.where(head_mask, jnp.concatenate([x] * GROUP_HEADS, axis=0), 0.0)


def _unstack_heads(z):
    out = z[0:CHUNK]
    for h in range(1, GROUP_HEADS):
        out = out + z[h * CHUNK:(h + 1) * CHUNK]
    return out


def _chunks_local(chains, fillers=()):
    fillers = list(fillers)

    def fill():
        if fillers:
            fillers.pop(0)()

    ngrp = RWKV_HEADS // GROUP_HEADS
    ti = lax.broadcasted_iota(I32, (CHUNK, CHUNK), 0)
    si = lax.broadcasted_iota(I32, (CHUNK, CHUNK), 1)
    srow = lax.broadcasted_iota(I32, (GSTACK, HALF), 0)
    slane = lax.broadcasted_iota(I32, (GSTACK, HALF), 1)
    head_mask = (srow >> 5) == (slane >> 6)
    mrow = lax.broadcasted_iota(I32, (GSTACK, GSTACK), 0)
    mcol = lax.broadcasted_iota(I32, (GSTACK, GSTACK), 1)
    same = (mrow >> 5) == (mcol >> 5)
    masks = {False: (same & (mrow > mcol), same & (mrow >= mcol)),
             True: (same & (mrow < mcol), same & (mrow <= mcol))}
    tris = {False: (ti >= si).astype(BF16), True: (ti <= si).astype(BF16)}

    cls = [_dot_exact_lhs(tris[rev], lw) for (_, _, _, lw, _, _, rev) in chains]
    st = []
    for (r, v, kk, lw, k, a, rev), cl in zip(chains, cls):
        tot = cl[0:1] if rev else cl[CHUNK - 1:CHUNK]
        e_neg = jnp.exp(-cl)
        e_tail = jnp.exp(tot - cl)
        beta = kk * a
        st.append(dict(rev=rev, v=v, alpha_b=-kk * jnp.exp(cl - lw), r_b=r * jnp.exp(cl), beta_b=beta * e_neg,
                       k_b=k * e_neg, bt=beta * e_tail, kt=k * e_tail, pc=jnp.exp(tot)))

    for s in st:
        s["sa"], s["sv"], s["lhs"], s["rhs"] = [], [], [], []
        for g in range(ngrp):
            lanes = slice(g * HALF, (g + 1) * HALF)
            sa_ = _stack_heads(s["alpha_b"][:, lanes], head_mask).astype(BF16)
            sr_ = _stack_heads(s["r_b"][:, lanes], head_mask).astype(BF16)
            s["sa"].append(sa_)
            s["sv"].append(_stack_heads(s["v"][:, lanes], head_mask).astype(BF16))
            s["lhs"].append(jnp.concatenate([sa_, sr_], axis=0))
            s["rhs"].append(jnp.concatenate([s["beta_b"][:, lanes].astype(BF16)] * GROUP_HEADS
                                            + [s["k_b"][:, lanes].astype(BF16)] * GROUP_HEADS, axis=0))
    pms = [[lax.dot_general(s["lhs"][g], s["rhs"][g], _NT, preferred_element_type=F32) for g in range(ngrp)]
           for s in st]
    zero = jnp.zeros((GSTACK, GSTACK), F32)
    drow = lax.broadcasted_iota(I32, (STACK, STACK), 0)
    dcol = lax.broadcasted_iota(I32, (STACK, STACK), 1)
    eye = jnp.where(drow == dcol, 1.0, 0.0)
    for s, pm in zip(st, pms):
        strict, incl = masks[s["rev"]]
        s["ak"] = [jnp.where(strict, pm[g][:GSTACK, GSTACK:], 0.0).astype(BF16) for g in range(ngrp)]
        s["rb"] = [jnp.where(incl, pm[g][GSTACK:, :GSTACK], 0.0).astype(BF16) for g in range(ngrp)]
        s["rk"] = [jnp.where(incl, pm[g][GSTACK:, GSTACK:], 0.0).astype(BF16) for g in range(ngrp)]
        ab = [jnp.where(strict, pm[g][:GSTACK, :GSTACK], 0.0) for g in range(ngrp)]
        s["apow"] = jnp.concatenate([jnp.concatenate([ab[0], zero], axis=1),
                                     jnp.concatenate([zero, ab[1]], axis=1)], axis=0)
        s["tinv"] = eye + s["apow"]

    for _ in range(int(math.log2(CHUNK)) - 1):
        sq = [_dot(s["apow"], s["apow"]) for s in st]
        fill()
        for s, x in zip(st, sq):
            s["apow"] = x
        pr = [_dot(s["tinv"], s["apow"]) for s in st]
        fill()
        for s, x in zip(st, pr):
            s["tinv"] = s["tinv"] + x
    while fillers:
        fill()

    akv = [[_dot(s["ak"][g], s["sv"][g]) for g in range(ngrp)] for s in st]
    rkv = [[_dot(s["rk"][g], s["sv"][g]) for g in range(ngrp)] for s in st]
    tw = [[_dot(s["tinv"][g * GSTACK:(g + 1) * GSTACK, g * GSTACK:(g + 1) * GSTACK],
                jnp.concatenate([s["sa"][g].astype(F32), akv[c][g]], axis=1)) for g in range(ngrp)]
          for c, s in enumerate(st)]
    rbx = [[_dot(s["rb"][g], tw[c][g]) for g in range(ngrp)] for c, s in enumerate(st)]
    outs = []
    for c, s in enumerate(st):
        cat = lambda f: jnp.concatenate([f(g) for g in range(ngrp)], axis=1)
        qt = s["r_b"] + cat(lambda g: _unstack_heads(rbx[c][g][:, :HALF]))
        wt = cat(lambda g: _unstack_heads(tw[c][g][:, :HALF]))
        yloc = cat(lambda g: _unstack_heads(rbx[c][g][:, HALF:] + rkv[c][g]))
        u = cat(lambda g: _unstack_heads(tw[c][g][:, HALF:]))
        outs.append((qt, wt, yloc, u, s["bt"], s["kt"], s["pc"]))
    return outs


def _rwkv_local_body(nt_seq, p_ref, pp_ref, pn_ref, mup_ref, mun_ref, w0_ref, w2_ref, a0_ref, a2_ref, g2_ref,
                     kk_ref, ka_ref, rk_ref, ones_ref,
                     g_out, bonus_out, v_out, qwf, yuf, bkf, pcf, qwr, yur, bkr, pcr,
                     r_s, v_s, kk_s, lw_s, k_s, a_s, ps_s):
    i = pl.program_id(0) % nt_seq
    p = p_ref[...]
    tm = p.shape[0]
    prev, nxt = _shifted(p, pp_ref[...], pn_ref[...], i == 0, i == nt_seq - 1)
    ps_s[...] = p + mup_ref[...] * (prev - p) + mun_ref[...] * (nxt - p)
    c = RWKV_WIDTH
    span = CHUNK * LOCAL_CHUNKS_PER_ITER

    def param_steps(h):
        rows = slice(h * span, (h + 1) * span)
        st = {}

        def split():
            ps = ps_s[rows, :]
            st["r"], st["k"], st["v"] = ps[:, 0:c], ps[:, c:2 * c], ps[:, 2 * c:3 * c]
            st["lw"], st["la"], st["lg"] = (ps[:, 3 * c + j * 128:3 * c + (j + 1) * 128] for j in range(3))
            v_out[rows, :] = st["v"]
            r_s[rows, :] = st["r"]
            v_s[rows, :] = st["v"]

        def decay_proj():
            st["w_raw"] = w0_ref[...] + _dot2(jnp.tanh(st["lw"]), w2_ref[...])

        def decay():
            lw_s[rows, :] = -jnp.exp(-jnp.logaddexp(-st["w_raw"], 0.0) - 0.5)

        def iclr():
            st["a"] = jax.nn.sigmoid(a0_ref[...] + _dot2(st["la"], a2_ref[...]))
            a_s[rows, :] = st["a"]

        def gate():
            g_out[rows, :] = _dot(jax.nn.sigmoid(st["lg"]), g2_ref[...])

        def key_norm():
            kk0 = st["k"] * kk_ref[...]
            kk_s[rows, :] = kk0 * lax.rsqrt(jnp.maximum(_seg_sum(kk0 * kk0, ones_ref[...]), 1e-24))

        def key_dir():
            ka = ka_ref[...]
            st["kf"] = st["k"] * (1.0 + (st["a"][:, :c] - 1.0) * ka)
            st["kr"] = st["k"] * (1.0 + (st["a"][:, c:] - 1.0) * ka)
            k_s[rows, :c] = st["kf"]
            k_s[rows, c:] = st["kr"]

        def bonus():
            rk = st["r"] * ((st["kf"] + st["kr"]) * 0.5) * rk_ref[...]
            bonus_out[rows, :] = _seg_sum(rk, ones_ref[...]) * st["v"]

        return [split, decay_proj, decay, iclr, gate, key_norm, key_dir, bonus]

    out_refs = ((qwf, yuf, bkf, pcf), (qwr, yur, bkr, pcr))
    nspan = tm // span
    for step in param_steps(0):
        step()
    for h in range(nspan):
        chains, where = [], []
        for uu in range(LOCAL_CHUNKS_PER_ITER):
            ci = h * LOCAL_CHUNKS_PER_ITER + uu
            rows = slice(ci * CHUNK, (ci + 1) * CHUNK)
            rc, vc, kkc = r_s[rows, :], v_s[rows, :], kk_s[rows, :]
            for d in range(2):
                lanes = slice(d * c, (d + 1) * c)
                chains.append((rc, vc, kkc, lw_s[rows, lanes], k_s[rows, lanes], a_s[rows, lanes], d == 1))
                where.append((ci, d))
        fillers = param_steps(h + 1) if h + 1 < nspan else []
        for (ci, d), (qt, wt, yloc, u, bt, kt, pcv) in zip(where, _chunks_local(chains, fillers)):
            qw, yu, bk, pc = out_refs[d]
            qw[ci] = jnp.concatenate([qt, wt], axis=0).astype(BF16)
            yu[ci] = jnp.concatenate([yloc, u], axis=0)
            bk[ci] = jnp.concatenate([bt, kt], axis=0).astype(BF16)
            pc[ci] = jnp.broadcast_to(pcv, (8, c))


def _rwkv_local(p_a, nt_seq, tm, prm):
    t = p_a.shape[0]
    nt = t // tm
    nc = tm // CHUNK
    c = RWKV_WIDTH
    rd, ri = RWKV_DECAY_RANK, RWKV_ICLR_RANK
    z = jnp.zeros((rd, c), F32)
    w2pad = jnp.concatenate([jnp.concatenate([prm["rwkv_w2"][0], z], 0), jnp.concatenate([z, prm["rwkv_w2"][1]], 0)], 1)
    z = jnp.zeros((ri, c), F32)
    a2pad = jnp.concatenate([jnp.concatenate([prm["rwkv_a2"][0], z], 0), jnp.concatenate([z, prm["rwkv_a2"][1]], 0)], 1)
    hd = np.arange(c) // RWKV_HEAD_DIM
    ones_bd = jnp.asarray(hd[:, None] == hd[None, :], BF16)
    full = lambda i: (0, 0)
    row = lambda i: (i, 0)
    row3 = lambda i: (i, 0, 0)
    nblk8 = t // 8
    tm8 = tm // 8
    vec = lambda n: pl.BlockSpec((1, n), full)
    chunk_out = lambda rows, dt: (pl.BlockSpec((nc, rows, c), row3), jax.ShapeDtypeStruct((t // CHUNK, rows, c), dt))
    per_dir = [chunk_out(2 * CHUNK, BF16), chunk_out(2 * CHUNK, F32), chunk_out(2 * CHUNK, BF16), chunk_out(8, F32)]
    outs = [(pl.BlockSpec((tm, c), row), jax.ShapeDtypeStruct((t, c), F32))] * 3 + per_dir + per_dir
    return pl.pallas_call(
        functools.partial(_rwkv_local_body, nt_seq),
        grid=(nt,),
        in_specs=[pl.BlockSpec((tm, C_RWKV_IN), row),
                  pl.BlockSpec((8, C_RWKV_IN), lambda i: (jnp.maximum(i * tm8 - 1, 0), 0)),
                  pl.BlockSpec((8, C_RWKV_IN), lambda i: (jnp.minimum((i + 1) * tm8, nblk8 - 1), 0)),
                  vec(C_RWKV_IN), vec(C_RWKV_IN), vec(2 * c), pl.BlockSpec((128, 2 * c), full),
                  vec(2 * c), pl.BlockSpec((128, 2 * c), full), pl.BlockSpec((RWKV_GATE_RANK, c), full),
                  vec(c), vec(c), vec(c), pl.BlockSpec((c, c), full)],
        out_specs=[o[0] for o in outs],
        out_shape=[o[1] for o in outs],
        scratch_shapes=[pltpu.VMEM((tm, c), F32)] * 3 + [pltpu.VMEM((tm, 2 * c), F32)] * 3
                       + [pltpu.VMEM((tm, C_RWKV_IN), F32)],
        compiler_params=_cparams("parallel"),
        name="rwkv_local",
    )(p_a, p_a, p_a, prm["rwkv_mu_prev"].reshape(1, -1), prm["rwkv_mu_next"].reshape(1, -1),
      prm["rwkv_w0"].reshape(1, 2 * c), w2pad, prm["rwkv_a0"].reshape(1, 2 * c), a2pad, prm["rwkv_g2"],
      prm["rwkv_k_k"].reshape(1, c), prm["rwkv_k_a"].reshape(1, c), prm["rwkv_r_k"].reshape(1, c), ones_bd)


SCAN_TILE = 128


def _rwkv_scan_body(nc, nseq, qwf, yuf, bkf, pcf, vf, qwr, yur, bkr, pcr, vr, yf_ref, yr_ref, s_ref):
    @pl.when(pl.program_id(0) == 0)
    def _():
        s_ref[...] = jnp.zeros(s_ref.shape, F32)

    brow = lax.broadcasted_iota(I32, (HALF, HALF), 0)
    bcol = lax.broadcasted_iota(I32, (HALF, HALF), 1)
    same_head = (brow >> 6) == (bcol >> 6)

    dirs = ((qwf, yuf, bkf, pcf, vf, yf_ref), (qwr, yur, bkr, pcr, vr, yr_ref))

    def step(j, carry):
        ch = []
        for b in range(nseq):
            for d, (qw_ref, yu_ref, bk_ref, pc_ref, v_ref, y_ref) in enumerate(dirs):
                ci = j if d == 0 else nc - 1 - j
                rows = pl.ds(pl.multiple_of(ci * CHUNK, CHUNK), CHUNK)
                qw, yu, bk = qw_ref[b, ci], yu_ref[b, ci], bk_ref[b, ci]
                pc, vc = pc_ref[b, ci][0:1], v_ref[b, rows, :]
                for hf in range(2):
                    lanes = slice(hf * HALF, (hf + 1) * HALF)
                    ch.append(dict(sidx=(b, d, hf), b=b, rows=rows, lanes=lanes, y_ref=y_ref, qw=qw[:, lanes],
                                   yu=yu[:, lanes], bk=bk[:, lanes], pc=pc[:, lanes], vc=vc[:, lanes]))
        for c in ch:
            c["s"] = s_ref[c["sidx"]]
        ys = [lax.dot_general(c["qw"], c["s"].astype(BF16), _NT, preferred_element_type=F32) + c["yu"] for c in ch]
        for c, y in zip(ch, ys):
            c["y_ref"][c["b"], c["rows"], c["lanes"]] = y[:CHUNK]
        sav = [jnp.concatenate([y[CHUNK:], c["vc"]], axis=0).astype(BF16) for c, y in zip(ch, ys)]
        upd = [lax.dot_general(x, c["bk"], _TN, preferred_element_type=F32) for c, x in zip(ch, sav)]
        for c, u in zip(ch, upd):
            s_ref[c["sidx"]] = c["s"] * c["pc"] + jnp.where(same_head, u, 0.0)
        return carry

    lax.fori_loop(0, nc, step, 0)


def _rwkv_scan(loc, nseq, seq_len):
    _, _, v, qwf, yuf, bkf, pcf, qwr, yur, bkr, pcr = loc
    tm = min(SCAN_TILE * max(1, 4 // nseq), seq_len)
    nt_seq = seq_len // tm
    nc = tm // CHUNK
    c = RWKV_WIDTH
    ncs = seq_len // CHUNK
    fwd = lambda i: (0, i, 0, 0)
    rev = lambda i: (0, nt_seq - 1 - i, 0, 0)
    fwd_v = lambda i: (0, i, 0)
    rev_v = lambda i: (0, nt_seq - 1 - i, 0)

    def specs(m4, m3):
        return [pl.BlockSpec((nseq, nc, 2 * CHUNK, c), m4), pl.BlockSpec((nseq, nc, 2 * CHUNK, c), m4),
                pl.BlockSpec((nseq, nc, 2 * CHUNK, c), m4), pl.BlockSpec((nseq, nc, 8, c), m4),
                pl.BlockSpec((nseq, tm, c), m3)]

    per_seq = lambda a: a.reshape((nseq, ncs) + a.shape[1:])
    v3 = v.reshape(nseq, seq_len, c)
    yf, yr = pl.pallas_call(
        functools.partial(_rwkv_scan_body, nc, nseq),
        grid=(nt_seq,),
        in_specs=specs(fwd, fwd_v) + specs(rev, rev_v),
        out_specs=[pl.BlockSpec((nseq, tm, c), fwd_v), pl.BlockSpec((nseq, tm, c), rev_v)],
        out_shape=[jax.ShapeDtypeStruct((nseq, seq_len, c), F32)] * 2,
        scratch_shapes=[pltpu.VMEM((nseq, 2, 2, HALF, HALF), F32)],
        compiler_params=_cparams("arbitrary"),
        name="rwkv_scan",
    )(per_seq(qwf), per_seq(yuf), per_seq(bkf), per_seq(pcf), v3,
      per_seq(qwr), per_seq(yur), per_seq(bkr), per_seq(pcr), v3)
    return yf.reshape(nseq * seq_len, c), yr.reshape(nseq * seq_len, c)


def _rwkv_out(yf, yr, g, bonus, ln_g, ln_b, ones_bd):
    y = yf + yr
    inv_n = 1.0 / RWKV_HEAD_DIM
    mean = _seg_sum(y, ones_bd) * inv_n
    yc = y - mean
    var = _seg_sum(yc * yc, ones_bd) * inv_n
    yn = yc * lax.rsqrt(var + RWKV_GN_EPS) * ln_g + ln_b
    return (yn + bonus) * g


def _rwkv_branch(p_a, nseq, seq_len, prm):
    tm = min(256, seq_len)
    nt_seq = seq_len // tm
    loc = _rwkv_local(p_a, nt_seq, tm, prm)
    yf, yr = _rwkv_scan(loc, nseq, seq_len)
    return yf, yr, loc[0], loc[1]


FFT_N2 = 128
FILTER_EMB_DIM = 33
FILTER_BANDS = (FILTER_EMB_DIM - 1) // 2
FILTER_HIDDEN = 64
FILTER_TARGET = 1e-2
FILTER_FAST_DECAY_PCT = 0.3
FILTER_SLOW_DECAY_PCT = 1.5


def _hyena_prep_body(nt_seq, p_ref, pp_ref, pn_ref, cw_ref, cb_ref, src_ref, x2_ref):
    i = pl.program_id(0) % nt_seq
    p = p_ref[...]
    prev, nxt = _shifted(p, pp_ref[...], pn_ref[...], i == 0, i == nt_seq - 1)
    cw = cw_ref[...]
    u = prev * cw[0:1] + p * cw[1:2] + nxt * cw[2:3] + cb_ref[...]
    c = HYENA_WIDTH
    src_ref[...] = u[:, 0:c] * u[:, 2 * c:3 * c]
    x2_ref[...] = u[:, c:2 * c]


def _hyena_prep(p_b, nt_seq, tm, prm):
    t = p_b.shape[0]
    c = HYENA_WIDTH
    row = lambda i: (i, 0)
    full = lambda i: (0, 0)
    nblk8 = t // 8
    tm8 = tm // 8
    cw = jnp.concatenate([prm["hyena_conv_w"], jnp.zeros((5, C_HYENA_IN), F32)], axis=0)
    return pl.pallas_call(
        functools.partial(_hyena_prep_body, nt_seq),
        grid=(t // tm,),
        in_specs=[pl.BlockSpec((tm, C_HYENA_IN), row),
                  pl.BlockSpec((8, C_HYENA_IN), lambda i: (jnp.maximum(i * tm8 - 1, 0), 0)),
                  pl.BlockSpec((8, C_HYENA_IN), lambda i: (jnp.minimum((i + 1) * tm8, nblk8 - 1), 0)),
                  pl.BlockSpec((8, C_HYENA_IN), full), pl.BlockSpec((1, C_HYENA_IN), full)],
        out_specs=[pl.BlockSpec((tm, c), row)] * 2,
        out_shape=[jax.ShapeDtypeStruct((t, c), F32)] * 2,
        compiler_params=_cparams("parallel"),
        name="hyena_prep",
    )(p_b, p_b, p_b, cw, prm["hyena_conv_b"].reshape(1, -1))


def _hyena_filter_body(seq_len, w1_ref, b1_ref, w2_ref, b2_ref, w3_ref, freq_ref, delta_ref, o_ref):
    rows = o_ref.shape[0]
    n = pl.program_id(0) * rows + lax.broadcasted_iota(I32, (rows, 128), 0)
    lane = lax.broadcasted_iota(I32, (rows, 128), 1)
    pos = jnp.where(n < seq_len, n, 2 * seq_len - n).astype(F32)
    t = pos * (1.0 / (seq_len - 1))
    omega = (2.0 * math.pi) * pos / seq_len
    band_step = (FILTER_BANDS - 1 - 1e-4) / (FILTER_BANDS - 1)
    band_idx = jnp.where(lane <= FILTER_BANDS, lane - 1, lane - 1 - FILTER_BANDS).astype(F32)
    arg = (1e-4 + band_idx * band_step) * omega
    phase = jnp.where(lane <= FILTER_BANDS, 0.5 * math.pi, math.pi)
    z = jnp.where(lane == 0, t, jnp.where(lane <= 2 * FILTER_BANDS, jnp.sin(arg + phase), 0.0))
    half = rows // 2
    freq = freq_ref[...]
    hid = jnp.sin(freq * (_dot3(z[:half], w1_ref[0]) + _dot3(z[half:], w1_ref[1]) + b1_ref[...]))
    hid = jnp.sin(freq * (_dot3(hid, w2_ref[...]) + b2_ref[...]))
    filt = jnp.concatenate([_dot3(hid, w3_ref[0]), _dot3(hid, w3_ref[1])], axis=0)
    nn = n[:, 0:1]
    tt = t[:, 0:1]
    sel = jnp.where(nn < seq_len, filt[:, :HYENA_WIDTH], jnp.where(nn > seq_len, filt[:, HYENA_WIDTH:], 0.0))
    o_ref[...] = sel * jnp.exp(-tt * delta_ref[...])


def _hyena_filter(seq_len, prm):
    rows = min(1024, 2 * seq_len)
    c = HYENA_WIDTH
    fh = FILTER_HIDDEN
    w1 = jnp.concatenate([prm["hyena_f_w1"], jnp.zeros((128 - FILTER_EMB_DIM, fh), F32)], axis=0)
    z1 = jnp.zeros_like(w1)
    w1p = jnp.stack([jnp.concatenate([w1, z1], axis=1), jnp.concatenate([z1, w1], axis=1)])
    w2 = prm["hyena_f_w2"]
    z2 = jnp.zeros_like(w2)
    w2p = jnp.concatenate([jnp.concatenate([w2, z2], axis=1), jnp.concatenate([z2, w2], axis=1)], axis=0)
    w3 = prm["hyena_f_w3"]
    z3 = jnp.zeros_like(w3)
    w3p = jnp.stack([jnp.concatenate([w3, z3], axis=0), jnp.concatenate([z3, w3], axis=0)])
    twice = lambda a: jnp.tile(a.reshape(1, fh), (1, 2))
    min_decay = math.log(FILTER_TARGET) / FILTER_SLOW_DECAY_PCT
    max_decay = math.log(FILTER_TARGET) / FILTER_FAST_DECAY_PCT
    deltas = jnp.abs(jnp.linspace(min_decay, max_decay, c, dtype=F32)).reshape(1, c)
    full = lambda i: (0, 0)
    full3 = lambda i: (0, 0, 0)
    return pl.pallas_call(
        functools.partial(_hyena_filter_body, seq_len),
        grid=(2 * seq_len // rows,),
        in_specs=[pl.BlockSpec((2, 128, 2 * fh), full3), pl.BlockSpec((1, 2 * fh), full),
                  pl.BlockSpec((2 * fh, 2 * fh), full), pl.BlockSpec((1, 2 * fh), full),
                  pl.BlockSpec((2, 2 * fh, 2 * c), full3), pl.BlockSpec((1, 2 * fh), full),
                  pl.BlockSpec((1, c), full)],
        out_specs=pl.BlockSpec((rows, c), lambda i: (i, 0)),
        out_shape=jax.ShapeDtypeStruct((2 * seq_len, c), F32),
        compiler_params=_cparams("parallel"),
        name="hyena_filter",
    )(w1p, twice(prm["hyena_f_b1"]), w2p, twice(prm["hyena_f_b2"]), w3p, twice(prm["hyena_f_freq"]), deltas)


FFT_N2_TILE = 16
FFT_LANE_TILE = 256


def _lmul_rows(m_ref, x_ref):
    m = m_ref[...]
    xt = pltpu.einshape("rjc->jrc", x_ref[0].astype(F32))
    ys = [jnp.dot(m, xt[j].astype(BF16), preferred_element_type=F32) for j in range(FFT_N2_TILE)]
    return pltpu.einshape("jrc->rjc", jnp.stack(ys, axis=0))


def _lmul_body(m_ref, x_ref, o_ref):
    o_ref[0] = _lmul_rows(m_ref, x_ref).astype(o_ref.dtype)


def _lmul_epilogue_body(m_ref, x_ref, src_ref, x2_ref, skip_ref, o_ref):
    o_ref[0] = x2_ref[0] * (_lmul_rows(m_ref, x_ref) + src_ref[0] * skip_ref[...])


def _lmul(m, x, extra=None):
    nfft, r_in, n2, c = x.shape
    r_out = m.shape[0]
    ct = FFT_LANE_TILE
    xs = pl.BlockSpec((1, r_in, FFT_N2_TILE, ct), lambda f, j, l: (f, 0, j, l))
    os_ = pl.BlockSpec((1, r_out, FFT_N2_TILE, ct), lambda f, j, l: (f, 0, j, l))
    ms = pl.BlockSpec(m.shape, lambda f, j, l: (0, 0))
    if extra is None:
        body, ins, specs, out_dtype = _lmul_body, (m, x), [ms, xs], BF16
    else:
        src, x2, skip = extra
        body, ins, out_dtype = _lmul_epilogue_body, (m, x, src, x2, skip), F32
        specs = [ms, xs, os_, os_, pl.BlockSpec((1, ct), lambda f, j, l: (0, l))]
    return pl.pallas_call(
        body,
        grid=(nfft, n2 // FFT_N2_TILE, c // ct),
        in_specs=specs,
        out_specs=os_,
        out_shape=jax.ShapeDtypeStruct((nfft, r_out, n2, c), out_dtype),
        compiler_params=_cparams("parallel", "parallel", "parallel"),
        name="fft_outer" if extra is None else "fft_outer_out",
    )(*ins)


def _fft_inner_body(kt, conv, g_ref, y_ref, *rest):
    if conv:
        h_ref, o_ref = rest
    else:
        (o_ref,) = rest
    n2 = FFT_N2
    for q in range(kt):
        g = g_ref[q]
        yk = jnp.concatenate([y_ref[0, 0, q], y_ref[0, 1, q]], axis=0).astype(BF16)
        z = jnp.dot(g, yk, preferred_element_type=F32)
        if conv:
            zr, zi = z[:n2], z[n2:]
            hr, hi = h_ref[0, 0, q], h_ref[0, 1, q]
            pr = zr * hr - zi * hi
            pi = zr * hi + zi * hr
            prod = jnp.concatenate([pr, pi], axis=0).astype(BF16)
            z = lax.dot_general(g, prod, _TN, preferred_element_type=F32)
        o_ref[0, 0, q] = z[:n2].astype(o_ref.dtype)
        o_ref[0, 1, q] = z[n2:].astype(o_ref.dtype)


def _fft_inner(g, y, h, kt):
    nfft, _, n1, n2, c = y.shape
    blk = pl.BlockSpec((1, 2, kt, n2, c), lambda f, j: (f, 0, j, 0, 0))
    gs = pl.BlockSpec((kt, 2 * n2, 2 * n2), lambda f, j: (j, 0, 0))
    conv = h is not None
    ins = (g, y, h) if conv else (g, y)
    specs = [gs, blk, pl.BlockSpec((1, 2, kt, n2, c), lambda f, j: (0, 0, j, 0, 0))] if conv else [gs, blk]
    return pl.pallas_call(
        functools.partial(_fft_inner_body, kt, conv),
        grid=(nfft, n1 // kt),
        in_specs=specs,
        out_specs=blk,
        out_shape=jax.ShapeDtypeStruct(y.shape, BF16 if conv else F32),
        compiler_params=_cparams("parallel", "parallel"),
        name="fft_inner_conv" if conv else "fft_inner",
    )(*ins)


def _dft_tables(seq_len):
    n = 2 * seq_len
    n2 = FFT_N2
    n1 = n // n2
    k1 = jnp.arange(n1, dtype=I32)
    ang1 = (2.0 * math.pi / n1) * ((k1[:, None] * k1[None, :]) % n1).astype(F32)
    c1, s1 = jnp.cos(ang1), jnp.sin(ang1)
    half = n1 // 2
    f_pair = jnp.concatenate([jnp.concatenate([c1[:, :half], s1[:, :half]], 1),
                              jnp.concatenate([-s1[:, :half], c1[:, :half]], 1)], 0)
    f_real_half = jnp.concatenate([c1[:, :half], -s1[:, :half]], 0)
    f_real_full = jnp.concatenate([c1, -s1], 0)
    ci, si = c1[:half] / n, s1[:half] / n
    b_real = jnp.concatenate([ci, -si], 1)
    b_pair = jnp.concatenate([b_real, jnp.concatenate([si, ci], 1)], 0)
    kk = k1[:, None, None] + n1 * jnp.arange(n2, dtype=I32)[None, :, None]
    ang2 = (2.0 * math.pi / n) * ((kk * jnp.arange(n2, dtype=I32)[None, None, :]) % n).astype(F32)
    c2, s2 = jnp.cos(ang2), jnp.sin(ang2)
    g = jnp.concatenate([jnp.concatenate([c2, s2], 2), jnp.concatenate([-s2, c2], 2)], 1)
    cast = lambda a: a.astype(BF16)
    return dict(f_pair=cast(f_pair), f_real_half=cast(f_real_half), f_real_full=cast(f_real_full),
                b_pair=cast(b_pair), b_real=cast(b_real), g=cast(g), n1=n1)


def _hyena_branch(p_b, nseq, seq_len, prm):
    tm = min(512, seq_len)
    c = HYENA_WIDTH
    src, x2 = _hyena_prep(p_b, seq_len // tm, tm, prm)
    tab = _dft_tables(seq_len)
    n1 = tab["n1"]
    n2 = FFT_N2
    kt = 4
    filt = _hyena_filter(seq_len, prm)
    hy = _lmul(tab["f_real_full"], filt.reshape(1, n1, n2, c))
    hspec = _fft_inner(tab["g"], hy.reshape(1, 2, n1, n2, c), None, kt)
    pair = nseq % 2 == 0
    nfft = nseq // 2 if pair else nseq
    rows = n1 if pair else n1 // 2
    xin = src.reshape(nfft, rows, n2, c)
    y = _lmul(tab["f_pair"] if pair else tab["f_real_half"], xin)
    w = _fft_inner(tab["g"], y.reshape(nfft, 2, n1, n2, c), hspec, kt)
    out = _lmul(tab["b_pair"] if pair else tab["b_real"], w.reshape(nfft, 2 * n1, n2, c),
                extra=(xin, x2.reshape(nfft, rows, n2, c), prm["hyena_skip"].reshape(1, c)))
    return out.reshape(nseq * seq_len, c)


MOE_GROUPS = 4
MOE_EXPERTS_PER_GROUP = 8
MOE_EXPERTS = MOE_GROUPS * MOE_EXPERTS_PER_GROUP
MOE_HIDDEN = D_MODEL // 2
ROUTE_LANES = 128


def _route(logits):
    lane = lax.broadcasted_iota(I32, logits.shape, 1)
    neg = -jnp.inf
    big = ROUTE_LANES
    gl = jnp.where(lane < MOE_GROUPS, logits, neg)
    gmax = jnp.max(gl, axis=-1, keepdims=True)
    grp = jnp.min(jnp.where(gl == gmax, lane, big), axis=-1, keepdims=True)
    p_grp = 1.0 / jnp.sum(jnp.exp(gl - gmax), axis=-1, keepdims=True)
    lo = MOE_GROUPS + grp * MOE_EXPERTS_PER_GROUP
    el = jnp.where((lane >= lo) & (lane < lo + MOE_EXPERTS_PER_GROUP), logits, neg)
    m1 = jnp.max(el, axis=-1, keepdims=True)
    i1 = jnp.min(jnp.where(el == m1, lane, big), axis=-1, keepdims=True)
    el2 = jnp.where(lane == i1, neg, el)
    m2 = jnp.max(el2, axis=-1, keepdims=True)
    i2 = jnp.min(jnp.where(el2 == m2, lane, big), axis=-1, keepdims=True)
    e2 = jnp.exp(m2 - m1)
    g1 = p_grp / (1.0 + e2)
    g2 = p_grp * e2 / (1.0 + e2)
    gates = jnp.where(lane == i1, g1, jnp.where(lane == i2, g2, 0.0))
    return pltpu.roll(gates, ROUTE_LANES - MOE_GROUPS, 1)


def _merge_body(x_ref, yf_ref, yr_ref, rg_ref, bonus_ref, lng_ref, lnb_ref, ones_ref, yb_ref, pg_ref,
                wua_ref, wub_ref, wo_ref, g_ref, wr_ref, br_ref, x1_ref, xn_ref, gwt_ref):
    ya = _rwkv_out(yf_ref[...], yr_ref[...], rg_ref[...], bonus_ref[...], lng_ref[...], lnb_ref[...], ones_ref[...])
    ga = jax.nn.sigmoid(pg_ref[:, :D_MODEL])
    gb = jax.nn.sigmoid(pg_ref[:, D_MODEL:])
    merged = ga * _dot(ya, wua_ref[...]) + gb * _dot(yb_ref[...], wub_ref[...])
    x1 = x_ref[...] + _dot(merged, wo_ref[...])
    x1_ref[...] = x1
    xn = x1 * lax.rsqrt(jnp.mean(x1 * x1, axis=-1, keepdims=True) + NORM_EPS) * g_ref[...]
    xn_ref[...] = xn.astype(BF16)
    logits = _dot3(xn, wr_ref[...]) + br_ref[...]
    gwt_ref[...] = _route(logits).T


def _merge(x, rwkv, yb, pg, prm, wts, tm):
    t = x.shape[0]
    c = RWKV_WIDTH
    row = lambda i: (i, 0)
    full = lambda i: (0, 0)
    hd = np.arange(c) // RWKV_HEAD_DIM
    ones_bd = jnp.asarray(hd[:, None] == hd[None, :], BF16)
    wr = jnp.concatenate([prm["moe_w_route_group"], prm["moe_w_route_expert"],
                          jnp.zeros((D_MODEL, ROUTE_LANES - MOE_GROUPS - MOE_EXPERTS), F32)], axis=1)
    br = jnp.concatenate([prm["moe_b_route_group"], prm["moe_b_route_expert"],
                          jnp.zeros((ROUTE_LANES - MOE_GROUPS - MOE_EXPERTS,), F32)]).reshape(1, ROUTE_LANES)
    return pl.pallas_call(
        _merge_body,
        grid=(t // tm,),
        in_specs=[pl.BlockSpec((tm, D_MODEL), row)] + [pl.BlockSpec((tm, c), row)] * 4
                 + [pl.BlockSpec((1, c), full)] * 2 + [pl.BlockSpec((c, c), full),
                  pl.BlockSpec((tm, HYENA_WIDTH), row), pl.BlockSpec((tm, C_GATES), row),
                  pl.BlockSpec((RWKV_WIDTH, D_MODEL), full), pl.BlockSpec((HYENA_WIDTH, D_MODEL), full),
                  pl.BlockSpec((D_MODEL, D_MODEL), full), pl.BlockSpec((1, D_MODEL), full),
                  pl.BlockSpec((D_MODEL, ROUTE_LANES), full), pl.BlockSpec((1, ROUTE_LANES), full)],
        out_specs=[pl.BlockSpec((tm, D_MODEL), row), pl.BlockSpec((tm, D_MODEL), row),
                   pl.BlockSpec((ROUTE_LANES, tm), lambda i: (0, i))],
        out_shape=[jax.ShapeDtypeStruct((t, D_MODEL), F32), jax.ShapeDtypeStruct((t, D_MODEL), BF16),
                   jax.ShapeDtypeStruct((ROUTE_LANES, t), F32)],
        compiler_params=_cparams("parallel"),
        name="merge_route",
    )(x, *rwkv, prm["rwkv_ln_g"].reshape(1, c), prm["rwkv_ln_b"].reshape(1, c), ones_bd, yb, pg,
      wts["w_up_a"], wts["w_up_b"], wts["w_out"], prm["norm_ffn_g"].reshape(1, D_MODEL), wr, br)


MOE_PAD = 16
MOE_BLK = 256
MOE_FFN_ROWS = 512


def _moe_slots(tt):
    ns = 2 * tt + MOE_EXPERTS * MOE_PAD
    return (ns + MOE_BLK - 1) // MOE_BLK * MOE_BLK


def _moe_run_copies(i, lstart_ref, npiece_ref, goff_ref, local_refs, global_refs, sems, to_global, wait):
    for e in range(MOE_EXPERTS):
        ls = lstart_ref[i * MOE_EXPERTS + e]
        go = goff_ref[i * MOE_EXPERTS + e]

        def piece(p, carry, ls=ls, go=go):
            lrows = pl.ds(pl.multiple_of(ls + p * MOE_PAD, MOE_PAD), MOE_PAD)
            grows = pl.ds(pl.multiple_of(go + p * MOE_PAD, MOE_PAD), MOE_PAD)
            for loc, glob, sem in zip(local_refs, global_refs, sems):
                src, dst = (loc.at[lrows, :], glob.at[grows, :]) if to_global else (glob.at[grows, :], loc.at[lrows, :])
                cp = pltpu.make_async_copy(src, dst, sem)
                if wait:
                    cp.wait()
                else:
                    cp.start()
            return carry

        lax.fori_loop(0, npiece_ref[i * MOE_EXPERTS + e], piece, 0)


def _moe_compact_body(lstart_ref, npiece_ref, goff_ref, xn_ref, gwt_ref, tri_ref, pcol_ref, xg0_ref, gg0_ref,
                      xg_out, gg_out, dcol_ref, xg_loc, gg_loc, sems):
    del xg0_ref, gg0_ref
    i = pl.program_id(0)
    tt = xn_ref.shape[0]
    ns = _moe_slots(tt)
    gw = gwt_ref[0:MOE_EXPERTS, :]
    sel = gw > 0.0
    rank = jnp.dot(sel.astype(BF16), tri_ref[...], preferred_element_type=F32)
    dest = pcol_ref[:, 0:1] + rank
    d_lo = jnp.min(jnp.where(sel, dest, 1e9), axis=0, keepdims=True)
    d_hi = jnp.max(jnp.where(sel, dest, -1.0), axis=0, keepdims=True)
    g_lo = jnp.sum(jnp.where(sel & (dest == d_lo), gw, 0.0), axis=0, keepdims=True)
    g_hi = jnp.sum(jnp.where(sel & (dest == d_hi), gw, 0.0), axis=0, keepdims=True)
    single = d_hi == d_lo
    g_hi = jnp.where(single, 0.0, g_hi)
    d_hi = jnp.where(single, -1.0, d_hi)
    rows8 = jnp.concatenate([d_lo, d_hi, g_lo, g_hi, jnp.zeros((4, tt), F32)], axis=0)
    dcol_ref[...] = jnp.concatenate([rows8, jnp.zeros((ROUTE_LANES - 8, tt), F32)], axis=0).T
    for blk in range(ns // MOE_BLK):
        r = (blk * MOE_BLK + lax.broadcasted_iota(I32, (MOE_BLK, tt), 0)).astype(F32)
        lo, hi = r == d_lo, r == d_hi
        rows = slice(blk * MOE_BLK, (blk + 1) * MOE_BLK)
        xg_loc[rows, :] = jnp.dot((lo | hi).astype(BF16), xn_ref[...], preferred_element_type=F32).astype(BF16)
        row_gate = jnp.sum(jnp.where(lo, g_lo, 0.0) + jnp.where(hi, g_hi, 0.0), axis=-1, keepdims=True)
        gg_loc[rows, :] = jnp.broadcast_to(row_gate, (MOE_BLK, ROUTE_LANES))
    copies = functools.partial(_moe_run_copies, i, lstart_ref, npiece_ref, goff_ref, (xg_loc, gg_loc),
                               (xg_out, gg_out), (sems.at[0], sems.at[1]), True)
    copies(False)
    copies(True)


def _moe_ffn_body(bexp_ref, nused_ref, xg_ref, gg_ref, wg_ref, wu_ref, wd_ref, og_ref):
    del bexp_ref
    used = pl.program_id(0) < nused_ref[0]

    @pl.when(jnp.logical_not(used))
    def _():
        og_ref[...] = jnp.zeros(og_ref.shape, BF16)

    @pl.when(used)
    def _():
        xg = xg_ref[...]
        hg = jnp.dot(xg, wg_ref[0], preferred_element_type=F32)
        hu = jnp.dot(xg, wu_ref[0], preferred_element_type=F32)
        hid = ((hg * jax.nn.sigmoid(hg)) * hu).astype(BF16)
        out = jnp.dot(hid, wd_ref[0], preferred_element_type=F32)
        og_ref[...] = (out * gg_ref[:, 0:1]).astype(BF16)


def _moe_scatter_body(final_norm, lstart_ref, npiece_ref, goff_ref, og_ref, x1_ref, dcol_ref, gf_ref, o_ref,
                      og_loc, sems):
    i = pl.program_id(0)
    tt = x1_ref.shape[0]
    ns = _moe_slots(tt)

    @pl.when(i == 0)
    def _():
        og_loc[...] = jnp.zeros(og_loc.shape, BF16)

    copies = functools.partial(_moe_run_copies, i, lstart_ref, npiece_ref, goff_ref, (og_loc,), (og_ref,),
                               (sems.at[0],), False)
    copies(False)
    copies(True)
    y = x1_ref[...]
    c_lo, c_hi = dcol_ref[:, 0:1], dcol_ref[:, 1:2]
    for blk in range(ns // MOE_BLK):
        r = (blk * MOE_BLK + lax.broadcasted_iota(I32, (tt, MOE_BLK), 1)).astype(F32)
        onehot = ((r == c_lo) | (r == c_hi)).astype(BF16)
        y = y + jnp.dot(onehot, og_loc[blk * MOE_BLK:(blk + 1) * MOE_BLK, :], preferred_element_type=F32)
    if final_norm:
        y = y * lax.rsqrt(jnp.mean(y * y, axis=-1, keepdims=True) + NORM_EPS) * gf_ref[...]
    o_ref[...] = y


def _moe_grouped(xn, gwt, x1, wts, norm_final_g, tt):
    t = xn.shape[0]
    final_norm = norm_final_g is not None
    if not final_norm:
        norm_final_g = jnp.ones((D_MODEL,), F32)
    nt = t // tt
    ns = _moe_slots(tt)
    ne = MOE_EXPERTS
    counts = jnp.sum((gwt[:ne] > 0.0).reshape(ne, nt, tt), axis=-1, dtype=I32).T
    padded = (counts + MOE_PAD - 1) // MOE_PAD * MOE_PAD
    lstart = jnp.cumsum(padded, axis=1) - padded
    region = (jnp.sum(padded, axis=0) + MOE_FFN_ROWS - 1) // MOE_FFN_ROWS * MOE_FFN_ROWS
    gend = jnp.cumsum(region)
    goff = (gend - region)[None, :] + jnp.cumsum(padded, axis=0) - padded
    ng = (2 * t + nt * ne * MOE_PAD + ne * MOE_FFN_ROWS + MOE_FFN_ROWS - 1) // MOE_FFN_ROWS * MOE_FFN_ROWS
    nb = ng // MOE_FFN_ROWS
    block_row = jnp.arange(nb, dtype=I32)[:, None] * MOE_FFN_ROWS
    bexp = jnp.minimum(jnp.sum(block_row >= gend[None, :], axis=1, dtype=I32), ne - 1)
    nused = (gend[-1] // MOE_FFN_ROWS).astype(I32).reshape(1)
    pcol = jnp.broadcast_to(lstart.astype(F32).reshape(nt * ne, 1), (nt * ne, ROUTE_LANES))
    idx = np.arange(tt)
    tri = jnp.asarray(idx[:, None] < idx[None, :], BF16)
    sched = (lstart.reshape(-1), (padded // MOE_PAD).reshape(-1), goff.reshape(-1).astype(I32))
    any_spec = pl.BlockSpec(memory_space=pl.ANY)

    xg, gg, dcol = pl.pallas_call(
        _moe_compact_body,
        grid_spec=pltpu.PrefetchScalarGridSpec(
            num_scalar_prefetch=3,
            grid=(nt,),
            in_specs=[pl.BlockSpec((tt, D_MODEL), lambda i, *_: (i, 0)),
                      pl.BlockSpec((ROUTE_LANES, tt), lambda i, *_: (0, i)),
                      pl.BlockSpec((tt, tt), lambda i, *_: (0, 0)),
                      pl.BlockSpec((ne, ROUTE_LANES), lambda i, *_: (i, 0)),
                      any_spec, any_spec],
            out_specs=[any_spec, any_spec, pl.BlockSpec((tt, ROUTE_LANES), lambda i, *_: (i, 0))],
            scratch_shapes=[pltpu.VMEM((ns, D_MODEL), BF16), pltpu.VMEM((ns, ROUTE_LANES), F32),
                            pltpu.SemaphoreType.DMA((2,))],
        ),
        out_shape=[jax.ShapeDtypeStruct((ng, D_MODEL), BF16), jax.ShapeDtypeStruct((ng, ROUTE_LANES), F32),
                   jax.ShapeDtypeStruct((t, ROUTE_LANES), F32)],
        input_output_aliases={7: 0, 8: 1},
        compiler_params=_cparams("arbitrary"),
        name="moe_compact",
    )(*sched, xn, gwt, tri, pcol, jnp.zeros((ng, D_MODEL), BF16), jnp.zeros((ng, ROUTE_LANES), F32))

    og = pl.pallas_call(
        _moe_ffn_body,
        grid_spec=pltpu.PrefetchScalarGridSpec(
            num_scalar_prefetch=2,
            grid=(nb,),
            in_specs=[pl.BlockSpec((MOE_FFN_ROWS, D_MODEL), lambda b, *_: (b, 0)),
                      pl.BlockSpec((MOE_FFN_ROWS, ROUTE_LANES), lambda b, *_: (b, 0)),
                      pl.BlockSpec((1, D_MODEL, MOE_HIDDEN), lambda b, be, nu: (be[b], 0, 0)),
                      pl.BlockSpec((1, D_MODEL, MOE_HIDDEN), lambda b, be, nu: (be[b], 0, 0)),
                      pl.BlockSpec((1, MOE_HIDDEN, D_MODEL), lambda b, be, nu: (be[b], 0, 0))],
            out_specs=pl.BlockSpec((MOE_FFN_ROWS, D_MODEL), lambda b, *_: (b, 0)),
        ),
        out_shape=jax.ShapeDtypeStruct((ng, D_MODEL), BF16),
        compiler_params=_cparams("arbitrary"),
        name="moe_ffn",
    )(bexp, nused, xg, gg, wts["moe_w_gate"], wts["moe_w_up"], wts["moe_w_down"])

    return pl.pallas_call(
        functools.partial(_moe_scatter_body, final_norm),
        grid_spec=pltpu.PrefetchScalarGridSpec(
            num_scalar_prefetch=3,
            grid=(nt,),
            in_specs=[any_spec,
                      pl.BlockSpec((tt, D_MODEL), lambda i, *_: (i, 0)),
                      pl.BlockSpec((tt, ROUTE_LANES), lambda i, *_: (i, 0)),
                      pl.BlockSpec((1, D_MODEL), lambda i, *_: (0, 0))],
            out_specs=pl.BlockSpec((tt, D_MODEL), lambda i, *_: (i, 0)),
            scratch_shapes=[pltpu.VMEM((ns, D_MODEL), BF16), pltpu.SemaphoreType.DMA((1,))],
        ),
        out_shape=jax.ShapeDtypeStruct((t, D_MODEL), F32),
        compiler_params=_cparams("arbitrary"),
        name="moe_scatter",
    )(*sched, og, x1, dcol, norm_final_g.reshape(1, D_MODEL))


def _trunk(x, prm, wts, norm_final_g):
    nseq, seq_len, _ = x.shape
    xf = x.reshape(nseq * seq_len, D_MODEL)
    p_a, p_b, p_g = _norm_in_proj(xf, prm["norm_mix_g"], wts["w_in"], min(256, seq_len))
    rwkv = _rwkv_branch(p_a, nseq, seq_len, prm)
    yb = _hyena_branch(p_b, nseq, seq_len, prm)
    x1, xn, gwt = _merge(xf, rwkv, yb, p_g, prm, wts, min(512, seq_len))
    out = _moe_grouped(xn, gwt, x1, wts, norm_final_g, min(1024, seq_len))
    return out.reshape(nseq, seq_len, D_MODEL)


def kernel(x_prompt, x_sample, norm_mix_g, w_in, rwkv_mu_prev, rwkv_mu_next, rwkv_w0, rwkv_w2, rwkv_a0, rwkv_a2, rwkv_g2, rwkv_k_k, rwkv_k_a, rwkv_r_k, rwkv_ln_g, rwkv_ln_b, hyena_conv_w, hyena_conv_b, hyena_f_w1, hyena_f_b1, hyena_f_w2, hyena_f_b2, hyena_f_w3, hyena_f_freq, hyena_skip, w_up_a, w_up_b, w_out, norm_ffn_g, moe_w_route_group, moe_b_route_group, moe_w_route_expert, moe_b_route_expert, moe_w_gate, moe_w_up, moe_w_down, norm_final_g):
    layer = dict(norm_mix_g=norm_mix_g, w_in=w_in, rwkv_mu_prev=rwkv_mu_prev, rwkv_mu_next=rwkv_mu_next,
                 rwkv_w0=rwkv_w0, rwkv_w2=rwkv_w2, rwkv_a0=rwkv_a0, rwkv_a2=rwkv_a2, rwkv_g2=rwkv_g2,
                 rwkv_k_k=rwkv_k_k, rwkv_k_a=rwkv_k_a, rwkv_r_k=rwkv_r_k, rwkv_ln_g=rwkv_ln_g, rwkv_ln_b=rwkv_ln_b,
                 hyena_conv_w=hyena_conv_w, hyena_conv_b=hyena_conv_b, hyena_f_w1=hyena_f_w1, hyena_f_b1=hyena_f_b1,
                 hyena_f_w2=hyena_f_w2, hyena_f_b2=hyena_f_b2, hyena_f_w3=hyena_f_w3, hyena_f_freq=hyena_f_freq,
                 hyena_skip=hyena_skip, w_up_a=w_up_a, w_up_b=w_up_b, w_out=w_out, norm_ffn_g=norm_ffn_g,
                 moe_w_route_group=moe_w_route_group, moe_b_route_group=moe_b_route_group,
                 moe_w_route_expert=moe_w_route_expert, moe_b_route_expert=moe_b_route_expert,
                 moe_w_gate=moe_w_gate, moe_w_up=moe_w_up, moe_w_down=moe_w_down)
    depth = norm_mix_g.shape[0]
    big = ("w_in", "w_up_a", "w_up_b", "w_out", "moe_w_gate", "moe_w_up", "moe_w_down")

    def trunk(x):
        for li in range(depth):
            prm = {k: v[li] for k, v in layer.items()}
            wts = {k: prm[k].astype(BF16) for k in big}
            last = li == depth - 1
            x = _trunk(x, prm, wts, norm_final_g if last else None)
        return x

    return (trunk(x_prompt), trunk(x_sample))
```

```python
import functools
import math

import jax
import jax.numpy as jnp
import numpy as np
from jax import lax
from jax.experimental import pallas as pl
from jax.experimental.pallas import tpu as pltpu

F32 = jnp.float32
BF16 = jnp.bfloat16
I32 = jnp.int32

D_MODEL = 1024
NORM_EPS = 1e-6
RWKV_HEADS = 8
RWKV_HEAD_DIM = 64
RWKV_WIDTH = RWKV_HEADS * RWKV_HEAD_DIM
RWKV_DECAY_RANK = 64
RWKV_ICLR_RANK = 64
RWKV_GATE_RANK = 128
RWKV_GN_EPS = 64e-5
HYENA_WIDTH = D_MODEL // 2
C_RWKV_IN = 3 * RWKV_WIDTH + 2 * RWKV_DECAY_RANK + 2 * RWKV_ICLR_RANK + RWKV_GATE_RANK
C_HYENA_IN = 3 * HYENA_WIDTH
C_GATES = 2 * D_MODEL

CHUNK = 32
STACK = RWKV_HEADS * CHUNK
HALF = RWKV_WIDTH // 2
GROUP_HEADS = 4
GSTACK = GROUP_HEADS * CHUNK
LOCAL_CHUNKS_PER_ITER = 4

VMEM_LIMIT = 48 * 1024 * 1024

_NN = (((1,), (0,)), ((), ()))
_NT = (((1,), (1,)), ((), ()))
_TN = (((0,), (0,)), ((), ()))


def _dot(a, b, dims=_NN):
    return lax.dot_general(a.astype(BF16), b.astype(BF16), dims, preferred_element_type=F32)


def _split2(x):
    hi = x.astype(BF16)
    lo = (x - hi.astype(F32)).astype(BF16)
    return hi, lo


def _dot3(a, b, dims=_NN):
    ah, al = _split2(a)
    bh, bl = _split2(b)
    dg = functools.partial(lax.dot_general, dimension_numbers=dims, preferred_element_type=F32)
    return dg(ah, bh) + (dg(ah, bl) + dg(al, bh))


def _dot2(a, b, dims=_NN):
    ah, al = _split2(a)
    bh = b.astype(BF16)
    dg = functools.partial(lax.dot_general, dimension_numbers=dims, preferred_element_type=F32)
    return dg(ah, bh) + dg(al, bh)


def _dot_exact_lhs(a_bf16, x):
    x1 = x.astype(BF16)
    r1 = x - x1.astype(F32)
    x2 = r1.astype(BF16)
    x3 = (r1 - x2.astype(F32)).astype(BF16)
    dg = functools.partial(lax.dot_general, dimension_numbers=_NN, preferred_element_type=F32)
    return dg(a_bf16, x1) + (dg(a_bf16, x2) + dg(a_bf16, x3))


def _seg_sum(x, ones_bd):
    x1, x2 = _split2(x)
    dg = functools.partial(lax.dot_general, dimension_numbers=_NN, preferred_element_type=F32)
    halves = []
    for hf in range(2):
        lanes = slice(hf * HALF, (hf + 1) * HALF)
        ones_h = ones_bd[lanes, lanes]
        halves.append(dg(x1[:, lanes], ones_h) + dg(x2[:, lanes], ones_h))
    return jnp.concatenate(halves, axis=1)


def _cparams(*sem):
    return pltpu.CompilerParams(dimension_semantics=tuple(sem), vmem_limit_bytes=VMEM_LIMIT)


def _norm_in_proj_body(x_ref, g_ref, wa_ref, wb_ref, wg_ref, pa_ref, pb_ref, pg_ref):
    x = x_ref[...]
    xn = x * lax.rsqrt(jnp.mean(x * x, axis=-1, keepdims=True) + NORM_EPS) * g_ref[...]
    xb = xn.astype(BF16)
    pa_ref[...] = jnp.dot(xb, wa_ref[...], preferred_element_type=F32)
    pb_ref[...] = jnp.dot(xb, wb_ref[...], preferred_element_type=F32)
    pg_ref[...] = jnp.dot(xb, wg_ref[...], preferred_element_type=F32).astype(BF16)


def _norm_in_proj(x, g, w_in, tm):
    t = x.shape[0]
    wa = w_in[:, :C_RWKV_IN].astype(BF16)
    wb = w_in[:, C_RWKV_IN:C_RWKV_IN + C_HYENA_IN].astype(BF16)
    wg = w_in[:, C_RWKV_IN + C_HYENA_IN:].astype(BF16)
    full = lambda i: (0, 0)
    row = lambda i: (i, 0)
    return pl.pallas_call(
        _norm_in_proj_body,
        grid=(t // tm,),
        in_specs=[pl.BlockSpec((tm, D_MODEL), row), pl.BlockSpec((1, D_MODEL), full),
                  pl.BlockSpec(wa.shape, full), pl.BlockSpec(wb.shape, full), pl.BlockSpec(wg.shape, full)],
        out_specs=[pl.BlockSpec((tm, C_RWKV_IN), row), pl.BlockSpec((tm, C_HYENA_IN), row),
                   pl.BlockSpec((tm, C_GATES), row)],
        out_shape=[jax.ShapeDtypeStruct((t, C_RWKV_IN), F32), jax.ShapeDtypeStruct((t, C_HYENA_IN), F32),
                   jax.ShapeDtypeStruct((t, C_GATES), BF16)],
        compiler_params=_cparams("parallel"),
        name="norm_in_proj",
    )(x, g.reshape(1, D_MODEL), wa, wb, wg)


def _shifted(p, prev_blk, next_blk, is_first, is_last):
    tm = p.shape[0]
    row = lax.broadcasted_iota(I32, p.shape, 0)
    prow = jnp.where(is_first, 0.0, prev_blk[7:8, :])
    nrow = jnp.where(is_last, 0.0, next_blk[0:1, :])
    prev = jnp.where(row == 0, prow, pltpu.roll(p, 1, 0))
    nxt = jnp.where(row == tm - 1, nrow, pltpu.roll(p, tm - 1, 0))
    return prev, nxt


def _stack_heads(x, head_mask):
    return jnp.where(head_mask, jnp.concatenate([x] * GROUP_HEADS, axis=0), 0.0)


def _unstack_heads(z):
    out = z[0:CHUNK]
    for h in range(1, GROUP_HEADS):
        out = out + z[h * CHUNK:(h + 1) * CHUNK]
    return out


def _chunks_local(chains, fillers=()):
    fillers = list(fillers)

    def fill():
        if fillers:
            fillers.pop(0)()

    ngrp = RWKV_HEADS // GROUP_HEADS
    ti = lax.broadcasted_iota(I32, (CHUNK, CHUNK), 0)
    si = lax.broadcasted_iota(I32, (CHUNK, CHUNK), 1)
    srow = lax.broadcasted_iota(I32, (GSTACK, HALF), 0)
    slane = lax.broadcasted_iota(I32, (GSTACK, HALF), 1)
    head_mask = (srow >> 5) == (slane >> 6)
    mrow = lax.broadcasted_iota(I32, (GSTACK, GSTACK), 0)
    mcol = lax.broadcasted_iota(I32, (GSTACK, GSTACK), 1)
    same = (mrow >> 5) == (mcol >> 5)
    masks = {False: (same & (mrow > mcol), same & (mrow >= mcol)),
             True: (same & (mrow < mcol), same & (mrow <= mcol))}
    tris = {False: (ti >= si).astype(BF16), True: (ti <= si).astype(BF16)}

    cls = [_dot_exact_lhs(tris[rev], lw) for (_, _, _, lw, _, _, rev) in chains]
    st = []
    for (r, v, kk, lw, k, a, rev), cl in zip(chains, cls):
        tot = cl[0:1] if rev else cl[CHUNK - 1:CHUNK]
        e_neg = jnp.exp(-cl)
        e_tail = jnp.exp(tot - cl)
        beta = kk * a
        st.append(dict(rev=rev, v=v, alpha_b=-kk * jnp.exp(cl - lw), r_b=r * jnp.exp(cl), beta_b=beta * e_neg,
                       k_b=k * e_neg, bt=beta * e_tail, kt=k * e_tail, pc=jnp.exp(tot)))

    for s in st:
        s["sa"], s["sv"], s["lhs"], s["rhs"] = [], [], [], []
        for g in range(ngrp):
            lanes = slice(g * HALF, (g + 1) * HALF)
            sa_ = _stack_heads(s["alpha_b"][:, lanes], head_mask).astype(BF16)
            sr_ = _stack_heads(s["r_b"][:, lanes], head_mask).astype(BF16)
            s["sa"].append(sa_)
            s["sv"].append(_stack_heads(s["v"][:, lanes], head_mask).astype(BF16))
            s["lhs"].append(jnp.concatenate([sa_, sr_], axis=0))
            s["rhs"].append(jnp.concatenate([s["beta_b"][:, lanes].astype(BF16)] * GROUP_HEADS
                                            + [s["k_b"][:, lanes].astype(BF16)] * GROUP_HEADS, axis=0))
    pms = [[lax.dot_general(s["lhs"][g], s["rhs"][g], _NT, preferred_element_type=F32) for g in range(ngrp)]
           for s in st]
    zero = jnp.zeros((GSTACK, GSTACK), F32)
    drow = lax.broadcasted_iota(I32, (STACK, STACK), 0)
    dcol = lax.broadcasted_iota(I32, (STACK, STACK), 1)
    eye = jnp.where(drow == dcol, 1.0, 0.0)
    for s, pm in zip(st, pms):
        strict, incl = masks[s["rev"]]
        s["ak"] = [jnp.where(strict, pm[g][:GSTACK, GSTACK:], 0.0).astype(BF16) for g in range(ngrp)]
        s["rb"] = [jnp.where(incl, pm[g][GSTACK:, :GSTACK], 0.0).astype(BF16) for g in range(ngrp)]
        s["rk"] = [jnp.where(incl, pm[g][GSTACK:, GSTACK:], 0.0).astype(BF16) for g in range(ngrp)]
        ab = [jnp.where(strict, pm[g][:GSTACK, :GSTACK], 0.0) for g in range(ngrp)]
        s["apow"] = jnp.concatenate([jnp.concatenate([ab[0], zero], axis=1),
                                     jnp.concatenate([zero, ab[1]], axis=1)], axis=0)
        s["tinv"] = eye + s["apow"]

    for _ in range(int(math.log2(CHUNK)) - 1):
        sq = [_dot(s["apow"], s["apow"]) for s in st]
        fill()
        for s, x in zip(st, sq):
            s["apow"] = x
        pr = [_dot(s["tinv"], s["apow"]) for s in st]
        fill()
        for s, x in zip(st, pr):
            s["tinv"] = s["tinv"] + x
    while fillers:
        fill()

    akv = [[_dot(s["ak"][g], s["sv"][g]) for g in range(ngrp)] for s in st]
    rkv = [[_dot(s["rk"][g], s["sv"][g]) for g in range(ngrp)] for s in st]
    tw = [[_dot(s["tinv"][g * GSTACK:(g + 1) * GSTACK, g * GSTACK:(g + 1) * GSTACK],
                jnp.concatenate([s["sa"][g].astype(F32), akv[c][g]], axis=1)) for g in range(ngrp)]
          for c, s in enumerate(st)]
    rbx = [[_dot(s["rb"][g], tw[c][g]) for g in range(ngrp)] for c, s in enumerate(st)]
    outs = []
    for c, s in enumerate(st):
        cat = lambda f: jnp.concatenate([f(g) for g in range(ngrp)], axis=1)
        qt = s["r_b"] + cat(lambda g: _unstack_heads(rbx[c][g][:, :HALF]))
        wt = cat(lambda g: _unstack_heads(tw[c][g][:, :HALF]))
        yloc = cat(lambda g: _unstack_heads(rbx[c][g][:, HALF:] + rkv[c][g]))
        u = cat(lambda g: _unstack_heads(tw[c][g][:, HALF:]))
        outs.append((qt, wt, yloc, u, s["bt"], s["kt"], s["pc"]))
    return outs


def _rwkv_local_body(nt_seq, p_ref, pp_ref, pn_ref, mup_ref, mun_ref, w0_ref, w2_ref, a0_ref, a2_ref, g2_ref,
                     kk_ref, ka_ref, rk_ref, ones_ref,
                     g_out, bonus_out, v_out, qwf, yuf, bkf, pcf, qwr, yur, bkr, pcr,
                     r_s, v_s, kk_s, lw_s, k_s, a_s, ps_s):
    i = pl.program_id(0) % nt_seq
    p = p_ref[...]
    tm = p.shape[0]
    prev, nxt = _shifted(p, pp_ref[...], pn_ref[...], i == 0, i == nt_seq - 1)
    ps_s[...] = p + mup_ref[...] * (prev - p) + mun_ref[...] * (nxt - p)
    c = RWKV_WIDTH
    span = CHUNK * LOCAL_CHUNKS_PER_ITER

    def param_steps(h):
        rows = slice(h * span, (h + 1) * span)
        st = {}

        def split():
            ps = ps_s[rows, :]
            st["r"], st["k"], st["v"] = ps[:, 0:c], ps[:, c:2 * c], ps[:, 2 * c:3 * c]
            st["lw"], st["la"], st["lg"] = (ps[:, 3 * c + j * 128:3 * c + (j + 1) * 128] for j in range(3))
            v_out[rows, :] = st["v"]
            r_s[rows, :] = st["r"]
            v_s[rows, :] = st["v"]

        def decay_proj():
            st["w_raw"] = w0_ref[...] + _dot2(jnp.tanh(st["lw"]), w2_ref[...])

        def decay():
            lw_s[rows, :] = -jnp.exp(-jnp.logaddexp(-st["w_raw"], 0.0) - 0.5)

        def iclr():
            st["a"] = jax.nn.sigmoid(a0_ref[...] + _dot2(st["la"], a2_ref[...]))
            a_s[rows, :] = st["a"]

        def gate():
            g_out[rows, :] = _dot(jax.nn.sigmoid(st["lg"]), g2_ref[...])

        def key_norm():
            kk0 = st["k"] * kk_ref[...]
            kk_s[rows, :] = kk0 * lax.rsqrt(jnp.maximum(_seg_sum(kk0 * kk0, ones_ref[...]), 1e-24))

        def key_dir():
            ka = ka_ref[...]
            st["kf"] = st["k"] * (1.0 + (st["a"][:, :c] - 1.0) * ka)
            st["kr"] = st["k"] * (1.0 + (st["a"][:, c:] - 1.0) * ka)
            k_s[rows, :c] = st["kf"]
            k_s[rows, c:] = st["kr"]

        def bonus():
            rk = st["r"] * ((st["kf"] + st["kr"]) * 0.5) * rk_ref[...]
            bonus_out[rows, :] = _seg_sum(rk, ones_ref[...]) * st["v"]

        return [split, decay_proj, decay, iclr, gate, key_norm, key_dir, bonus]

    out_refs = ((qwf, yuf, bkf, pcf), (qwr, yur, bkr, pcr))
    nspan = tm // span
    for step in param_steps(0):
        step()
    for h in range(nspan):
        chains, where = [], []
        for uu in range(LOCAL_CHUNKS_PER_ITER):
            ci = h * LOCAL_CHUNKS_PER_ITER + uu
            rows = slice(ci * CHUNK, (ci + 1) * CHUNK)
            rc, vc, kkc = r_s[rows, :], v_s[rows, :], kk_s[rows, :]
            for d in range(2):
                lanes = slice(d * c, (d + 1) * c)
                chains.append((rc, vc, kkc, lw_s[rows, lanes], k_s[rows, lanes], a_s[rows, lanes], d == 1))
                where.append((ci, d))
        fillers = param_steps(h + 1) if h + 1 < nspan else []
        for (ci, d), (qt, wt, yloc, u, bt, kt, pcv) in zip(where, _chunks_local(chains, fillers)):
            qw, yu, bk, pc = out_refs[d]
            qw[ci] = jnp.concatenate([qt, wt], axis=0).astype(BF16)
            yu[ci] = jnp.concatenate([yloc, u], axis=0)
            bk[ci] = jnp.concatenate([bt, kt], axis=0).astype(BF16)
            pc[ci] = jnp.broadcast_to(pcv, (8, c))


def _rwkv_local(p_a, nt_seq, tm, prm):
    t = p_a.shape[0]
    nt = t // tm
    nc = tm // CHUNK
    c = RWKV_WIDTH
    rd, ri = RWKV_DECAY_RANK, RWKV_ICLR_RANK
    z = jnp.zeros((rd, c), F32)
    w2pad = jnp.concatenate([jnp.concatenate([prm["rwkv_w2"][0], z], 0), jnp.concatenate([z, prm["rwkv_w2"][1]], 0)], 1)
    z = jnp.zeros((ri, c), F32)
    a2pad = jnp.concatenate([jnp.concatenate([prm["rwkv_a2"][0], z], 0), jnp.concatenate([z, prm["rwkv_a2"][1]], 0)], 1)
    hd = np.arange(c) // RWKV_HEAD_DIM
    ones_bd = jnp.asarray(hd[:, None] == hd[None, :], BF16)
    full = lambda i: (0, 0)
    row = lambda i: (i, 0)
    row3 = lambda i: (i, 0, 0)
    nblk8 = t // 8
    tm8 = tm // 8
    vec = lambda n: pl.BlockSpec((1, n), full)
    chunk_out = lambda rows, dt: (pl.BlockSpec((nc, rows, c), row3), jax.ShapeDtypeStruct((t // CHUNK, rows, c), dt))
    per_dir = [chunk_out(2 * CHUNK, BF16), chunk_out(2 * CHUNK, F32), chunk_out(2 * CHUNK, BF16), chunk_out(8, F32)]
    outs = [(pl.BlockSpec((tm, c), row), jax.ShapeDtypeStruct((t, c), F32))] * 3 + per_dir + per_dir
    return pl.pallas_call(
        functools.partial(_rwkv_local_body, nt_seq),
        grid=(nt,),
        in_specs=[pl.BlockSpec((tm, C_RWKV_IN), row),
                  pl.BlockSpec((8, C_RWKV_IN), lambda i: (jnp.maximum(i * tm8 - 1, 0), 0)),
                  pl.BlockSpec((8, C_RWKV_IN), lambda i: (jnp.minimum((i + 1) * tm8, nblk8 - 1), 0)),
                  vec(C_RWKV_IN), vec(C_RWKV_IN), vec(2 * c), pl.BlockSpec((128, 2 * c), full),
                  vec(2 * c), pl.BlockSpec((128, 2 * c), full), pl.BlockSpec((RWKV_GATE_RANK, c), full),
                  vec(c), vec(c), vec(c), pl.BlockSpec((c, c), full)],
        out_specs=[o[0] for o in outs],
        out_shape=[o[1] for o in outs],
        scratch_shapes=[pltpu.VMEM((tm, c), F32)] * 3 + [pltpu.VMEM((tm, 2 * c), F32)] * 3
                       + [pltpu.VMEM((tm, C_RWKV_IN), F32)],
        compiler_params=_cparams("parallel"),
        name="rwkv_local",
    )(p_a, p_a, p_a, prm["rwkv_mu_prev"].reshape(1, -1), prm["rwkv_mu_next"].reshape(1, -1),
      prm["rwkv_w0"].reshape(1, 2 * c), w2pad, prm["rwkv_a0"].reshape(1, 2 * c), a2pad, prm["rwkv_g2"],
      prm["rwkv_k_k"].reshape(1, c), prm["rwkv_k_a"].reshape(1, c), prm["rwkv_r_k"].reshape(1, c), ones_bd)


SCAN_TILE = 128


def _rwkv_scan_body(nc, nseq, qwf, yuf, bkf, pcf, vf, qwr, yur, bkr, pcr, vr, yf_ref, yr_ref, s_ref):
    @pl.when(pl.program_id(0) == 0)
    def _():
        s_ref[...] = jnp.zeros(s_ref.shape, F32)

    brow = lax.broadcasted_iota(I32, (HALF, HALF), 0)
    bcol = lax.broadcasted_iota(I32, (HALF, HALF), 1)
    same_head = (brow >> 6) == (bcol >> 6)

    dirs = ((qwf, yuf, bkf, pcf, vf, yf_ref), (qwr, yur, bkr, pcr, vr, yr_ref))

    def step(j, carry):
        ch = []
        for b in range(nseq):
            for d, (qw_ref, yu_ref, bk_ref, pc_ref, v_ref, y_ref) in enumerate(dirs):
                ci = j if d == 0 else nc - 1 - j
                rows = pl.ds(pl.multiple_of(ci * CHUNK, CHUNK), CHUNK)
                qw, yu, bk = qw_ref[b, ci], yu_ref[b, ci], bk_ref[b, ci]
                pc, vc = pc_ref[b, ci][0:1], v_ref[b, rows, :]
                for hf in range(2):
                    lanes = slice(hf * HALF, (hf + 1) * HALF)
                    ch.append(dict(sidx=(b, d, hf), b=b, rows=rows, lanes=lanes, y_ref=y_ref, qw=qw[:, lanes],
                                   yu=yu[:, lanes], bk=bk[:, lanes], pc=pc[:, lanes], vc=vc[:, lanes]))
        for c in ch:
            c["s"] = s_ref[c["sidx"]]
        ys = [lax.dot_general(c["qw"], c["s"].astype(BF16), _NT, preferred_element_type=F32) + c["yu"] for c in ch]
        for c, y in zip(ch, ys):
            c["y_ref"][c["b"], c["rows"], c["lanes"]] = y[:CHUNK]
        sav = [jnp.concatenate([y[CHUNK:], c["vc"]], axis=0).astype(BF16) for c, y in zip(ch, ys)]
        upd = [lax.dot_general(x, c["bk"], _TN, preferred_element_type=F32) for c, x in zip(ch, sav)]
        for c, u in zip(ch, upd):
            s_ref[c["sidx"]] = c["s"] * c["pc"] + jnp.where(same_head, u, 0.0)
        return carry

    lax.fori_loop(0, nc, step, 0)


def _rwkv_scan(loc, nseq, seq_len):
    _, _, v, qwf, yuf, bkf, pcf, qwr, yur, bkr, pcr = loc
    tm = min(SCAN_TILE * max(1, 4 // nseq), seq_len)
    nt_seq = seq_len // tm
    nc = tm // CHUNK
    c = RWKV_WIDTH
    ncs = seq_len // CHUNK
    fwd = lambda i: (0, i, 0, 0)
    rev = lambda i: (0, nt_seq - 1 - i, 0, 0)
    fwd_v = lambda i: (0, i, 0)
    rev_v = lambda i: (0, nt_seq - 1 - i, 0)

    def specs(m4, m3):
        return [pl.BlockSpec((nseq, nc, 2 * CHUNK, c), m4), pl.BlockSpec((nseq, nc, 2 * CHUNK, c), m4),
                pl.BlockSpec((nseq, nc, 2 * CHUNK, c), m4), pl.BlockSpec((nseq, nc, 8, c), m4),
                pl.BlockSpec((nseq, tm, c), m3)]

    per_seq = lambda a: a.reshape((nseq, ncs) + a.shape[1:])
    v3 = v.reshape(nseq, seq_len, c)
    yf, yr = pl.pallas_call(
        functools.partial(_rwkv_scan_body, nc, nseq),
        grid=(nt_seq,),
        in_specs=specs(fwd, fwd_v) + specs(rev, rev_v),
        out_specs=[pl.BlockSpec((nseq, tm, c), fwd_v), pl.BlockSpec((nseq, tm, c), rev_v)],
        out_shape=[jax.ShapeDtypeStruct((nseq, seq_len, c), F32)] * 2,
        scratch_shapes=[pltpu.VMEM((nseq, 2, 2, HALF, HALF), F32)],
        compiler_params=_cparams("arbitrary"),
        name="rwkv_scan",
    )(per_seq(qwf), per_seq(yuf), per_seq(bkf), per_seq(pcf), v3,
      per_seq(qwr), per_seq(yur), per_seq(bkr), per_seq(pcr), v3)
    return yf.reshape(nseq * seq_len, c), yr.reshape(nseq * seq_len, c)


def _rwkv_out(yf, yr, g, bonus, ln_g, ln_b, ones_bd):
    y = yf + yr
    inv_n = 1.0 / RWKV_HEAD_DIM
    mean = _seg_sum(y, ones_bd) * inv_n
    yc = y - mean
    var = _seg_sum(yc * yc, ones_bd) * inv_n
    yn = yc * lax.rsqrt(var + RWKV_GN_EPS) * ln_g + ln_b
    return (yn + bonus) * g


def _rwkv_branch(p_a, nseq, seq_len, prm):
    tm = min(512, seq_len)
    nt_seq = seq_len // tm
    loc = _rwkv_local(p_a, nt_seq, tm, prm)
    yf, yr = _rwkv_scan(loc, nseq, seq_len)
    return yf, yr, loc[0], loc[1]


FFT_N2 = 128
FILTER_EMB_DIM = 33
FILTER_BANDS = (FILTER_EMB_DIM - 1) // 2
FILTER_HIDDEN = 64
FILTER_TARGET = 1e-2
FILTER_FAST_DECAY_PCT = 0.3
FILTER_SLOW_DECAY_PCT = 1.5


def _hyena_prep_body(nt_seq, p_ref, pp_ref, pn_ref, cw_ref, cb_ref, src_ref, x2_ref):
    i = pl.program_id(0) % nt_seq
    p = p_ref[...]
    prev, nxt = _shifted(p, pp_ref[...], pn_ref[...], i == 0, i == nt_seq - 1)
    cw = cw_ref[...]
    u = prev * cw[0:1] + p * cw[1:2] + nxt * cw[2:3] + cb_ref[...]
    c = HYENA_WIDTH
    src_ref[...] = u[:, 0:c] * u[:, 2 * c:3 * c]
    x2_ref[...] = u[:, c:2 * c]


def _hyena_prep(p_b, nt_seq, tm, prm):
    t = p_b.shape[0]
    c = HYENA_WIDTH
    row = lambda i: (i, 0)
    full = lambda i: (0, 0)
    nblk8 = t // 8
    tm8 = tm // 8
    cw = jnp.concatenate([prm["hyena_conv_w"], jnp.zeros((5, C_HYENA_IN), F32)], axis=0)
    return pl.pallas_call(
        functools.partial(_hyena_prep_body, nt_seq),
        grid=(t // tm,),
        in_specs=[pl.BlockSpec((tm, C_HYENA_IN), row),
                  pl.BlockSpec((8, C_HYENA_IN), lambda i: (jnp.maximum(i * tm8 - 1, 0), 0)),
                  pl.BlockSpec((8, C_HYENA_IN), lambda i: (jnp.minimum((i + 1) * tm8, nblk8 - 1), 0)),
                  pl.BlockSpec((8, C_HYENA_IN), full), pl.BlockSpec((1, C_HYENA_IN), full)],
        out_specs=[pl.BlockSpec((tm, c), row)] * 2,
        out_shape=[jax.ShapeDtypeStruct((t, c), F32)] * 2,
        compiler_params=_cparams("parallel"),
        name="hyena_prep",
    )(p_b, p_b, p_b, cw, prm["hyena_conv_b"].reshape(1, -1))


def _hyena_filter_body(seq_len, w1_ref, b1_ref, w2_ref, b2_ref, w3_ref, freq_ref, delta_ref, o_ref):
    rows = o_ref.shape[0]
    n = pl.program_id(0) * rows + lax.broadcasted_iota(I32, (rows, 128), 0)
    lane = lax.broadcasted_iota(I32, (rows, 128), 1)
    pos = jnp.where(n < seq_len, n, 2 * seq_len - n).astype(F32)
    t = pos * (1.0 / (seq_len - 1))
    omega = (2.0 * math.pi) * pos / seq_len
    band_step = (FILTER_BANDS - 1 - 1e-4) / (FILTER_BANDS - 1)
    band_idx = jnp.where(lane <= FILTER_BANDS, lane - 1, lane - 1 - FILTER_BANDS).astype(F32)
    arg = (1e-4 + band_idx * band_step) * omega
    phase = jnp.where(lane <= FILTER_BANDS, 0.5 * math.pi, math.pi)
    z = jnp.where(lane == 0, t, jnp.where(lane <= 2 * FILTER_BANDS, jnp.sin(arg + phase), 0.0))
    half = rows // 2
    freq = freq_ref[...]
    hid = jnp.sin(freq * (_dot3(z[:half], w1_ref[0]) + _dot3(z[half:], w1_ref[1]) + b1_ref[...]))
    hid = jnp.sin(freq * (_dot3(hid, w2_ref[...]) + b2_ref[...]))
    filt = jnp.concatenate([_dot3(hid, w3_ref[0]), _dot3(hid, w3_ref[1])], axis=0)
    nn = n[:, 0:1]
    tt = t[:, 0:1]
    sel = jnp.where(nn < seq_len, filt[:, :HYENA_WIDTH], jnp.where(nn > seq_len, filt[:, HYENA_WIDTH:], 0.0))
    o_ref[...] = sel * jnp.exp(-tt * delta_ref[...])


def _hyena_filter(seq_len, prm):
    rows = min(1024, 2 * seq_len)
    c = HYENA_WIDTH
    fh = FILTER_HIDDEN
    w1 = jnp.concatenate([prm["hyena_f_w1"], jnp.zeros((128 - FILTER_EMB_DIM, fh), F32)], axis=0)
    z1 = jnp.zeros_like(w1)
    w1p = jnp.stack([jnp.concatenate([w1, z1], axis=1), jnp.concatenate([z1, w1], axis=1)])
    w2 = prm["hyena_f_w2"]
    z2 = jnp.zeros_like(w2)
    w2p = jnp.concatenate([jnp.concatenate([w2, z2], axis=1), jnp.concatenate([z2, w2], axis=1)], axis=0)
    w3 = prm["hyena_f_w3"]
    z3 = jnp.zeros_like(w3)
    w3p = jnp.stack([jnp.concatenate([w3, z3], axis=0), jnp.concatenate([z3, w3], axis=0)])
    twice = lambda a: jnp.tile(a.reshape(1, fh), (1, 2))
    min_decay = math.log(FILTER_TARGET) / FILTER_SLOW_DECAY_PCT
    max_decay = math.log(FILTER_TARGET) / FILTER_FAST_DECAY_PCT
    deltas = jnp.abs(jnp.linspace(min_decay, max_decay, c, dtype=F32)).reshape(1, c)
    full = lambda i: (0, 0)
    full3 = lambda i: (0, 0, 0)
    return pl.pallas_call(
        functools.partial(_hyena_filter_body, seq_len),
        grid=(2 * seq_len // rows,),
        in_specs=[pl.BlockSpec((2, 128, 2 * fh), full3), pl.BlockSpec((1, 2 * fh), full),
                  pl.BlockSpec((2 * fh, 2 * fh), full), pl.BlockSpec((1, 2 * fh), full),
                  pl.BlockSpec((2, 2 * fh, 2 * c), full3), pl.BlockSpec((1, 2 * fh), full),
                  pl.BlockSpec((1, c), full)],
        out_specs=pl.BlockSpec((rows, c), lambda i: (i, 0)),
        out_shape=jax.ShapeDtypeStruct((2 * seq_len, c), F32),
        compiler_params=_cparams("parallel"),
        name="hyena_filter",
    )(w1p, twice(prm["hyena_f_b1"]), w2p, twice(prm["hyena_f_b2"]), w3p, twice(prm["hyena_f_freq"]), deltas)


FFT_N2_TILE = 16
FFT_LANE_TILE = 256


def _lmul_rows(m_ref, x_ref):
    m = m_ref[...]
    xt = pltpu.einshape("rjc->jrc", x_ref[0].astype(F32))
    ys = [jnp.dot(m, xt[j].astype(BF16), preferred_element_type=F32) for j in range(FFT_N2_TILE)]
    return pltpu.einshape("jrc->rjc", jnp.stack(ys, axis=0))


def _lmul_body(m_ref, x_ref, o_ref):
    o_ref[0] = _lmul_rows(m_ref, x_ref).astype(o_ref.dtype)


def _lmul_epilogue_body(m_ref, x_ref, src_ref, x2_ref, skip_ref, o_ref):
    o_ref[0] = x2_ref[0] * (_lmul_rows(m_ref, x_ref) + src_ref[0] * skip_ref[...])


def _lmul(m, x, extra=None):
    nfft, r_in, n2, c = x.shape
    r_out = m.shape[0]
    ct = FFT_LANE_TILE
    xs = pl.BlockSpec((1, r_in, FFT_N2_TILE, ct), lambda f, j, l: (f, 0, j, l))
    os_ = pl.BlockSpec((1, r_out, FFT_N2_TILE, ct), lambda f, j, l: (f, 0, j, l))
    ms = pl.BlockSpec(m.shape, lambda f, j, l: (0, 0))
    if extra is None:
        body, ins, specs, out_dtype = _lmul_body, (m, x), [ms, xs], BF16
    else:
        src, x2, skip = extra
        body, ins, out_dtype = _lmul_epilogue_body, (m, x, src, x2, skip), F32
        specs = [ms, xs, os_, os_, pl.BlockSpec((1, ct), lambda f, j, l: (0, l))]
    return pl.pallas_call(
        body,
        grid=(nfft, n2 // FFT_N2_TILE, c // ct),
        in_specs=specs,
        out_specs=os_,
        out_shape=jax.ShapeDtypeStruct((nfft, r_out, n2, c), out_dtype),
        compiler_params=_cparams("parallel", "parallel", "parallel"),
        name="fft_outer" if extra is None else "fft_outer_out",
    )(*ins)


def _fft_inner_body(kt, conv, g_ref, y_ref, *rest):
    if conv:
        h_ref, o_ref = rest
    else:
        (o_ref,) = rest
    n2 = FFT_N2
    for q in range(kt):
        g = g_ref[q]
        yk = jnp.concatenate([y_ref[0, 0, q], y_ref[0, 1, q]], axis=0).astype(BF16)
        z = jnp.dot(g, yk, preferred_element_type=F32)
        if conv:
            zr, zi = z[:n2], z[n2:]
            hr, hi = h_ref[0, 0, q], h_ref[0, 1, q]
            pr = zr * hr - zi * hi
            pi = zr * hi + zi * hr
            prod = jnp.concatenate([pr, pi], axis=0).astype(BF16)
            z = lax.dot_general(g, prod, _TN, preferred_element_type=F32)
        o_ref[0, 0, q] = z[:n2].astype(o_ref.dtype)
        o_ref[0, 1, q] = z[n2:].astype(o_ref.dtype)


def _fft_inner(g, y, h, kt):
    nfft, _, n1, n2, c = y.shape
    blk = pl.BlockSpec((1, 2, kt, n2, c), lambda f, j: (f, 0, j, 0, 0))
    gs = pl.BlockSpec((kt, 2 * n2, 2 * n2), lambda f, j: (j, 0, 0))
    conv = h is not None
    ins = (g, y, h) if conv else (g, y)
    specs = [gs, blk, pl.BlockSpec((1, 2, kt, n2, c), lambda f, j: (0, 0, j, 0, 0))] if conv else [gs, blk]
    return pl.pallas_call(
        functools.partial(_fft_inner_body, kt, conv),
        grid=(nfft, n1 // kt),
        in_specs=specs,
        out_specs=blk,
        out_shape=jax.ShapeDtypeStruct(y.shape, BF16 if conv else F32),
        compiler_params=_cparams("parallel", "parallel"),
        name="fft_inner_conv" if conv else "fft_inner",
    )(*ins)


def _dft_tables(seq_len):
    n = 2 * seq_len
    n2 = FFT_N2
    n1 = n // n2
    k1 = jnp.arange(n1, dtype=I32)
    ang1 = (2.0 * math.pi / n1) * ((k1[:, None] * k1[None, :]) % n1).astype(F32)
    c1, s1 = jnp.cos(ang1), jnp.sin(ang1)
    half = n1 // 2
    f_pair = jnp.concatenate([jnp.concatenate([c1[:, :half], s1[:, :half]], 1),
                              jnp.concatenate([-s1[:, :half], c1[:, :half]], 1)], 0)
    f_real_half = jnp.concatenate([c1[:, :half], -s1[:, :half]], 0)
    f_real_full = jnp.concatenate([c1, -s1], 0)
    ci, si = c1[:half] / n, s1[:half] / n
    b_real = jnp.concatenate([ci, -si], 1)
    b_pair = jnp.concatenate([b_real, jnp.concatenate([si, ci], 1)], 0)
    kk = k1[:, None, None] + n1 * jnp.arange(n2, dtype=I32)[None, :, None]
    ang2 = (2.0 * math.pi / n) * ((kk * jnp.arange(n2, dtype=I32)[None, None, :]) % n).astype(F32)
    c2, s2 = jnp.cos(ang2), jnp.sin(ang2)
    g = jnp.concatenate([jnp.concatenate([c2, s2], 2), jnp.concatenate([-s2, c2], 2)], 1)
    cast = lambda a: a.astype(BF16)
    return dict(f_pair=cast(f_pair), f_real_half=cast(f_real_half), f_real_full=cast(f_real_full),
                b_pair=cast(b_pair), b_real=cast(b_real), g=cast(g), n1=n1)


def _hyena_branch(p_b, nseq, seq_len, prm):
    tm = min(512, seq_len)
    c = HYENA_WIDTH
    src, x2 = _hyena_prep(p_b, seq_len // tm, tm, prm)
    tab = _dft_tables(seq_len)
    n1 = tab["n1"]
    n2 = FFT_N2
    kt = 4
    filt = _hyena_filter(seq_len, prm)
    hy = _lmul(tab["f_real_full"], filt.reshape(1, n1, n2, c))
    hspec = _fft_inner(tab["g"], hy.reshape(1, 2, n1, n2, c), None, kt)
    pair = nseq % 2 == 0
    nfft = nseq // 2 if pair else nseq
    rows = n1 if pair else n1 // 2
    xin = src.reshape(nfft, rows, n2, c)
    y = _lmul(tab["f_pair"] if pair else tab["f_real_half"], xin)
    w = _fft_inner(tab["g"], y.reshape(nfft, 2, n1, n2, c), hspec, kt)
    out = _lmul(tab["b_pair"] if pair else tab["b_real"], w.reshape(nfft, 2 * n1, n2, c),
                extra=(xin, x2.reshape(nfft, rows, n2, c), prm["hyena_skip"].reshape(1, c)))
    return out.reshape(nseq * seq_len, c)


MOE_GROUPS = 4
MOE_EXPERTS_PER_GROUP = 8
MOE_EXPERTS = MOE_GROUPS * MOE_EXPERTS_PER_GROUP
MOE_HIDDEN = D_MODEL // 2
ROUTE_LANES = 128


def _route(logits):
    lane = lax.broadcasted_iota(I32, logits.shape, 1)
    neg = -jnp.inf
    big = ROUTE_LANES
    gl = jnp.where(lane < MOE_GROUPS, logits, neg)
    gmax = jnp.max(gl, axis=-1, keepdims=True)
    grp = jnp.min(jnp.where(gl == gmax, lane, big), axis=-1, keepdims=True)
    p_grp = 1.0 / jnp.sum(jnp.exp(gl - gmax), axis=-1, keepdims=True)
    lo = MOE_GROUPS + grp * MOE_EXPERTS_PER_GROUP
    el = jnp.where((lane >= lo) & (lane < lo + MOE_EXPERTS_PER_GROUP), logits, neg)
    m1 = jnp.max(el, axis=-1, keepdims=True)
    i1 = jnp.min(jnp.where(el == m1, lane, big), axis=-1, keepdims=True)
    el2 = jnp.where(lane == i1, neg, el)
    m2 = jnp.max(el2, axis=-1, keepdims=True)
    i2 = jnp.min(jnp.where(el2 == m2, lane, big), axis=-1, keepdims=True)
    e2 = jnp.exp(m2 - m1)
    g1 = p_grp / (1.0 + e2)
    g2 = p_grp * e2 / (1.0 + e2)
    gates = jnp.where(lane == i1, g1, jnp.where(lane == i2, g2, 0.0))
    return pltpu.roll(gates, ROUTE_LANES - MOE_GROUPS, 1)


def _merge_body(x_ref, yf_ref, yr_ref, rg_ref, bonus_ref, lng_ref, lnb_ref, ones_ref, yb_ref, pg_ref,
                wua_ref, wub_ref, wo_ref, g_ref, wr_ref, br_ref, x1_ref, xn_ref, gwt_ref):
    ya = _rwkv_out(yf_ref[...], yr_ref[...], rg_ref[...], bonus_ref[...], lng_ref[...], lnb_ref[...], ones_ref[...])
    ga = jax.nn.sigmoid(pg_ref[:, :D_MODEL].astype(F32))
    gb = jax.nn.sigmoid(pg_ref[:, D_MODEL:].astype(F32))
    merged = ga * _dot(ya, wua_ref[...]) + gb * _dot(yb_ref[...], wub_ref[...])
    x1 = x_ref[...] + _dot(merged, wo_ref[...])
    x1_ref[...] = x1
    xn = x1 * lax.rsqrt(jnp.mean(x1 * x1, axis=-1, keepdims=True) + NORM_EPS) * g_ref[...]
    xn_ref[...] = xn.astype(BF16)
    logits = _dot3(xn, wr_ref[...]) + br_ref[...]
    gwt_ref[...] = _route(logits).T


def _merge(x, rwkv, yb, pg, prm, wts, tm):
    t = x.shape[0]
    c = RWKV_WIDTH
    row = lambda i: (i, 0)
    full = lambda i: (0, 0)
    hd = np.arange(c) // RWKV_HEAD_DIM
    ones_bd = jnp.asarray(hd[:, None] == hd[None, :], BF16)
    wr = jnp.concatenate([prm["moe_w_route_group"], prm["moe_w_route_expert"],
                          jnp.zeros((D_MODEL, ROUTE_LANES - MOE_GROUPS - MOE_EXPERTS), F32)], axis=1)
    br = jnp.concatenate([prm["moe_b_route_group"], prm["moe_b_route_expert"],
                          jnp.zeros((ROUTE_LANES - MOE_GROUPS - MOE_EXPERTS,), F32)]).reshape(1, ROUTE_LANES)
    return pl.pallas_call(
        _merge_body,
        grid=(t // tm,),
        in_specs=[pl.BlockSpec((tm, D_MODEL), row)] + [pl.BlockSpec((tm, c), row)] * 4
                 + [pl.BlockSpec((1, c), full)] * 2 + [pl.BlockSpec((c, c), full),
                  pl.BlockSpec((tm, HYENA_WIDTH), row), pl.BlockSpec((tm, C_GATES), row),
                  pl.BlockSpec((RWKV_WIDTH, D_MODEL), full), pl.BlockSpec((HYENA_WIDTH, D_MODEL), full),
                  pl.BlockSpec((D_MODEL, D_MODEL), full), pl.BlockSpec((1, D_MODEL), full),
                  pl.BlockSpec((D_MODEL, ROUTE_LANES), full), pl.BlockSpec((1, ROUTE_LANES), full)],
        out_specs=[pl.BlockSpec((tm, D_MODEL), row), pl.BlockSpec((tm, D_MODEL), row),
                   pl.BlockSpec((ROUTE_LANES, tm), lambda i: (0, i))],
        out_shape=[jax.ShapeDtypeStruct((t, D_MODEL), F32), jax.ShapeDtypeStruct((t, D_MODEL), BF16),
                   jax.ShapeDtypeStruct((ROUTE_LANES, t), F32)],
        compiler_params=_cparams("parallel"),
        name="merge_route",
    )(x, *rwkv, prm["rwkv_ln_g"].reshape(1, c), prm["rwkv_ln_b"].reshape(1, c), ones_bd, yb, pg,
      wts["w_up_a"], wts["w_up_b"], wts["w_out"], prm["norm_ffn_g"].reshape(1, D_MODEL), wr, br)


MOE_PAD = 16
MOE_BLK = 256
MOE_FFN_ROWS = 512


def _moe_slots(tt):
    ns = 2 * tt + MOE_EXPERTS * MOE_PAD
    return (ns + MOE_BLK - 1) // MOE_BLK * MOE_BLK


def _moe_run_copies(i, lstart_ref, npiece_ref, goff_ref, local_refs, global_refs, sems, to_global, wait):
    for e in range(MOE_EXPERTS):
        ls = lstart_ref[i * MOE_EXPERTS + e]
        go = goff_ref[i * MOE_EXPERTS + e]

        def piece(p, carry, ls=ls, go=go):
            lrows = pl.ds(pl.multiple_of(ls + p * MOE_PAD, MOE_PAD), MOE_PAD)
            grows = pl.ds(pl.multiple_of(go + p * MOE_PAD, MOE_PAD), MOE_PAD)
            for loc, glob, sem in zip(local_refs, global_refs, sems):
                src, dst = (loc.at[lrows, :], glob.at[grows, :]) if to_global else (glob.at[grows, :], loc.at[lrows, :])
                cp = pltpu.make_async_copy(src, dst, sem)
                if wait:
                    cp.wait()
                else:
                    cp.start()
            return carry

        lax.fori_loop(0, npiece_ref[i * MOE_EXPERTS + e], piece, 0)


def _moe_compact_body(lstart_ref, npiece_ref, goff_ref, xn_ref, gwt_ref, tri_ref, pcol_ref, xg0_ref, gg0_ref,
                      xg_out, gg_out, dcol_ref, xg_loc, gg_loc, sems):
    del xg0_ref, gg0_ref
    i = pl.program_id(0)
    tt = xn_ref.shape[0]
    ns = _moe_slots(tt)
    gw = gwt_ref[0:MOE_EXPERTS, :]
    sel = gw > 0.0
    rank = jnp.dot(sel.astype(BF16), tri_ref[...], preferred_element_type=F32)
    dest = pcol_ref[:, 0:1] + rank
    d_lo = jnp.min(jnp.where(sel, dest, 1e9), axis=0, keepdims=True)
    d_hi = jnp.max(jnp.where(sel, dest, -1.0), axis=0, keepdims=True)
    g_lo = jnp.sum(jnp.where(sel & (dest == d_lo), gw, 0.0), axis=0, keepdims=True)
    g_hi = jnp.sum(jnp.where(sel & (dest == d_hi), gw, 0.0), axis=0, keepdims=True)
    single = d_hi == d_lo
    g_hi = jnp.where(single, 0.0, g_hi)
    d_hi = jnp.where(single, -1.0, d_hi)
    rows8 = jnp.concatenate([d_lo, d_hi, g_lo, g_hi, jnp.zeros((4, tt), F32)], axis=0)
    dcol_ref[...] = jnp.concatenate([rows8, jnp.zeros((ROUTE_LANES - 8, tt), F32)], axis=0).T
    for blk in range(ns // MOE_BLK):
        r = (blk * MOE_BLK + lax.broadcasted_iota(I32, (MOE_BLK, tt), 0)).astype(F32)
        lo, hi = r == d_lo, r == d_hi
        rows = slice(blk * MOE_BLK, (blk + 1) * MOE_BLK)
        xg_loc[rows, :] = jnp.dot((lo | hi).astype(BF16), xn_ref[...], preferred_element_type=F32).astype(BF16)
        row_gate = jnp.sum(jnp.where(lo, g_lo, 0.0) + jnp.where(hi, g_hi, 0.0), axis=-1, keepdims=True)
        gg_loc[rows, :] = jnp.broadcast_to(row_gate, (MOE_BLK, ROUTE_LANES))
    copies = functools.partial(_moe_run_copies, i, lstart_ref, npiece_ref, goff_ref, (xg_loc, gg_loc),
                               (xg_out, gg_out), (sems.at[0], sems.at[1]), True)
    copies(False)
    copies(True)


def _moe_ffn_body(bexp_ref, nused_ref, xg_ref, gg_ref, wg_ref, wu_ref, wd_ref, og_ref):
    del bexp_ref
    used = pl.program_id(0) < nused_ref[0]

    @pl.when(jnp.logical_not(used))
    def _():
        og_ref[...] = jnp.zeros(og_ref.shape, BF16)

    @pl.when(used)
    def _():
        xg = xg_ref[...]
        hg = jnp.dot(xg, wg_ref[0], preferred_element_type=F32)
        hu = jnp.dot(xg, wu_ref[0], preferred_element_type=F32)
        hid = ((hg * jax.nn.sigmoid(hg)) * hu).astype(BF16)
        out = jnp.dot(hid, wd_ref[0], preferred_element_type=F32)
        og_ref[...] = (out * gg_ref[:, 0:1]).astype(BF16)


def _moe_scatter_body(final_norm, lstart_ref, npiece_ref, goff_ref, og_ref, x1_ref, dcol_ref, gf_ref, o_ref,
                      og_loc, sems):
    i = pl.program_id(0)
    tt = x1_ref.shape[0]
    ns = _moe_slots(tt)

    @pl.when(i == 0)
    def _():
        og_loc[...] = jnp.zeros(og_loc.shape, BF16)

    copies = functools.partial(_moe_run_copies, i, lstart_ref, npiece_ref, goff_ref, (og_loc,), (og_ref,),
                               (sems.at[0],), False)
    copies(False)
    copies(True)
    y = x1_ref[...]
    c_lo, c_hi = dcol_ref[:, 0:1], dcol_ref[:, 1:2]
    for blk in range(ns // MOE_BLK):
        r = (blk * MOE_BLK + lax.broadcasted_iota(I32, (tt, MOE_BLK), 1)).astype(F32)
        onehot = ((r == c_lo) | (r == c_hi)).astype(BF16)
        y = y + jnp.dot(onehot, og_loc[blk * MOE_BLK:(blk + 1) * MOE_BLK, :], preferred_element_type=F32)
    if final_norm:
        y = y * lax.rsqrt(jnp.mean(y * y, axis=-1, keepdims=True) + NORM_EPS) * gf_ref[...]
    o_ref[...] = y


def _moe_grouped(xn, gwt, x1, wts, norm_final_g, tt):
    t = xn.shape[0]
    final_norm = norm_final_g is not None
    if not final_norm:
        norm_final_g = jnp.ones((D_MODEL,), F32)
    nt = t // tt
    ns = _moe_slots(tt)
    ne = MOE_EXPERTS
    counts = jnp.sum((gwt[:ne] > 0.0).reshape(ne, nt, tt), axis=-1, dtype=I32).T
    padded = (counts + MOE_PAD - 1) // MOE_PAD * MOE_PAD
    lstart = jnp.cumsum(padded, axis=1) - padded
    region = (jnp.sum(padded, axis=0) + MOE_FFN_ROWS - 1) // MOE_FFN_ROWS * MOE_FFN_ROWS
    gend = jnp.cumsum(region)
    goff = (gend - region)[None, :] + jnp.cumsum(padded, axis=0) - padded
    ng = (2 * t + nt * ne * MOE_PAD + ne * MOE_FFN_ROWS + MOE_FFN_ROWS - 1) // MOE_FFN_ROWS * MOE_FFN_ROWS
    nb = ng // MOE_FFN_ROWS
    block_row = jnp.arange(nb, dtype=I32)[:, None] * MOE_FFN_ROWS
    bexp = jnp.minimum(jnp.sum(block_row >= gend[None, :], axis=1, dtype=I32), ne - 1)
    nused = (gend[-1] // MOE_FFN_ROWS).astype(I32).reshape(1)
    pcol = jnp.broadcast_to(lstart.astype(F32).reshape(nt * ne, 1), (nt * ne, ROUTE_LANES))
    idx = np.arange(tt)
    tri = jnp.asarray(idx[:, None] < idx[None, :], BF16)
    sched = (lstart.reshape(-1), (padded // MOE_PAD).reshape(-1), goff.reshape(-1).astype(I32))
    any_spec = pl.BlockSpec(memory_space=pl.ANY)

    xg, gg, dcol = pl.pallas_call(
        _moe_compact_body,
        grid_spec=pltpu.PrefetchScalarGridSpec(
            num_scalar_prefetch=3,
            grid=(nt,),
            in_specs=[pl.BlockSpec((tt, D_MODEL), lambda i, *_: (i, 0)),
                      pl.BlockSpec((ROUTE_LANES, tt), lambda i, *_: (0, i)),
                      pl.BlockSpec((tt, tt), lambda i, *_: (0, 0)),
                      pl.BlockSpec((ne, ROUTE_LANES), lambda i, *_: (i, 0)),
                      any_spec, any_spec],
            out_specs=[any_spec, any_spec, pl.BlockSpec((tt, ROUTE_LANES), lambda i, *_: (i, 0))],
            scratch_shapes=[pltpu.VMEM((ns, D_MODEL), BF16), pltpu.VMEM((ns, ROUTE_LANES), F32),
                            pltpu.SemaphoreType.DMA((2,))],
        ),
        out_shape=[jax.ShapeDtypeStruct((ng, D_MODEL), BF16), jax.ShapeDtypeStruct((ng, ROUTE_LANES), F32),
                   jax.ShapeDtypeStruct((t, ROUTE_LANES), F32)],
        input_output_aliases={7: 0, 8: 1},
        compiler_params=_cparams("arbitrary"),
        name="moe_compact",
    )(*sched, xn, gwt, tri, pcol, jnp.zeros((ng, D_MODEL), BF16), jnp.zeros((ng, ROUTE_LANES), F32))

    og = pl.pallas_call(
        _moe_ffn_body,
        grid_spec=pltpu.PrefetchScalarGridSpec(
            num_scalar_prefetch=2,
            grid=(nb,),
            in_specs=[pl.BlockSpec((MOE_FFN_ROWS, D_MODEL), lambda b, *_: (b, 0)),
                      pl.BlockSpec((MOE_FFN_ROWS, ROUTE_LANES), lambda b, *_: (b, 0)),
                      pl.BlockSpec((1, D_MODEL, MOE_HIDDEN), lambda b, be, nu: (be[b], 0, 0)),
                      pl.BlockSpec((1, D_MODEL, MOE_HIDDEN), lambda b, be, nu: (be[b], 0, 0)),
                      pl.BlockSpec((1, MOE_HIDDEN, D_MODEL), lambda b, be, nu: (be[b], 0, 0))],
            out_specs=pl.BlockSpec((MOE_FFN_ROWS, D_MODEL), lambda b, *_: (b, 0)),
        ),
        out_shape=jax.ShapeDtypeStruct((ng, D_MODEL), BF16),
        compiler_params=_cparams("arbitrary"),
        name="moe_ffn",
    )(bexp, nused, xg, gg, wts["moe_w_gate"], wts["moe_w_up"], wts["moe_w_down"])

    return pl.pallas_call(
        functools.partial(_moe_scatter_body, final_norm),
        grid_spec=pltpu.PrefetchScalarGridSpec(
            num_scalar_prefetch=3,
            grid=(nt,),
            in_specs=[any_spec,
                      pl.BlockSpec((tt, D_MODEL), lambda i, *_: (i, 0)),
                      pl.BlockSpec((tt, ROUTE_LANES), lambda i, *_: (i, 0)),
                      pl.BlockSpec((1, D_MODEL), lambda i, *_: (0, 0))],
            out_specs=pl.BlockSpec((tt, D_MODEL), lambda i, *_: (i, 0)),
            scratch_shapes=[pltpu.VMEM((ns, D_MODEL), BF16), pltpu.SemaphoreType.DMA((1,))],
        ),
        out_shape=jax.ShapeDtypeStruct((t, D_MODEL), F32),
        compiler_params=_cparams("arbitrary"),
        name="moe_scatter",
    )(*sched, og, x1, dcol, norm_final_g.reshape(1, D_MODEL))


def _trunk(x, prm, wts, norm_final_g):
    nseq, seq_len, _ = x.shape
    xf = x.reshape(nseq * seq_len, D_MODEL)
    p_a, p_b, p_g = _norm_in_proj(xf, prm["norm_mix_g"], wts["w_in"], min(512, seq_len))
    rwkv = _rwkv_branch(p_a, nseq, seq_len, prm)
    yb = _hyena_branch(p_b, nseq, seq_len, prm)
    x1, xn, gwt = _merge(xf, rwkv, yb, p_g, prm, wts, min(512, seq_len))
    out = _moe_grouped(xn, gwt, x1, wts, norm_final_g, min(1024, seq_len))
    return out.reshape(nseq, seq_len, D_MODEL)


def kernel(x_prompt, x_sample, norm_mix_g, w_in, rwkv_mu_prev, rwkv_mu_next, rwkv_w0, rwkv_w2, rwkv_a0, rwkv_a2, rwkv_g2, rwkv_k_k, rwkv_k_a, rwkv_r_k, rwkv_ln_g, rwkv_ln_b, hyena_conv_w, hyena_conv_b, hyena_f_w1, hyena_f_b1, hyena_f_w2, hyena_f_b2, hyena_f_w3, hyena_f_freq, hyena_skip, w_up_a, w_up_b, w_out, norm_ffn_g, moe_w_route_group, moe_b_route_group, moe_w_route_expert, moe_b_route_expert, moe_w_gate, moe_w_up, moe_w_down, norm_final_g):
    layer = dict(norm_mix_g=norm_mix_g, w_in=w_in, rwkv_mu_prev=rwkv_mu_prev, rwkv_mu_next=rwkv_mu_next,
                 rwkv_w0=rwkv_w0, rwkv_w2=rwkv_w2, rwkv_a0=rwkv_a0, rwkv_a2=rwkv_a2, rwkv_g2=rwkv_g2,
                 rwkv_k_k=rwkv_k_k, rwkv_k_a=rwkv_k_a, rwkv_r_k=rwkv_r_k, rwkv_ln_g=rwkv_ln_g, rwkv_ln_b=rwkv_ln_b,
                 hyena_conv_w=hyena_conv_w, hyena_conv_b=hyena_conv_b, hyena_f_w1=hyena_f_w1, hyena_f_b1=hyena_f_b1,
                 hyena_f_w2=hyena_f_w2, hyena_f_b2=hyena_f_b2, hyena_f_w3=hyena_f_w3, hyena_f_freq=hyena_f_freq,
                 hyena_skip=hyena_skip, w_up_a=w_up_a, w_up_b=w_up_b, w_out=w_out, norm_ffn_g=norm_ffn_g,
                 moe_w_route_group=moe_w_route_group, moe_b_route_group=moe_b_route_group,
                 moe_w_route_expert=moe_w_route_expert, moe_b_route_expert=moe_b_route_expert,
                 moe_w_gate=moe_w_gate, moe_w_up=moe_w_up, moe_w_down=moe_w_down)
    depth = norm_mix_g.shape[0]
    big = ("w_in", "w_up_a", "w_up_b", "w_out", "moe_w_gate", "moe_w_up", "moe_w_down")

    def trunk(x):
        for li in range(depth):
            prm = {k: v[li] for k, v in layer.items()}
            wts = {k: prm[k].astype(BF16) for k in big}
            last = li == depth - 1
            x = _trunk(x, prm, wts, norm_final_g if last else None)
        return x

    return (trunk(x_prompt), trunk(x_sample))
```

```python
import functools
import math

import jax
import jax.numpy as jnp
import numpy as np
from jax import lax
from jax.experimental import pallas as pl
from jax.experimental.pallas import tpu as pltpu

F32 = jnp.float32
BF16 = jnp.bfloat16
I32 = jnp.int32

D_MODEL = 1024
NORM_EPS = 1e-6
RWKV_HEADS = 8
RWKV_HEAD_DIM = 64
RWKV_WIDTH = RWKV_HEADS * RWKV_HEAD_DIM
RWKV_DECAY_RANK = 64
RWKV_ICLR_RANK = 64
RWKV_GATE_RANK = 128
RWKV_GN_EPS = 64e-5
HYENA_WIDTH = D_MODEL // 2
C_RWKV_IN = 3 * RWKV_WIDTH + 2 * RWKV_DECAY_RANK + 2 * RWKV_ICLR_RANK + RWKV_GATE_RANK
C_HYENA_IN = 3 * HYENA_WIDTH
C_GATES = 2 * D_MODEL

CHUNK = 32
STACK = RWKV_HEADS * CHUNK
HALF = RWKV_WIDTH // 2
GROUP_HEADS = 4
GSTACK = GROUP_HEADS * CHUNK
LOCAL_CHUNKS_PER_ITER = 4

VMEM_LIMIT = 48 * 1024 * 1024

_NN = (((1,), (0,)), ((), ()))
_NT = (((1,), (1,)), ((), ()))
_TN = (((0,), (0,)), ((), ()))


def _dot(a, b, dims=_NN):
    return lax.dot_general(a.astype(BF16), b.astype(BF16), dims, preferred_element_type=F32)


def _split2(x):
    hi = x.astype(BF16)
    lo = (x - hi.astype(F32)).astype(BF16)
    return hi, lo


def _dot3(a, b, dims=_NN):
    ah, al = _split2(a)
    bh, bl = _split2(b)
    dg = functools.partial(lax.dot_general, dimension_numbers=dims, preferred_element_type=F32)
    return dg(ah, bh) + (dg(ah, bl) + dg(al, bh))


def _dot2(a, b, dims=_NN):
    ah, al = _split2(a)
    bh = b.astype(BF16)
    dg = functools.partial(lax.dot_general, dimension_numbers=dims, preferred_element_type=F32)
    return dg(ah, bh) + dg(al, bh)


def _dot_exact_lhs(a_bf16, x):
    x1 = x.astype(BF16)
    r1 = x - x1.astype(F32)
    x2 = r1.astype(BF16)
    x3 = (r1 - x2.astype(F32)).astype(BF16)
    dg = functools.partial(lax.dot_general, dimension_numbers=_NN, preferred_element_type=F32)
    return dg(a_bf16, x1) + (dg(a_bf16, x2) + dg(a_bf16, x3))


def _seg_sum(x, ones_bd):
    x1, x2 = _split2(x)
    dg = functools.partial(lax.dot_general, dimension_numbers=_NN, preferred_element_type=F32)
    halves = []
    for hf in range(2):
        lanes = slice(hf * HALF, (hf + 1) * HALF)
        ones_h = ones_bd[lanes, lanes]
        halves.append(dg(x1[:, lanes], ones_h) + dg(x2[:, lanes], ones_h))
    return jnp.concatenate(halves, axis=1)


def _cparams(*sem):
    return pltpu.CompilerParams(dimension_semantics=tuple(sem), vmem_limit_bytes=VMEM_LIMIT)


def _norm_in_proj_body(x_ref, g_ref, wa_ref, wb_ref, wg_ref, pa_ref, pb_ref, pg_ref):
    x = x_ref[...]
    xn = x * lax.rsqrt(jnp.mean(x * x, axis=-1, keepdims=True) + NORM_EPS) * g_ref[...]
    xb = xn.astype(BF16)
    pa_ref[...] = jnp.dot(xb, wa_ref[...], preferred_element_type=F32)
    pb_ref[...] = jnp.dot(xb, wb_ref[...], preferred_element_type=F32)
    pg_ref[...] = jnp.dot(xb, wg_ref[...], preferred_element_type=F32).astype(BF16)


def _norm_in_proj(x, g, w_in, tm):
    t = x.shape[0]
    wa = w_in[:, :C_RWKV_IN].astype(BF16)
    wb = w_in[:, C_RWKV_IN:C_RWKV_IN + C_HYENA_IN].astype(BF16)
    wg = w_in[:, C_RWKV_IN + C_HYENA_IN:].astype(BF16)
    full = lambda i: (0, 0)
    row = lambda i: (i, 0)
    return pl.pallas_call(
        _norm_in_proj_body,
        grid=(t // tm,),
        in_specs=[pl.BlockSpec((tm, D_MODEL), row), pl.BlockSpec((1, D_MODEL), full),
                  pl.BlockSpec(wa.shape, full), pl.BlockSpec(wb.shape, full), pl.BlockSpec(wg.shape, full)],
        out_specs=[pl.BlockSpec((tm, C_RWKV_IN), row), pl.BlockSpec((tm, C_HYENA_IN), row),
                   pl.BlockSpec((tm, C_GATES), row)],
        out_shape=[jax.ShapeDtypeStruct((t, C_RWKV_IN), F32), jax.ShapeDtypeStruct((t, C_HYENA_IN), F32),
                   jax.ShapeDtypeStruct((t, C_GATES), BF16)],
        compiler_params=_cparams("parallel"),
        name="norm_in_proj",
    )(x, g.reshape(1, D_MODEL), wa, wb, wg)


def _shifted(p, prev_blk, next_blk, is_first, is_last):
    tm = p.shape[0]
    row = lax.broadcasted_iota(I32, p.shape, 0)
    prow = jnp.where(is_first, 0.0, prev_blk[7:8, :])
    nrow = jnp.where(is_last, 0.0, next_blk[0:1, :])
    prev = jnp.where(row == 0, prow, pltpu.roll(p, 1, 0))
    nxt = jnp.where(row == tm - 1, nrow, pltpu.roll(p, tm - 1, 0))
    return prev, nxt


def _stack_heads(x, head_mask):
    return jnp.where(head_mask, jnp.concatenate([x] * GROUP_HEADS, axis=0), 0.0)


def _unstack_heads(z):
    out = z[0:CHUNK]
    for h in range(1, GROUP_HEADS):
        out = out + z[h * CHUNK:(h + 1) * CHUNK]
    return out


def _chunks_local(chains, fillers=()):
    fillers = list(fillers)

    def fill():
        if fillers:
            fillers.pop(0)()

    ngrp = RWKV_HEADS // GROUP_HEADS
    ti = lax.broadcasted_iota(I32, (CHUNK, CHUNK), 0)
    si = lax.broadcasted_iota(I32, (CHUNK, CHUNK), 1)
    srow = lax.broadcasted_iota(I32, (GSTACK, HALF), 0)
    slane = lax.broadcasted_iota(I32, (GSTACK, HALF), 1)
    head_mask = (srow >> 5) == (slane >> 6)
    mrow = lax.broadcasted_iota(I32, (GSTACK, GSTACK), 0)
    mcol = lax.broadcasted_iota(I32, (GSTACK, GSTACK), 1)
    same = (mrow >> 5) == (mcol >> 5)
    masks = {False: (same & (mrow > mcol), same & (mrow >= mcol)),
             True: (same & (mrow < mcol), same & (mrow <= mcol))}
    tris = {False: (ti >= si).astype(BF16), True: (ti <= si).astype(BF16)}

    cls = [_dot_exact_lhs(tris[rev], lw) for (_, _, _, lw, _, _, rev) in chains]
    st = []
    for (r, v, kk, lw, k, a, rev), cl in zip(chains, cls):
        tot = cl[0:1] if rev else cl[CHUNK - 1:CHUNK]
        e_neg = jnp.exp(-cl)
        e_tail = jnp.exp(tot - cl)
        beta = kk * a
        st.append(dict(rev=rev, v=v, alpha_b=-kk * jnp.exp(cl - lw), r_b=r * jnp.exp(cl), beta_b=beta * e_neg,
                       k_b=k * e_neg, bt=beta * e_tail, kt=k * e_tail, pc=jnp.exp(tot)))

    for s in st:
        s["sa"], s["sv"], s["lhs"], s["rhs"] = [], [], [], []
        for g in range(ngrp):
            lanes = slice(g * HALF, (g + 1) * HALF)
            sa_ = _stack_heads(s["alpha_b"][:, lanes], head_mask).astype(BF16)
            sr_ = _stack_heads(s["r_b"][:, lanes], head_mask).astype(BF16)
            s["sa"].append(sa_)
            s["sv"].append(_stack_heads(s["v"][:, lanes], head_mask).astype(BF16))
            s["lhs"].append(jnp.concatenate([sa_, sr_], axis=0))
            s["rhs"].append(jnp.concatenate([s["beta_b"][:, lanes].astype(BF16)] * GROUP_HEADS
                                            + [s["k_b"][:, lanes].astype(BF16)] * GROUP_HEADS, axis=0))
    pms = [[lax.dot_general(s["lhs"][g], s["rhs"][g], _NT, preferred_element_type=F32) for g in range(ngrp)]
           for s in st]
    zero = jnp.zeros((GSTACK, GSTACK), F32)
    drow = lax.broadcasted_iota(I32, (STACK, STACK), 0)
    dcol = lax.broadcasted_iota(I32, (STACK, STACK), 1)
    eye = jnp.where(drow == dcol, 1.0, 0.0)
    for s, pm in zip(st, pms):
        strict, incl = masks[s["rev"]]
        s["ak"] = [jnp.where(strict, pm[g][:GSTACK, GSTACK:], 0.0).astype(BF16) for g in range(ngrp)]
        s["rb"] = [jnp.where(incl, pm[g][GSTACK:, :GSTACK], 0.0).astype(BF16) for g in range(ngrp)]
        s["rk"] = [jnp.where(incl, pm[g][GSTACK:, GSTACK:], 0.0).astype(BF16) for g in range(ngrp)]
        ab = [jnp.where(strict, pm[g][:GSTACK, :GSTACK], 0.0) for g in range(ngrp)]
        s["apow"] = jnp.concatenate([jnp.concatenate([ab[0], zero], axis=1),
                                     jnp.concatenate([zero, ab[1]], axis=1)], axis=0)
        s["tinv"] = eye + s["apow"]

    for _ in range(int(math.log2(CHUNK)) - 1):
        sq = [_dot(s["apow"], s["apow"]) for s in st]
        fill()
        for s, x in zip(st, sq):
            s["apow"] = x
        pr = [_dot(s["tinv"], s["apow"]) for s in st]
        fill()
        for s, x in zip(st, pr):
            s["tinv"] = s["tinv"] + x
    while fillers:
        fill()

    akv = [[_dot(s["ak"][g], s["sv"][g]) for g in range(ngrp)] for s in st]
    rkv = [[_dot(s["rk"][g], s["sv"][g]) for g in range(ngrp)] for s in st]
    tw = [[_dot(s["tinv"][g * GSTACK:(g + 1) * GSTACK, g * GSTACK:(g + 1) * GSTACK],
                jnp.concatenate([s["sa"][g].astype(F32), akv[c][g]], axis=1)) for g in range(ngrp)]
          for c, s in enumerate(st)]
    rbx = [[_dot(s["rb"][g], tw[c][g]) for g in range(ngrp)] for c, s in enumerate(st)]
    outs = []
    for c, s in enumerate(st):
        cat = lambda f: jnp.concatenate([f(g) for g in range(ngrp)], axis=1)
        qt = s["r_b"] + cat(lambda g: _unstack_heads(rbx[c][g][:, :HALF]))
        wt = cat(lambda g: _unstack_heads(tw[c][g][:, :HALF]))
        yloc = cat(lambda g: _unstack_heads(rbx[c][g][:, HALF:] + rkv[c][g]))
        u = cat(lambda g: _unstack_heads(tw[c][g][:, HALF:]))
        outs.append((qt, wt, yloc, u, s["bt"], s["kt"], s["pc"]))
    return outs


def _rwkv_local_body(nt_seq, p_ref, pp_ref, pn_ref, mup_ref, mun_ref, w0_ref, w2_ref, a0_ref, a2_ref, g2_ref,
                     kk_ref, ka_ref, rk_ref, ones_ref,
                     g_out, bonus_out, v_out, qwf, yuf, bkf, pcf, qwr, yur, bkr, pcr,
                     r_s, v_s, kk_s, lw_s, k_s, a_s, ps_s):
    i = pl.program_id(0) % nt_seq
    p = p_ref[...]
    tm = p.shape[0]
    prev, nxt = _shifted(p, pp_ref[...], pn_ref[...], i == 0, i == nt_seq - 1)
    ps_s[...] = p + mup_ref[...] * (prev - p) + mun_ref[...] * (nxt - p)
    c = RWKV_WIDTH
    span = CHUNK * LOCAL_CHUNKS_PER_ITER

    def param_steps(h):
        rows = slice(h * span, (h + 1) * span)
        st = {}

        def split():
            ps = ps_s[rows, :]
            st["r"], st["k"], st["v"] = ps[:, 0:c], ps[:, c:2 * c], ps[:, 2 * c:3 * c]
            st["lw"], st["la"], st["lg"] = (ps[:, 3 * c + j * 128:3 * c + (j + 1) * 128] for j in range(3))
            v_out[rows, :] = st["v"]
            r_s[rows, :] = st["r"]
            v_s[rows, :] = st["v"]

        def decay_proj():
            st["w_raw"] = w0_ref[...] + _dot2(jnp.tanh(st["lw"]), w2_ref[...])

        def decay():
            lw_s[rows, :] = -jnp.exp(-jnp.logaddexp(-st["w_raw"], 0.0) - 0.5)

        def iclr():
            st["a"] = jax.nn.sigmoid(a0_ref[...] + _dot2(st["la"], a2_ref[...]))
            a_s[rows, :] = st["a"]

        def gate():
            g_out[rows, :] = _dot(jax.nn.sigmoid(st["lg"]), g2_ref[...])

        def key_norm():
            kk0 = st["k"] * kk_ref[...]
            kk_s[rows, :] = kk0 * lax.rsqrt(jnp.maximum(_seg_sum(kk0 * kk0, ones_ref[...]), 1e-24))

        def key_dir():
            ka = ka_ref[...]
            st["kf"] = st["k"] * (1.0 + (st["a"][:, :c] - 1.0) * ka)
            st["kr"] = st["k"] * (1.0 + (st["a"][:, c:] - 1.0) * ka)
            k_s[rows, :c] = st["kf"]
            k_s[rows, c:] = st["kr"]

        def bonus():
            rk = st["r"] * ((st["kf"] + st["kr"]) * 0.5) * rk_ref[...]
            bonus_out[rows, :] = _seg_sum(rk, ones_ref[...]) * st["v"]

        return [split, decay_proj, decay, iclr, gate, key_norm, key_dir, bonus]

    out_refs = ((qwf, yuf, bkf, pcf), (qwr, yur, bkr, pcr))
    nspan = tm // span
    for step in param_steps(0):
        step()
    for h in range(nspan):
        chains, where = [], []
        for uu in range(LOCAL_CHUNKS_PER_ITER):
            ci = h * LOCAL_CHUNKS_PER_ITER + uu
            rows = slice(ci * CHUNK, (ci + 1) * CHUNK)
            rc, vc, kkc = r_s[rows, :], v_s[rows, :], kk_s[rows, :]
            for d in range(2):
                lanes = slice(d * c, (d + 1) * c)
                chains.append((rc, vc, kkc, lw_s[rows, lanes], k_s[rows, lanes], a_s[rows, lanes], d == 1))
                where.append((ci, d))
        fillers = param_steps(h + 1) if h + 1 < nspan else []
        for (ci, d), (qt, wt, yloc, u, bt, kt, pcv) in zip(where, _chunks_local(chains, fillers)):
            qw, yu, bk, pc = out_refs[d]
            qw[ci] = jnp.concatenate([qt, wt], axis=0).astype(BF16)
            yu[ci] = jnp.concatenate([yloc, u], axis=0)
            bk[ci] = jnp.concatenate([bt, kt], axis=0).astype(BF16)
            pc[ci] = jnp.broadcast_to(pcv, (8, c))


def _rwkv_local(p_a, nt_seq, tm, prm):
    t = p_a.shape[0]
    nt = t // tm
    nc = tm // CHUNK
    c = RWKV_WIDTH
    rd, ri = RWKV_DECAY_RANK, RWKV_ICLR_RANK
    z = jnp.zeros((rd, c), F32)
    w2pad = jnp.concatenate([jnp.concatenate([prm["rwkv_w2"][0], z], 0), jnp.concatenate([z, prm["rwkv_w2"][1]], 0)], 1)
    z = jnp.zeros((ri, c), F32)
    a2pad = jnp.concatenate([jnp.concatenate([prm["rwkv_a2"][0], z], 0), jnp.concatenate([z, prm["rwkv_a2"][1]], 0)], 1)
    hd = np.arange(c) // RWKV_HEAD_DIM
    ones_bd = jnp.asarray(hd[:, None] == hd[None, :], BF16)
    full = lambda i: (0, 0)
    row = lambda i: (i, 0)
    row3 = lambda i: (i, 0, 0)
    nblk8 = t // 8
    tm8 = tm // 8
    vec = lambda n: pl.BlockSpec((1, n), full)
    chunk_out = lambda rows, dt: (pl.BlockSpec((nc, rows, c), row3), jax.ShapeDtypeStruct((t // CHUNK, rows, c), dt))
    per_dir = [chunk_out(2 * CHUNK, BF16), chunk_out(2 * CHUNK, F32), chunk_out(2 * CHUNK, BF16), chunk_out(8, F32)]
    outs = [(pl.BlockSpec((tm, c), row), jax.ShapeDtypeStruct((t, c), F32))] * 3 + per_dir + per_dir
    return pl.pallas_call(
        functools.partial(_rwkv_local_body, nt_seq),
        grid=(nt,),
        in_specs=[pl.BlockSpec((tm, C_RWKV_IN), row),
                  pl.BlockSpec((8, C_RWKV_IN), lambda i: (jnp.maximum(i * tm8 - 1, 0), 0)),
                  pl.BlockSpec((8, C_RWKV_IN), lambda i: (jnp.minimum((i + 1) * tm8, nblk8 - 1), 0)),
                  vec(C_RWKV_IN), vec(C_RWKV_IN), vec(2 * c), pl.BlockSpec((128, 2 * c), full),
                  vec(2 * c), pl.BlockSpec((128, 2 * c), full), pl.BlockSpec((RWKV_GATE_RANK, c), full),
                  vec(c), vec(c), vec(c), pl.BlockSpec((c, c), full)],
        out_specs=[o[0] for o in outs],
        out_shape=[o[1] for o in outs],
        scratch_shapes=[pltpu.VMEM((tm, c), F32)] * 3 + [pltpu.VMEM((tm, 2 * c), F32)] * 3
                       + [pltpu.VMEM((tm, C_RWKV_IN), F32)],
        compiler_params=_cparams("parallel"),
        name="rwkv_local",
    )(p_a, p_a, p_a, prm["rwkv_mu_prev"].reshape(1, -1), prm["rwkv_mu_next"].reshape(1, -1),
      prm["rwkv_w0"].reshape(1, 2 * c), w2pad, prm["rwkv_a0"].reshape(1, 2 * c), a2pad, prm["rwkv_g2"],
      prm["rwkv_k_k"].reshape(1, c), prm["rwkv_k_a"].reshape(1, c), prm["rwkv_r_k"].reshape(1, c), ones_bd)


SCAN_TILE = 128


def _rwkv_scan_body(nc, nseq, qwf, yuf, bkf, pcf, vf, qwr, yur, bkr, pcr, vr, yf_ref, yr_ref, s_ref):
    @pl.when(pl.program_id(0) == 0)
    def _():
        s_ref[...] = jnp.zeros(s_ref.shape, F32)

    brow = lax.broadcasted_iota(I32, (HALF, HALF), 0)
    bcol = lax.broadcasted_iota(I32, (HALF, HALF), 1)
    same_head = (brow >> 6) == (bcol >> 6)

    dirs = ((qwf, yuf, bkf, pcf, vf, yf_ref), (qwr, yur, bkr, pcr, vr, yr_ref))

    def step(j, carry):
        ch = []
        for b in range(nseq):
            for d, (qw_ref, yu_ref, bk_ref, pc_ref, v_ref, y_ref) in enumerate(dirs):
                ci = j if d == 0 else nc - 1 - j
                rows = pl.ds(pl.multiple_of(ci * CHUNK, CHUNK), CHUNK)
                qw, yu, bk = qw_ref[b, ci], yu_ref[b, ci], bk_ref[b, ci]
                pc, vc = pc_ref[b, ci][0:1], v_ref[b, rows, :]
                for hf in range(2):
                    lanes = slice(hf * HALF, (hf + 1) * HALF)
                    ch.append(dict(sidx=(b, d, hf), b=b, rows=rows, lanes=lanes, y_ref=y_ref, qw=qw[:, lanes],
                                   yu=yu[:, lanes], bk=bk[:, lanes], pc=pc[:, lanes], vc=vc[:, lanes]))
        for c in ch:
            c["s"] = s_ref[c["sidx"]]
        ys = [lax.dot_general(c["qw"], c["s"].astype(BF16), _NT, preferred_element_type=F32) + c["yu"] for c in ch]
        for c, y in zip(ch, ys):
            c["y_ref"][c["b"], c["rows"], c["lanes"]] = y[:CHUNK]
        sav = [jnp.concatenate([y[CHUNK:], c["vc"]], axis=0).astype(BF16) for c, y in zip(ch, ys)]
        upd = [lax.dot_general(x, c["bk"], _TN, preferred_element_type=F32) for c, x in zip(ch, sav)]
        for c, u in zip(ch, upd):
            s_ref[c["sidx"]] = c["s"] * c["pc"] + jnp.where(same_head, u, 0.0)
        return carry

    lax.fori_loop(0, nc, step, 0)


def _rwkv_scan(loc, nseq, seq_len):
    _, _, v, qwf, yuf, bkf, pcf, qwr, yur, bkr, pcr = loc
    tm = min(SCAN_TILE * max(1, 4 // nseq), seq_len)
    nt_seq = seq_len // tm
    nc = tm // CHUNK
    c = RWKV_WIDTH
    ncs = seq_len // CHUNK
    fwd = lambda i: (0, i, 0, 0)
    rev = lambda i: (0, nt_seq - 1 - i, 0, 0)
    fwd_v = lambda i: (0, i, 0)
    rev_v = lambda i: (0, nt_seq - 1 - i, 0)

    def specs(m4, m3):
        return [pl.BlockSpec((nseq, nc, 2 * CHUNK, c), m4), pl.BlockSpec((nseq, nc, 2 * CHUNK, c), m4),
                pl.BlockSpec((nseq, nc, 2 * CHUNK, c), m4), pl.BlockSpec((nseq, nc, 8, c), m4),
                pl.BlockSpec((nseq, tm, c), m3)]

    per_seq = lambda a: a.reshape((nseq, ncs) + a.shape[1:])
    v3 = v.reshape(nseq, seq_len, c)
    yf, yr = pl.pallas_call(
        functools.partial(_rwkv_scan_body, nc, nseq),
        grid=(nt_seq,),
        in_specs=specs(fwd, fwd_v) + specs(rev, rev_v),
        out_specs=[pl.BlockSpec((nseq, tm, c), fwd_v), pl.BlockSpec((nseq, tm, c), rev_v)],
        out_shape=[jax.ShapeDtypeStruct((nseq, seq_len, c), F32)] * 2,
        scratch_shapes=[pltpu.VMEM((nseq, 2, 2, HALF, HALF), F32)],
        compiler_params=_cparams("arbitrary"),
        name="rwkv_scan",
    )(per_seq(qwf), per_seq(yuf), per_seq(bkf), per_seq(pcf), v3,
      per_seq(qwr), per_seq(yur), per_seq(bkr), per_seq(pcr), v3)
    return yf.reshape(nseq * seq_len, c), yr.reshape(nseq * seq_len, c)


def _rwkv_out(yf, yr, g, bonus, ln_g, ln_b, ones_bd):
    y = yf + yr
    inv_n = 1.0 / RWKV_HEAD_DIM
    mean = _seg_sum(y, ones_bd) * inv_n
    yc = y - mean
    var = _seg_sum(yc * yc, ones_bd) * inv_n
    yn = yc * lax.rsqrt(var + RWKV_GN_EPS) * ln_g + ln_b
    return (yn + bonus) * g


def _rwkv_branch(p_a, nseq, seq_len, prm):
    tm = min(512, seq_len)
    nt_seq = seq_len // tm
    loc = _rwkv_local(p_a, nt_seq, tm, prm)
    yf, yr = _rwkv_scan(loc, nseq, seq_len)
    return yf, yr, loc[0], loc[1]


FFT_N2 = 128
FILTER_EMB_DIM = 33
FILTER_BANDS = (FILTER_EMB_DIM - 1) // 2
FILTER_HIDDEN = 64
FILTER_TARGET = 1e-2
FILTER_FAST_DECAY_PCT = 0.3
FILTER_SLOW_DECAY_PCT = 1.5


def _hyena_prep_body(nt_seq, p_ref, pp_ref, pn_ref, cw_ref, cb_ref, src_ref, x2_ref):
    i = pl.program_id(0) % nt_seq
    p = p_ref[...]
    prev, nxt = _shifted(p, pp_ref[...], pn_ref[...], i == 0, i == nt_seq - 1)
    cw = cw_ref[...]
    u = prev * cw[0:1] + p * cw[1:2] + nxt * cw[2:3] + cb_ref[...]
    c = HYENA_WIDTH
    src_ref[...] = u[:, 0:c] * u[:, 2 * c:3 * c]
    x2_ref[...] = u[:, c:2 * c]


def _hyena_prep(p_b, nt_seq, tm, prm):
    t = p_b.shape[0]
    c = HYENA_WIDTH
    row = lambda i: (i, 0)
    full = lambda i: (0, 0)
    nblk8 = t // 8
    tm8 = tm // 8
    cw = jnp.concatenate([prm["hyena_conv_w"], jnp.zeros((5, C_HYENA_IN), F32)], axis=0)
    return pl.pallas_call(
        functools.partial(_hyena_prep_body, nt_seq),
        grid=(t // tm,),
        in_specs=[pl.BlockSpec((tm, C_HYENA_IN), row),
                  pl.BlockSpec((8, C_HYENA_IN), lambda i: (jnp.maximum(i * tm8 - 1, 0), 0)),
                  pl.BlockSpec((8, C_HYENA_IN), lambda i: (jnp.minimum((i + 1) * tm8, nblk8 - 1), 0)),
                  pl.BlockSpec((8, C_HYENA_IN), full), pl.BlockSpec((1, C_HYENA_IN), full)],
        out_specs=[pl.BlockSpec((tm, c), row)] * 2,
        out_shape=[jax.ShapeDtypeStruct((t, c), F32)] * 2,
        compiler_params=_cparams("parallel"),
        name="hyena_prep",
    )(p_b, p_b, p_b, cw, prm["hyena_conv_b"].reshape(1, -1))


def _hyena_filter_body(seq_len, w1_ref, b1_ref, w2_ref, b2_ref, w3_ref, freq_ref, delta_ref, o_ref):
    rows = o_ref.shape[0]
    n = pl.program_id(0) * rows + lax.broadcasted_iota(I32, (rows, 128), 0)
    lane = lax.broadcasted_iota(I32, (rows, 128), 1)
    pos = jnp.where(n < seq_len, n, 2 * seq_len - n).astype(F32)
    t = pos * (1.0 / (seq_len - 1))
    omega = (2.0 * math.pi) * pos / seq_len
    band_step = (FILTER_BANDS - 1 - 1e-4) / (FILTER_BANDS - 1)
    band_idx = jnp.where(lane <= FILTER_BANDS, lane - 1, lane - 1 - FILTER_BANDS).astype(F32)
    arg = (1e-4 + band_idx * band_step) * omega
    phase = jnp.where(lane <= FILTER_BANDS, 0.5 * math.pi, math.pi)
    z = jnp.where(lane == 0, t, jnp.where(lane <= 2 * FILTER_BANDS, jnp.sin(arg + phase), 0.0))
    half = rows // 2
    freq = freq_ref[...]
    hid = jnp.sin(freq * (_dot3(z[:half], w1_ref[0]) + _dot3(z[half:], w1_ref[1]) + b1_ref[...]))
    hid = jnp.sin(freq * (_dot3(hid, w2_ref[...]) + b2_ref[...]))
    filt = jnp.concatenate([_dot3(hid, w3_ref[0]), _dot3(hid, w3_ref[1])], axis=0)
    nn = n[:, 0:1]
    tt = t[:, 0:1]
    sel = jnp.where(nn < seq_len, filt[:, :HYENA_WIDTH], jnp.where(nn > seq_len, filt[:, HYENA_WIDTH:], 0.0))
    o_ref[...] = sel * jnp.exp(-tt * delta_ref[...])


def _hyena_filter(seq_len, prm):
    rows = min(1024, 2 * seq_len)
    c = HYENA_WIDTH
    fh = FILTER_HIDDEN
    w1 = jnp.concatenate([prm["hyena_f_w1"], jnp.zeros((128 - FILTER_EMB_DIM, fh), F32)], axis=0)
    z1 = jnp.zeros_like(w1)
    w1p = jnp.stack([jnp.concatenate([w1, z1], axis=1), jnp.concatenate([z1, w1], axis=1)])
    w2 = prm["hyena_f_w2"]
    z2 = jnp.zeros_like(w2)
    w2p = jnp.concatenate([jnp.concatenate([w2, z2], axis=1), jnp.concatenate([z2, w2], axis=1)], axis=0)
    w3 = prm["hyena_f_w3"]
    z3 = jnp.zeros_like(w3)
    w3p = jnp.stack([jnp.concatenate([w3, z3], axis=0), jnp.concatenate([z3, w3], axis=0)])
    twice = lambda a: jnp.tile(a.reshape(1, fh), (1, 2))
    min_decay = math.log(FILTER_TARGET) / FILTER_SLOW_DECAY_PCT
    max_decay = math.log(FILTER_TARGET) / FILTER_FAST_DECAY_PCT
    deltas = jnp.abs(jnp.linspace(min_decay, max_decay, c, dtype=F32)).reshape(1, c)
    full = lambda i: (0, 0)
    full3 = lambda i: (0, 0, 0)
    return pl.pallas_call(
        functools.partial(_hyena_filter_body, seq_len),
        grid=(2 * seq_len // rows,),
        in_specs=[pl.BlockSpec((2, 128, 2 * fh), full3), pl.BlockSpec((1, 2 * fh), full),
                  pl.BlockSpec((2 * fh, 2 * fh), full), pl.BlockSpec((1, 2 * fh), full),
                  pl.BlockSpec((2, 2 * fh, 2 * c), full3), pl.BlockSpec((1, 2 * fh), full),
                  pl.BlockSpec((1, c), full)],
        out_specs=pl.BlockSpec((rows, c), lambda i: (i, 0)),
        out_shape=jax.ShapeDtypeStruct((2 * seq_len, c), F32),
        compiler_params=_cparams("parallel"),
        name="hyena_filter",
    )(w1p, twice(prm["hyena_f_b1"]), w2p, twice(prm["hyena_f_b2"]), w3p, twice(prm["hyena_f_freq"]), deltas)


FFT_N2_TILE = 16
FFT_LANE_TILE = 256


def _lmul_rows(m_ref, x_ref):
    m = m_ref[...]
    xt = pltpu.einshape("rjc->jrc", x_ref[0].astype(F32))
    ys = [jnp.dot(m, xt[j].astype(BF16), preferred_element_type=F32) for j in range(FFT_N2_TILE)]
    return pltpu.einshape("jrc->rjc", jnp.stack(ys, axis=0))


def _lmul_body(m_ref, x_ref, o_ref):
    o_ref[0] = _lmul_rows(m_ref, x_ref).astype(o_ref.dtype)


def _lmul_epilogue_body(m_ref, x_ref, src_ref, x2_ref, skip_ref, o_ref):
    o_ref[0] = x2_ref[0] * (_lmul_rows(m_ref, x_ref) + src_ref[0] * skip_ref[...])


def _lmul(m, x, extra=None):
    nfft, r_in, n2, c = x.shape
    r_out = m.shape[0]
    ct = FFT_LANE_TILE
    xs = pl.BlockSpec((1, r_in, FFT_N2_TILE, ct), lambda f, j, l: (f, 0, j, l))
    os_ = pl.BlockSpec((1, r_out, FFT_N2_TILE, ct), lambda f, j, l: (f, 0, j, l))
    ms = pl.BlockSpec(m.shape, lambda f, j, l: (0, 0))
    if extra is None:
        body, ins, specs, out_dtype = _lmul_body, (m, x), [ms, xs], BF16
    else:
        src, x2, skip = extra
        body, ins, out_dtype = _lmul_epilogue_body, (m, x, src, x2, skip), F32
        specs = [ms, xs, os_, os_, pl.BlockSpec((1, ct), lambda f, j, l: (0, l))]
    return pl.pallas_call(
        body,
        grid=(nfft, n2 // FFT_N2_TILE, c // ct),
        in_specs=specs,
        out_specs=os_,
        out_shape=jax.ShapeDtypeStruct((nfft, r_out, n2, c), out_dtype),
        compiler_params=_cparams("parallel", "parallel", "parallel"),
        name="fft_outer" if extra is None else "fft_outer_out",
    )(*ins)


def _fft_inner_body(kt, conv, g_ref, y_ref, *rest):
    if conv:
        h_ref, o_ref = rest
    else:
        (o_ref,) = rest
    n2 = FFT_N2
    for q in range(kt):
        g = g_ref[q]
        yk = jnp.concatenate([y_ref[0, 0, q], y_ref[0, 1, q]], axis=0).astype(BF16)
        z = jnp.dot(g, yk, preferred_element_type=F32)
        if conv:
            zr, zi = z[:n2], z[n2:]
            hr, hi = h_ref[0, 0, q], h_ref[0, 1, q]
            pr = zr * hr - zi * hi
            pi = zr * hi + zi * hr
            prod = jnp.concatenate([pr, pi], axis=0).astype(BF16)
            z = lax.dot_general(g, prod, _TN, preferred_element_type=F32)
        o_ref[0, 0, q] = z[:n2].astype(o_ref.dtype)
        o_ref[0, 1, q] = z[n2:].astype(o_ref.dtype)


def _fft_inner(g, y, h, kt):
    nfft, _, n1, n2, c = y.shape
    blk = pl.BlockSpec((1, 2, kt, n2, c), lambda f, j: (f, 0, j, 0, 0))
    gs = pl.BlockSpec((kt, 2 * n2, 2 * n2), lambda f, j: (j, 0, 0))
    conv = h is not None
    ins = (g, y, h) if conv else (g, y)
    specs = [gs, blk, pl.BlockSpec((1, 2, kt, n2, c), lambda f, j: (0, 0, j, 0, 0))] if conv else [gs, blk]
    return pl.pallas_call(
        functools.partial(_fft_inner_body, kt, conv),
        grid=(nfft, n1 // kt),
        in_specs=specs,
        out_specs=blk,
        out_shape=jax.ShapeDtypeStruct(y.shape, BF16 if conv else F32),
        compiler_params=_cparams("parallel", "parallel"),
        name="fft_inner_conv" if conv else "fft_inner",
    )(*ins)


def _dft_tables(seq_len):
    n = 2 * seq_len
    n2 = FFT_N2
    n1 = n // n2
    k1 = jnp.arange(n1, dtype=I32)
    ang1 = (2.0 * math.pi / n1) * ((k1[:, None] * k1[None, :]) % n1).astype(F32)
    c1, s1 = jnp.cos(ang1), jnp.sin(ang1)
    half = n1 // 2
    f_pair = jnp.concatenate([jnp.concatenate([c1[:, :half], s1[:, :half]], 1),
                              jnp.concatenate([-s1[:, :half], c1[:, :half]], 1)], 0)
    f_real_half = jnp.concatenate([c1[:, :half], -s1[:, :half]], 0)
    f_real_full = jnp.concatenate([c1, -s1], 0)
    ci, si = c1[:half] / n, s1[:half] / n
    b_real = jnp.concatenate([ci, -si], 1)
    b_pair = jnp.concatenate([b_real, jnp.concatenate([si, ci], 1)], 0)
    kk = k1[:, None, None] + n1 * jnp.arange(n2, dtype=I32)[None, :, None]
    ang2 = (2.0 * math.pi / n) * ((kk * jnp.arange(n2, dtype=I32)[None, None, :]) % n).astype(F32)
    c2, s2 = jnp.cos(ang2), jnp.sin(ang2)
    g = jnp.concatenate([jnp.concatenate([c2, s2], 2), jnp.concatenate([-s2, c2], 2)], 1)
    cast = lambda a: a.astype(BF16)
    return dict(f_pair=cast(f_pair), f_real_half=cast(f_real_half), f_real_full=cast(f_real_full),
                b_pair=cast(b_pair), b_real=cast(b_real), g=cast(g), n1=n1)


def _hyena_branch(p_b, nseq, seq_len, prm):
    tm = min(512, seq_len)
    c = HYENA_WIDTH
    src, x2 = _hyena_prep(p_b, seq_len // tm, tm, prm)
    tab = _dft_tables(seq_len)
    n1 = tab["n1"]
    n2 = FFT_N2
    kt = 4
    filt = _hyena_filter(seq_len, prm)
    hy = _lmul(tab["f_real_full"], filt.reshape(1, n1, n2, c))
    hspec = _fft_inner(tab["g"], hy.reshape(1, 2, n1, n2, c), None, kt)
    pair = nseq % 2 == 0
    nfft = nseq // 2 if pair else nseq
    rows = n1 if pair else n1 // 2
    xin = src.reshape(nfft, rows, n2, c)
    y = _lmul(tab["f_pair"] if pair else tab["f_real_half"], xin)
    w = _fft_inner(tab["g"], y.reshape(nfft, 2, n1, n2, c), hspec, kt)
    out = _lmul(tab["b_pair"] if pair else tab["b_real"], w.reshape(nfft, 2 * n1, n2, c),
                extra=(xin, x2.reshape(nfft, rows, n2, c), prm["hyena_skip"].reshape(1, c)))
    return out.reshape(nseq * seq_len, c)


MOE_GROUPS = 4
MOE_EXPERTS_PER_GROUP = 8
MOE_EXPERTS = MOE_GROUPS * MOE_EXPERTS_PER_GROUP
MOE_HIDDEN = D_MODEL // 2
ROUTE_LANES = 128


def _route(logits):
    lane = lax.broadcasted_iota(I32, logits.shape, 1)
    neg = -jnp.inf
    big = ROUTE_LANES
    gl = jnp.where(lane < MOE_GROUPS, logits, neg)
    gmax = jnp.max(gl, axis=-1, keepdims=True)
    grp = jnp.min(jnp.where(gl == gmax, lane, big), axis=-1, keepdims=True)
    p_grp = 1.0 / jnp.sum(jnp.exp(gl - gmax), axis=-1, keepdims=True)
    lo = MOE_GROUPS + grp * MOE_EXPERTS_PER_GROUP
    el = jnp.where((lane >= lo) & (lane < lo + MOE_EXPERTS_PER_GROUP), logits, neg)
    m1 = jnp.max(el, axis=-1, keepdims=True)
    i1 = jnp.min(jnp.where(el == m1, lane, big), axis=-1, keepdims=True)
    el2 = jnp.where(lane == i1, neg, el)
    m2 = jnp.max(el2, axis=-1, keepdims=True)
    i2 = jnp.min(jnp.where(el2 == m2, lane, big), axis=-1, keepdims=True)
    e2 = jnp.exp(m2 - m1)
    g1 = p_grp / (1.0 + e2)
    g2 = p_grp * e2 / (1.0 + e2)
    gates = jnp.where(lane == i1, g1, jnp.where(lane == i2, g2, 0.0))
    return pltpu.roll(gates, ROUTE_LANES - MOE_GROUPS, 1)


def _merge_body(x_ref, yf_ref, yr_ref, rg_ref, bonus_ref, lng_ref, lnb_ref, ones_ref, yb_ref, pg_ref,
                wua_ref, wub_ref, wo_ref, g_ref, wr_ref, br_ref, x1_ref, xn_ref, gwt_ref):
    ya = _rwkv_out(yf_ref[...], yr_ref[...], rg_ref[...], bonus_ref[...], lng_ref[...], lnb_ref[...], ones_ref[...])
    ga = jax.nn.sigmoid(pg_ref[:, :D_MODEL].astype(F32))
    gb = jax.nn.sigmoid(pg_ref[:, D_MODEL:].astype(F32))
    merged = ga * _dot(ya, wua_ref[...]) + gb * _dot(yb_ref[...], wub_ref[...])
    x1 = x_ref[...] + _dot(merged, wo_ref[...])
    x1_ref[...] = x1
    xn = x1 * lax.rsqrt(jnp.mean(x1 * x1, axis=-1, keepdims=True) + NORM_EPS) * g_ref[...]
    xn_ref[...] = xn.astype(BF16)
    logits = _dot3(xn, wr_ref[...]) + br_ref[...]
    gwt_ref[...] = _route(logits).T


def _merge(x, rwkv, yb, pg, prm, wts, tm):
    t = x.shape[0]
    c = RWKV_WIDTH
    row = lambda i: (i, 0)
    full = lambda i: (0, 0)
    hd = np.arange(c) // RWKV_HEAD_DIM
    ones_bd = jnp.asarray(hd[:, None] == hd[None, :], BF16)
    wr = jnp.concatenate([prm["moe_w_route_group"], prm["moe_w_route_expert"],
                          jnp.zeros((D_MODEL, ROUTE_LANES - MOE_GROUPS - MOE_EXPERTS), F32)], axis=1)
    br = jnp.concatenate([prm["moe_b_route_group"], prm["moe_b_route_expert"],
                          jnp.zeros((ROUTE_LANES - MOE_GROUPS - MOE_EXPERTS,), F32)]).reshape(1, ROUTE_LANES)
    return pl.pallas_call(
        _merge_body,
        grid=(t // tm,),
        in_specs=[pl.BlockSpec((tm, D_MODEL), row)] + [pl.BlockSpec((tm, c), row)] * 4
                 + [pl.BlockSpec((1, c), full)] * 2 + [pl.BlockSpec((c, c), full),
                  pl.BlockSpec((tm, HYENA_WIDTH), row), pl.BlockSpec((tm, C_GATES), row),
                  pl.BlockSpec((RWKV_WIDTH, D_MODEL), full), pl.BlockSpec((HYENA_WIDTH, D_MODEL), full),
                  pl.BlockSpec((D_MODEL, D_MODEL), full), pl.BlockSpec((1, D_MODEL), full),
                  pl.BlockSpec((D_MODEL, ROUTE_LANES), full), pl.BlockSpec((1, ROUTE_LANES), full)],
        out_specs=[pl.BlockSpec((tm, D_MODEL), row), pl.BlockSpec((tm, D_MODEL), row),
                   pl.BlockSpec((ROUTE_LANES, tm), lambda i: (0, i))],
        out_shape=[jax.ShapeDtypeStruct((t, D_MODEL), F32), jax.ShapeDtypeStruct((t, D_MODEL), BF16),
                   jax.ShapeDtypeStruct((ROUTE_LANES, t), F32)],
        compiler_params=_cparams("parallel"),
        name="merge_route",
    )(x, *rwkv, prm["rwkv_ln_g"].reshape(1, c), prm["rwkv_ln_b"].reshape(1, c), ones_bd, yb, pg,
      wts["w_up_a"], wts["w_up_b"], wts["w_out"], prm["norm_ffn_g"].reshape(1, D_MODEL), wr, br)


MOE_PAD = 16
MOE_BLK = 256
MOE_FFN_ROWS = 512


def _moe_slots(tt):
    ns = 2 * tt + MOE_EXPERTS * MOE_PAD
    return (ns + MOE_BLK - 1) // MOE_BLK * MOE_BLK


def _moe_used_rows(i, lstart_ref, npiece_ref):
    last = i * MOE_EXPERTS + MOE_EXPERTS - 1
    return lstart_ref[last] + npiece_ref[last] * MOE_PAD


def _moe_run_copies(i, lstart_ref, npiece_ref, goff_ref, local_refs, global_refs, sems, to_global, wait):
    for e in range(MOE_EXPERTS):
        ls = lstart_ref[i * MOE_EXPERTS + e]
        go = goff_ref[i * MOE_EXPERTS + e]

        def piece(p, carry, ls=ls, go=go):
            lrows = pl.ds(pl.multiple_of(ls + p * MOE_PAD, MOE_PAD), MOE_PAD)
            grows = pl.ds(pl.multiple_of(go + p * MOE_PAD, MOE_PAD), MOE_PAD)
            for loc, glob, sem in zip(local_refs, global_refs, sems):
                src, dst = (loc.at[lrows, :], glob.at[grows, :]) if to_global else (glob.at[grows, :], loc.at[lrows, :])
                cp = pltpu.make_async_copy(src, dst, sem)
                if wait:
                    cp.wait()
                else:
                    cp.start()
            return carry

        lax.fori_loop(0, npiece_ref[i * MOE_EXPERTS + e], piece, 0)


def _moe_compact_body(lstart_ref, npiece_ref, goff_ref, xn_ref, gwt_ref, tri_ref, pcol_ref, xg0_ref, gg0_ref,
                      xg_out, gg_out, dcol_ref, xg_loc, gg_loc, sems):
    del xg0_ref, gg0_ref
    i = pl.program_id(0)
    tt = xn_ref.shape[0]
    ns = _moe_slots(tt)
    gw = gwt_ref[0:MOE_EXPERTS, :]
    sel = gw > 0.0
    rank = jnp.dot(sel.astype(BF16), tri_ref[...], preferred_element_type=F32)
    dest = pcol_ref[:, 0:1] + rank
    d_lo = jnp.min(jnp.where(sel, dest, 1e9), axis=0, keepdims=True)
    d_hi = jnp.max(jnp.where(sel, dest, -1.0), axis=0, keepdims=True)
    g_lo = jnp.sum(jnp.where(sel & (dest == d_lo), gw, 0.0), axis=0, keepdims=True)
    g_hi = jnp.sum(jnp.where(sel & (dest == d_hi), gw, 0.0), axis=0, keepdims=True)
    single = d_hi == d_lo
    g_hi = jnp.where(single, 0.0, g_hi)
    d_hi = jnp.where(single, -1.0, d_hi)
    rows8 = jnp.concatenate([d_lo, d_hi, g_lo, g_hi, jnp.zeros((4, tt), F32)], axis=0)
    dcol_ref[...] = jnp.concatenate([rows8, jnp.zeros((ROUTE_LANES - 8, tt), F32)], axis=0).T
    used_rows = _moe_used_rows(i, lstart_ref, npiece_ref)

    def gather_block(blk):
        r = (blk * MOE_BLK + lax.broadcasted_iota(I32, (MOE_BLK, tt), 0)).astype(F32)
        lo, hi = r == d_lo, r == d_hi
        rows = slice(blk * MOE_BLK, (blk + 1) * MOE_BLK)
        xg_loc[rows, :] = jnp.dot((lo | hi).astype(BF16), xn_ref[...], preferred_element_type=F32).astype(BF16)
        row_gate = jnp.sum(jnp.where(lo, g_lo, 0.0) + jnp.where(hi, g_hi, 0.0), axis=-1, keepdims=True)
        gg_loc[rows, :] = jnp.broadcast_to(row_gate, (MOE_BLK, ROUTE_LANES))

    for blk in range(ns // MOE_BLK):
        if blk * MOE_BLK < 2 * tt:
            gather_block(blk)
        else:
            pl.when(blk * MOE_BLK < used_rows)(functools.partial(gather_block, blk))
    copies = functools.partial(_moe_run_copies, i, lstart_ref, npiece_ref, goff_ref, (xg_loc, gg_loc),
                               (xg_out, gg_out), (sems.at[0], sems.at[1]), True)
    copies(False)
    copies(True)


def _moe_ffn_body(bexp_ref, nused_ref, xg_ref, gg_ref, wg_ref, wu_ref, wd_ref, og_ref):
    del bexp_ref
    used = pl.program_id(0) < nused_ref[0]

    @pl.when(jnp.logical_not(used))
    def _():
        og_ref[...] = jnp.zeros(og_ref.shape, BF16)

    @pl.when(used)
    def _():
        xg = xg_ref[...]
        hg = jnp.dot(xg, wg_ref[0], preferred_element_type=F32)
        hu = jnp.dot(xg, wu_ref[0], preferred_element_type=F32)
        hid = ((hg * jax.nn.sigmoid(hg)) * hu).astype(BF16)
        out = jnp.dot(hid, wd_ref[0], preferred_element_type=F32)
        og_ref[...] = (out * gg_ref[:, 0:1]).astype(BF16)


def _moe_scatter_body(final_norm, lstart_ref, npiece_ref, goff_ref, og_ref, x1_ref, dcol_ref, gf_ref, o_ref,
                      og_loc, sems):
    i = pl.program_id(0)
    tt = x1_ref.shape[0]
    ns = _moe_slots(tt)

    @pl.when(i == 0)
    def _():
        og_loc[...] = jnp.zeros(og_loc.shape, BF16)

    copies = functools.partial(_moe_run_copies, i, lstart_ref, npiece_ref, goff_ref, (og_loc,), (og_ref,),
                               (sems.at[0],), False)
    copies(False)
    copies(True)
    used_rows = _moe_used_rows(i, lstart_ref, npiece_ref)
    c_lo, c_hi = dcol_ref[:, 0:1], dcol_ref[:, 1:2]

    def scatter_block(blk):
        r = (blk * MOE_BLK + lax.broadcasted_iota(I32, (tt, MOE_BLK), 1)).astype(F32)
        onehot = ((r == c_lo) | (r == c_hi)).astype(BF16)
        return jnp.dot(onehot, og_loc[blk * MOE_BLK:(blk + 1) * MOE_BLK, :], preferred_element_type=F32)

    always = [blk for blk in range(ns // MOE_BLK) if blk * MOE_BLK < 2 * tt]
    y = x1_ref[...]
    for blk in always:
        y = y + scatter_block(blk)
    o_ref[...] = y
    for blk in range(len(always), ns // MOE_BLK):

        @pl.when(blk * MOE_BLK < used_rows)
        def _(blk=blk):
            o_ref[...] += scatter_block(blk)

    if final_norm:
        y = o_ref[...]
        o_ref[...] = y * lax.rsqrt(jnp.mean(y * y, axis=-1, keepdims=True) + NORM_EPS) * gf_ref[...]


def _moe_grouped(xn, gwt, x1, wts, norm_final_g, tt):
    t = xn.shape[0]
    final_norm = norm_final_g is not None
    if not final_norm:
        norm_final_g = jnp.ones((D_MODEL,), F32)
    nt = t // tt
    ns = _moe_slots(tt)
    ne = MOE_EXPERTS
    counts = jnp.sum((gwt[:ne] > 0.0).reshape(ne, nt, tt), axis=-1, dtype=I32).T
    padded = (counts + MOE_PAD - 1) // MOE_PAD * MOE_PAD
    lstart = jnp.cumsum(padded, axis=1) - padded
    region = (jnp.sum(padded, axis=0) + MOE_FFN_ROWS - 1) // MOE_FFN_ROWS * MOE_FFN_ROWS
    gend = jnp.cumsum(region)
    goff = (gend - region)[None, :] + jnp.cumsum(padded, axis=0) - padded
    ng = (2 * t + nt * ne * MOE_PAD + ne * MOE_FFN_ROWS + MOE_FFN_ROWS - 1) // MOE_FFN_ROWS * MOE_FFN_ROWS
    nb = ng // MOE_FFN_ROWS
    block_row = jnp.arange(nb, dtype=I32)[:, None] * MOE_FFN_ROWS
    bexp = jnp.minimum(jnp.sum(block_row >= gend[None, :], axis=1, dtype=I32), ne - 1)
    nused = (gend[-1] // MOE_FFN_ROWS).astype(I32).reshape(1)
    pcol = jnp.broadcast_to(lstart.astype(F32).reshape(nt * ne, 1), (nt * ne, ROUTE_LANES))
    idx = np.arange(tt)
    tri = jnp.asarray(idx[:, None] < idx[None, :], BF16)
    sched = (lstart.reshape(-1), (padded // MOE_PAD).reshape(-1), goff.reshape(-1).astype(I32))
    any_spec = pl.BlockSpec(memory_space=pl.ANY)

    xg, gg, dcol = pl.pallas_call(
        _moe_compact_body,
        grid_spec=pltpu.PrefetchScalarGridSpec(
            num_scalar_prefetch=3,
            grid=(nt,),
            in_specs=[pl.BlockSpec((tt, D_MODEL), lambda i, *_: (i, 0)),
                      pl.BlockSpec((ROUTE_LANES, tt), lambda i, *_: (0, i)),
                      pl.BlockSpec((tt, tt), lambda i, *_: (0, 0)),
                      pl.BlockSpec((ne, ROUTE_LANES), lambda i, *_: (i, 0)),
                      any_spec, any_spec],
            out_specs=[any_spec, any_spec, pl.BlockSpec((tt, ROUTE_LANES), lambda i, *_: (i, 0))],
            scratch_shapes=[pltpu.VMEM((ns, D_MODEL), BF16), pltpu.VMEM((ns, ROUTE_LANES), F32),
                            pltpu.SemaphoreType.DMA((2,))],
        ),
        out_shape=[jax.ShapeDtypeStruct((ng, D_MODEL), BF16), jax.ShapeDtypeStruct((ng, ROUTE_LANES), F32),
                   jax.ShapeDtypeStruct((t, ROUTE_LANES), F32)],
        input_output_aliases={7: 0, 8: 1},
        compiler_params=_cparams("arbitrary"),
        name="moe_compact",
    )(*sched, xn, gwt, tri, pcol, jnp.zeros((ng, D_MODEL), BF16), jnp.zeros((ng, ROUTE_LANES), F32))

    og = pl.pallas_call(
        _moe_ffn_body,
        grid_spec=pltpu.PrefetchScalarGridSpec(
            num_scalar_prefetch=2,
            grid=(nb,),
            in_specs=[pl.BlockSpec((MOE_FFN_ROWS, D_MODEL), lambda b, *_: (b, 0)),
                      pl.BlockSpec((MOE_FFN_ROWS, ROUTE_LANES), lambda b, *_: (b, 0)),
                      pl.BlockSpec((1, D_MODEL, MOE_HIDDEN), lambda b, be, nu: (be[b], 0, 0)),
                      pl.BlockSpec((1, D_MODEL, MOE_HIDDEN), lambda b, be, nu: (be[b], 0, 0)),
                      pl.BlockSpec((1, MOE_HIDDEN, D_MODEL), lambda b, be, nu: (be[b], 0, 0))],
            out_specs=pl.BlockSpec((MOE_FFN_ROWS, D_MODEL), lambda b, *_: (b, 0)),
        ),
        out_shape=jax.ShapeDtypeStruct((ng, D_MODEL), BF16),
        compiler_params=_cparams("arbitrary"),
        name="moe_ffn",
    )(bexp, nused, xg, gg, wts["moe_w_gate"], wts["moe_w_up"], wts["moe_w_down"])

    return pl.pallas_call(
        functools.partial(_moe_scatter_body, final_norm),
        grid_spec=pltpu.PrefetchScalarGridSpec(
            num_scalar_prefetch=3,
            grid=(nt,),
            in_specs=[any_spec,
                      pl.BlockSpec((tt, D_MODEL), lambda i, *_: (i, 0)),
                      pl.BlockSpec((tt, ROUTE_LANES), lambda i, *_: (i, 0)),
                      pl.BlockSpec((1, D_MODEL), lambda i, *_: (0, 0))],
            out_specs=pl.BlockSpec((tt, D_MODEL), lambda i, *_: (i, 0)),
            scratch_shapes=[pltpu.VMEM((ns, D_MODEL), BF16), pltpu.SemaphoreType.DMA((1,))],
        ),
        out_shape=jax.ShapeDtypeStruct((t, D_MODEL), F32),
        compiler_params=_cparams("arbitrary"),
        name="moe_scatter",
    )(*sched, og, x1, dcol, norm_final_g.reshape(1, D_MODEL))


def _trunk(x, prm, wts, norm_final_g):
    nseq, seq_len, _ = x.shape
    xf = x.reshape(nseq * seq_len, D_MODEL)
    p_a, p_b, p_g = _norm_in_proj(xf, prm["norm_mix_g"], wts["w_in"], min(512, seq_len))
    rwkv = _rwkv_branch(p_a, nseq, seq_len, prm)
    yb = _hyena_branch(p_b, nseq, seq_len, prm)
    x1, xn, gwt = _merge(xf, rwkv, yb, p_g, prm, wts, min(512, seq_len))
    out = _moe_grouped(xn, gwt, x1, wts, norm_final_g, min(1024, seq_len))
    return out.reshape(nseq, seq_len, D_MODEL)


def kernel(x_prompt, x_sample, norm_mix_g, w_in, rwkv_mu_prev, rwkv_mu_next, rwkv_w0, rwkv_w2, rwkv_a0, rwkv_a2, rwkv_g2, rwkv_k_k, rwkv_k_a, rwkv_r_k, rwkv_ln_g, rwkv_ln_b, hyena_conv_w, hyena_conv_b, hyena_f_w1, hyena_f_b1, hyena_f_w2, hyena_f_b2, hyena_f_w3, hyena_f_freq, hyena_skip, w_up_a, w_up_b, w_out, norm_ffn_g, moe_w_route_group, moe_b_route_group, moe_w_route_expert, moe_b_route_expert, moe_w_gate, moe_w_up, moe_w_down, norm_final_g):
    layer = dict(norm_mix_g=norm_mix_g, w_in=w_in, rwkv_mu_prev=rwkv_mu_prev, rwkv_mu_next=rwkv_mu_next,
                 rwkv_w0=rwkv_w0, rwkv_w2=rwkv_w2, rwkv_a0=rwkv_a0, rwkv_a2=rwkv_a2, rwkv_g2=rwkv_g2,
                 rwkv_k_k=rwkv_k_k, rwkv_k_a=rwkv_k_a, rwkv_r_k=rwkv_r_k, rwkv_ln_g=rwkv_ln_g, rwkv_ln_b=rwkv_ln_b,
                 hyena_conv_w=hyena_conv_w, hyena_conv_b=hyena_conv_b, hyena_f_w1=hyena_f_w1, hyena_f_b1=hyena_f_b1,
                 hyena_f_w2=hyena_f_w2, hyena_f_b2=hyena_f_b2, hyena_f_w3=hyena_f_w3, hyena_f_freq=hyena_f_freq,
                 hyena_skip=hyena_skip, w_up_a=w_up_a, w_up_b=w_up_b, w_out=w_out, norm_ffn_g=norm_ffn_g,
                 moe_w_route_group=moe_w_route_group, moe_b_route_group=moe_b_route_group,
                 moe_w_route_expert=moe_w_route_expert, moe_b_route_expert=moe_b_route_expert,
                 moe_w_gate=moe_w_gate, moe_w_up=moe_w_up, moe_w_down=moe_w_down)
    depth = norm_mix_g.shape[0]
    big = ("w_in", "w_up_a", "w_up_b", "w_out", "moe_w_gate", "moe_w_up", "moe_w_down")

    def trunk(x):
        for li in range(depth):
            prm = {k: v[li] for k, v in layer.items()}
            wts = {k: prm[k].astype(BF16) for k in big}
            last = li == depth - 1
            x = _trunk(x, prm, wts, norm_final_g if last else None)
        return x

    return (trunk(x_prompt), trunk(x_sample))
```

```python
import functools
import math

import jax
import jax.numpy as jnp
import numpy as np
from jax import lax
from jax.experimental import pallas as pl
from jax.experimental.pallas import tpu as pltpu

F32 = jnp.float32
BF16 = jnp.bfloat16
I32 = jnp.int32

D_MODEL = 1024
NORM_EPS = 1e-6
RWKV_HEADS = 8
RWKV_HEAD_DIM = 64
RWKV_WIDTH = RWKV_HEADS * RWKV_HEAD_DIM
RWKV_DECAY_RANK = 64
RWKV_ICLR_RANK = 64
RWKV_GATE_RANK = 128
RWKV_GN_EPS = 64e-5
HYENA_WIDTH = D_MODEL // 2
C_RWKV_IN = 3 * RWKV_WIDTH + 2 * RWKV_DECAY_RANK + 2 * RWKV_ICLR_RANK + RWKV_GATE_RANK
C_HYENA_IN = 3 * HYENA_WIDTH
C_GATES = 2 * D_MODEL

CHUNK = 32
STACK = RWKV_HEADS * CHUNK
HALF = RWKV_WIDTH // 2
GROUP_HEADS = 4
GSTACK = GROUP_HEADS * CHUNK
LOCAL_CHUNKS_PER_ITER = 4

VMEM_LIMIT = 48 * 1024 * 1024

_NN = (((1,), (0,)), ((), ()))
_NT = (((1,), (1,)), ((), ()))
_TN = (((0,), (0,)), ((), ()))


def _dot(a, b, dims=_NN):
    return lax.dot_general(a.astype(BF16), b.astype(BF16), dims, preferred_element_type=F32)


def _split2(x):
    hi = x.astype(BF16)
    lo = (x - hi.astype(F32)).astype(BF16)
    return hi, lo


def _dot3(a, b, dims=_NN):
    ah, al = _split2(a)
    bh, bl = _split2(b)
    dg = functools.partial(lax.dot_general, dimension_numbers=dims, preferred_element_type=F32)
    return dg(ah, bh) + (dg(ah, bl) + dg(al, bh))


def _dot2(a, b, dims=_NN):
    ah, al = _split2(a)
    bh = b.astype(BF16)
    dg = functools.partial(lax.dot_general, dimension_numbers=dims, preferred_element_type=F32)
    return dg(ah, bh) + dg(al, bh)


def _dot_exact_lhs(a_bf16, x):
    x1, x2 = _split2(x)
    dg = functools.partial(lax.dot_general, dimension_numbers=_NN, preferred_element_type=F32)
    return dg(a_bf16, x1) + dg(a_bf16, x2)


def _seg_sum(x, ones_bd):
    x1, x2 = _split2(x)
    dg = functools.partial(lax.dot_general, dimension_numbers=_NN, preferred_element_type=F32)
    halves = []
    for hf in range(2):
        lanes = slice(hf * HALF, (hf + 1) * HALF)
        ones_h = ones_bd[lanes, lanes]
        halves.append(dg(x1[:, lanes], ones_h) + dg(x2[:, lanes], ones_h))
    return jnp.concatenate(halves, axis=1)


def _cparams(*sem):
    return pltpu.CompilerParams(dimension_semantics=tuple(sem), vmem_limit_bytes=VMEM_LIMIT)


def _norm_in_proj_body(x_ref, g_ref, wa_ref, wb_ref, wg_ref, pa_ref, pb_ref, pg_ref):
    x = x_ref[...]
    xn = x * lax.rsqrt(jnp.mean(x * x, axis=-1, keepdims=True) + NORM_EPS) * g_ref[...]
    xb = xn.astype(BF16)
    pa_ref[...] = jnp.dot(xb, wa_ref[...], preferred_element_type=F32)
    pb_ref[...] = jnp.dot(xb, wb_ref[...], preferred_element_type=F32)
    pg_ref[...] = jnp.dot(xb, wg_ref[...], preferred_element_type=F32).astype(BF16)


def _norm_in_proj(x, g, w_in, tm):
    t = x.shape[0]
    wa = w_in[:, :C_RWKV_IN].astype(BF16)
    wb = w_in[:, C_RWKV_IN:C_RWKV_IN + C_HYENA_IN].astype(BF16)
    wg = w_in[:, C_RWKV_IN + C_HYENA_IN:].astype(BF16)
    full = lambda i: (0, 0)
    row = lambda i: (i, 0)
    return pl.pallas_call(
        _norm_in_proj_body,
        grid=(t // tm,),
        in_specs=[pl.BlockSpec((tm, D_MODEL), row), pl.BlockSpec((1, D_MODEL), full),
                  pl.BlockSpec(wa.shape, full), pl.BlockSpec(wb.shape, full), pl.BlockSpec(wg.shape, full)],
        out_specs=[pl.BlockSpec((tm, C_RWKV_IN), row), pl.BlockSpec((tm, C_HYENA_IN), row),
                   pl.BlockSpec((tm, C_GATES), row)],
        out_shape=[jax.ShapeDtypeStruct((t, C_RWKV_IN), F32), jax.ShapeDtypeStruct((t, C_HYENA_IN), F32),
                   jax.ShapeDtypeStruct((t, C_GATES), BF16)],
        compiler_params=_cparams("parallel"),
        name="norm_in_proj",
    )(x, g.reshape(1, D_MODEL), wa, wb, wg)


def _shifted(p, prev_blk, next_blk, is_first, is_last):
    tm = p.shape[0]
    row = lax.broadcasted_iota(I32, p.shape, 0)
    prow = jnp.where(is_first, 0.0, prev_blk[7:8, :])
    nrow = jnp.where(is_last, 0.0, next_blk[0:1, :])
    prev = jnp.where(row == 0, prow, pltpu.roll(p, 1, 0))
    nxt = jnp.where(row == tm - 1, nrow, pltpu.roll(p, tm - 1, 0))
    return prev, nxt


def _stack_heads(x, head_mask):
    return jnp.where(head_mask, jnp.concatenate([x] * GROUP_HEADS, axis=0), 0.0)


def _unstack_heads(z):
    out = z[0:CHUNK]
    for h in range(1, GROUP_HEADS):
        out = out + z[h * CHUNK:(h + 1) * CHUNK]
    return out


def _chunks_local(chains, fillers=()):
    fillers = list(fillers)

    def fill():
        if fillers:
            fillers.pop(0)()

    ngrp = RWKV_HEADS // GROUP_HEADS
    ti = lax.broadcasted_iota(I32, (CHUNK, CHUNK), 0)
    si = lax.broadcasted_iota(I32, (CHUNK, CHUNK), 1)
    srow = lax.broadcasted_iota(I32, (GSTACK, HALF), 0)
    slane = lax.broadcasted_iota(I32, (GSTACK, HALF), 1)
    head_mask = (srow >> 5) == (slane >> 6)
    mrow = lax.broadcasted_iota(I32, (GSTACK, GSTACK), 0)
    mcol = lax.broadcasted_iota(I32, (GSTACK, GSTACK), 1)
    same = (mrow >> 5) == (mcol >> 5)
    masks = {False: (same & (mrow > mcol), same & (mrow >= mcol)),
             True: (same & (mrow < mcol), same & (mrow <= mcol))}
    tris = {False: (ti >= si).astype(BF16), True: (ti <= si).astype(BF16)}

    cls = [_dot_exact_lhs(tris[rev], lw) for (_, _, _, lw, _, _, rev) in chains]
    st = []
    for (r, v, kk, lw, k, a, rev), cl in zip(chains, cls):
        tot = cl[0:1] if rev else cl[CHUNK - 1:CHUNK]
        e_neg = jnp.exp(-cl)
        e_tail = jnp.exp(tot - cl)
        beta = kk * a
        st.append(dict(rev=rev, v=v, alpha_b=-kk * jnp.exp(cl - lw), r_b=r * jnp.exp(cl), beta_b=beta * e_neg,
                       k_b=k * e_neg, bt=beta * e_tail, kt=k * e_tail, pc=jnp.exp(tot)))

    for s in st:
        s["sa"], s["sv"], s["lhs"], s["rhs"] = [], [], [], []
        for g in range(ngrp):
            lanes = slice(g * HALF, (g + 1) * HALF)
            sa_ = _stack_heads(s["alpha_b"][:, lanes], head_mask).astype(BF16)
            sr_ = _stack_heads(s["r_b"][:, lanes], head_mask).astype(BF16)
            s["sa"].append(sa_)
            s["sv"].append(_stack_heads(s["v"][:, lanes], head_mask).astype(BF16))
            s["lhs"].append(jnp.concatenate([sa_, sr_], axis=0))
            s["rhs"].append(jnp.concatenate([s["beta_b"][:, lanes].astype(BF16)] * GROUP_HEADS
                                            + [s["k_b"][:, lanes].astype(BF16)] * GROUP_HEADS, axis=0))
    pms = [[lax.dot_general(s["lhs"][g], s["rhs"][g], _NT, preferred_element_type=F32) for g in range(ngrp)]
           for s in st]
    zero = jnp.zeros((GSTACK, GSTACK), F32)
    drow = lax.broadcasted_iota(I32, (STACK, STACK), 0)
    dcol = lax.broadcasted_iota(I32, (STACK, STACK), 1)
    eye = jnp.where(drow == dcol, 1.0, 0.0)
    for s, pm in zip(st, pms):
        strict, incl = masks[s["rev"]]
        s["ak"] = [jnp.where(strict, pm[g][:GSTACK, GSTACK:], 0.0).astype(BF16) for g in range(ngrp)]
        s["rb"] = [jnp.where(incl, pm[g][GSTACK:, :GSTACK], 0.0).astype(BF16) for g in range(ngrp)]
        s["rk"] = [jnp.where(incl, pm[g][GSTACK:, GSTACK:], 0.0).astype(BF16) for g in range(ngrp)]
        ab = [jnp.where(strict, pm[g][:GSTACK, :GSTACK], 0.0) for g in range(ngrp)]
        s["apow"] = jnp.concatenate([jnp.concatenate([ab[0], zero], axis=1),
                                     jnp.concatenate([zero, ab[1]], axis=1)], axis=0)
        s["tinv"] = eye + s["apow"]

    for _ in range(int(math.log2(CHUNK)) - 1):
        sq = [_dot(s["apow"], s["apow"]) for s in st]
        fill()
        for s, x in zip(st, sq):
            s["apow"] = x
        pr = [_dot(s["tinv"], s["apow"]) for s in st]
        fill()
        for s, x in zip(st, pr):
            s["tinv"] = s["tinv"] + x
    while fillers:
        fill()

    akv = [[_dot(s["ak"][g], s["sv"][g]) for g in range(ngrp)] for s in st]
    rkv = [[_dot(s["rk"][g], s["sv"][g]) for g in range(ngrp)] for s in st]
    tw = [[_dot(s["tinv"][g * GSTACK:(g + 1) * GSTACK, g * GSTACK:(g + 1) * GSTACK],
                jnp.concatenate([s["sa"][g].astype(F32), akv[c][g]], axis=1)) for g in range(ngrp)]
          for c, s in enumerate(st)]
    rbx = [[_dot(s["rb"][g], tw[c][g]) for g in range(ngrp)] for c, s in enumerate(st)]
    outs = []
    for c, s in enumerate(st):
        cat = lambda f: jnp.concatenate([f(g) for g in range(ngrp)], axis=1)
        qt = s["r_b"] + cat(lambda g: _unstack_heads(rbx[c][g][:, :HALF]))
        wt = cat(lambda g: _unstack_heads(tw[c][g][:, :HALF]))
        yloc = cat(lambda g: _unstack_heads(rbx[c][g][:, HALF:] + rkv[c][g]))
        u = cat(lambda g: _unstack_heads(tw[c][g][:, HALF:]))
        outs.append((qt, wt, yloc, u, s["bt"], s["kt"], s["pc"]))
    return outs


def _rwkv_local_body(nt_seq, p_ref, pp_ref, pn_ref, mup_ref, mun_ref, w0_ref, w2_ref, a0_ref, a2_ref, g2_ref,
                     kk_ref, ka_ref, rk_ref, ones_ref,
                     g_out, bonus_out, v_out, qwf, yuf, bkf, pcf, qwr, yur, bkr, pcr,
                     r_s, v_s, kk_s, lw_s, k_s, a_s, ps_s):
    i = pl.program_id(0) % nt_seq
    p = p_ref[...]
    tm = p.shape[0]
    prev, nxt = _shifted(p, pp_ref[...], pn_ref[...], i == 0, i == nt_seq - 1)
    ps_s[...] = p + mup_ref[...] * (prev - p) + mun_ref[...] * (nxt - p)
    c = RWKV_WIDTH
    span = CHUNK * LOCAL_CHUNKS_PER_ITER

    def param_steps(h):
        rows = slice(h * span, (h + 1) * span)
        st = {}

        def split():
            ps = ps_s[rows, :]
            st["r"], st["k"], st["v"] = ps[:, 0:c], ps[:, c:2 * c], ps[:, 2 * c:3 * c]
            st["lw"], st["la"], st["lg"] = (ps[:, 3 * c + j * 128:3 * c + (j + 1) * 128] for j in range(3))
            v_out[rows, :] = st["v"]
            r_s[rows, :] = st["r"]
            v_s[rows, :] = st["v"]

        def decay_proj():
            st["w_raw"] = w0_ref[...] + _dot2(jnp.tanh(st["lw"]), w2_ref[...])

        def decay():
            lw_s[rows, :] = -jnp.exp(-jnp.logaddexp(-st["w_raw"], 0.0) - 0.5)

        def iclr():
            st["a"] = jax.nn.sigmoid(a0_ref[...] + _dot2(st["la"], a2_ref[...]))
            a_s[rows, :] = st["a"]

        def gate():
            g_out[rows, :] = _dot(jax.nn.sigmoid(st["lg"]), g2_ref[...])

        def key_norm():
            kk0 = st["k"] * kk_ref[...]
            kk_s[rows, :] = kk0 * lax.rsqrt(jnp.maximum(_seg_sum(kk0 * kk0, ones_ref[...]), 1e-24))

        def key_dir():
            ka = ka_ref[...]
            st["kf"] = st["k"] * (1.0 + (st["a"][:, :c] - 1.0) * ka)
            st["kr"] = st["k"] * (1.0 + (st["a"][:, c:] - 1.0) * ka)
            k_s[rows, :c] = st["kf"]
            k_s[rows, c:] = st["kr"]

        def bonus():
            rk = st["r"] * ((st["kf"] + st["kr"]) * 0.5) * rk_ref[...]
            bonus_out[rows, :] = _seg_sum(rk, ones_ref[...]) * st["v"]

        return [split, decay_proj, decay, iclr, gate, key_norm, key_dir, bonus]

    out_refs = ((qwf, yuf, bkf, pcf), (qwr, yur, bkr, pcr))
    nspan = tm // span
    for step in param_steps(0):
        step()
    for h in range(nspan):
        chains, where = [], []
        for uu in range(LOCAL_CHUNKS_PER_ITER):
            ci = h * LOCAL_CHUNKS_PER_ITER + uu
            rows = slice(ci * CHUNK, (ci + 1) * CHUNK)
            rc, vc, kkc = r_s[rows, :], v_s[rows, :], kk_s[rows, :]
            for d in range(2):
                lanes = slice(d * c, (d + 1) * c)
                chains.append((rc, vc, kkc, lw_s[rows, lanes], k_s[rows, lanes], a_s[rows, lanes], d == 1))
                where.append((ci, d))
        fillers = param_steps(h + 1) if h + 1 < nspan else []
        for (ci, d), (qt, wt, yloc, u, bt, kt, pcv) in zip(where, _chunks_local(chains, fillers)):
            qw, yu, bk, pc = out_refs[d]
            qw[ci] = jnp.concatenate([qt, wt], axis=0).astype(BF16)
            yu[ci] = jnp.concatenate([yloc, u], axis=0)
            bk[ci] = jnp.concatenate([bt, kt], axis=0).astype(BF16)
            pc[ci] = jnp.broadcast_to(pcv, (8, c))


def _rwkv_local(p_a, nt_seq, tm, prm):
    t = p_a.shape[0]
    nt = t // tm
    nc = tm // CHUNK
    c = RWKV_WIDTH
    rd, ri = RWKV_DECAY_RANK, RWKV_ICLR_RANK
    z = jnp.zeros((rd, c), F32)
    w2pad = jnp.concatenate([jnp.concatenate([prm["rwkv_w2"][0], z], 0), jnp.concatenate([z, prm["rwkv_w2"][1]], 0)], 1)
    z = jnp.zeros((ri, c), F32)
    a2pad = jnp.concatenate([jnp.concatenate([prm["rwkv_a2"][0], z], 0), jnp.concatenate([z, prm["rwkv_a2"][1]], 0)], 1)
    hd = np.arange(c) // RWKV_HEAD_DIM
    ones_bd = jnp.asarray(hd[:, None] == hd[None, :], BF16)
    full = lambda i: (0, 0)
    row = lambda i: (i, 0)
    row3 = lambda i: (i, 0, 0)
    nblk8 = t // 8
    tm8 = tm // 8
    vec = lambda n: pl.BlockSpec((1, n), full)
    chunk_out = lambda rows, dt: (pl.BlockSpec((nc, rows, c), row3), jax.ShapeDtypeStruct((t // CHUNK, rows, c), dt))
    per_dir = [chunk_out(2 * CHUNK, BF16), chunk_out(2 * CHUNK, F32), chunk_out(2 * CHUNK, BF16), chunk_out(8, F32)]
    outs = [(pl.BlockSpec((tm, c), row), jax.ShapeDtypeStruct((t, c), F32))] * 3 + per_dir + per_dir
    return pl.pallas_call(
        functools.partial(_rwkv_local_body, nt_seq),
        grid=(nt,),
        in_specs=[pl.BlockSpec((tm, C_RWKV_IN), row),
                  pl.BlockSpec((8, C_RWKV_IN), lambda i: (jnp.maximum(i * tm8 - 1, 0), 0)),
                  pl.BlockSpec((8, C_RWKV_IN), lambda i: (jnp.minimum((i + 1) * tm8, nblk8 - 1), 0)),
                  vec(C_RWKV_IN), vec(C_RWKV_IN), vec(2 * c), pl.BlockSpec((128, 2 * c), full),
                  vec(2 * c), pl.BlockSpec((128, 2 * c), full), pl.BlockSpec((RWKV_GATE_RANK, c), full),
                  vec(c), vec(c), vec(c), pl.BlockSpec((c, c), full)],
        out_specs=[o[0] for o in outs],
        out_shape=[o[1] for o in outs],
        scratch_shapes=[pltpu.VMEM((tm, c), F32)] * 3 + [pltpu.VMEM((tm, 2 * c), F32)] * 3
                       + [pltpu.VMEM((tm, C_RWKV_IN), F32)],
        compiler_params=_cparams("parallel"),
        name="rwkv_local",
    )(p_a, p_a, p_a, prm["rwkv_mu_prev"].reshape(1, -1), prm["rwkv_mu_next"].reshape(1, -1),
      prm["rwkv_w0"].reshape(1, 2 * c), w2pad, prm["rwkv_a0"].reshape(1, 2 * c), a2pad, prm["rwkv_g2"],
      prm["rwkv_k_k"].reshape(1, c), prm["rwkv_k_a"].reshape(1, c), prm["rwkv_r_k"].reshape(1, c), ones_bd)


SCAN_TILE = 128


def _rwkv_scan_body(nc, nseq, qwf, yuf, bkf, pcf, vf, qwr, yur, bkr, pcr, vr, yf_ref, yr_ref, s_ref):
    @pl.when(pl.program_id(0) == 0)
    def _():
        s_ref[...] = jnp.zeros(s_ref.shape, F32)

    brow = lax.broadcasted_iota(I32, (HALF, HALF), 0)
    bcol = lax.broadcasted_iota(I32, (HALF, HALF), 1)
    same_head = (brow >> 6) == (bcol >> 6)

    dirs = ((qwf, yuf, bkf, pcf, vf, yf_ref), (qwr, yur, bkr, pcr, vr, yr_ref))

    def step(j, carry):
        ch = []
        for b in range(nseq):
            for d, (qw_ref, yu_ref, bk_ref, pc_ref, v_ref, y_ref) in enumerate(dirs):
                ci = j if d == 0 else nc - 1 - j
                rows = pl.ds(pl.multiple_of(ci * CHUNK, CHUNK), CHUNK)
                qw, yu, bk = qw_ref[b, ci], yu_ref[b, ci], bk_ref[b, ci]
                pc, vc = pc_ref[b, ci][0:1], v_ref[b, rows, :]
                for hf in range(2):
                    lanes = slice(hf * HALF, (hf + 1) * HALF)
                    ch.append(dict(sidx=(b, d, hf), b=b, rows=rows, lanes=lanes, y_ref=y_ref, qw=qw[:, lanes],
                                   yu=yu[:, lanes], bk=bk[:, lanes], pc=pc[:, lanes], vc=vc[:, lanes]))
        for c in ch:
            c["s"] = s_ref[c["sidx"]]
        ys = [lax.dot_general(c["qw"], c["s"].astype(BF16), _NT, preferred_element_type=F32) + c["yu"] for c in ch]
        for c, y in zip(ch, ys):
            c["y_ref"][c["b"], c["rows"], c["lanes"]] = y[:CHUNK]
        sav = [jnp.concatenate([y[CHUNK:], c["vc"]], axis=0).astype(BF16) for c, y in zip(ch, ys)]
        upd = [lax.dot_general(x, c["bk"], _TN, preferred_element_type=F32) for c, x in zip(ch, sav)]
        for c, u in zip(ch, upd):
            s_ref[c["sidx"]] = c["s"] * c["pc"] + jnp.where(same_head, u, 0.0)
        return carry

    lax.fori_loop(0, nc, step, 0)


def _rwkv_scan(loc, nseq, seq_len):
    _, _, v, qwf, yuf, bkf, pcf, qwr, yur, bkr, pcr = loc
    tm = min(SCAN_TILE * max(1, 4 // nseq), seq_len)
    nt_seq = seq_len // tm
    nc = tm // CHUNK
    c = RWKV_WIDTH
    ncs = seq_len // CHUNK
    fwd = lambda i: (0, i, 0, 0)
    rev = lambda i: (0, nt_seq - 1 - i, 0, 0)
    fwd_v = lambda i: (0, i, 0)
    rev_v = lambda i: (0, nt_seq - 1 - i, 0)

    def specs(m4, m3):
        return [pl.BlockSpec((nseq, nc, 2 * CHUNK, c), m4), pl.BlockSpec((nseq, nc, 2 * CHUNK, c), m4),
                pl.BlockSpec((nseq, nc, 2 * CHUNK, c), m4), pl.BlockSpec((nseq, nc, 8, c), m4),
                pl.BlockSpec((nseq, tm, c), m3)]

    per_seq = lambda a: a.reshape((nseq, ncs) + a.shape[1:])
    v3 = v.reshape(nseq, seq_len, c)
    yf, yr = pl.pallas_call(
        functools.partial(_rwkv_scan_body, nc, nseq),
        grid=(nt_seq,),
        in_specs=specs(fwd, fwd_v) + specs(rev, rev_v),
        out_specs=[pl.BlockSpec((nseq, tm, c), fwd_v), pl.BlockSpec((nseq, tm, c), rev_v)],
        out_shape=[jax.ShapeDtypeStruct((nseq, seq_len, c), F32)] * 2,
        scratch_shapes=[pltpu.VMEM((nseq, 2, 2, HALF, HALF), F32)],
        compiler_params=_cparams("arbitrary"),
        name="rwkv_scan",
    )(per_seq(qwf), per_seq(yuf), per_seq(bkf), per_seq(pcf), v3,
      per_seq(qwr), per_seq(yur), per_seq(bkr), per_seq(pcr), v3)
    return yf.reshape(nseq * seq_len, c), yr.reshape(nseq * seq_len, c)


def _rwkv_out(yf, yr, g, bonus, ln_g, ln_b, ones_bd):
    y = yf + yr
    inv_n = 1.0 / RWKV_HEAD_DIM
    mean = _seg_sum(y, ones_bd) * inv_n
    yc = y - mean
    var = _seg_sum(yc * yc, ones_bd) * inv_n
    yn = yc * lax.rsqrt(var + RWKV_GN_EPS) * ln_g + ln_b
    return (yn + bonus) * g


def _rwkv_branch(p_a, nseq, seq_len, prm):
    tm = min(512, seq_len)
    nt_seq = seq_len // tm
    loc = _rwkv_local(p_a, nt_seq, tm, prm)
    yf, yr = _rwkv_scan(loc, nseq, seq_len)
    return yf, yr, loc[0], loc[1]


FFT_N2 = 128
FILTER_EMB_DIM = 33
FILTER_BANDS = (FILTER_EMB_DIM - 1) // 2
FILTER_HIDDEN = 64
FILTER_TARGET = 1e-2
FILTER_FAST_DECAY_PCT = 0.3
FILTER_SLOW_DECAY_PCT = 1.5


def _hyena_prep_body(nt_seq, p_ref, pp_ref, pn_ref, cw_ref, cb_ref, src_ref, x2_ref):
    i = pl.program_id(0) % nt_seq
    p = p_ref[...]
    prev, nxt = _shifted(p, pp_ref[...], pn_ref[...], i == 0, i == nt_seq - 1)
    cw = cw_ref[...]
    u = prev * cw[0:1] + p * cw[1:2] + nxt * cw[2:3] + cb_ref[...]
    c = HYENA_WIDTH
    src_ref[...] = u[:, 0:c] * u[:, 2 * c:3 * c]
    x2_ref[...] = u[:, c:2 * c]


def _hyena_prep(p_b, nt_seq, tm, prm):
    t = p_b.shape[0]
    c = HYENA_WIDTH
    row = lambda i: (i, 0)
    full = lambda i: (0, 0)
    nblk8 = t // 8
    tm8 = tm // 8
    cw = jnp.concatenate([prm["hyena_conv_w"], jnp.zeros((5, C_HYENA_IN), F32)], axis=0)
    return pl.pallas_call(
        functools.partial(_hyena_prep_body, nt_seq),
        grid=(t // tm,),
        in_specs=[pl.BlockSpec((tm, C_HYENA_IN), row),
                  pl.BlockSpec((8, C_HYENA_IN), lambda i: (jnp.maximum(i * tm8 - 1, 0), 0)),
                  pl.BlockSpec((8, C_HYENA_IN), lambda i: (jnp.minimum((i + 1) * tm8, nblk8 - 1), 0)),
                  pl.BlockSpec((8, C_HYENA_IN), full), pl.BlockSpec((1, C_HYENA_IN), full)],
        out_specs=[pl.BlockSpec((tm, c), row)] * 2,
        out_shape=[jax.ShapeDtypeStruct((t, c), F32)] * 2,
        compiler_params=_cparams("parallel"),
        name="hyena_prep",
    )(p_b, p_b, p_b, cw, prm["hyena_conv_b"].reshape(1, -1))


def _hyena_filter_body(seq_len, w1_ref, b1_ref, w2_ref, b2_ref, w3_ref, freq_ref, delta_ref, o_ref):
    rows = o_ref.shape[0]
    n = pl.program_id(0) * rows + lax.broadcasted_iota(I32, (rows, 128), 0)
    lane = lax.broadcasted_iota(I32, (rows, 128), 1)
    pos = jnp.where(n < seq_len, n, 2 * seq_len - n).astype(F32)
    t = pos * (1.0 / (seq_len - 1))
    omega = (2.0 * math.pi) * pos / seq_len
    band_step = (FILTER_BANDS - 1 - 1e-4) / (FILTER_BANDS - 1)
    band_idx = jnp.where(lane <= FILTER_BANDS, lane - 1, lane - 1 - FILTER_BANDS).astype(F32)
    arg = (1e-4 + band_idx * band_step) * omega
    phase = jnp.where(lane <= FILTER_BANDS, 0.5 * math.pi, math.pi)
    z = jnp.where(lane == 0, t, jnp.where(lane <= 2 * FILTER_BANDS, jnp.sin(arg + phase), 0.0))
    half = rows // 2
    freq = freq_ref[...]
    hid = jnp.sin(freq * (_dot3(z[:half], w1_ref[0]) + _dot3(z[half:], w1_ref[1]) + b1_ref[...]))
    hid = jnp.sin(freq * (_dot3(hid, w2_ref[...]) + b2_ref[...]))
    filt = jnp.concatenate([_dot3(hid, w3_ref[0]), _dot3(hid, w3_ref[1])], axis=0)
    nn = n[:, 0:1]
    tt = t[:, 0:1]
    sel = jnp.where(nn < seq_len, filt[:, :HYENA_WIDTH], jnp.where(nn > seq_len, filt[:, HYENA_WIDTH:], 0.0))
    o_ref[...] = sel * jnp.exp(-tt * delta_ref[...])


def _hyena_filter(seq_len, prm):
    rows = min(1024, 2 * seq_len)
    c = HYENA_WIDTH
    fh = FILTER_HIDDEN
    w1 = jnp.concatenate([prm["hyena_f_w1"], jnp.zeros((128 - FILTER_EMB_DIM, fh), F32)], axis=0)
    z1 = jnp.zeros_like(w1)
    w1p = jnp.stack([jnp.concatenate([w1, z1], axis=1), jnp.concatenate([z1, w1], axis=1)])
    w2 = prm["hyena_f_w2"]
    z2 = jnp.zeros_like(w2)
    w2p = jnp.concatenate([jnp.concatenate([w2, z2], axis=1), jnp.concatenate([z2, w2], axis=1)], axis=0)
    w3 = prm["hyena_f_w3"]
    z3 = jnp.zeros_like(w3)
    w3p = jnp.stack([jnp.concatenate([w3, z3], axis=0), jnp.concatenate([z3, w3], axis=0)])
    twice = lambda a: jnp.tile(a.reshape(1, fh), (1, 2))
    min_decay = math.log(FILTER_TARGET) / FILTER_SLOW_DECAY_PCT
    max_decay = math.log(FILTER_TARGET) / FILTER_FAST_DECAY_PCT
    deltas = jnp.abs(jnp.linspace(min_decay, max_decay, c, dtype=F32)).reshape(1, c)
    full = lambda i: (0, 0)
    full3 = lambda i: (0, 0, 0)
    return pl.pallas_call(
        functools.partial(_hyena_filter_body, seq_len),
        grid=(2 * seq_len // rows,),
        in_specs=[pl.BlockSpec((2, 128, 2 * fh), full3), pl.BlockSpec((1, 2 * fh), full),
                  pl.BlockSpec((2 * fh, 2 * fh), full), pl.BlockSpec((1, 2 * fh), full),
                  pl.BlockSpec((2, 2 * fh, 2 * c), full3), pl.BlockSpec((1, 2 * fh), full),
                  pl.BlockSpec((1, c), full)],
        out_specs=pl.BlockSpec((rows, c), lambda i: (i, 0)),
        out_shape=jax.ShapeDtypeStruct((2 * seq_len, c), F32),
        compiler_params=_cparams("parallel"),
        name="hyena_filter",
    )(w1p, twice(prm["hyena_f_b1"]), w2p, twice(prm["hyena_f_b2"]), w3p, twice(prm["hyena_f_freq"]), deltas)


FFT_N2_TILE = 16
FFT_LANE_TILE = 256


def _lmul_rows(m_ref, x_ref):
    m = m_ref[...]
    xt = pltpu.einshape("rjc->jrc", x_ref[0].astype(F32))
    ys = [jnp.dot(m, xt[j].astype(BF16), preferred_element_type=F32) for j in range(FFT_N2_TILE)]
    return pltpu.einshape("jrc->rjc", jnp.stack(ys, axis=0))


def _lmul_body(m_ref, x_ref, o_ref):
    o_ref[0] = _lmul_rows(m_ref, x_ref).astype(o_ref.dtype)


def _lmul_epilogue_body(m_ref, x_ref, src_ref, x2_ref, skip_ref, o_ref):
    o_ref[0] = x2_ref[0] * (_lmul_rows(m_ref, x_ref) + src_ref[0] * skip_ref[...])


def _lmul(m, x, extra=None):
    nfft, r_in, n2, c = x.shape
    r_out = m.shape[0]
    ct = FFT_LANE_TILE
    xs = pl.BlockSpec((1, r_in, FFT_N2_TILE, ct), lambda f, j, l: (f, 0, j, l))
    os_ = pl.BlockSpec((1, r_out, FFT_N2_TILE, ct), lambda f, j, l: (f, 0, j, l))
    ms = pl.BlockSpec(m.shape, lambda f, j, l: (0, 0))
    if extra is None:
        body, ins, specs, out_dtype = _lmul_body, (m, x), [ms, xs], BF16
    else:
        src, x2, skip = extra
        body, ins, out_dtype = _lmul_epilogue_body, (m, x, src, x2, skip), F32
        specs = [ms, xs, os_, os_, pl.BlockSpec((1, ct), lambda f, j, l: (0, l))]
    return pl.pallas_call(
        body,
        grid=(nfft, n2 // FFT_N2_TILE, c // ct),
        in_specs=specs,
        out_specs=os_,
        out_shape=jax.ShapeDtypeStruct((nfft, r_out, n2, c), out_dtype),
        compiler_params=_cparams("parallel", "parallel", "parallel"),
        name="fft_outer" if extra is None else "fft_outer_out",
    )(*ins)


def _fft_inner_body(kt, conv, g_ref, y_ref, *rest):
    if conv:
        h_ref, o_ref = rest
    else:
        (o_ref,) = rest
    n2 = FFT_N2
    for q in range(kt):
        g = g_ref[q]
        yk = jnp.concatenate([y_ref[0, 0, q], y_ref[0, 1, q]], axis=0).astype(BF16)
        z = jnp.dot(g, yk, preferred_element_type=F32)
        if conv:
            zr, zi = z[:n2], z[n2:]
            hr, hi = h_ref[0, 0, q].astype(F32), h_ref[0, 1, q].astype(F32)
            pr = zr * hr - zi * hi
            pi = zr * hi + zi * hr
            prod = jnp.concatenate([pr, pi], axis=0).astype(BF16)
            z = lax.dot_general(g, prod, _TN, preferred_element_type=F32)
        o_ref[0, 0, q] = z[:n2].astype(o_ref.dtype)
        o_ref[0, 1, q] = z[n2:].astype(o_ref.dtype)


def _fft_inner(g, y, h, kt):
    nfft, _, n1, n2, c = y.shape
    blk = pl.BlockSpec((1, 2, kt, n2, c), lambda f, j: (f, 0, j, 0, 0))
    gs = pl.BlockSpec((kt, 2 * n2, 2 * n2), lambda f, j: (j, 0, 0))
    conv = h is not None
    ins = (g, y, h) if conv else (g, y)
    specs = [gs, blk, pl.BlockSpec((1, 2, kt, n2, c), lambda f, j: (0, 0, j, 0, 0))] if conv else [gs, blk]
    return pl.pallas_call(
        functools.partial(_fft_inner_body, kt, conv),
        grid=(nfft, n1 // kt),
        in_specs=specs,
        out_specs=blk,
        out_shape=jax.ShapeDtypeStruct(y.shape, BF16),
        compiler_params=_cparams("parallel", "parallel"),
        name="fft_inner_conv" if conv else "fft_inner",
    )(*ins)


def _dft_tables(seq_len):
    n = 2 * seq_len
    n2 = FFT_N2
    n1 = n // n2
    k1 = jnp.arange(n1, dtype=I32)
    ang1 = (2.0 * math.pi / n1) * ((k1[:, None] * k1[None, :]) % n1).astype(F32)
    c1, s1 = jnp.cos(ang1), jnp.sin(ang1)
    half = n1 // 2
    f_pair = jnp.concatenate([jnp.concatenate([c1[:, :half], s1[:, :half]], 1),
                              jnp.concatenate([-s1[:, :half], c1[:, :half]], 1)], 0)
    f_real_half = jnp.concatenate([c1[:, :half], -s1[:, :half]], 0)
    f_real_full = jnp.concatenate([c1, -s1], 0)
    ci, si = c1[:half] / n, s1[:half] / n
    b_real = jnp.concatenate([ci, -si], 1)
    b_pair = jnp.concatenate([b_real, jnp.concatenate([si, ci], 1)], 0)
    kk = k1[:, None, None] + n1 * jnp.arange(n2, dtype=I32)[None, :, None]
    ang2 = (2.0 * math.pi / n) * ((kk * jnp.arange(n2, dtype=I32)[None, None, :]) % n).astype(F32)
    c2, s2 = jnp.cos(ang2), jnp.sin(ang2)
    g = jnp.concatenate([jnp.concatenate([c2, s2], 2), jnp.concatenate([-s2, c2], 2)], 1)
    cast = lambda a: a.astype(BF16)
    return dict(f_pair=cast(f_pair), f_real_half=cast(f_real_half), f_real_full=cast(f_real_full),
                b_pair=cast(b_pair), b_real=cast(b_real), g=cast(g), n1=n1)


def _hyena_branch(p_b, nseq, seq_len, prm):
    tm = min(512, seq_len)
    c = HYENA_WIDTH
    src, x2 = _hyena_prep(p_b, seq_len // tm, tm, prm)
    tab = _dft_tables(seq_len)
    n1 = tab["n1"]
    n2 = FFT_N2
    kt = 4
    filt = _hyena_filter(seq_len, prm)
    hy = _lmul(tab["f_real_full"], filt.reshape(1, n1, n2, c))
    hspec = _fft_inner(tab["g"], hy.reshape(1, 2, n1, n2, c), None, kt)
    pair = nseq % 2 == 0
    nfft = nseq // 2 if pair else nseq
    rows = n1 if pair else n1 // 2
    xin = src.reshape(nfft, rows, n2, c)
    y = _lmul(tab["f_pair"] if pair else tab["f_real_half"], xin)
    w = _fft_inner(tab["g"], y.reshape(nfft, 2, n1, n2, c), hspec, kt)
    out = _lmul(tab["b_pair"] if pair else tab["b_real"], w.reshape(nfft, 2 * n1, n2, c),
                extra=(xin, x2.reshape(nfft, rows, n2, c), prm["hyena_skip"].reshape(1, c)))
    return out.reshape(nseq * seq_len, c)


MOE_GROUPS = 4
MOE_EXPERTS_PER_GROUP = 8
MOE_EXPERTS = MOE_GROUPS * MOE_EXPERTS_PER_GROUP
MOE_HIDDEN = D_MODEL // 2
ROUTE_LANES = 128


def _route(logits):
    lane = lax.broadcasted_iota(I32, logits.shape, 1)
    neg = -jnp.inf
    big = ROUTE_LANES
    gl = jnp.where(lane < MOE_GROUPS, logits, neg)
    gmax = jnp.max(gl, axis=-1, keepdims=True)
    grp = jnp.min(jnp.where(gl == gmax, lane, big), axis=-1, keepdims=True)
    p_grp = 1.0 / jnp.sum(jnp.exp(gl - gmax), axis=-1, keepdims=True)
    lo = MOE_GROUPS + grp * MOE_EXPERTS_PER_GROUP
    el = jnp.where((lane >= lo) & (lane < lo + MOE_EXPERTS_PER_GROUP), logits, neg)
    m1 = jnp.max(el, axis=-1, keepdims=True)
    i1 = jnp.min(jnp.where(el == m1, lane, big), axis=-1, keepdims=True)
    el2 = jnp.where(lane == i1, neg, el)
    m2 = jnp.max(el2, axis=-1, keepdims=True)
    i2 = jnp.min(jnp.where(el2 == m2, lane, big), axis=-1, keepdims=True)
    e2 = jnp.exp(m2 - m1)
    g1 = p_grp / (1.0 + e2)
    g2 = p_grp * e2 / (1.0 + e2)
    gates = jnp.where(lane == i1, g1, jnp.where(lane == i2, g2, 0.0))
    return pltpu.roll(gates, ROUTE_LANES - MOE_GROUPS, 1)


def _merge_body(x_ref, yf_ref, yr_ref, rg_ref, bonus_ref, lng_ref, lnb_ref, ones_ref, yb_ref, pg_ref,
                wua_ref, wub_ref, wo_ref, g_ref, wr_ref, br_ref, x1_ref, xn_ref, gwt_ref):
    ya = _rwkv_out(yf_ref[...], yr_ref[...], rg_ref[...], bonus_ref[...], lng_ref[...], lnb_ref[...], ones_ref[...])
    ga = jax.nn.sigmoid(pg_ref[:, :D_MODEL].astype(F32))
    gb = jax.nn.sigmoid(pg_ref[:, D_MODEL:].astype(F32))
    merged = ga * _dot(ya, wua_ref[...]) + gb * _dot(yb_ref[...], wub_ref[...])
    x1 = x_ref[...] + _dot(merged, wo_ref[...])
    x1_ref[...] = x1
    xn = x1 * lax.rsqrt(jnp.mean(x1 * x1, axis=-1, keepdims=True) + NORM_EPS) * g_ref[...]
    xn_ref[...] = xn.astype(BF16)
    logits = _dot3(xn, wr_ref[...]) + br_ref[...]
    gwt_ref[...] = _route(logits).T


def _merge(x, rwkv, yb, pg, prm, wts, tm):
    t = x.shape[0]
    c = RWKV_WIDTH
    row = lambda i: (i, 0)
    full = lambda i: (0, 0)
    hd = np.arange(c) // RWKV_HEAD_DIM
    ones_bd = jnp.asarray(hd[:, None] == hd[None, :], BF16)
    wr = jnp.concatenate([prm["moe_w_route_group"], prm["moe_w_route_expert"],
                          jnp.zeros((D_MODEL, ROUTE_LANES - MOE_GROUPS - MOE_EXPERTS), F32)], axis=1)
    br = jnp.concatenate([prm["moe_b_route_group"], prm["moe_b_route_expert"],
                          jnp.zeros((ROUTE_LANES - MOE_GROUPS - MOE_EXPERTS,), F32)]).reshape(1, ROUTE_LANES)
    return pl.pallas_call(
        _merge_body,
        grid=(t // tm,),
        in_specs=[pl.BlockSpec((tm, D_MODEL), row)] + [pl.BlockSpec((tm, c), row)] * 4
                 + [pl.BlockSpec((1, c), full)] * 2 + [pl.BlockSpec((c, c), full),
                  pl.BlockSpec((tm, HYENA_WIDTH), row), pl.BlockSpec((tm, C_GATES), row),
                  pl.BlockSpec((RWKV_WIDTH, D_MODEL), full), pl.BlockSpec((HYENA_WIDTH, D_MODEL), full),
                  pl.BlockSpec((D_MODEL, D_MODEL), full), pl.BlockSpec((1, D_MODEL), full),
                  pl.BlockSpec((D_MODEL, ROUTE_LANES), full), pl.BlockSpec((1, ROUTE_LANES), full)],
        out_specs=[pl.BlockSpec((tm, D_MODEL), row), pl.BlockSpec((tm, D_MODEL), row),
                   pl.BlockSpec((ROUTE_LANES, tm), lambda i: (0, i))],
        out_shape=[jax.ShapeDtypeStruct((t, D_MODEL), F32), jax.ShapeDtypeStruct((t, D_MODEL), BF16),
                   jax.ShapeDtypeStruct((ROUTE_LANES, t), F32)],
        compiler_params=_cparams("parallel"),
        name="merge_route",
    )(x, *rwkv, prm["rwkv_ln_g"].reshape(1, c), prm["rwkv_ln_b"].reshape(1, c), ones_bd, yb, pg,
      wts["w_up_a"], wts["w_up_b"], wts["w_out"], prm["norm_ffn_g"].reshape(1, D_MODEL), wr, br)


MOE_PAD = 16
MOE_BLK = 256
MOE_FFN_ROWS = 512


def _moe_slots(tt):
    ns = 2 * tt + MOE_EXPERTS * MOE_PAD
    return (ns + MOE_BLK - 1) // MOE_BLK * MOE_BLK


def _moe_run_copies(i, lstart_ref, npiece_ref, goff_ref, local_refs, global_refs, sems, to_global, wait):
    for e in range(MOE_EXPERTS):
        ls = lstart_ref[i * MOE_EXPERTS + e]
        go = goff_ref[i * MOE_EXPERTS + e]

        def piece(p, carry, ls=ls, go=go):
            lrows = pl.ds(pl.multiple_of(ls + p * MOE_PAD, MOE_PAD), MOE_PAD)
            grows = pl.ds(pl.multiple_of(go + p * MOE_PAD, MOE_PAD), MOE_PAD)
            for loc, glob, sem in zip(local_refs, global_refs, sems):
                src, dst = (loc.at[lrows, :], glob.at[grows, :]) if to_global else (glob.at[grows, :], loc.at[lrows, :])
                cp = pltpu.make_async_copy(src, dst, sem)
                if wait:
                    cp.wait()
                else:
                    cp.start()
            return carry

        lax.fori_loop(0, npiece_ref[i * MOE_EXPERTS + e], piece, 0)


def _moe_compact_body(lstart_ref, npiece_ref, goff_ref, xn_ref, gwt_ref, tri_ref, pcol_ref, xg0_ref, gg0_ref,
                      xg_out, gg_out, dcol_ref, xg_loc, gg_loc, sems):
    del xg0_ref, gg0_ref
    i = pl.program_id(0)
    tt = xn_ref.shape[0]
    ns = _moe_slots(tt)
    gw = gwt_ref[0:MOE_EXPERTS, :]
    sel = gw > 0.0
    rank = jnp.dot(sel.astype(BF16), tri_ref[...], preferred_element_type=F32)
    dest = pcol_ref[:, 0:1] + rank
    d_lo = jnp.min(jnp.where(sel, dest, 1e9), axis=0, keepdims=True)
    d_hi = jnp.max(jnp.where(sel, dest, -1.0), axis=0, keepdims=True)
    g_lo = jnp.sum(jnp.where(sel & (dest == d_lo), gw, 0.0), axis=0, keepdims=True)
    g_hi = jnp.sum(jnp.where(sel & (dest == d_hi), gw, 0.0), axis=0, keepdims=True)
    single = d_hi == d_lo
    g_hi = jnp.where(single, 0.0, g_hi)
    d_hi = jnp.where(single, -1.0, d_hi)
    rows8 = jnp.concatenate([d_lo, d_hi, g_lo, g_hi, jnp.zeros((4, tt), F32)], axis=0)
    dcol_ref[...] = jnp.concatenate([rows8, jnp.zeros((ROUTE_LANES - 8, tt), F32)], axis=0).T
    last = i * MOE_EXPERTS + MOE_EXPERTS - 1
    used_rows = lstart_ref[last] + npiece_ref[last] * MOE_PAD

    def gather_block(blk):
        r = (blk * MOE_BLK + lax.broadcasted_iota(I32, (MOE_BLK, tt), 0)).astype(F32)
        lo, hi = r == d_lo, r == d_hi
        rows = slice(blk * MOE_BLK, (blk + 1) * MOE_BLK)
        xg_loc[rows, :] = jnp.dot((lo | hi).astype(BF16), xn_ref[...], preferred_element_type=F32).astype(BF16)
        row_gate = jnp.sum(jnp.where(lo, g_lo, 0.0) + jnp.where(hi, g_hi, 0.0), axis=-1, keepdims=True)
        gg_loc[rows, :] = jnp.broadcast_to(row_gate, (MOE_BLK, ROUTE_LANES))

    for blk in range(ns // MOE_BLK):
        if blk * MOE_BLK < 2 * tt:
            gather_block(blk)
        else:
            pl.when(blk * MOE_BLK < used_rows)(functools.partial(gather_block, blk))
    copies = functools.partial(_moe_run_copies, i, lstart_ref, npiece_ref, goff_ref, (xg_loc, gg_loc),
                               (xg_out, gg_out), (sems.at[0], sems.at[1]), True)
    copies(False)
    copies(True)


def _moe_ffn_body(bexp_ref, nused_ref, xg_ref, gg_ref, wg_ref, wu_ref, wd_ref, og_ref):
    del bexp_ref
    used = pl.program_id(0) < nused_ref[0]

    @pl.when(jnp.logical_not(used))
    def _():
        og_ref[...] = jnp.zeros(og_ref.shape, BF16)

    @pl.when(used)
    def _():
        xg = xg_ref[...]
        hg = jnp.dot(xg, wg_ref[0], preferred_element_type=F32)
        hu = jnp.dot(xg, wu_ref[0], preferred_element_type=F32)
        hid = ((hg * jax.nn.sigmoid(hg)) * hu).astype(BF16)
        out = jnp.dot(hid, wd_ref[0], preferred_element_type=F32)
        og_ref[...] = (out * gg_ref[:, 0:1]).astype(BF16)


def _moe_scatter_body(final_norm, lstart_ref, npiece_ref, goff_ref, og_ref, x1_ref, dcol_ref, gf_ref, o_ref,
                      og_loc, sems):
    i = pl.program_id(0)
    tt = x1_ref.shape[0]
    ns = _moe_slots(tt)

    @pl.when(i == 0)
    def _():
        og_loc[...] = jnp.zeros(og_loc.shape, BF16)

    copies = functools.partial(_moe_run_copies, i, lstart_ref, npiece_ref, goff_ref, (og_loc,), (og_ref,),
                               (sems.at[0],), False)
    copies(False)
    copies(True)
    y = x1_ref[...]
    c_lo, c_hi = dcol_ref[:, 0:1], dcol_ref[:, 1:2]
    for blk in range(ns // MOE_BLK):
        r = (blk * MOE_BLK + lax.broadcasted_iota(I32, (tt, MOE_BLK), 1)).astype(F32)
        onehot = ((r == c_lo) | (r == c_hi)).astype(BF16)
        y = y + jnp.dot(onehot, og_loc[blk * MOE_BLK:(blk + 1) * MOE_BLK, :], preferred_element_type=F32)
    if final_norm:
        y = y * lax.rsqrt(jnp.mean(y * y, axis=-1, keepdims=True) + NORM_EPS) * gf_ref[...]
    o_ref[...] = y


def _moe_grouped(xn, gwt, x1, wts, norm_final_g, tt):
    t = xn.shape[0]
    final_norm = norm_final_g is not None
    if not final_norm:
        norm_final_g = jnp.ones((D_MODEL,), F32)
    nt = t // tt
    ns = _moe_slots(tt)
    ne = MOE_EXPERTS
    counts = jnp.sum((gwt[:ne] > 0.0).reshape(ne, nt, tt), axis=-1, dtype=I32).T
    padded = (counts + MOE_PAD - 1) // MOE_PAD * MOE_PAD
    lstart = jnp.cumsum(padded, axis=1) - padded
    region = (jnp.sum(padded, axis=0) + MOE_FFN_ROWS - 1) // MOE_FFN_ROWS * MOE_FFN_ROWS
    gend = jnp.cumsum(region)
    goff = (gend - region)[None, :] + jnp.cumsum(padded, axis=0) - padded
    ng = (2 * t + nt * ne * MOE_PAD + ne * MOE_FFN_ROWS + MOE_FFN_ROWS - 1) // MOE_FFN_ROWS * MOE_FFN_ROWS
    nb = ng // MOE_FFN_ROWS
    block_row = jnp.arange(nb, dtype=I32)[:, None] * MOE_FFN_ROWS
    bexp = jnp.minimum(jnp.sum(block_row >= gend[None, :], axis=1, dtype=I32), ne - 1)
    nused = (gend[-1] // MOE_FFN_ROWS).astype(I32).reshape(1)
    pcol = jnp.broadcast_to(lstart.astype(F32).reshape(nt * ne, 1), (nt * ne, ROUTE_LANES))
    idx = np.arange(tt)
    tri = jnp.asarray(idx[:, None] < idx[None, :], BF16)
    sched = (lstart.reshape(-1), (padded // MOE_PAD).reshape(-1), goff.reshape(-1).astype(I32))
    any_spec = pl.BlockSpec(memory_space=pl.ANY)

    xg, gg, dcol = pl.pallas_call(
        _moe_compact_body,
        grid_spec=pltpu.PrefetchScalarGridSpec(
            num_scalar_prefetch=3,
            grid=(nt,),
            in_specs=[pl.BlockSpec((tt, D_MODEL), lambda i, *_: (i, 0)),
                      pl.BlockSpec((ROUTE_LANES, tt), lambda i, *_: (0, i)),
                      pl.BlockSpec((tt, tt), lambda i, *_: (0, 0)),
                      pl.BlockSpec((ne, ROUTE_LANES), lambda i, *_: (i, 0)),
                      any_spec, any_spec],
            out_specs=[any_spec, any_spec, pl.BlockSpec((tt, ROUTE_LANES), lambda i, *_: (i, 0))],
            scratch_shapes=[pltpu.VMEM((ns, D_MODEL), BF16), pltpu.VMEM((ns, ROUTE_LANES), F32),
                            pltpu.SemaphoreType.DMA((2,))],
        ),
        out_shape=[jax.ShapeDtypeStruct((ng, D_MODEL), BF16), jax.ShapeDtypeStruct((ng, ROUTE_LANES), F32),
                   jax.ShapeDtypeStruct((t, ROUTE_LANES), F32)],
        input_output_aliases={7: 0, 8: 1},
        compiler_params=_cparams("arbitrary"),
        name="moe_compact",
    )(*sched, xn, gwt, tri, pcol, jnp.zeros((ng, D_MODEL), BF16), jnp.zeros((ng, ROUTE_LANES), F32))

    og = pl.pallas_call(
        _moe_ffn_body,
        grid_spec=pltpu.PrefetchScalarGridSpec(
            num_scalar_prefetch=2,
            grid=(nb,),
            in_specs=[pl.BlockSpec((MOE_FFN_ROWS, D_MODEL), lambda b, *_: (b, 0)),
                      pl.BlockSpec((MOE_FFN_ROWS, ROUTE_LANES), lambda b, *_: (b, 0)),
                      pl.BlockSpec((1, D_MODEL, MOE_HIDDEN), lambda b, be, nu: (be[b], 0, 0)),
                      pl.BlockSpec((1, D_MODEL, MOE_HIDDEN), lambda b, be, nu: (be[b], 0, 0)),
                      pl.BlockSpec((1, MOE_HIDDEN, D_MODEL), lambda b, be, nu: (be[b], 0, 0))],
            out_specs=pl.BlockSpec((MOE_FFN_ROWS, D_MODEL), lambda b, *_: (b, 0)),
        ),
        out_shape=jax.ShapeDtypeStruct((ng, D_MODEL), BF16),
        compiler_params=_cparams("arbitrary"),
        name="moe_ffn",
    )(bexp, nused, xg, gg, wts["moe_w_gate"], wts["moe_w_up"], wts["moe_w_down"])

    return pl.pallas_call(
        functools.partial(_moe_scatter_body, final_norm),
        grid_spec=pltpu.PrefetchScalarGridSpec(
            num_scalar_prefetch=3,
            grid=(nt,),
            in_specs=[any_spec,
                      pl.BlockSpec((tt, D_MODEL), lambda i, *_: (i, 0)),
                      pl.BlockSpec((tt, ROUTE_LANES), lambda i, *_: (i, 0)),
                      pl.BlockSpec((1, D_MODEL), lambda i, *_: (0, 0))],
            out_specs=pl.BlockSpec((tt, D_MODEL), lambda i, *_: (i, 0)),
            scratch_shapes=[pltpu.VMEM((ns, D_MODEL), BF16), pltpu.SemaphoreType.DMA((1,))],
        ),
        out_shape=jax.ShapeDtypeStruct((t, D_MODEL), F32),
        compiler_params=_cparams("arbitrary"),
        name="moe_scatter",
    )(*sched, og, x1, dcol, norm_final_g.reshape(1, D_MODEL))


def _trunk(x, prm, wts, norm_final_g):
    nseq, seq_len, _ = x.shape
    xf = x.reshape(nseq * seq_len, D_MODEL)
    p_a, p_b, p_g = _norm_in_proj(xf, prm["norm_mix_g"], wts["w_in"], min(512, seq_len))
    rwkv = _rwkv_branch(p_a, nseq, seq_len, prm)
    yb = _hyena_branch(p_b, nseq, seq_len, prm)
    x1, xn, gwt = _merge(xf, rwkv, yb, p_g, prm, wts, min(512, seq_len))
    out = _moe_grouped(xn, gwt, x1, wts, norm_final_g, min(1024, seq_len))
    return out.reshape(nseq, seq_len, D_MODEL)


def kernel(x_prompt, x_sample, norm_mix_g, w_in, rwkv_mu_prev, rwkv_mu_next, rwkv_w0, rwkv_w2, rwkv_a0, rwkv_a2, rwkv_g2, rwkv_k_k, rwkv_k_a, rwkv_r_k, rwkv_ln_g, rwkv_ln_b, hyena_conv_w, hyena_conv_b, hyena_f_w1, hyena_f_b1, hyena_f_w2, hyena_f_b2, hyena_f_w3, hyena_f_freq, hyena_skip, w_up_a, w_up_b, w_out, norm_ffn_g, moe_w_route_group, moe_b_route_group, moe_w_route_expert, moe_b_route_expert, moe_w_gate, moe_w_up, moe_w_down, norm_final_g):
    layer = dict(norm_mix_g=norm_mix_g, w_in=w_in, rwkv_mu_prev=rwkv_mu_prev, rwkv_mu_next=rwkv_mu_next,
                 rwkv_w0=rwkv_w0, rwkv_w2=rwkv_w2, rwkv_a0=rwkv_a0, rwkv_a2=rwkv_a2, rwkv_g2=rwkv_g2,
                 rwkv_k_k=rwkv_k_k, rwkv_k_a=rwkv_k_a, rwkv_r_k=rwkv_r_k, rwkv_ln_g=rwkv_ln_g, rwkv_ln_b=rwkv_ln_b,
                 hyena_conv_w=hyena_conv_w, hyena_conv_b=hyena_conv_b, hyena_f_w1=hyena_f_w1, hyena_f_b1=hyena_f_b1,
                 hyena_f_w2=hyena_f_w2, hyena_f_b2=hyena_f_b2, hyena_f_w3=hyena_f_w3, hyena_f_freq=hyena_f_freq,
                 hyena_skip=hyena_skip, w_up_a=w_up_a, w_up_b=w_up_b, w_out=w_out, norm_ffn_g=norm_ffn_g,
                 moe_w_route_group=moe_w_route_group, moe_b_route_group=moe_b_route_group,
                 moe_w_route_expert=moe_w_route_expert, moe_b_route_expert=moe_b_route_expert,
                 moe_w_gate=moe_w_gate, moe_w_up=moe_w_up, moe_w_down=moe_w_down)
    depth = norm_mix_g.shape[0]
    big = ("w_in", "w_up_a", "w_up_b", "w_out", "moe_w_gate", "moe_w_up", "moe_w_down")

    def trunk(x):
        for li in range(depth):
            prm = {k: v[li] for k, v in layer.items()}
            wts = {k: prm[k].astype(BF16) for k in big}
            last = li == depth - 1
            x = _trunk(x, prm, wts, norm_final_g if last else None)
        return x

    return (trunk(x_prompt), trunk(x_sample))
```
